```python
import math
import jax, jax.numpy as jnp
from jax import lax
import numpy as np

D_MODEL = 2048
BATCH = 1
SEQ = 8192
DEPTH = 4

BLOCK = 128
DIFF_HEADS = 8
DIFF_QK_DIM = 64
DIFF_V_DIM = 2 * DIFF_QK_DIM
MLA_HEADS = 8
MLA_Q_RANK = 512
MLA_KV_RANK = 512
MLA_NOPE = 128
MLA_ROPE = 64
MLA_V = 128
GQA_HEADS = 8
GQA_KV_HEADS = 2
GQA_GROUP = GQA_HEADS // GQA_KV_HEADS
GQA_DIM = 128
WINDOW = 128
N_BRANCH = 3
BRANCH_WIDTH = 1024
MEM_LEN = 256
XA_HEADS = 4
XA_DIM = 128
D_FF = 5632
ROPE_THETA = 10000.0
EPS = 1e-6
DIFF_SUBLN_EPS = 1e-5
NEG_INF = -1e30

DIFF_Q_COLS = DIFF_HEADS * 2 * DIFF_QK_DIM
DIFF_K_COLS = DIFF_HEADS * 2 * DIFF_QK_DIM
DIFF_V_COLS = DIFF_HEADS * DIFF_V_DIM
GQA_Q_COLS = GQA_HEADS * GQA_DIM
GQA_KV_COLS = GQA_KV_HEADS * GQA_DIM
IN_WIDTHS = (DIFF_Q_COLS, DIFF_K_COLS, DIFF_V_COLS, MLA_Q_RANK, MLA_KV_RANK, MLA_ROPE,
             GQA_Q_COLS, GQA_KV_COLS, GQA_KV_COLS)
D_IN = sum(IN_WIDTHS)
IN_SPLITS = tuple(int(v) for v in np.cumsum(IN_WIDTHS)[:-1])

kernel_name = "hybrid_gated_diff_mla_swa_encoder"


def rms_norm(x, g, eps=EPS):
    xf = x.astype(jnp.float32)
    y = xf * lax.rsqrt(jnp.mean(xf * xf, axis=-1, keepdims=True) + eps)
    return (y * g.astype(jnp.float32)).astype(x.dtype)


def rope_tables(positions, dim):
    inv = ROPE_THETA ** (-jnp.arange(0, dim, 2, dtype=jnp.float32) / dim)
    ang = positions.astype(jnp.float32)[..., None] * inv
    return jnp.cos(ang), jnp.sin(ang)


def apply_rope(x, cos, sin):
    shape = cos.shape[:2] + (1,) * (x.ndim - 3) + cos.shape[-1:]
    c = cos.reshape(shape).astype(x.dtype)
    s = sin.reshape(shape).astype(x.dtype)
    x1, x2 = jnp.split(x, 2, axis=-1)
    return jnp.concatenate([x1 * c - x2 * s, x2 * c + x1 * s], axis=-1)


def to_query_blocks(t):
    b, s = t.shape[:2]
    t = t.reshape((b, s // BLOCK, BLOCK) + t.shape[2:])
    return jnp.moveaxis(t, 1, 0)


def from_query_blocks(t):
    t = jnp.moveaxis(t, 0, 1)
    return t.reshape((t.shape[0], t.shape[1] * t.shape[2]) + t.shape[3:])


def swiglu(x, w_gu, w_down):
    g, u = jnp.split(x @ w_gu, 2, axis=-1)
    return (jax.nn.silu(g) * u) @ w_down


def diff_attention(q, k, v, lam, sub_g, lambda_init):
    scale = DIFF_QK_DIM ** -0.5

    def block(qb):
        s = jnp.einsum('bqhcd,bkhcd->bhcqk', qb, k).astype(jnp.float32) * scale
        p = jax.nn.softmax(s, axis=-1)
        w = p[:, :, 0] - lam * p[:, :, 1]
        return jnp.einsum('bhqk,bkhd->bqhd', w.astype(v.dtype), v)

    o = from_query_blocks(lax.map(block, to_query_blocks(q)))
    o = rms_norm(o, sub_g, eps=DIFF_SUBLN_EPS) * (1.0 - lambda_init)
    return o.reshape(o.shape[:2] + (DIFF_HEADS * DIFF_V_DIM,))


def mla_attention(c_q, c_kv, k_rope, g_q, g_kv, w_uq, w_ukv, cos, sin):
    b, s = c_q.shape[:2]
    q = (rms_norm(c_q, g_q) @ w_uq).reshape(b, s, MLA_HEADS, MLA_NOPE + MLA_ROPE)
    q_nope = q[..., :MLA_NOPE]
    q_rope = apply_rope(q[..., MLA_NOPE:], cos, sin)
    kv = (rms_norm(c_kv, g_kv) @ w_ukv).reshape(b, s, MLA_HEADS, MLA_NOPE + MLA_V)
    k_nope = kv[..., :MLA_NOPE]
    v = kv[..., MLA_NOPE:]
    k_r = apply_rope(k_rope, cos, sin)
    scale = (MLA_NOPE + MLA_ROPE) ** -0.5

    def block(qs):
        qn, qr = qs
        sc = (jnp.einsum('bqhd,bkhd->bhqk', qn, k_nope).astype(jnp.float32)
              + jnp.einsum('bqhr,bkr->bhqk', qr, k_r).astype(jnp.float32)) * scale
        p = jax.nn.softmax(sc, axis=-1)
        return jnp.einsum('bhqk,bkhd->bqhd', p.astype(v.dtype), v)

    o = from_query_blocks(lax.map(block, (to_query_blocks(q_nope), to_query_blocks(q_rope))))
    return o.reshape(b, s, MLA_HEADS * MLA_V)


def window_gqa(q, k, v, sink):
    b, s = q.shape[:2]
    nb = s // BLOCK
    qb = q.reshape(b, nb, BLOCK, GQA_KV_HEADS, GQA_GROUP, GQA_DIM)

    def band(t):
        tb = t.reshape(b, nb, BLOCK, GQA_KV_HEADS, GQA_DIM)
        tp = jnp.pad(tb, ((0, 0), (1, 1), (0, 0), (0, 0), (0, 0)))
        return jnp.concatenate([tp[:, :-2], tp[:, 1:-1], tp[:, 2:]], axis=2)

    kb, vb = band(k), band(v)
    sc = jnp.einsum('bnqhgd,bnkhd->bnhgqk', qb, kb).astype(jnp.float32) * (GQA_DIM ** -0.5)
    r = jnp.arange(BLOCK)[:, None]
    c = jnp.arange(3 * BLOCK)[None, :]
    rel = c - BLOCK - r
    kpos = jnp.arange(nb)[:, None, None] * BLOCK - BLOCK + c
    valid = (jnp.abs(rel) <= WINDOW)[None] & (kpos >= 0) & (kpos < s)
    sc = jnp.where(valid[None, :, None, None], sc, NEG_INF)
    sink_col = jnp.broadcast_to(
        sink.astype(jnp.float32).reshape(1, 1, GQA_KV_HEADS, GQA_GROUP, 1, 1), sc.shape[:-1] + (1,))
    p = jax.nn.softmax(jnp.concatenate([sc, sink_col], axis=-1), axis=-1)[..., :-1]
    o = jnp.einsum('bnhgqk,bnkhd->bnqhgd', p.astype(v.dtype), vb)
    return o.reshape(b, s, GQA_HEADS * GQA_DIM)


def memory_cross_attention(hn, memn, wq, wkv, wo):
    b, s = hn.shape[:2]
    m = memn.shape[1]
    q = (hn @ wq).reshape(b, s, XA_HEADS, XA_DIM)
    kv = (memn @ wkv).reshape(b, m, 2, XA_HEADS, XA_DIM)
    sc = jnp.einsum('bqhd,bkhd->bhqk', q, kv[:, :, 0]).astype(jnp.float32) * (XA_DIM ** -0.5)
    p = jax.nn.softmax(sc, axis=-1)
    o = jnp.einsum('bhqk,bkhd->bqhd', p.astype(hn.dtype), kv[:, :, 1])
    return o.reshape(b, s, XA_HEADS * XA_DIM) @ wo


def setup_inputs(seed: int = 0) -> dict:
    key = jax.random.key(seed)
    ks = jax.random.split(key, 32)
    f32 = jnp.float32

    def w(k, shape, fan_in):
        return jax.random.normal(k, shape, f32) * (fan_in ** -0.5)

    def gain(k, shape):
        return 1.0 + 0.02 * jax.random.normal(k, shape, f32)

    L, D = DEPTH, D_MODEL
    return {
        'x': jax.random.normal(ks[0], (BATCH, SEQ, D), f32),
        'mem': jax.random.normal(ks[1], (BATCH, MEM_LEN, D), f32),
        'positions': jnp.broadcast_to(jnp.arange(SEQ, dtype=jnp.int32), (BATCH, SEQ)),
        'ffn1_norm': gain(ks[2], (L, D)),
        'ffn1_w_gu': w(ks[3], (L, D, 2 * D_FF), D),
        'ffn1_w_down': w(ks[4], (L, D_FF, D), D_FF),
        'mix_norm': gain(ks[5], (L, D)),
        'w_in': w(ks[6], (L, D, D_IN), D),
        'diff_lambda': 0.1 * jax.random.normal(ks[7], (L, 4, DIFF_QK_DIM), f32),
        'diff_subln': gain(ks[8], (L, DIFF_V_DIM)),
        'mla_q_norm': gain(ks[9], (L, MLA_Q_RANK)),
        'mla_kv_norm': gain(ks[10], (L, MLA_KV_RANK)),
        'mla_w_uq': w(ks[11], (L, MLA_Q_RANK, MLA_HEADS * (MLA_NOPE + MLA_ROPE)), MLA_Q_RANK),
        'mla_w_ukv': w(ks[12], (L, MLA_KV_RANK, MLA_HEADS * (MLA_NOPE + MLA_V)), MLA_KV_RANK),
        'gqa_sink': 0.5 * jax.random.normal(ks[13], (L, GQA_HEADS), f32),
        'w_branch': w(ks[14], (L, N_BRANCH, BRANCH_WIDTH, D), BRANCH_WIDTH),
        'w_gate': w(ks[15], (L, D, N_BRANCH * D), D),
        'b_gate': 0.02 * jax.random.normal(ks[16], (L, N_BRANCH * D), f32),
        'w_o': w(ks[17], (L, D, D), D),
        'xa_norm': gain(ks[18], (L, D)),
        'mem_norm': gain(ks[19], (L, D)),
        'xa_wq': w(ks[20], (L, D, XA_HEADS * XA_DIM), D),
        'xa_wkv': w(ks[21], (L, D, 2 * XA_HEADS * XA_DIM), D),
        'xa_wo': w(ks[22], (L, XA_HEADS * XA_DIM, D), XA_HEADS * XA_DIM),
        'ffn2_norm': gain(ks[23], (L, D)),
        'ffn2_w_gu': w(ks[24], (L, D, 2 * D_FF), D),
        'ffn2_w_down': w(ks[25], (L, D_FF, D), D_FF),
        'final_norm': gain(ks[26], (D,)),
    }


def reference(x, mem, positions, ffn1_norm, ffn1_w_gu, ffn1_w_down, mix_norm, w_in,
              diff_lambda, diff_subln, mla_q_norm, mla_kv_norm, mla_w_uq, mla_w_ukv,
              gqa_sink, w_branch, w_gate, b_gate, w_o, xa_norm, mem_norm, xa_wq, xa_wkv,
              xa_wo, ffn2_norm, ffn2_w_gu, ffn2_w_down, final_norm):
    b, s, d = x.shape
    cos64, sin64 = rope_tables(positions, DIFF_QK_DIM)
    cos128, sin128 = rope_tables(positions, GQA_DIM)
    h = x
    for l in range(DEPTH):
        lambda_init = 0.8 - 0.6 * math.exp(-0.3 * l)
        h = h + 0.5 * swiglu(rms_norm(h, ffn1_norm[l]), ffn1_w_gu[l], ffn1_w_down[l])

        u = rms_norm(h, mix_norm[l])
        dq, dk, dv, cq, ckv, kr, gq, gk, gv = jnp.split(u @ w_in[l], IN_SPLITS, axis=-1)

        dq = apply_rope(dq.reshape(b, s, DIFF_HEADS, 2, DIFF_QK_DIM), cos64, sin64)
        dk = apply_rope(dk.reshape(b, s, DIFF_HEADS, 2, DIFF_QK_DIM), cos64, sin64)
        dv = dv.reshape(b, s, DIFF_HEADS, DIFF_V_DIM)
        lp = diff_lambda[l].astype(jnp.float32)
        lam = jnp.exp(jnp.sum(lp[0] * lp[1])) - jnp.exp(jnp.sum(lp[2] * lp[3])) + lambda_init
        o_diff = diff_attention(dq, dk, dv, lam, diff_subln[l], lambda_init)

        o_mla = mla_attention(cq, ckv, kr, mla_q_norm[l], mla_kv_norm[l],
                              mla_w_uq[l], mla_w_ukv[l], cos64, sin64)

        gq = apply_rope(gq.reshape(b, s, GQA_HEADS, GQA_DIM), cos128, sin128)
        gk = apply_rope(gk.reshape(b, s, GQA_KV_HEADS, GQA_DIM), cos128, sin128)
        gv = gv.reshape(b, s, GQA_KV_HEADS, GQA_DIM)
        o_gqa = window_gqa(gq, gk, gv, gqa_sink[l])

        branches = jnp.stack([o_diff, o_mla, o_gqa], axis=2)
        gate = jax.nn.sigmoid((u @ w_gate[l] + b_gate[l]).astype(jnp.float32))
        gate = gate.astype(u.dtype).reshape(b, s, N_BRANCH, d)
        merged = jnp.sum(jnp.einsum('bsne,ned->bsnd', branches, w_branch[l]) * gate, axis=2)
        h = h + merged @ w_o[l]

        h = h + memory_cross_attention(rms_norm(h, xa_norm[l]), rms_norm(mem, mem_norm[l]),
                                       xa_wq[l], xa_wkv[l], xa_wo[l])

        h = h + 0.5 * swiglu(rms_norm(h, ffn2_norm[l]), ffn2_w_gu[l], ffn2_w_down[l])
    return rms_norm(h, final_norm)
```

```python
import functools
import math

import jax
import jax.numpy as jnp
from jax import lax
from jax.experimental import pallas as pl
from jax.experimental.pallas import tpu as pltpu

F32 = jnp.float32
BF16 = jnp.bfloat16

DIFF_HEADS = 8
DIFF_QK_DIM = 64
DIFF_V_DIM = 128
MLA_HEADS = 8
MLA_Q_RANK = 512
MLA_KV_RANK = 512
MLA_NOPE = 128
MLA_ROPE = 64
MLA_V = 128
GQA_HEADS = 8
GQA_KV_HEADS = 2
GQA_GROUP = GQA_HEADS // GQA_KV_HEADS
GQA_DIM = 128
WINDOW = 128
N_BRANCH = 3
BRANCH_WIDTH = 1024
XA_HEADS = 4
XA_DIM = 128
D_FF = 5632
ROPE_THETA = 10000.0
EPS = 1e-6
DIFF_SUBLN_EPS = 1e-5
NEG_INF = -1e30
LOG2E = math.log2(math.e)

LANES = 128
V7X_VMEM_LIMIT_BYTES = 56 * 1024 * 1024


def _cparams(semantics):
    return pltpu.CompilerParams(dimension_semantics=semantics, vmem_limit_bytes=V7X_VMEM_LIMIT_BYTES)


def _pick(dim, pref):
    if dim <= pref:
        return dim
    b = pref
    while dim % b:
        b //= 2
    return b


def _rope_table_kernel(pos_ref, inv_ref, sign_ref, cos_ref, sin_ref):
    ang = pos_ref[...] * inv_ref[...]
    cos_ref[...] = jnp.cos(ang)
    sin_ref[...] = jnp.sin(ang) * sign_ref[...]


def _rope_tables(pos_b, dim):
    s = pos_b.shape[0]
    half = dim // 2
    inv = ROPE_THETA ** (-jnp.arange(0, dim, 2, dtype=F32) / dim)
    inv_row = jnp.tile(inv, LANES // half)[None, :]
    sign_row = jnp.tile(jnp.concatenate([-jnp.ones((half,), F32), jnp.ones((half,), F32)]), LANES // dim)[None, :]
    bm = _pick(s, 1024)
    return pl.pallas_call(
        _rope_table_kernel,
        grid=(s // bm,),
        in_specs=[pl.BlockSpec((bm, LANES), lambda i: (i, 0)),
                  pl.BlockSpec((1, LANES), lambda i: (0, 0)),
                  pl.BlockSpec((1, LANES), lambda i: (0, 0))],
        out_specs=[pl.BlockSpec((bm, LANES), lambda i: (i, 0))] * 2,
        out_shape=[jax.ShapeDtypeStruct((s, LANES), F32)] * 2,
        compiler_params=_cparams(("parallel",)),
        name="rope_tables",
    )(pos_b, inv_row, sign_row)


def _rope_block(x, cos, sin_signed, half):
    if 2 * half == LANES:
        partner = pltpu.roll(x, half, axis=1)
    else:
        lane = lax.broadcasted_iota(jnp.int32, x.shape, 1)
        first = (lane % (2 * half)) < half
        partner = jnp.where(first, pltpu.roll(x, LANES - half, axis=1), pltpu.roll(x, half, axis=1))
    return x * cos + partner * sin_signed


def _rmsnorm_kernel(x_ref, g_ref, o_ref, *, eps):
    x = x_ref[...]
    y = x * lax.rsqrt(jnp.mean(x * x, axis=-1, keepdims=True) + eps)
    o_ref[...] = (y * g_ref[...]).astype(o_ref.dtype)


def _rmsnorm(x, g_stack, l, out_dtype=BF16):
    m, d = x.shape
    bm = _pick(m, 512)
    if l is None:
        g_spec = pl.BlockSpec((1, d), lambda i: (0, 0))
    else:
        g_spec = pl.BlockSpec((None, 1, d), lambda i: (l, 0, 0))
    return pl.pallas_call(
        functools.partial(_rmsnorm_kernel, eps=EPS),
        grid=(m // bm,),
        in_specs=[pl.BlockSpec((bm, d), lambda i: (i, 0)), g_spec],
        out_specs=pl.BlockSpec((bm, d), lambda i: (i, 0)),
        out_shape=jax.ShapeDtypeStruct((m, d), out_dtype),
        compiler_params=_cparams(("parallel",)),
        name="rmsnorm",
    )(x, g_stack)


def _mm_swiglu_kernel(x_ref, wg_ref, wu_ref, o_ref):
    x = x_ref[...]
    g = jnp.dot(x, wg_ref[...], preferred_element_type=F32)
    u = jnp.dot(x, wu_ref[...], preferred_element_type=F32)
    o_ref[...] = (g * jax.nn.sigmoid(g) * u).astype(o_ref.dtype)


def _ffn_up(xn, w_gu, l):
    m, d = xn.shape
    n = w_gu.shape[-1] // 2
    bm = _pick(m, 512)
    bn = n // 4 if n % (4 * LANES) == 0 else n
    nj = n // bn
    return pl.pallas_call(
        _mm_swiglu_kernel,
        grid=(nj, m // bm),
        in_specs=[pl.BlockSpec((bm, d), lambda j, i: (i, 0)),
                  pl.BlockSpec((None, d, bn), lambda j, i: (l, 0, j)),
                  pl.BlockSpec((None, d, bn), lambda j, i: (l, 0, j + nj))],
        out_specs=pl.BlockSpec((bm, bn), lambda j, i: (i, j)),
        out_shape=jax.ShapeDtypeStruct((m, n), BF16),
        compiler_params=_cparams(("parallel", "parallel")),
        name="ffn_up",
    )(xn, w_gu, w_gu)


def _mm_residual_kernel(a_ref, w_ref, r_ref, o_ref, *, scale):
    acc = jnp.dot(a_ref[...], w_ref[...], preferred_element_type=F32)
    o_ref[...] = r_ref[...] + scale * acc


def _mm_residual(a, w, l, res, scale):
    m, k = a.shape
    n = w.shape[-1]
    bm = _pick(m, 512)
    bn = _pick(n, 512) if k > 2048 else n
    return pl.pallas_call(
        functools.partial(_mm_residual_kernel, scale=scale),
        grid=(n // bn, m // bm),
        in_specs=[pl.BlockSpec((bm, k), lambda j, i: (i, 0)),
                  pl.BlockSpec((None, k, bn), lambda j, i: (l, 0, j)),
                  pl.BlockSpec((bm, bn), lambda j, i: (i, j))],
        out_specs=pl.BlockSpec((bm, bn), lambda j, i: (i, j)),
        out_shape=jax.ShapeDtypeStruct((m, n), F32),
        compiler_params=_cparams(("parallel", "parallel")),
        name="mm_residual",
    )(a, w, res)


def _mm_plain_kernel(x_ref, w_ref, o_ref):
    o_ref[...] = jnp.dot(x_ref[...], w_ref[...], preferred_element_type=F32).astype(o_ref.dtype)


def _mm_plain(x, w, l=None):
    m, k = x.shape
    n = w.shape[-1]
    bm = _pick(m, 512)
    if l is None:
        w_spec = pl.BlockSpec((k, n), lambda i: (0, 0))
    else:
        w_spec = pl.BlockSpec((None, k, n), lambda i: (l, 0, 0))
    return pl.pallas_call(
        _mm_plain_kernel,
        grid=(m // bm,),
        in_specs=[pl.BlockSpec((bm, k), lambda i: (i, 0)), w_spec],
        out_specs=pl.BlockSpec((bm, n), lambda i: (i, 0)),
        out_shape=jax.ShapeDtypeStruct((m, n), BF16),
        compiler_params=_cparams(("parallel",)),
        name="mm_plain",
    )(x, w)


def _mm_rope_kernel(x_ref, w_ref, cs_ref, cos_ref, sin_ref, o_ref, *, half, pattern):
    acc = jnp.dot(x_ref[...], w_ref[...], preferred_element_type=F32) * cs_ref[...]
    cos = cos_ref[...]
    sin = sin_ref[...]
    for b, roped in enumerate(pattern):
        blk = acc[:, b * LANES:(b + 1) * LANES]
        if roped:
            blk = _rope_block(blk, cos, sin, half)
        o_ref[:, b * LANES:(b + 1) * LANES] = blk.astype(o_ref.dtype)


def _mm_rope(x, w, colscale, cos, sin, half, pattern):
    m, k = x.shape
    n = w.shape[-1]
    assert n == LANES * len(pattern)
    bm = _pick(m, 512)
    return pl.pallas_call(
        functools.partial(_mm_rope_kernel, half=half, pattern=pattern),
        grid=(m // bm,),
        in_specs=[pl.BlockSpec((bm, k), lambda i: (i, 0)),
                  pl.BlockSpec((k, n), lambda i: (0, 0)),
                  pl.BlockSpec((1, n), lambda i: (0, 0)),
                  pl.BlockSpec((bm, LANES), lambda i: (i, 0)),
                  pl.BlockSpec((bm, LANES), lambda i: (i, 0))],
        out_specs=pl.BlockSpec((bm, n), lambda i: (i, 0)),
        out_shape=jax.ShapeDtypeStruct((m, n), BF16),
        compiler_params=_cparams(("parallel",)),
        name="mm_rope",
    )(x, w, colscale, cos, sin)


def _mm_rmsnorm_kernel(x_ref, w_ref, g_ref, o_ref):
    acc = jnp.dot(x_ref[...], w_ref[...], preferred_element_type=F32)
    y = acc * lax.rsqrt(jnp.mean(acc * acc, axis=-1, keepdims=True) + EPS)
    o_ref[...] = (y * g_ref[...]).astype(o_ref.dtype)


def _mm_rmsnorm(x, w, g):
    m, k = x.shape
    t, _, r = g.shape
    bm = _pick(m, 512)
    return pl.pallas_call(
        _mm_rmsnorm_kernel,
        grid=(t, m // bm),
        in_specs=[pl.BlockSpec((bm, k), lambda j, i: (i, 0)),
                  pl.BlockSpec((k, r), lambda j, i: (0, j)),
                  pl.BlockSpec((None, 1, r), lambda j, i: (j, 0, 0))],
        out_specs=pl.BlockSpec((None, bm, r), lambda j, i: (j, i, 0)),
        out_shape=jax.ShapeDtypeStruct((t, m, r), BF16),
        compiler_params=_cparams(("parallel", "parallel")),
        name="mm_rmsnorm",
    )(x, w, g)


def _diff_attn_kernel(lp_ref, q_ref, k_ref, v_ref, g_ref, o_ref, qc_s, m_s, l_s, acc_s, *, lambda_init):
    kv = pl.program_id(2)

    @pl.when(kv == 0)
    def _init():
        q = q_ref[...]
        lane = lax.broadcasted_iota(jnp.int32, q.shape, 1)
        zero = jnp.zeros_like(q)
        qc_s[0] = jnp.where(lane < DIFF_QK_DIM, q, zero)
        qc_s[1] = jnp.where(lane >= DIFF_QK_DIM, q, zero)
        m_s[...] = jnp.full(m_s.shape, -jnp.inf, F32)
        l_s[...] = jnp.zeros(l_s.shape, F32)
        acc_s[...] = jnp.zeros(acc_s.shape, F32)

    k = k_ref[...]
    v = v_ref[...]
    for c in range(2):
        s = lax.dot_general(qc_s[c], k, (((1,), (1,)), ((), ())), preferred_element_type=F32)
        m_prev = m_s[c]
        m_new = jnp.maximum(m_prev, jnp.max(s, axis=1, keepdims=True))
        alpha = jnp.exp2(m_prev - m_new)
        p = jnp.exp2(s - m_new)
        l_s[c] = alpha * l_s[c] + jnp.sum(p, axis=1, keepdims=True)
        acc_s[c] = alpha * acc_s[c] + jnp.dot(p.astype(BF16), v, preferred_element_type=F32)
        m_s[c] = m_new

    @pl.when(kv == pl.num_programs(2) - 1)
    def _finish():
        lp = lp_ref[...]
        lam = (jnp.exp(jnp.sum(lp[0:1] * lp[1:2], axis=1, keepdims=True))
               - jnp.exp(jnp.sum(lp[2:3] * lp[3:4], axis=1, keepdims=True)) + lambda_init)
        o = acc_s[0] / l_s[0] - lam * (acc_s[1] / l_s[1])
        y = o * lax.rsqrt(jnp.mean(o * o, axis=-1, keepdims=True) + DIFF_SUBLN_EPS)
        o_ref[...] = (y * g_ref[...] * (1.0 - lambda_init)).astype(o_ref.dtype)


def _diff_attention(qk, v, diff_lambda, diff_subln, l, lambda_init):
    s = qk.shape[0]
    bq = _pick(s, 512)
    tk = _pick(s, 512)
    h = DIFF_HEADS
    return pl.pallas_call(
        functools.partial(_diff_attn_kernel, lambda_init=lambda_init),
        grid=(h, s // bq, s // tk),
        in_specs=[pl.BlockSpec((None, 4, DIFF_QK_DIM), lambda hh, i, j: (l, 0, 0)),
                  pl.BlockSpec((bq, LANES), lambda hh, i, j: (i, hh)),
                  pl.BlockSpec((tk, LANES), lambda hh, i, j: (j, h + hh)),
                  pl.BlockSpec((tk, LANES), lambda hh, i, j: (j, hh)),
                  pl.BlockSpec((None, 1, DIFF_V_DIM), lambda hh, i, j: (l, 0, 0))],
        out_specs=pl.BlockSpec((bq, LANES), lambda hh, i, j: (i, hh)),
        out_shape=jax.ShapeDtypeStruct((s, h * DIFF_V_DIM), BF16),
        scratch_shapes=[pltpu.VMEM((2, bq, LANES), BF16),
                        pltpu.VMEM((2, bq, 1), F32),
                        pltpu.VMEM((2, bq, 1), F32),
                        pltpu.VMEM((2, bq, DIFF_V_DIM), F32)],
        compiler_params=_cparams(("parallel", "parallel", "arbitrary")),
        name="diff_attention",
    )(diff_lambda, qk, qk, v, diff_subln)


def _mla_attn_kernel(q_ref, kn_ref, kr_ref, v_ref, o_ref, m_s, l_s, acc_s):
    kv = pl.program_id(2)

    @pl.when(kv == 0)
    def _init():
        m_s[...] = jnp.full(m_s.shape, -jnp.inf, F32)
        l_s[...] = jnp.zeros(l_s.shape, F32)
        acc_s[...] = jnp.zeros(acc_s.shape, F32)

    kcat = jnp.concatenate([kn_ref[...], kr_ref[...]], axis=1)
    s = lax.dot_general(q_ref[...], kcat, (((1,), (1,)), ((), ())), preferred_element_type=F32)
    m_prev = m_s[...]
    m_new = jnp.maximum(m_prev, jnp.max(s, axis=1, keepdims=True))
    alpha = jnp.exp2(m_prev - m_new)
    p = jnp.exp2(s - m_new)
    l_s[...] = alpha * l_s[...] + jnp.sum(p, axis=1, keepdims=True)
    acc_s[...] = alpha * acc_s[...] + jnp.dot(p.astype(BF16), v_ref[...], preferred_element_type=F32)
    m_s[...] = m_new

    @pl.when(kv == pl.num_programs(2) - 1)
    def _finish():
        o_ref[...] = (acc_s[...] / l_s[...]).astype(o_ref.dtype)


def _mla_attention(q, kv_up, qk):
    s = q.shape[0]
    bq = _pick(s, 512)
    tk = _pick(s, 512)
    h = MLA_HEADS
    kr_blk = 2 * DIFF_HEADS
    return pl.pallas_call(
        _mla_attn_kernel,
        grid=(h, s // bq, s // tk),
        in_specs=[pl.BlockSpec((bq, 2 * LANES), lambda hh, i, j: (i, hh)),
                  pl.BlockSpec((tk, LANES), lambda hh, i, j: (j, 2 * hh)),
                  pl.BlockSpec((tk, LANES), lambda hh, i, j: (j, kr_blk)),
                  pl.BlockSpec((tk, LANES), lambda hh, i, j: (j, 2 * hh + 1))],
        out_specs=pl.BlockSpec((bq, LANES), lambda hh, i, j: (i, hh)),
        out_shape=jax.ShapeDtypeStruct((s, h * MLA_V), BF16),
        scratch_shapes=[pltpu.VMEM((bq, 1), F32),
                        pltpu.VMEM((bq, 1), F32),
                        pltpu.VMEM((bq, MLA_V), F32)],
        compiler_params=_cparams(("parallel", "parallel", "arbitrary")),
        name="mla_attention",
    )(q, kv_up, qk, kv_up)


def _gqa_kernel(sink_ref, q_ref, kp_ref, kc_ref, kn_ref, vp_ref, vc_ref, vn_ref, o_ref, *, layer, seq, bq):
    g = pl.program_id(0)
    i = pl.program_id(1)
    kb = jnp.concatenate([kp_ref[...], kc_ref[...], kn_ref[...]], axis=0)
    vb = jnp.concatenate([vp_ref[...], vc_ref[...], vn_ref[...]], axis=0)
    qpos = i * bq + lax.broadcasted_iota(jnp.int32, (bq, 3 * bq), 0)
    kpos = (i - 1) * bq + lax.broadcasted_iota(jnp.int32, (bq, 3 * bq), 1)
    valid = (jnp.abs(kpos - qpos) <= WINDOW) & (kpos >= 0) & (kpos < seq)
    for hh in range(GQA_GROUP):
        q = q_ref[:, hh * GQA_DIM:(hh + 1) * GQA_DIM]
        s = lax.dot_general(q, kb, (((1,), (1,)), ((), ())), preferred_element_type=F32)
        s = jnp.where(valid, s, NEG_INF)
        sink = sink_ref[layer * GQA_HEADS + g * GQA_GROUP + hh] * LOG2E
        m = jnp.maximum(jnp.max(s, axis=1, keepdims=True), sink)
        p = jnp.exp2(s - m)
        denom = jnp.sum(p, axis=1, keepdims=True) + jnp.exp2(sink - m)
        o = jnp.dot(p.astype(BF16), vb, preferred_element_type=F32) / denom
        o_ref[:, hh * GQA_DIM:(hh + 1) * GQA_DIM] = o.astype(o_ref.dtype)


def _window_gqa(gqk, vv, sink_flat, l):
    s = gqk.shape[0]
    bq = _pick(s, 256)
    assert bq >= WINDOW
    nb = s // bq
    qw = GQA_GROUP * GQA_DIM
    kcol = GQA_HEADS
    vcol = DIFF_HEADS

    def prev(g, i):
        return jnp.maximum(i - 1, 0)

    def nxt(g, i):
        return jnp.minimum(i + 1, nb - 1)

    return pl.pallas_call(
        functools.partial(_gqa_kernel, layer=l, seq=s, bq=bq),
        grid=(GQA_KV_HEADS, nb),
        in_specs=[pl.BlockSpec(memory_space=pltpu.SMEM),
                  pl.BlockSpec((bq, qw), lambda g, i: (i, g)),
                  pl.BlockSpec((bq, LANES), lambda g, i: (prev(g, i), kcol + g)),
                  pl.BlockSpec((bq, LANES), lambda g, i: (i, kcol + g)),
                  pl.BlockSpec((bq, LANES), lambda g, i: (nxt(g, i), kcol + g)),
                  pl.BlockSpec((bq, LANES), lambda g, i: (prev(g, i), vcol + g)),
                  pl.BlockSpec((bq, LANES), lambda g, i: (i, vcol + g)),
                  pl.BlockSpec((bq, LANES), lambda g, i: (nxt(g, i), vcol + g))],
        out_specs=pl.BlockSpec((bq, qw), lambda g, i: (i, g)),
        out_shape=jax.ShapeDtypeStruct((s, GQA_HEADS * GQA_DIM), BF16),
        compiler_params=_cparams(("parallel", "parallel")),
        name="window_gqa",
    )(sink_flat, gqk, gqk, gqk, gqk, vv, vv, vv)


def _merge_kernel(u_ref, o0_ref, o1_ref, o2_ref, wg0_ref, wg1_ref, wg2_ref, b0_ref, b1_ref, b2_ref,
                  wb0_ref, wb1_ref, wb2_ref, out_ref):
    u = u_ref[...]
    acc = None
    for o_ref, wg_ref, b_ref, wb_ref in ((o0_ref, wg0_ref, b0_ref, wb0_ref),
                                         (o1_ref, wg1_ref, b1_ref, wb1_ref),
                                         (o2_ref, wg2_ref, b2_ref, wb2_ref)):
        gate = jax.nn.sigmoid(jnp.dot(u, wg_ref[...], preferred_element_type=F32) + b_ref[...])
        term = jnp.dot(o_ref[...], wb_ref[...], preferred_element_type=F32) * gate
        acc = term if acc is None else acc + term
    out_ref[...] = acc.astype(out_ref.dtype)


def _merge(u, o_diff, o_mla, o_gqa, w_gate, b_gate, w_branch, l):
    m, d = u.shape
    bw = o_diff.shape[1]
    bm = _pick(m, 512)
    bn = _pick(d, 512)
    nj = d // bn
    o_spec = pl.BlockSpec((bm, bw), lambda j, i: (i, 0))

    def wg_spec(n):
        return pl.BlockSpec((None, d, bn), lambda j, i: (l, 0, n * nj + j))

    def b_spec(n):
        return pl.BlockSpec((None, 1, bn), lambda j, i: (l, 0, n * nj + j))

    def wb_spec(n):
        return pl.BlockSpec((None, None, bw, bn), lambda j, i: (l, n, 0, j))

    return pl.pallas_call(
        _merge_kernel,
        grid=(nj, m // bm),
        in_specs=[pl.BlockSpec((bm, d), lambda j, i: (i, 0)), o_spec, o_spec, o_spec,
                  wg_spec(0), wg_spec(1), wg_spec(2), b_spec(0), b_spec(1), b_spec(2),
                  wb_spec(0), wb_spec(1), wb_spec(2)],
        out_specs=pl.BlockSpec((bm, bn), lambda j, i: (i, j)),
        out_shape=jax.ShapeDtypeStruct((m, d), BF16),
        compiler_params=_cparams(("parallel", "parallel")),
        name="branch_merge",
    )(u, o_diff, o_mla, o_gqa, w_gate, w_gate, w_gate, b_gate, b_gate, b_gate,
      w_branch, w_branch, w_branch)


def _xattn_kernel(hn_ref, h_ref, wq_ref, k_ref, v_ref, wo_ref, o_ref):
    q = jnp.dot(hn_ref[...], wq_ref[...], preferred_element_type=F32) * (XA_DIM ** -0.5 * LOG2E)
    q = q.astype(BF16)
    heads = []
    for hh in range(XA_HEADS):
        sl = slice(hh * XA_DIM, (hh + 1) * XA_DIM)
        s = lax.dot_general(q[:, sl], k_ref[:, sl], (((1,), (1,)), ((), ())), preferred_element_type=F32)
        p = jnp.exp2(s - jnp.max(s, axis=1, keepdims=True))
        denom = jnp.sum(p, axis=1, keepdims=True)
        heads.append((jnp.dot(p.astype(BF16), v_ref[:, sl], preferred_element_type=F32) / denom).astype(BF16))
    o = jnp.concatenate(heads, axis=1)
    o_ref[...] = h_ref[...] + jnp.dot(o, wo_ref[...], preferred_element_type=F32)


def _cross_attention(hn, h, xa_wq, memkv, xa_wo, l):
    m, d = hn.shape
    w = XA_HEADS * XA_DIM
    ml = memkv.shape[0]
    bm = _pick(m, 512)
    return pl.pallas_call(
        _xattn_kernel,
        grid=(m // bm,),
        in_specs=[pl.BlockSpec((bm, d), lambda i: (i, 0)),
                  pl.BlockSpec((bm, d), lambda i: (i, 0)),
                  pl.BlockSpec((None, d, w), lambda i: (l, 0, 0)),
                  pl.BlockSpec((ml, w), lambda i: (0, 0)),
                  pl.BlockSpec((ml, w), lambda i: (0, 1)),
                  pl.BlockSpec((None, w, d), lambda i: (l, 0, 0))],
        out_specs=pl.BlockSpec((bm, d), lambda i: (i, 0)),
        out_shape=jax.ShapeDtypeStruct((m, d), F32),
        compiler_params=_cparams(("parallel",)),
        name="cross_attention",
    )(hn, h, xa_wq, memkv, memkv, xa_wo)


def _pad_cols(w, width):
    return jnp.pad(w, ((0, 0), (0, width - w.shape[-1])))


def kernel(x, mem, positions, ffn1_norm, ffn1_w_gu, ffn1_w_down, mix_norm, w_in, diff_lambda, diff_subln,
           mla_q_norm, mla_kv_norm, mla_w_uq, mla_w_ukv, gqa_sink, w_branch, w_gate, b_gate, w_o, xa_norm,
           mem_norm, xa_wq, xa_wkv, xa_wo, ffn2_norm, ffn2_w_gu, ffn2_w_down, final_norm):
    b, s, d = x.shape
    assert b == 1
    depth = ffn1_norm.shape[0]
    h = x.reshape(s, d)
    mem2 = mem.reshape(mem.shape[1], d)

    pos_b = jnp.broadcast_to(positions.reshape(s, 1).astype(F32), (s, LANES))
    cos64, sin64 = _rope_tables(pos_b, DIFF_QK_DIM)
    cos128, sin128 = _rope_tables(pos_b, GQA_DIM)

    bf = lambda w: w.astype(BF16)
    ffn1_w_gu_b, ffn1_w_down_b = bf(ffn1_w_gu), bf(ffn1_w_down)
    ffn2_w_gu_b, ffn2_w_down_b = bf(ffn2_w_gu), bf(ffn2_w_down)
    w_branch_b, w_gate_b, w_o_b = bf(w_branch), bf(w_gate), bf(w_o)
    xa_wq_b, xa_wkv_b, xa_wo_b = bf(xa_wq), bf(xa_wkv), bf(xa_wo)
    b_gate3 = b_gate.reshape(depth, 1, -1)
    norm3 = lambda g: g.reshape(depth, 1, -1)
    ffn1_norm3, mix_norm3, xa_norm3, mem_norm3, ffn2_norm3 = map(norm3, (ffn1_norm, mix_norm, xa_norm, mem_norm, ffn2_norm))
    diff_subln3 = diff_subln.reshape(depth, 1, -1)
    sink_flat = gqa_sink.reshape(-1)

    o_dq, o_dk, o_dv = 0, 1024, 2048
    o_cq, o_ckv, o_kr = 3072, 3584, 4096
    o_gq, o_gk, o_gv = 4160, 5184, 5440
    diff_scale = DIFF_QK_DIM ** -0.5 * LOG2E
    mla_scale = (MLA_NOPE + MLA_ROPE) ** -0.5 * LOG2E
    gqa_scale = GQA_DIM ** -0.5 * LOG2E
    cs_a = jnp.concatenate([jnp.full((1, 1024), diff_scale, F32), jnp.ones((1, 1024 + LANES), F32)], axis=1)
    cs_g = jnp.concatenate([jnp.full((1, 1024), gqa_scale, F32), jnp.ones((1, 256), F32)], axis=1)
    cs_q = jnp.full((1, MLA_HEADS * 2 * LANES), mla_scale, F32)

    for l in range(depth):
        lambda_init = 0.8 - 0.6 * math.exp(-0.3 * l)

        a = _ffn_up(_rmsnorm(h, ffn1_norm3, l), ffn1_w_gu_b, l)
        h = _mm_residual(a, ffn1_w_down_b, l, h, 0.5)

        u = _rmsnorm(h, mix_norm3, l)
        wl = w_in[l]
        w_a = bf(jnp.concatenate([wl[:, o_dq:o_dv], _pad_cols(wl[:, o_kr:o_gq], LANES)], axis=1))
        w_v = bf(jnp.concatenate([wl[:, o_dv:o_cq], wl[:, o_gv:]], axis=1))
        w_c = bf(wl[:, o_cq:o_kr])
        w_g = bf(wl[:, o_gq:o_gv])
        qk = _mm_rope(u, w_a, cs_a, cos64, sin64, DIFF_QK_DIM // 2, (True,) * 17)
        vv = _mm_plain(u, w_v)
        gains = jnp.stack([mla_q_norm[l], mla_kv_norm[l]])[:, None, :]
        cn = _mm_rmsnorm(u, w_c, gains)
        gqk = _mm_rope(u, w_g, cs_g, cos128, sin128, GQA_DIM // 2, (True,) * 10)

        o_diff = _diff_attention(qk, vv, diff_lambda, diff_subln3, l, lambda_init)

        wuq = mla_w_uq[l].reshape(MLA_Q_RANK, MLA_HEADS, MLA_NOPE + MLA_ROPE)
        wuq = jnp.pad(wuq, ((0, 0), (0, 0), (0, 2 * LANES - MLA_NOPE - MLA_ROPE)))
        wuq = bf(wuq.reshape(MLA_Q_RANK, MLA_HEADS * 2 * LANES))
        q_mla = _mm_rope(cn[0], wuq, cs_q, cos64, sin64, MLA_ROPE // 2, (False, True) * MLA_HEADS)
        kv_up = _mm_plain(cn[1], bf(mla_w_ukv[l]))
        o_mla = _mla_attention(q_mla, kv_up, qk)

        o_gqa = _window_gqa(gqk, vv, sink_flat, l)

        merged = _merge(u, o_diff, o_mla, o_gqa, w_gate_b, b_gate3, w_branch_b, l)
        h = _mm_residual(merged, w_o_b, l, h, 1.0)

        memkv = _mm_plain(_rmsnorm(mem2, mem_norm3, l), xa_wkv_b, l)
        h = _cross_attention(_rmsnorm(h, xa_norm3, l), h, xa_wq_b, memkv, xa_wo_b, l)

        a = _ffn_up(_rmsnorm(h, ffn2_norm3, l), ffn2_w_gu_b, l)
        h = _mm_residual(a, ffn2_w_down_b, l, h, 0.5)

    out = _rmsnorm(h, final_norm.reshape(1, d), None, out_dtype=F32)
    return out.reshape(b, s, d)
```

```python
import functools
import math

import jax
import jax.numpy as jnp
from jax import lax
from jax.experimental import pallas as pl
from jax.experimental.pallas import tpu as pltpu

F32 = jnp.float32
BF16 = jnp.bfloat16

DIFF_HEADS = 8
DIFF_QK_DIM = 64
DIFF_V_DIM = 128
MLA_HEADS = 8
MLA_Q_RANK = 512
MLA_KV_RANK = 512
MLA_NOPE = 128
MLA_ROPE = 64
MLA_V = 128
GQA_HEADS = 8
GQA_KV_HEADS = 2
GQA_GROUP = GQA_HEADS // GQA_KV_HEADS
GQA_DIM = 128
WINDOW = 128
N_BRANCH = 3
BRANCH_WIDTH = 1024
XA_HEADS = 4
XA_DIM = 128
D_FF = 5632
ROPE_THETA = 10000.0
EPS = 1e-6
DIFF_SUBLN_EPS = 1e-5
NEG_INF = -1e30
LOG2E = math.log2(math.e)

LANES = 128
V7X_VMEM_LIMIT_BYTES = 56 * 1024 * 1024


def _cparams(semantics):
    return pltpu.CompilerParams(dimension_semantics=semantics, vmem_limit_bytes=V7X_VMEM_LIMIT_BYTES)


def _pick(dim, pref):
    if dim <= pref:
        return dim
    b = pref
    while dim % b:
        b //= 2
    return b


def _rope_table_kernel(pos_ref, inv_ref, sign_ref, cos_ref, sin_ref):
    ang = pos_ref[...] * inv_ref[...]
    cos_ref[...] = jnp.cos(ang)
    sin_ref[...] = jnp.sin(ang) * sign_ref[...]


def _rope_tables(pos_b, dim):
    s = pos_b.shape[0]
    half = dim // 2
    inv = ROPE_THETA ** (-jnp.arange(0, dim, 2, dtype=F32) / dim)
    inv_row = jnp.tile(inv, LANES // half)[None, :]
    sign_row = jnp.tile(jnp.concatenate([-jnp.ones((half,), F32), jnp.ones((half,), F32)]), LANES // dim)[None, :]
    bm = _pick(s, 1024)
    return pl.pallas_call(
        _rope_table_kernel,
        grid=(s // bm,),
        in_specs=[pl.BlockSpec((bm, LANES), lambda i: (i, 0)),
                  pl.BlockSpec((1, LANES), lambda i: (0, 0)),
                  pl.BlockSpec((1, LANES), lambda i: (0, 0))],
        out_specs=[pl.BlockSpec((bm, LANES), lambda i: (i, 0))] * 2,
        out_shape=[jax.ShapeDtypeStruct((s, LANES), F32)] * 2,
        compiler_params=_cparams(("parallel",)),
        name="rope_tables",
    )(pos_b, inv_row, sign_row)


def _rope_block(x, cos, sin_signed, half):
    if 2 * half == LANES:
        partner = pltpu.roll(x, half, axis=1)
    else:
        lane = lax.broadcasted_iota(jnp.int32, x.shape, 1)
        first = (lane % (2 * half)) < half
        partner = jnp.where(first, pltpu.roll(x, LANES - half, axis=1), pltpu.roll(x, half, axis=1))
    return x * cos + partner * sin_signed


def _rmsnorm_kernel(x_ref, g_ref, o_ref, *, eps):
    x = x_ref[...]
    y = x * lax.rsqrt(jnp.mean(x * x, axis=-1, keepdims=True) + eps)
    o_ref[...] = (y * g_ref[...]).astype(o_ref.dtype)


def _rmsnorm(x, g_stack, l, out_dtype=BF16):
    m, d = x.shape
    bm = _pick(m, 512)
    if l is None:
        g_spec = pl.BlockSpec((1, d), lambda i: (0, 0))
    else:
        g_spec = pl.BlockSpec((None, 1, d), lambda i: (l, 0, 0))
    return pl.pallas_call(
        functools.partial(_rmsnorm_kernel, eps=EPS),
        grid=(m // bm,),
        in_specs=[pl.BlockSpec((bm, d), lambda i: (i, 0)), g_spec],
        out_specs=pl.BlockSpec((bm, d), lambda i: (i, 0)),
        out_shape=jax.ShapeDtypeStruct((m, d), out_dtype),
        compiler_params=_cparams(("parallel",)),
        name="rmsnorm",
    )(x, g_stack)


def _mm_swiglu_kernel(x_ref, wg_ref, wu_ref, o_ref):
    x = x_ref[...]
    g = jnp.dot(x, wg_ref[...], preferred_element_type=F32)
    u = jnp.dot(x, wu_ref[...], preferred_element_type=F32)
    o_ref[...] = (g * jax.nn.sigmoid(g) * u).astype(o_ref.dtype)


def _ffn_up(xn, w_gu, l):
    m, d = xn.shape
    n = w_gu.shape[-1] // 2
    bm = _pick(m, 512)
    bn = n // 4 if n % (4 * LANES) == 0 else n
    nj = n // bn
    return pl.pallas_call(
        _mm_swiglu_kernel,
        grid=(nj, m // bm),
        in_specs=[pl.BlockSpec((bm, d), lambda j, i: (i, 0)),
                  pl.BlockSpec((None, d, bn), lambda j, i: (l, 0, j)),
                  pl.BlockSpec((None, d, bn), lambda j, i: (l, 0, j + nj))],
        out_specs=pl.BlockSpec((bm, bn), lambda j, i: (i, j)),
        out_shape=jax.ShapeDtypeStruct((m, n), BF16),
        compiler_params=_cparams(("parallel", "parallel")),
        name="ffn_up",
    )(xn, w_gu, w_gu)


def _mm_residual_kernel(a_ref, w_ref, r_ref, o_ref, *, scale):
    acc = jnp.dot(a_ref[...], w_ref[...], preferred_element_type=F32)
    o_ref[...] = r_ref[...] + scale * acc


def _mm_residual(a, w, l, res, scale):
    m, k = a.shape
    n = w.shape[-1]
    bm = _pick(m, 512)
    bn = _pick(n, 512) if k > 2048 else n
    return pl.pallas_call(
        functools.partial(_mm_residual_kernel, scale=scale),
        grid=(n // bn, m // bm),
        in_specs=[pl.BlockSpec((bm, k), lambda j, i: (i, 0)),
                  pl.BlockSpec((None, k, bn), lambda j, i: (l, 0, j)),
                  pl.BlockSpec((bm, bn), lambda j, i: (i, j))],
        out_specs=pl.BlockSpec((bm, bn), lambda j, i: (i, j)),
        out_shape=jax.ShapeDtypeStruct((m, n), F32),
        compiler_params=_cparams(("parallel", "parallel")),
        name="mm_residual",
    )(a, w, res)


def _mm_plain_kernel(x_ref, w_ref, o_ref):
    o_ref[...] = jnp.dot(x_ref[...], w_ref[...], preferred_element_type=F32).astype(o_ref.dtype)


def _mm_plain(x, w, l=None):
    m, k = x.shape
    n = w.shape[-1]
    bm = _pick(m, 512)
    if l is None:
        w_spec = pl.BlockSpec((k, n), lambda i: (0, 0))
    else:
        w_spec = pl.BlockSpec((None, k, n), lambda i: (l, 0, 0))
    return pl.pallas_call(
        _mm_plain_kernel,
        grid=(m // bm,),
        in_specs=[pl.BlockSpec((bm, k), lambda i: (i, 0)), w_spec],
        out_specs=pl.BlockSpec((bm, n), lambda i: (i, 0)),
        out_shape=jax.ShapeDtypeStruct((m, n), BF16),
        compiler_params=_cparams(("parallel",)),
        name="mm_plain",
    )(x, w)


def _mm_rope_kernel(x_ref, w_ref, cs_ref, cos_ref, sin_ref, o_ref, *, half, pattern):
    acc = jnp.dot(x_ref[...], w_ref[...], preferred_element_type=F32) * cs_ref[...]
    cos = cos_ref[...]
    sin = sin_ref[...]
    for b, roped in enumerate(pattern):
        blk = acc[:, b * LANES:(b + 1) * LANES]
        if roped:
            blk = _rope_block(blk, cos, sin, half)
        o_ref[:, b * LANES:(b + 1) * LANES] = blk.astype(o_ref.dtype)


def _mm_rope(x, w, colscale, cos, sin, half, pattern):
    m, k = x.shape
    n = w.shape[-1]
    assert n == LANES * len(pattern)
    bm = _pick(m, 512)
    return pl.pallas_call(
        functools.partial(_mm_rope_kernel, half=half, pattern=pattern),
        grid=(m // bm,),
        in_specs=[pl.BlockSpec((bm, k), lambda i: (i, 0)),
                  pl.BlockSpec((k, n), lambda i: (0, 0)),
                  pl.BlockSpec((1, n), lambda i: (0, 0)),
                  pl.BlockSpec((bm, LANES), lambda i: (i, 0)),
                  pl.BlockSpec((bm, LANES), lambda i: (i, 0))],
        out_specs=pl.BlockSpec((bm, n), lambda i: (i, 0)),
        out_shape=jax.ShapeDtypeStruct((m, n), BF16),
        compiler_params=_cparams(("parallel",)),
        name="mm_rope",
    )(x, w, colscale, cos, sin)


def _mm_rmsnorm_kernel(x_ref, w_ref, g_ref, o_ref):
    acc = jnp.dot(x_ref[...], w_ref[...], preferred_element_type=F32)
    y = acc * lax.rsqrt(jnp.mean(acc * acc, axis=-1, keepdims=True) + EPS)
    o_ref[...] = (y * g_ref[...]).astype(o_ref.dtype)


def _mm_rmsnorm(x, w, g):
    m, k = x.shape
    t, _, r = g.shape
    bm = _pick(m, 512)
    return pl.pallas_call(
        _mm_rmsnorm_kernel,
        grid=(t, m // bm),
        in_specs=[pl.BlockSpec((bm, k), lambda j, i: (i, 0)),
                  pl.BlockSpec((k, r), lambda j, i: (0, j)),
                  pl.BlockSpec((None, 1, r), lambda j, i: (j, 0, 0))],
        out_specs=pl.BlockSpec((None, bm, r), lambda j, i: (j, i, 0)),
        out_shape=jax.ShapeDtypeStruct((t, m, r), BF16),
        compiler_params=_cparams(("parallel", "parallel")),
        name="mm_rmsnorm",
    )(x, w, g)


def _softmax_pv(q, k_chunk, v_ref, s_ref, tk):
    nchunks, bq, _ = s_ref.shape
    nblk = tk // LANES

    mlane = None
    for j in range(nchunks):
        s = lax.dot_general(q, k_chunk(j), (((1,), (1,)), ((), ())), preferred_element_type=F32)
        s_ref[j] = s
        for b in range(nblk):
            blk = s[:, b * LANES:(b + 1) * LANES]
            mlane = blk if mlane is None else jnp.maximum(mlane, blk)
    m = jnp.broadcast_to(jnp.max(mlane, axis=1, keepdims=True), (bq, LANES))

    lsum = None
    acc = None
    for j in range(nchunks):
        s = s_ref[j]
        ps = []
        for b in range(nblk):
            p = jnp.exp2(s[:, b * LANES:(b + 1) * LANES] - m)
            lsum = p if lsum is None else lsum + p
            ps.append(p.astype(BF16))
        pv = jnp.dot(jnp.concatenate(ps, axis=1), v_ref[j * tk:(j + 1) * tk, :], preferred_element_type=F32)
        acc = pv if acc is None else acc + pv
    return acc / jnp.sum(lsum, axis=1, keepdims=True)


def _diff_attn_kernel(lp_ref, q_ref, k_ref, v_ref, g_ref, o_ref, s_ref, *, lambda_init, tk):
    q = q_ref[...]
    lane = lax.broadcasted_iota(jnp.int32, q.shape, 1)
    zero = jnp.zeros_like(q)

    def k_chunk(j):
        return k_ref[j * tk:(j + 1) * tk, :]

    o0 = _softmax_pv(jnp.where(lane < DIFF_QK_DIM, q, zero), k_chunk, v_ref, s_ref, tk)
    o1 = _softmax_pv(jnp.where(lane >= DIFF_QK_DIM, q, zero), k_chunk, v_ref, s_ref, tk)
    lp = lp_ref[...]
    lam = (jnp.exp(jnp.sum(lp[0:1] * lp[1:2], axis=1, keepdims=True))
           - jnp.exp(jnp.sum(lp[2:3] * lp[3:4], axis=1, keepdims=True)) + lambda_init)
    o = o0 - lam * o1
    y = o * lax.rsqrt(jnp.mean(o * o, axis=-1, keepdims=True) + DIFF_SUBLN_EPS)
    o_ref[...] = (y * g_ref[...] * (1.0 - lambda_init)).astype(o_ref.dtype)


def _attn_tiles(s):
    return _pick(s, 256), _pick(s, 1024)


def _diff_attention(qk, v, diff_lambda, diff_subln, l, lambda_init):
    s = qk.shape[0]
    bq, tk = _attn_tiles(s)
    h = DIFF_HEADS
    return pl.pallas_call(
        functools.partial(_diff_attn_kernel, lambda_init=lambda_init, tk=tk),
        grid=(h, s // bq),
        in_specs=[pl.BlockSpec((None, 4, DIFF_QK_DIM), lambda hh, i: (l, 0, 0)),
                  pl.BlockSpec((bq, LANES), lambda hh, i: (i, hh)),
                  pl.BlockSpec((s, LANES), lambda hh, i: (0, h + hh)),
                  pl.BlockSpec((s, LANES), lambda hh, i: (0, hh)),
                  pl.BlockSpec((None, 1, DIFF_V_DIM), lambda hh, i: (l, 0, 0))],
        out_specs=pl.BlockSpec((bq, LANES), lambda hh, i: (i, hh)),
        out_shape=jax.ShapeDtypeStruct((s, h * DIFF_V_DIM), BF16),
        scratch_shapes=[pltpu.VMEM((s // tk, bq, tk), F32)],
        compiler_params=_cparams(("parallel", "parallel")),
        name="diff_attention",
    )(diff_lambda, qk, qk, v, diff_subln)


def _mla_attn_kernel(q_ref, kn_ref, kr_ref, v_ref, o_ref, s_ref, *, tk):
    def k_chunk(j):
        return jnp.concatenate([kn_ref[j * tk:(j + 1) * tk, :], kr_ref[j * tk:(j + 1) * tk, :]], axis=1)

    o_ref[...] = _softmax_pv(q_ref[...], k_chunk, v_ref, s_ref, tk).astype(o_ref.dtype)


def _mla_attention(q, kv_up, qk):
    s = q.shape[0]
    bq, tk = _attn_tiles(s)
    h = MLA_HEADS
    kr_blk = 2 * DIFF_HEADS
    return pl.pallas_call(
        functools.partial(_mla_attn_kernel, tk=tk),
        grid=(h, s // bq),
        in_specs=[pl.BlockSpec((bq, 2 * LANES), lambda hh, i: (i, hh)),
                  pl.BlockSpec((s, LANES), lambda hh, i: (0, 2 * hh)),
                  pl.BlockSpec((s, LANES), lambda hh, i: (0, kr_blk)),
                  pl.BlockSpec((s, LANES), lambda hh, i: (0, 2 * hh + 1))],
        out_specs=pl.BlockSpec((bq, LANES), lambda hh, i: (i, hh)),
        out_shape=jax.ShapeDtypeStruct((s, h * MLA_V), BF16),
        scratch_shapes=[pltpu.VMEM((s // tk, bq, tk), F32)],
        compiler_params=_cparams(("parallel", "parallel")),
        name="mla_attention",
    )(q, kv_up, qk, kv_up)


def _gqa_kernel(sink_ref, q_ref, kp_ref, kc_ref, kn_ref, vp_ref, vc_ref, vn_ref, o_ref, *, layer, seq, bq):
    g = pl.program_id(0)
    i = pl.program_id(1)
    kb = jnp.concatenate([kp_ref[...], kc_ref[...], kn_ref[...]], axis=0)
    vb = jnp.concatenate([vp_ref[...], vc_ref[...], vn_ref[...]], axis=0)
    qpos = i * bq + lax.broadcasted_iota(jnp.int32, (bq, 3 * bq), 0)
    kpos = (i - 1) * bq + lax.broadcasted_iota(jnp.int32, (bq, 3 * bq), 1)
    valid = (jnp.abs(kpos - qpos) <= WINDOW) & (kpos >= 0) & (kpos < seq)
    for hh in range(GQA_GROUP):
        q = q_ref[:, hh * GQA_DIM:(hh + 1) * GQA_DIM]
        s = lax.dot_general(q, kb, (((1,), (1,)), ((), ())), preferred_element_type=F32)
        s = jnp.where(valid, s, NEG_INF)
        sink = sink_ref[layer * GQA_HEADS + g * GQA_GROUP + hh] * LOG2E
        m = jnp.maximum(jnp.max(s, axis=1, keepdims=True), sink)
        p = jnp.exp2(s - m)
        denom = jnp.sum(p, axis=1, keepdims=True) + jnp.exp2(sink - m)
        o = jnp.dot(p.astype(BF16), vb, preferred_element_type=F32) / denom
        o_ref[:, hh * GQA_DIM:(hh + 1) * GQA_DIM] = o.astype(o_ref.dtype)


def _window_gqa(gqk, vv, sink_flat, l):
    s = gqk.shape[0]
    bq = _pick(s, 256)
    assert bq >= WINDOW
    nb = s // bq
    qw = GQA_GROUP * GQA_DIM
    kcol = GQA_HEADS
    vcol = DIFF_HEADS

    def prev(g, i):
        return jnp.maximum(i - 1, 0)

    def nxt(g, i):
        return jnp.minimum(i + 1, nb - 1)

    return pl.pallas_call(
        functools.partial(_gqa_kernel, layer=l, seq=s, bq=bq),
        grid=(GQA_KV_HEADS, nb),
        in_specs=[pl.BlockSpec(memory_space=pltpu.SMEM),
                  pl.BlockSpec((bq, qw), lambda g, i: (i, g)),
                  pl.BlockSpec((bq, LANES), lambda g, i: (prev(g, i), kcol + g)),
                  pl.BlockSpec((bq, LANES), lambda g, i: (i, kcol + g)),
                  pl.BlockSpec((bq, LANES), lambda g, i: (nxt(g, i), kcol + g)),
                  pl.BlockSpec((bq, LANES), lambda g, i: (prev(g, i), vcol + g)),
                  pl.BlockSpec((bq, LANES), lambda g, i: (i, vcol + g)),
                  pl.BlockSpec((bq, LANES), lambda g, i: (nxt(g, i), vcol + g))],
        out_specs=pl.BlockSpec((bq, qw), lambda g, i: (i, g)),
        out_shape=jax.ShapeDtypeStruct((s, GQA_HEADS * GQA_DIM), BF16),
        compiler_params=_cparams(("parallel", "parallel")),
        name="window_gqa",
    )(sink_flat, gqk, gqk, gqk, gqk, vv, vv, vv)


def _merge_kernel(u_ref, o0_ref, o1_ref, o2_ref, wg0_ref, wg1_ref, wg2_ref, b0_ref, b1_ref, b2_ref,
                  wb0_ref, wb1_ref, wb2_ref, out_ref):
    u = u_ref[...]
    acc = None
    for o_ref, wg_ref, b_ref, wb_ref in ((o0_ref, wg0_ref, b0_ref, wb0_ref),
                                         (o1_ref, wg1_ref, b1_ref, wb1_ref),
                                         (o2_ref, wg2_ref, b2_ref, wb2_ref)):
        gate = jax.nn.sigmoid(jnp.dot(u, wg_ref[...], preferred_element_type=F32) + b_ref[...])
        term = jnp.dot(o_ref[...], wb_ref[...], preferred_element_type=F32) * gate
        acc = term if acc is None else acc + term
    out_ref[...] = acc.astype(out_ref.dtype)


def _merge(u, o_diff, o_mla, o_gqa, w_gate, b_gate, w_branch, l):
    m, d = u.shape
    bw = o_diff.shape[1]
    bm = _pick(m, 512)
    bn = _pick(d, 512)
    nj = d // bn
    o_spec = pl.BlockSpec((bm, bw), lambda j, i: (i, 0))

    def wg_spec(n):
        return pl.BlockSpec((None, d, bn), lambda j, i: (l, 0, n * nj + j))

    def b_spec(n):
        return pl.BlockSpec((None, 1, bn), lambda j, i: (l, 0, n * nj + j))

    def wb_spec(n):
        return pl.BlockSpec((None, None, bw, bn), lambda j, i: (l, n, 0, j))

    return pl.pallas_call(
        _merge_kernel,
        grid=(nj, m // bm),
        in_specs=[pl.BlockSpec((bm, d), lambda j, i: (i, 0)), o_spec, o_spec, o_spec,
                  wg_spec(0), wg_spec(1), wg_spec(2), b_spec(0), b_spec(1), b_spec(2),
                  wb_spec(0), wb_spec(1), wb_spec(2)],
        out_specs=pl.BlockSpec((bm, bn), lambda j, i: (i, j)),
        out_shape=jax.ShapeDtypeStruct((m, d), BF16),
        compiler_params=_cparams(("parallel", "parallel")),
        name="branch_merge",
    )(u, o_diff, o_mla, o_gqa, w_gate, w_gate, w_gate, b_gate, b_gate, b_gate,
      w_branch, w_branch, w_branch)


def _xattn_kernel(hn_ref, h_ref, wq_ref, k_ref, v_ref, wo_ref, o_ref):
    q = jnp.dot(hn_ref[...], wq_ref[...], preferred_element_type=F32) * (XA_DIM ** -0.5 * LOG2E)
    q = q.astype(BF16)
    heads = []
    for hh in range(XA_HEADS):
        sl = slice(hh * XA_DIM, (hh + 1) * XA_DIM)
        s = lax.dot_general(q[:, sl], k_ref[:, sl], (((1,), (1,)), ((), ())), preferred_element_type=F32)
        p = jnp.exp2(s - jnp.max(s, axis=1, keepdims=True))
        denom = jnp.sum(p, axis=1, keepdims=True)
        heads.append((jnp.dot(p.astype(BF16), v_ref[:, sl], preferred_element_type=F32) / denom).astype(BF16))
    o = jnp.concatenate(heads, axis=1)
    o_ref[...] = h_ref[...] + jnp.dot(o, wo_ref[...], preferred_element_type=F32)


def _cross_attention(hn, h, xa_wq, memkv, xa_wo, l):
    m, d = hn.shape
    w = XA_HEADS * XA_DIM
    ml = memkv.shape[0]
    bm = _pick(m, 512)
    return pl.pallas_call(
        _xattn_kernel,
        grid=(m // bm,),
        in_specs=[pl.BlockSpec((bm, d), lambda i: (i, 0)),
                  pl.BlockSpec((bm, d), lambda i: (i, 0)),
                  pl.BlockSpec((None, d, w), lambda i: (l, 0, 0)),
                  pl.BlockSpec((ml, w), lambda i: (0, 0)),
                  pl.BlockSpec((ml, w), lambda i: (0, 1)),
                  pl.BlockSpec((None, w, d), lambda i: (l, 0, 0))],
        out_specs=pl.BlockSpec((bm, d), lambda i: (i, 0)),
        out_shape=jax.ShapeDtypeStruct((m, d), F32),
        compiler_params=_cparams(("parallel",)),
        name="cross_attention",
    )(hn, h, xa_wq, memkv, memkv, xa_wo)


def _pad_cols(w, width):
    return jnp.pad(w, ((0, 0), (0, width - w.shape[-1])))


def kernel(x, mem, positions, ffn1_norm, ffn1_w_gu, ffn1_w_down, mix_norm, w_in, diff_lambda, diff_subln,
           mla_q_norm, mla_kv_norm, mla_w_uq, mla_w_ukv, gqa_sink, w_branch, w_gate, b_gate, w_o, xa_norm,
           mem_norm, xa_wq, xa_wkv, xa_wo, ffn2_norm, ffn2_w_gu, ffn2_w_down, final_norm):
    b, s, d = x.shape
    assert b == 1
    depth = ffn1_norm.shape[0]
    h = x.reshape(s, d)
    mem2 = mem.reshape(mem.shape[1], d)

    pos_b = jnp.broadcast_to(positions.reshape(s, 1).astype(F32), (s, LANES))
    cos64, sin64 = _rope_tables(pos_b, DIFF_QK_DIM)
    cos128, sin128 = _rope_tables(pos_b, GQA_DIM)

    bf = lambda w: w.astype(BF16)
    ffn1_w_gu_b, ffn1_w_down_b = bf(ffn1_w_gu), bf(ffn1_w_down)
    ffn2_w_gu_b, ffn2_w_down_b = bf(ffn2_w_gu), bf(ffn2_w_down)
    w_branch_b, w_gate_b, w_o_b = bf(w_branch), bf(w_gate), bf(w_o)
    xa_wq_b, xa_wkv_b, xa_wo_b = bf(xa_wq), bf(xa_wkv), bf(xa_wo)
    b_gate3 = b_gate.reshape(depth, 1, -1)
    norm3 = lambda g: g.reshape(depth, 1, -1)
    ffn1_norm3, mix_norm3, xa_norm3, mem_norm3, ffn2_norm3 = map(norm3, (ffn1_norm, mix_norm, xa_norm, mem_norm, ffn2_norm))
    diff_subln3 = diff_subln.reshape(depth, 1, -1)
    sink_flat = gqa_sink.reshape(-1)

    o_dq, o_dk, o_dv = 0, 1024, 2048
    o_cq, o_ckv, o_kr = 3072, 3584, 4096
    o_gq, o_gk, o_gv = 4160, 5184, 5440
    diff_scale = DIFF_QK_DIM ** -0.5 * LOG2E
    mla_scale = (MLA_NOPE + MLA_ROPE) ** -0.5 * LOG2E
    gqa_scale = GQA_DIM ** -0.5 * LOG2E
    cs_a = jnp.concatenate([jnp.full((1, 1024), diff_scale, F32), jnp.ones((1, 1024 + LANES), F32)], axis=1)
    cs_g = jnp.concatenate([jnp.full((1, 1024), gqa_scale, F32), jnp.ones((1, 256), F32)], axis=1)
    cs_q = jnp.full((1, MLA_HEADS * 2 * LANES), mla_scale, F32)

    for l in range(depth):
        lambda_init = 0.8 - 0.6 * math.exp(-0.3 * l)

        a = _ffn_up(_rmsnorm(h, ffn1_norm3, l), ffn1_w_gu_b, l)
        h = _mm_residual(a, ffn1_w_down_b, l, h, 0.5)

        u = _rmsnorm(h, mix_norm3, l)
        wl = w_in[l]
        w_a = bf(jnp.concatenate([wl[:, o_dq:o_dv], _pad_cols(wl[:, o_kr:o_gq], LANES)], axis=1))
        w_v = bf(jnp.concatenate([wl[:, o_dv:o_cq], wl[:, o_gv:]], axis=1))
        w_c = bf(wl[:, o_cq:o_kr])
        w_g = bf(wl[:, o_gq:o_gv])
        qk = _mm_rope(u, w_a, cs_a, cos64, sin64, DIFF_QK_DIM // 2, (True,) * 17)
        vv = _mm_plain(u, w_v)
        gains = jnp.stack([mla_q_norm[l], mla_kv_norm[l]])[:, None, :]
        cn = _mm_rmsnorm(u, w_c, gains)
        gqk = _mm_rope(u, w_g, cs_g, cos128, sin128, GQA_DIM // 2, (True,) * 10)

        o_diff = _diff_attention(qk, vv, diff_lambda, diff_subln3, l, lambda_init)

        wuq = mla_w_uq[l].reshape(MLA_Q_RANK, MLA_HEADS, MLA_NOPE + MLA_ROPE)
        wuq = jnp.pad(wuq, ((0, 0), (0, 0), (0, 2 * LANES - MLA_NOPE - MLA_ROPE)))
        wuq = bf(wuq.reshape(MLA_Q_RANK, MLA_HEADS * 2 * LANES))
        q_mla = _mm_rope(cn[0], wuq, cs_q, cos64, sin64, MLA_ROPE // 2, (False, True) * MLA_HEADS)
        kv_up = _mm_plain(cn[1], bf(mla_w_ukv[l]))
        o_mla = _mla_attention(q_mla, kv_up, qk)

        o_gqa = _window_gqa(gqk, vv, sink_flat, l)

        merged = _merge(u, o_diff, o_mla, o_gqa, w_gate_b, b_gate3, w_branch_b, l)
        h = _mm_residual(merged, w_o_b, l, h, 1.0)

        memkv = _mm_plain(_rmsnorm(mem2, mem_norm3, l), xa_wkv_b, l)
        h = _cross_attention(_rmsnorm(h, xa_norm3, l), h, xa_wq_b, memkv, xa_wo_b, l)

        a = _ffn_up(_rmsnorm(h, ffn2_norm3, l), ffn2_w_gu_b, l)
        h = _mm_residual(a, ffn2_w_down_b, l, h, 0.5)

    out = _rmsnorm(h, final_norm.reshape(1, d), None, out_dtype=F32)
    return out.reshape(b, s, d)
```

```python
import functools
import math

import jax
import jax.numpy as jnp
from jax import lax
from jax.experimental import pallas as pl
from jax.experimental.pallas import tpu as pltpu

F32 = jnp.float32
BF16 = jnp.bfloat16

DIFF_HEADS = 8
DIFF_QK_DIM = 64
DIFF_V_DIM = 128
MLA_HEADS = 8
MLA_Q_RANK = 512
MLA_KV_RANK = 512
MLA_NOPE = 128
MLA_ROPE = 64
MLA_V = 128
GQA_HEADS = 8
GQA_KV_HEADS = 2
GQA_GROUP = GQA_HEADS // GQA_KV_HEADS
GQA_DIM = 128
WINDOW = 128
N_BRANCH = 3
BRANCH_WIDTH = 1024
XA_HEADS = 4
XA_DIM = 128
D_FF = 5632
ROPE_THETA = 10000.0
EPS = 1e-6
DIFF_SUBLN_EPS = 1e-5
NEG_INF = -1e30
LOG2E = math.log2(math.e)

LANES = 128
V7X_VMEM_LIMIT_BYTES = 56 * 1024 * 1024


def _cparams(semantics):
    return pltpu.CompilerParams(dimension_semantics=semantics, vmem_limit_bytes=V7X_VMEM_LIMIT_BYTES)


def _pick(dim, pref):
    if dim <= pref:
        return dim
    b = pref
    while dim % b:
        b //= 2
    return b


def _rope_table_kernel(pos_ref, inv_ref, sign_ref, cos_ref, sin_ref):
    ang = pos_ref[...] * inv_ref[...]
    cos_ref[...] = jnp.cos(ang)
    sin_ref[...] = jnp.sin(ang) * sign_ref[...]


def _rope_tables(pos_b, dim):
    s = pos_b.shape[0]
    half = dim // 2
    inv = ROPE_THETA ** (-jnp.arange(0, dim, 2, dtype=F32) / dim)
    inv_row = jnp.tile(inv, LANES // half)[None, :]
    sign_row = jnp.tile(jnp.concatenate([-jnp.ones((half,), F32), jnp.ones((half,), F32)]), LANES // dim)[None, :]
    bm = _pick(s, 1024)
    return pl.pallas_call(
        _rope_table_kernel,
        grid=(s // bm,),
        in_specs=[pl.BlockSpec((bm, LANES), lambda i: (i, 0)),
                  pl.BlockSpec((1, LANES), lambda i: (0, 0)),
                  pl.BlockSpec((1, LANES), lambda i: (0, 0))],
        out_specs=[pl.BlockSpec((bm, LANES), lambda i: (i, 0))] * 2,
        out_shape=[jax.ShapeDtypeStruct((s, LANES), F32)] * 2,
        compiler_params=_cparams(("parallel",)),
        name="rope_tables",
    )(pos_b, inv_row, sign_row)


def _rope_block(x, cos, sin_signed, half):
    if 2 * half == LANES:
        partner = pltpu.roll(x, half, axis=1)
    else:
        lane = lax.broadcasted_iota(jnp.int32, x.shape, 1)
        first = (lane % (2 * half)) < half
        partner = jnp.where(first, pltpu.roll(x, LANES - half, axis=1), pltpu.roll(x, half, axis=1))
    return x * cos + partner * sin_signed


def _rmsnorm_kernel(x_ref, g_ref, o_ref, *, eps):
    x = x_ref[...]
    y = x * lax.rsqrt(jnp.mean(x * x, axis=-1, keepdims=True) + eps)
    o_ref[...] = (y * g_ref[...]).astype(o_ref.dtype)


def _rmsnorm(x, g_stack, l, out_dtype=BF16):
    m, d = x.shape
    bm = _pick(m, 512)
    if l is None:
        g_spec = pl.BlockSpec((1, d), lambda i: (0, 0))
    else:
        g_spec = pl.BlockSpec((None, 1, d), lambda i: (l, 0, 0))
    return pl.pallas_call(
        functools.partial(_rmsnorm_kernel, eps=EPS),
        grid=(m // bm,),
        in_specs=[pl.BlockSpec((bm, d), lambda i: (i, 0)), g_spec],
        out_specs=pl.BlockSpec((bm, d), lambda i: (i, 0)),
        out_shape=jax.ShapeDtypeStruct((m, d), out_dtype),
        compiler_params=_cparams(("parallel",)),
        name="rmsnorm",
    )(x, g_stack)


def _cast_weights_once(pairs):
    @pl.when(pl.program_id(1) == 0)
    def _():
        for w_ref, w_s in pairs:
            w_s[...] = w_ref[...].astype(BF16)


def _norm_rows(x, g, eps=EPS):
    return (x * lax.rsqrt(jnp.mean(x * x, axis=-1, keepdims=True) + eps) * g).astype(BF16)


def _mm_swiglu_kernel(x_ref, g_ref, wg_ref, wu_ref, o_ref, wg_s, wu_s):
    _cast_weights_once(((wg_ref, wg_s), (wu_ref, wu_s)))
    x = _norm_rows(x_ref[...], g_ref[...])
    g = jnp.dot(x, wg_s[...], preferred_element_type=F32)
    u = jnp.dot(x, wu_s[...], preferred_element_type=F32)
    o_ref[...] = (g * jax.nn.sigmoid(g) * u).astype(o_ref.dtype)


def _ffn_up(h, gain, w_gu, l):
    m, d = h.shape
    n = w_gu.shape[-1] // 2
    bm = _pick(m, 1024)
    bn = _pick(n, 512)
    nj = n // bn
    return pl.pallas_call(
        _mm_swiglu_kernel,
        grid=(nj, m // bm),
        in_specs=[pl.BlockSpec((bm, d), lambda j, i: (i, 0)),
                  pl.BlockSpec((None, 1, d), lambda j, i: (l, 0, 0)),
                  pl.BlockSpec((None, d, bn), lambda j, i: (l, 0, j)),
                  pl.BlockSpec((None, d, bn), lambda j, i: (l, 0, j + nj))],
        out_specs=pl.BlockSpec((bm, bn), lambda j, i: (i, j)),
        out_shape=jax.ShapeDtypeStruct((m, n), BF16),
        scratch_shapes=[pltpu.VMEM((d, bn), BF16), pltpu.VMEM((d, bn), BF16)],
        compiler_params=_cparams(("arbitrary", "arbitrary")),
        name="ffn_up",
    )(h, gain, w_gu, w_gu)


def _mm_residual_kernel(a_ref, w_ref, r_ref, o_ref, w_s, *, scale):
    _cast_weights_once(((w_ref, w_s),))
    acc = jnp.dot(a_ref[...], w_s[...], preferred_element_type=F32)
    o_ref[...] = r_ref[...] + scale * acc


def _mm_residual(a, w, l, res, scale):
    m, k = a.shape
    n = w.shape[-1]
    bm = _pick(m, 512)
    bn = _pick(n, 512)
    return pl.pallas_call(
        functools.partial(_mm_residual_kernel, scale=scale),
        grid=(n // bn, m // bm),
        in_specs=[pl.BlockSpec((bm, k), lambda j, i: (i, 0)),
                  pl.BlockSpec((None, k, bn), lambda j, i: (l, 0, j)),
                  pl.BlockSpec((bm, bn), lambda j, i: (i, j))],
        out_specs=pl.BlockSpec((bm, bn), lambda j, i: (i, j)),
        out_shape=jax.ShapeDtypeStruct((m, n), F32),
        scratch_shapes=[pltpu.VMEM((k, bn), BF16)],
        compiler_params=_cparams(("arbitrary", "arbitrary")),
        name="mm_residual",
    )(a, w, res)


def _mm_plain_kernel(x_ref, w_ref, o_ref):
    o_ref[...] = jnp.dot(x_ref[...], w_ref[...], preferred_element_type=F32).astype(o_ref.dtype)


def _mm_plain(x, w, l=None):
    m, k = x.shape
    n = w.shape[-1]
    bm = _pick(m, 512)
    if l is None:
        w_spec = pl.BlockSpec((k, n), lambda i: (0, 0))
    else:
        w_spec = pl.BlockSpec((None, k, n), lambda i: (l, 0, 0))
    return pl.pallas_call(
        _mm_plain_kernel,
        grid=(m // bm,),
        in_specs=[pl.BlockSpec((bm, k), lambda i: (i, 0)), w_spec],
        out_specs=pl.BlockSpec((bm, n), lambda i: (i, 0)),
        out_shape=jax.ShapeDtypeStruct((m, n), BF16),
        compiler_params=_cparams(("parallel",)),
        name="mm_plain",
    )(x, w)


def _mm_rope_kernel(x_ref, w_ref, cs_ref, cos_ref, sin_ref, o_ref, *, half, pattern):
    acc = jnp.dot(x_ref[...], w_ref[...], preferred_element_type=F32) * cs_ref[...]
    cos = cos_ref[...]
    sin = sin_ref[...]
    for b, roped in enumerate(pattern):
        blk = acc[:, b * LANES:(b + 1) * LANES]
        if roped:
            blk = _rope_block(blk, cos, sin, half)
        o_ref[:, b * LANES:(b + 1) * LANES] = blk.astype(o_ref.dtype)


def _mm_rope(x, w, colscale, cos, sin, half, pattern):
    m, k = x.shape
    n = w.shape[-1]
    assert n == LANES * len(pattern)
    bm = _pick(m, 512)
    return pl.pallas_call(
        functools.partial(_mm_rope_kernel, half=half, pattern=pattern),
        grid=(m // bm,),
        in_specs=[pl.BlockSpec((bm, k), lambda i: (i, 0)),
                  pl.BlockSpec((k, n), lambda i: (0, 0)),
                  pl.BlockSpec((1, n), lambda i: (0, 0)),
                  pl.BlockSpec((bm, LANES), lambda i: (i, 0)),
                  pl.BlockSpec((bm, LANES), lambda i: (i, 0))],
        out_specs=pl.BlockSpec((bm, n), lambda i: (i, 0)),
        out_shape=jax.ShapeDtypeStruct((m, n), BF16),
        compiler_params=_cparams(("parallel",)),
        name="mm_rope",
    )(x, w, colscale, cos, sin)


def _mm_rmsnorm_kernel(x_ref, w_ref, g_ref, o_ref):
    acc = jnp.dot(x_ref[...], w_ref[...], preferred_element_type=F32)
    y = acc * lax.rsqrt(jnp.mean(acc * acc, axis=-1, keepdims=True) + EPS)
    o_ref[...] = (y * g_ref[...]).astype(o_ref.dtype)


def _mm_rmsnorm(x, w, g):
    m, k = x.shape
    t, _, r = g.shape
    bm = _pick(m, 512)
    return pl.pallas_call(
        _mm_rmsnorm_kernel,
        grid=(t, m // bm),
        in_specs=[pl.BlockSpec((bm, k), lambda j, i: (i, 0)),
                  pl.BlockSpec((k, r), lambda j, i: (0, j)),
                  pl.BlockSpec((None, 1, r), lambda j, i: (j, 0, 0))],
        out_specs=pl.BlockSpec((None, bm, r), lambda j, i: (j, i, 0)),
        out_shape=jax.ShapeDtypeStruct((t, m, r), BF16),
        compiler_params=_cparams(("parallel", "parallel")),
        name="mm_rmsnorm",
    )(x, w, g)


def _logits(q, k, mlane):
    s = lax.dot_general(q, k, (((1,), (1,)), ((), ())), preferred_element_type=F32)
    for b in range(s.shape[1] // LANES):
        blk = s[:, b * LANES:(b + 1) * LANES]
        mlane = blk if mlane is None else jnp.maximum(mlane, blk)
    return s, mlane


def _row_max(mlane):
    return jnp.broadcast_to(jnp.max(mlane, axis=1, keepdims=True), mlane.shape)


def _weigh(s, m, v1, acc):
    ps = [jnp.exp2(s[:, b * LANES:(b + 1) * LANES] - m).astype(BF16) for b in range(s.shape[1] // LANES)]
    pv = jnp.dot(jnp.concatenate(ps, axis=1), v1, preferred_element_type=F32)
    return pv if acc is None else acc + pv


def _softmax_pv(qs, k_chunk, v1_chunk, nchunks):
    n = len(qs)
    logits = [[None] * nchunks for _ in range(n)]
    acc = [None] * n
    mlane = None
    for j in range(nchunks):
        logits[0][j], mlane = _logits(qs[0], k_chunk(j), mlane)
    m = _row_max(mlane)
    for p in range(1, n):
        mlane = None
        for j in range(nchunks):
            acc[p - 1] = _weigh(logits[p - 1][j], m, v1_chunk(j), acc[p - 1])
            logits[p][j], mlane = _logits(qs[p], k_chunk(j), mlane)
        m = _row_max(mlane)
    for j in range(nchunks):
        acc[n - 1] = _weigh(logits[n - 1][j], m, v1_chunk(j), acc[n - 1])
    return [a[:, :a.shape[1] - LANES] / a[:, a.shape[1] - LANES:] for a in acc]


def _attn_tiles(s):
    bq = _pick(s, 256)
    return bq, (2 if s % (2 * bq) == 0 else 1), _pick(s, 1024)


def _diff_attn_kernel(lp_ref, q_ref, k_ref, v_ref, g_ref, o_ref, *, lambda_init, tk, nsub):
    bq = q_ref.shape[0] // nsub
    ones = jnp.ones((tk, LANES), BF16)
    k_chunk = lambda j: k_ref[j * tk:(j + 1) * tk, :]
    v1_chunk = lambda j: jnp.concatenate([v_ref[j * tk:(j + 1) * tk, :], ones], axis=1)
    qs = []
    for r in range(nsub):
        q = q_ref[r * bq:(r + 1) * bq, :]
        lane = lax.broadcasted_iota(jnp.int32, q.shape, 1)
        zero = jnp.zeros_like(q)
        qs.append(jnp.where(lane < DIFF_QK_DIM, q, zero))
        qs.append(jnp.where(lane >= DIFF_QK_DIM, q, zero))
    outs = _softmax_pv(qs, k_chunk, v1_chunk, k_ref.shape[0] // tk)
    lp = lp_ref[...]
    lam = (jnp.exp(jnp.sum(lp[0:1] * lp[1:2], axis=1, keepdims=True))
           - jnp.exp(jnp.sum(lp[2:3] * lp[3:4], axis=1, keepdims=True)) + lambda_init)
    for r in range(nsub):
        o = outs[2 * r] - lam * outs[2 * r + 1]
        y = o * lax.rsqrt(jnp.mean(o * o, axis=-1, keepdims=True) + DIFF_SUBLN_EPS)
        o_ref[r * bq:(r + 1) * bq, :] = (y * g_ref[...] * (1.0 - lambda_init)).astype(o_ref.dtype)


def _diff_attention(qk, v, diff_lambda, diff_subln, l, lambda_init):
    s = qk.shape[0]
    bq, nsub, tk = _attn_tiles(s)
    rows = bq * nsub
    h = DIFF_HEADS
    return pl.pallas_call(
        functools.partial(_diff_attn_kernel, lambda_init=lambda_init, tk=tk, nsub=nsub),
        grid=(h, s // rows),
        in_specs=[pl.BlockSpec((None, 4, DIFF_QK_DIM), lambda hh, i: (l, 0, 0)),
                  pl.BlockSpec((rows, LANES), lambda hh, i: (i, hh)),
                  pl.BlockSpec((s, LANES), lambda hh, i: (0, h + hh)),
                  pl.BlockSpec((s, LANES), lambda hh, i: (0, hh)),
                  pl.BlockSpec((None, 1, DIFF_V_DIM), lambda hh, i: (l, 0, 0))],
        out_specs=pl.BlockSpec((rows, LANES), lambda hh, i: (i, hh)),
        out_shape=jax.ShapeDtypeStruct((s, h * DIFF_V_DIM), BF16),
        compiler_params=_cparams(("parallel", "parallel")),
        name="diff_attention",
    )(diff_lambda, qk, qk, v, diff_subln)


def _mla_attn_kernel(q_ref, kn_ref, kr_ref, v_ref, o_ref, *, tk, nsub):
    bq = q_ref.shape[0] // nsub
    ones = jnp.ones((tk, LANES), BF16)
    k_chunk = lambda j: jnp.concatenate([kn_ref[j * tk:(j + 1) * tk, :], kr_ref[j * tk:(j + 1) * tk, :]], axis=1)
    v1_chunk = lambda j: jnp.concatenate([v_ref[j * tk:(j + 1) * tk, :], ones], axis=1)
    qs = [q_ref[r * bq:(r + 1) * bq, :] for r in range(nsub)]
    outs = _softmax_pv(qs, k_chunk, v1_chunk, kn_ref.shape[0] // tk)
    for r in range(nsub):
        o_ref[r * bq:(r + 1) * bq, :] = outs[r].astype(o_ref.dtype)


def _mla_attention(q, kv_up, qk):
    s = q.shape[0]
    bq, nsub, tk = _attn_tiles(s)
    rows = bq * nsub
    h = MLA_HEADS
    kr_blk = 2 * DIFF_HEADS
    return pl.pallas_call(
        functools.partial(_mla_attn_kernel, tk=tk, nsub=nsub),
        grid=(h, s // rows),
        in_specs=[pl.BlockSpec((rows, 2 * LANES), lambda hh, i: (i, hh)),
                  pl.BlockSpec((s, LANES), lambda hh, i: (0, 2 * hh)),
                  pl.BlockSpec((s, LANES), lambda hh, i: (0, kr_blk)),
                  pl.BlockSpec((s, LANES), lambda hh, i: (0, 2 * hh + 1))],
        out_specs=pl.BlockSpec((rows, LANES), lambda hh, i: (i, hh)),
        out_shape=jax.ShapeDtypeStruct((s, h * MLA_V), BF16),
        compiler_params=_cparams(("parallel", "parallel")),
        name="mla_attention",
    )(q, kv_up, qk, kv_up)


def _gqa_kernel(sink_ref, q_ref, kp_ref, kc_ref, kn_ref, vp_ref, vc_ref, vn_ref, o_ref, *, layer, seq, bq):
    g = pl.program_id(0)
    i = pl.program_id(1)
    nk = bq + 2 * WINDOW
    kb = jnp.concatenate([kp_ref[...], kc_ref[...], kn_ref[...]], axis=0)
    vb = jnp.concatenate([vp_ref[...], vc_ref[...], vn_ref[...]], axis=0)
    v1 = jnp.concatenate([vb, jnp.ones_like(vb)], axis=1)
    qpos = i * bq + lax.broadcasted_iota(jnp.int32, (bq, nk), 0)
    kpos = i * bq - WINDOW + lax.broadcasted_iota(jnp.int32, (bq, nk), 1)
    valid = (jnp.abs(kpos - qpos) <= WINDOW) & (kpos >= 0) & (kpos < seq)
    bias = jnp.where(valid, 0.0, NEG_INF).astype(F32)
    for hh in range(GQA_GROUP):
        q = q_ref[:, hh * GQA_DIM:(hh + 1) * GQA_DIM]
        s = lax.dot_general(q, kb, (((1,), (1,)), ((), ())), preferred_element_type=F32) + bias
        sink = sink_ref[layer * GQA_HEADS + g * GQA_GROUP + hh] * LOG2E
        m = jnp.maximum(jnp.max(s, axis=1, keepdims=True), sink)
        pv = jnp.dot(jnp.exp2(s - m).astype(BF16), v1, preferred_element_type=F32)
        denom = pv[:, GQA_DIM:] + jnp.exp2(sink - m)
        o_ref[:, hh * GQA_DIM:(hh + 1) * GQA_DIM] = (pv[:, :GQA_DIM] / denom).astype(o_ref.dtype)


def _window_gqa(gqk, vv, sink_flat, l):
    s = gqk.shape[0]
    bq = _pick(s, 256)
    assert bq % WINDOW == 0 and WINDOW == LANES
    nb = s // bq
    r = bq // WINDOW
    nh = s // WINDOW
    qw = GQA_GROUP * GQA_DIM
    kcol = GQA_HEADS
    vcol = DIFF_HEADS

    def prev(i):
        return jnp.maximum(i * r - 1, 0)

    def nxt(i):
        return jnp.minimum((i + 1) * r, nh - 1)

    return pl.pallas_call(
        functools.partial(_gqa_kernel, layer=l, seq=s, bq=bq),
        grid=(GQA_KV_HEADS, nb),
        in_specs=[pl.BlockSpec(memory_space=pltpu.SMEM),
                  pl.BlockSpec((bq, qw), lambda g, i: (i, g)),
                  pl.BlockSpec((WINDOW, LANES), lambda g, i: (prev(i), kcol + g)),
                  pl.BlockSpec((bq, LANES), lambda g, i: (i, kcol + g)),
                  pl.BlockSpec((WINDOW, LANES), lambda g, i: (nxt(i), kcol + g)),
                  pl.BlockSpec((WINDOW, LANES), lambda g, i: (prev(i), vcol + g)),
                  pl.BlockSpec((bq, LANES), lambda g, i: (i, vcol + g)),
                  pl.BlockSpec((WINDOW, LANES), lambda g, i: (nxt(i), vcol + g))],
        out_specs=pl.BlockSpec((bq, qw), lambda g, i: (i, g)),
        out_shape=jax.ShapeDtypeStruct((s, GQA_HEADS * GQA_DIM), BF16),
        compiler_params=_cparams(("parallel", "parallel")),
        name="window_gqa",
    )(sink_flat, gqk, gqk, gqk, gqk, vv, vv, vv)


def _merge_kernel(u_ref, o0_ref, o1_ref, o2_ref, wg0_ref, wg1_ref, wg2_ref, b0_ref, b1_ref, b2_ref,
                  wb0_ref, wb1_ref, wb2_ref, out_ref, wg0_s, wg1_s, wg2_s, wb0_s, wb1_s, wb2_s):
    _cast_weights_once(((wg0_ref, wg0_s), (wg1_ref, wg1_s), (wg2_ref, wg2_s),
                        (wb0_ref, wb0_s), (wb1_ref, wb1_s), (wb2_ref, wb2_s)))
    u = u_ref[...]
    acc = None
    for o_ref, wg_s, b_ref, wb_s in ((o0_ref, wg0_s, b0_ref, wb0_s),
                                     (o1_ref, wg1_s, b1_ref, wb1_s),
                                     (o2_ref, wg2_s, b2_ref, wb2_s)):
        gate = jax.nn.sigmoid(jnp.dot(u, wg_s[...], preferred_element_type=F32) + b_ref[...])
        term = jnp.dot(o_ref[...], wb_s[...], preferred_element_type=F32) * gate
        acc = term if acc is None else acc + term
    out_ref[...] = acc.astype(out_ref.dtype)


def _merge(u, o_diff, o_mla, o_gqa, w_gate, b_gate, w_branch, l):
    m, d = u.shape
    bw = o_diff.shape[1]
    bm = _pick(m, 1024)
    bn = _pick(d, 256)
    nj = d // bn
    o_spec = pl.BlockSpec((bm, bw), lambda j, i: (i, 0))

    def wg_spec(n):
        return pl.BlockSpec((None, d, bn), lambda j, i: (l, 0, n * nj + j))

    def b_spec(n):
        return pl.BlockSpec((None, 1, bn), lambda j, i: (l, 0, n * nj + j))

    def wb_spec(n):
        return pl.BlockSpec((None, None, bw, bn), lambda j, i: (l, n, 0, j))

    return pl.pallas_call(
        _merge_kernel,
        grid=(nj, m // bm),
        in_specs=[pl.BlockSpec((bm, d), lambda j, i: (i, 0)), o_spec, o_spec, o_spec,
                  wg_spec(0), wg_spec(1), wg_spec(2), b_spec(0), b_spec(1), b_spec(2),
                  wb_spec(0), wb_spec(1), wb_spec(2)],
        out_specs=pl.BlockSpec((bm, bn), lambda j, i: (i, j)),
        out_shape=jax.ShapeDtypeStruct((m, d), BF16),
        scratch_shapes=[pltpu.VMEM((d, bn), BF16)] * 3 + [pltpu.VMEM((bw, bn), BF16)] * 3,
        compiler_params=_cparams(("arbitrary", "arbitrary")),
        name="branch_merge",
    )(u, o_diff, o_mla, o_gqa, w_gate, w_gate, w_gate, b_gate, b_gate, b_gate,
      w_branch, w_branch, w_branch)


def _xattn_kernel(h_ref, g_ref, wq_ref, k_ref, v_ref, wo_ref, o_ref, wq_s, wo_s):
    @pl.when(pl.program_id(0) == 0)
    def _():
        wq_s[...] = wq_ref[...].astype(BF16)
        wo_s[...] = wo_ref[...].astype(BF16)

    hn = _norm_rows(h_ref[...], g_ref[...])
    q = jnp.dot(hn, wq_s[...], preferred_element_type=F32) * (XA_DIM ** -0.5 * LOG2E)
    q = q.astype(BF16)
    heads = []
    for hh in range(XA_HEADS):
        sl = slice(hh * XA_DIM, (hh + 1) * XA_DIM)
        s = lax.dot_general(q[:, sl], k_ref[:, sl], (((1,), (1,)), ((), ())), preferred_element_type=F32)
        p = jnp.exp2(s - jnp.max(s, axis=1, keepdims=True))
        denom = jnp.sum(p, axis=1, keepdims=True)
        heads.append((jnp.dot(p.astype(BF16), v_ref[:, sl], preferred_element_type=F32) / denom).astype(BF16))
    o = jnp.concatenate(heads, axis=1)
    o_ref[...] = h_ref[...] + jnp.dot(o, wo_s[...], preferred_element_type=F32)


def _cross_attention(h, gain, xa_wq, memkv, xa_wo, l):
    m, d = h.shape
    w = XA_HEADS * XA_DIM
    ml = memkv.shape[0]
    bm = _pick(m, 512)
    return pl.pallas_call(
        _xattn_kernel,
        grid=(m // bm,),
        in_specs=[pl.BlockSpec((bm, d), lambda i: (i, 0)),
                  pl.BlockSpec((None, 1, d), lambda i: (l, 0, 0)),
                  pl.BlockSpec((None, d, w), lambda i: (l, 0, 0)),
                  pl.BlockSpec((ml, w), lambda i: (0, 0)),
                  pl.BlockSpec((ml, w), lambda i: (0, 1)),
                  pl.BlockSpec((None, w, d), lambda i: (l, 0, 0))],
        out_specs=pl.BlockSpec((bm, d), lambda i: (i, 0)),
        out_shape=jax.ShapeDtypeStruct((m, d), F32),
        scratch_shapes=[pltpu.VMEM((d, w), BF16), pltpu.VMEM((w, d), BF16)],
        compiler_params=_cparams(("arbitrary",)),
        name="cross_attention",
    )(h, gain, xa_wq, memkv, memkv, xa_wo)


def _pad_cols(w, width):
    return jnp.pad(w, ((0, 0), (0, width - w.shape[-1])))


def kernel(x, mem, positions, ffn1_norm, ffn1_w_gu, ffn1_w_down, mix_norm, w_in, diff_lambda, diff_subln,
           mla_q_norm, mla_kv_norm, mla_w_uq, mla_w_ukv, gqa_sink, w_branch, w_gate, b_gate, w_o, xa_norm,
           mem_norm, xa_wq, xa_wkv, xa_wo, ffn2_norm, ffn2_w_gu, ffn2_w_down, final_norm):
    b, s, d = x.shape
    assert b == 1
    depth = ffn1_norm.shape[0]
    h = x.reshape(s, d)
    mem2 = mem.reshape(mem.shape[1], d)

    pos_b = jnp.broadcast_to(positions.reshape(s, 1).astype(F32), (s, LANES))
    cos64, sin64 = _rope_tables(pos_b, DIFF_QK_DIM)
    cos128, sin128 = _rope_tables(pos_b, GQA_DIM)

    bf = lambda w: w.astype(BF16)
    xa_wkv_b = bf(xa_wkv)
    b_gate3 = b_gate.reshape(depth, 1, -1)
    norm3 = lambda g: g.reshape(depth, 1, -1)
    ffn1_norm3, mix_norm3, xa_norm3, mem_norm3, ffn2_norm3 = map(norm3, (ffn1_norm, mix_norm, xa_norm, mem_norm, ffn2_norm))
    diff_subln3 = diff_subln.reshape(depth, 1, -1)
    sink_flat = gqa_sink.reshape(-1)

    o_dq, o_dk, o_dv = 0, 1024, 2048
    o_cq, o_ckv, o_kr = 3072, 3584, 4096
    o_gq, o_gk, o_gv = 4160, 5184, 5440
    diff_scale = DIFF_QK_DIM ** -0.5 * LOG2E
    mla_scale = (MLA_NOPE + MLA_ROPE) ** -0.5 * LOG2E
    gqa_scale = GQA_DIM ** -0.5 * LOG2E
    cs_a = jnp.concatenate([jnp.full((1, 1024), diff_scale, F32), jnp.ones((1, 1024 + LANES), F32)], axis=1)
    cs_g = jnp.concatenate([jnp.full((1, 1024), gqa_scale, F32), jnp.ones((1, 256), F32)], axis=1)
    cs_q = jnp.full((1, MLA_HEADS * 2 * LANES), mla_scale, F32)

    for l in range(depth):
        lambda_init = 0.8 - 0.6 * math.exp(-0.3 * l)

        a = _ffn_up(h, ffn1_norm3, ffn1_w_gu, l)
        h = _mm_residual(a, ffn1_w_down, l, h, 0.5)

        u = _rmsnorm(h, mix_norm3, l)
        wl = w_in[l]
        w_a = bf(jnp.concatenate([wl[:, o_dq:o_dv], _pad_cols(wl[:, o_kr:o_gq], LANES)], axis=1))
        w_v = bf(jnp.concatenate([wl[:, o_dv:o_cq], wl[:, o_gv:]], axis=1))
        w_c = bf(wl[:, o_cq:o_kr])
        w_g = bf(wl[:, o_gq:o_gv])
        qk = _mm_rope(u, w_a, cs_a, cos64, sin64, DIFF_QK_DIM // 2, (True,) * 17)
        vv = _mm_plain(u, w_v)
        gains = jnp.stack([mla_q_norm[l], mla_kv_norm[l]])[:, None, :]
        cn = _mm_rmsnorm(u, w_c, gains)
        gqk = _mm_rope(u, w_g, cs_g, cos128, sin128, GQA_DIM // 2, (True,) * 10)

        o_diff = _diff_attention(qk, vv, diff_lambda, diff_subln3, l, lambda_init)

        wuq = mla_w_uq[l].reshape(MLA_Q_RANK, MLA_HEADS, MLA_NOPE + MLA_ROPE)
        wuq = jnp.pad(wuq, ((0, 0), (0, 0), (0, 2 * LANES - MLA_NOPE - MLA_ROPE)))
        wuq = bf(wuq.reshape(MLA_Q_RANK, MLA_HEADS * 2 * LANES))
        q_mla = _mm_rope(cn[0], wuq, cs_q, cos64, sin64, MLA_ROPE // 2, (False, True) * MLA_HEADS)
        kv_up = _mm_plain(cn[1], bf(mla_w_ukv[l]))
        o_mla = _mla_attention(q_mla, kv_up, qk)

        o_gqa = _window_gqa(gqk, vv, sink_flat, l)

        merged = _merge(u, o_diff, o_mla, o_gqa, w_gate, b_gate3, w_branch, l)
        h = _mm_residual(merged, w_o, l, h, 1.0)

        memkv = _mm_plain(_rmsnorm(mem2, mem_norm3, l), xa_wkv_b, l)
        h = _cross_attention(h, xa_norm3, xa_wq, memkv, xa_wo, l)

        a = _ffn_up(h, ffn2_norm3, ffn2_w_gu, l)
        h = _mm_residual(a, ffn2_w_down, l, h, 0.5)

    out = _rmsnorm(h, final_norm.reshape(1, d), None, out_dtype=F32)
    return out.reshape(b, s, d)
```

```python
import functools
import math

import jax
import jax.numpy as jnp
from jax import lax
from jax.experimental import pallas as pl
from jax.experimental.pallas import tpu as pltpu

F32 = jnp.float32
BF16 = jnp.bfloat16

DIFF_HEADS = 8
DIFF_QK_DIM = 64
DIFF_V_DIM = 128
MLA_HEADS = 8
MLA_Q_RANK = 512
MLA_KV_RANK = 512
MLA_NOPE = 128
MLA_ROPE = 64
MLA_V = 128
GQA_HEADS = 8
GQA_KV_HEADS = 2
GQA_GROUP = GQA_HEADS // GQA_KV_HEADS
GQA_DIM = 128
WINDOW = 128
N_BRANCH = 3
BRANCH_WIDTH = 1024
XA_HEADS = 4
XA_DIM = 128
D_FF = 5632
ROPE_THETA = 10000.0
EPS = 1e-6
DIFF_SUBLN_EPS = 1e-5
NEG_INF = -1e30
LOG2E = math.log2(math.e)

LANES = 128
V7X_VMEM_LIMIT_BYTES = 56 * 1024 * 1024


def _cparams(semantics):
    return pltpu.CompilerParams(dimension_semantics=semantics, vmem_limit_bytes=V7X_VMEM_LIMIT_BYTES)


def _pick(dim, pref):
    if dim <= pref:
        return dim
    b = pref
    while dim % b:
        b //= 2
    return b


def _rope_table_kernel(pos_ref, inv_ref, sign_ref, cos_ref, sin_ref):
    ang = pos_ref[...] * inv_ref[...]
    cos_ref[...] = jnp.cos(ang)
    sin_ref[...] = jnp.sin(ang) * sign_ref[...]


def _rope_tables(pos_b, dim):
    s = pos_b.shape[0]
    half = dim // 2
    inv = ROPE_THETA ** (-jnp.arange(0, dim, 2, dtype=F32) / dim)
    inv_row = jnp.tile(inv, LANES // half)[None, :]
    sign_row = jnp.tile(jnp.concatenate([-jnp.ones((half,), F32), jnp.ones((half,), F32)]), LANES // dim)[None, :]
    bm = _pick(s, 1024)
    return pl.pallas_call(
        _rope_table_kernel,
        grid=(s // bm,),
        in_specs=[pl.BlockSpec((bm, LANES), lambda i: (i, 0)),
                  pl.BlockSpec((1, LANES), lambda i: (0, 0)),
                  pl.BlockSpec((1, LANES), lambda i: (0, 0))],
        out_specs=[pl.BlockSpec((bm, LANES), lambda i: (i, 0))] * 2,
        out_shape=[jax.ShapeDtypeStruct((s, LANES), F32)] * 2,
        compiler_params=_cparams(("parallel",)),
        name="rope_tables",
    )(pos_b, inv_row, sign_row)


def _rope_block(x, cos, sin_signed, half):
    if 2 * half == LANES:
        partner = pltpu.roll(x, half, axis=1)
    else:
        lane = lax.broadcasted_iota(jnp.int32, x.shape, 1)
        first = (lane % (2 * half)) < half
        partner = jnp.where(first, pltpu.roll(x, LANES - half, axis=1), pltpu.roll(x, half, axis=1))
    return x * cos + partner * sin_signed


def _rmsnorm_kernel(x_ref, g_ref, o_ref, *, eps):
    x = x_ref[...]
    y = x * lax.rsqrt(jnp.mean(x * x, axis=-1, keepdims=True) + eps)
    o_ref[...] = (y * g_ref[...]).astype(o_ref.dtype)


def _rmsnorm(x, g_stack, l, out_dtype=BF16):
    m, d = x.shape
    bm = _pick(m, 512)
    if l is None:
        g_spec = pl.BlockSpec((1, d), lambda i: (0, 0))
    else:
        g_spec = pl.BlockSpec((None, 1, d), lambda i: (l, 0, 0))
    return pl.pallas_call(
        functools.partial(_rmsnorm_kernel, eps=EPS),
        grid=(m // bm,),
        in_specs=[pl.BlockSpec((bm, d), lambda i: (i, 0)), g_spec],
        out_specs=pl.BlockSpec((bm, d), lambda i: (i, 0)),
        out_shape=jax.ShapeDtypeStruct((m, d), out_dtype),
        compiler_params=_cparams(("parallel",)),
        name="rmsnorm",
    )(x, g_stack)


def _cast_weights_once(pairs):
    @pl.when(pl.program_id(1) == 0)
    def _():
        for w_ref, w_s in pairs:
            w_s[...] = w_ref[...].astype(BF16)


def _norm_rows(x, g, eps=EPS):
    return (x * lax.rsqrt(jnp.mean(x * x, axis=-1, keepdims=True) + eps) * g).astype(BF16)


def _mm_swiglu_kernel(x_ref, gc_ref, wg_ref, wu_ref, o_ref, wg_s, wu_s):
    @pl.when(pl.program_id(1) == 0)
    def _():
        gc = gc_ref[...]
        wg_s[...] = (wg_ref[...] * gc).astype(BF16)
        wu_s[...] = (wu_ref[...] * gc).astype(BF16)

    x = x_ref[...]
    xf = x.astype(F32)
    rs = lax.rsqrt(jnp.mean(xf * xf, axis=-1, keepdims=True) + EPS)
    g = jnp.dot(x, wg_s[...], preferred_element_type=F32) * rs
    u = jnp.dot(x, wu_s[...], preferred_element_type=F32) * rs
    o_ref[...] = (g * jax.nn.sigmoid(g) * u).astype(o_ref.dtype)


def _ffn_up(hb, gain_col, w_gu, l):
    m, d = hb.shape
    n = w_gu.shape[-1] // 2
    bm = _pick(m, 1024)
    bn = _pick(n, 512)
    nj = n // bn
    return pl.pallas_call(
        _mm_swiglu_kernel,
        grid=(nj, m // bm),
        in_specs=[pl.BlockSpec((bm, d), lambda j, i: (i, 0)),
                  pl.BlockSpec((None, d, 1), lambda j, i: (l, 0, 0)),
                  pl.BlockSpec((None, d, bn), lambda j, i: (l, 0, j)),
                  pl.BlockSpec((None, d, bn), lambda j, i: (l, 0, j + nj))],
        out_specs=pl.BlockSpec((bm, bn), lambda j, i: (i, j)),
        out_shape=jax.ShapeDtypeStruct((m, n), BF16),
        scratch_shapes=[pltpu.VMEM((d, bn), BF16), pltpu.VMEM((d, bn), BF16)],
        compiler_params=_cparams(("arbitrary", "arbitrary")),
        name="ffn_up",
    )(hb, gain_col, w_gu, w_gu)


def _mm_residual_kernel(a_ref, w_ref, r_ref, o_ref, *rest, scale):
    w_s = rest[-1]
    _cast_weights_once(((w_ref, w_s),))
    out = r_ref[...] + scale * jnp.dot(a_ref[...], w_s[...], preferred_element_type=F32)
    o_ref[...] = out
    if len(rest) == 2:
        rest[0][...] = out.astype(BF16)


def _mm_residual(a, w, l, res, scale, emit_bf16=False):
    m, k = a.shape
    n = w.shape[-1]
    bm, bn = (_pick(m, 1024), _pick(n, 1024)) if k <= 2048 else (_pick(m, 512), _pick(n, 512))
    o_spec = pl.BlockSpec((bm, bn), lambda j, i: (i, j))
    out_shape = jax.ShapeDtypeStruct((m, n), F32)
    return pl.pallas_call(
        functools.partial(_mm_residual_kernel, scale=scale),
        grid=(n // bn, m // bm),
        in_specs=[pl.BlockSpec((bm, k), lambda j, i: (i, 0)),
                  pl.BlockSpec((None, k, bn), lambda j, i: (l, 0, j)),
                  o_spec],
        out_specs=[o_spec, o_spec] if emit_bf16 else o_spec,
        out_shape=[out_shape, jax.ShapeDtypeStruct((m, n), BF16)] if emit_bf16 else out_shape,
        scratch_shapes=[pltpu.VMEM((k, bn), BF16)],
        compiler_params=_cparams(("arbitrary", "arbitrary")),
        name="mm_residual",
    )(a, w, res)


def _mm_plain_kernel(x_ref, w_ref, o_ref):
    o_ref[...] = jnp.dot(x_ref[...], w_ref[...], preferred_element_type=F32).astype(o_ref.dtype)


def _mm_plain(x, w, l=None):
    m, k = x.shape
    n = w.shape[-1]
    bm = _pick(m, 512)
    if l is None:
        w_spec = pl.BlockSpec((k, n), lambda i: (0, 0))
    else:
        w_spec = pl.BlockSpec((None, k, n), lambda i: (l, 0, 0))
    return pl.pallas_call(
        _mm_plain_kernel,
        grid=(m // bm,),
        in_specs=[pl.BlockSpec((bm, k), lambda i: (i, 0)), w_spec],
        out_specs=pl.BlockSpec((bm, n), lambda i: (i, 0)),
        out_shape=jax.ShapeDtypeStruct((m, n), BF16),
        compiler_params=_cparams(("parallel",)),
        name="mm_plain",
    )(x, w)


def _mm_rope_kernel(x_ref, w_ref, cs_ref, cos_ref, sin_ref, o_ref, *, half, pattern):
    acc = jnp.dot(x_ref[...], w_ref[...], preferred_element_type=F32) * cs_ref[...]
    cos = cos_ref[...]
    sin = sin_ref[...]
    for b, roped in enumerate(pattern):
        blk = acc[:, b * LANES:(b + 1) * LANES]
        if roped:
            blk = _rope_block(blk, cos, sin, half)
        o_ref[:, b * LANES:(b + 1) * LANES] = blk.astype(o_ref.dtype)


def _mm_rope(x, w, colscale, cos, sin, half, pattern):
    m, k = x.shape
    n = w.shape[-1]
    assert n == LANES * len(pattern)
    bm = _pick(m, 512)
    return pl.pallas_call(
        functools.partial(_mm_rope_kernel, half=half, pattern=pattern),
        grid=(m // bm,),
        in_specs=[pl.BlockSpec((bm, k), lambda i: (i, 0)),
                  pl.BlockSpec((k, n), lambda i: (0, 0)),
                  pl.BlockSpec((1, n), lambda i: (0, 0)),
                  pl.BlockSpec((bm, LANES), lambda i: (i, 0)),
                  pl.BlockSpec((bm, LANES), lambda i: (i, 0))],
        out_specs=pl.BlockSpec((bm, n), lambda i: (i, 0)),
        out_shape=jax.ShapeDtypeStruct((m, n), BF16),
        compiler_params=_cparams(("parallel",)),
        name="mm_rope",
    )(x, w, colscale, cos, sin)


def _mm_rmsnorm_kernel(x_ref, w_ref, g_ref, o_ref):
    acc = jnp.dot(x_ref[...], w_ref[...], preferred_element_type=F32)
    y = acc * lax.rsqrt(jnp.mean(acc * acc, axis=-1, keepdims=True) + EPS)
    o_ref[...] = (y * g_ref[...]).astype(o_ref.dtype)


def _mm_rmsnorm(x, w, g):
    m, k = x.shape
    t, _, r = g.shape
    bm = _pick(m, 512)
    return pl.pallas_call(
        _mm_rmsnorm_kernel,
        grid=(t, m // bm),
        in_specs=[pl.BlockSpec((bm, k), lambda j, i: (i, 0)),
                  pl.BlockSpec((k, r), lambda j, i: (0, j)),
                  pl.BlockSpec((None, 1, r), lambda j, i: (j, 0, 0))],
        out_specs=pl.BlockSpec((None, bm, r), lambda j, i: (j, i, 0)),
        out_shape=jax.ShapeDtypeStruct((t, m, r), BF16),
        compiler_params=_cparams(("parallel", "parallel")),
        name="mm_rmsnorm",
    )(x, w, g)


def _logits(q, k, mlane):
    s = lax.dot_general(q, k, (((1,), (1,)), ((), ())), preferred_element_type=F32)
    for b in range(s.shape[1] // LANES):
        blk = s[:, b * LANES:(b + 1) * LANES]
        mlane = blk if mlane is None else jnp.maximum(mlane, blk)
    return s, mlane


def _row_max(mlane):
    return jnp.broadcast_to(jnp.max(mlane, axis=1, keepdims=True), mlane.shape)


def _weigh(s, m, v1, acc):
    ps = [jnp.exp2(s[:, b * LANES:(b + 1) * LANES] - m).astype(BF16) for b in range(s.shape[1] // LANES)]
    pv = jnp.dot(jnp.concatenate(ps, axis=1), v1, preferred_element_type=F32)
    return pv if acc is None else acc + pv


def _softmax_pv(qs, k_chunk, v1_chunk, nchunks):
    n = len(qs)
    logits = [[None] * nchunks for _ in range(n)]
    acc = [None] * n
    mlane = None
    for j in range(nchunks):
        logits[0][j], mlane = _logits(qs[0], k_chunk(j), mlane)
    m = _row_max(mlane)
    for p in range(1, n):
        mlane = None
        for j in range(nchunks):
            acc[p - 1] = _weigh(logits[p - 1][j], m, v1_chunk(j), acc[p - 1])
            logits[p][j], mlane = _logits(qs[p], k_chunk(j), mlane)
        m = _row_max(mlane)
    for j in range(nchunks):
        acc[n - 1] = _weigh(logits[n - 1][j], m, v1_chunk(j), acc[n - 1])
    return [a[:, :a.shape[1] - LANES] / a[:, a.shape[1] - LANES:] for a in acc]


def _attn_tiles(s):
    bq = _pick(s, 256)
    return bq, (2 if s % (2 * bq) == 0 else 1), _pick(s, 1024)


def _diff_attn_kernel(lp_ref, q_ref, k_ref, v_ref, g_ref, o_ref, *, lambda_init, tk, nsub):
    bq = q_ref.shape[0] // nsub
    ones = jnp.ones((tk, LANES), BF16)
    k_chunk = lambda j: k_ref[j * tk:(j + 1) * tk, :]
    v1_chunk = lambda j: jnp.concatenate([v_ref[j * tk:(j + 1) * tk, :], ones], axis=1)
    qs = []
    for r in range(nsub):
        q = q_ref[r * bq:(r + 1) * bq, :]
        lane = lax.broadcasted_iota(jnp.int32, q.shape, 1)
        zero = jnp.zeros_like(q)
        qs.append(jnp.where(lane < DIFF_QK_DIM, q, zero))
        qs.append(jnp.where(lane >= DIFF_QK_DIM, q, zero))
    outs = _softmax_pv(qs, k_chunk, v1_chunk, k_ref.shape[0] // tk)
    lp = lp_ref[...]
    lam = (jnp.exp(jnp.sum(lp[0:1] * lp[1:2], axis=1, keepdims=True))
           - jnp.exp(jnp.sum(lp[2:3] * lp[3:4], axis=1, keepdims=True)) + lambda_init)
    for r in range(nsub):
        o = outs[2 * r] - lam * outs[2 * r + 1]
        y = o * lax.rsqrt(jnp.mean(o * o, axis=-1, keepdims=True) + DIFF_SUBLN_EPS)
        o_ref[r * bq:(r + 1) * bq, :] = (y * g_ref[...] * (1.0 - lambda_init)).astype(o_ref.dtype)


def _diff_attention(qk, v, diff_lambda, diff_subln, l, lambda_init):
    s = qk.shape[0]
    bq, nsub, tk = _attn_tiles(s)
    rows = bq * nsub
    h = DIFF_HEADS
    return pl.pallas_call(
        functools.partial(_diff_attn_kernel, lambda_init=lambda_init, tk=tk, nsub=nsub),
        grid=(h, s // rows),
        in_specs=[pl.BlockSpec((None, 4, DIFF_QK_DIM), lambda hh, i: (l, 0, 0)),
                  pl.BlockSpec((rows, LANES), lambda hh, i: (i, hh)),
                  pl.BlockSpec((s, LANES), lambda hh, i: (0, h + hh)),
                  pl.BlockSpec((s, LANES), lambda hh, i: (0, hh)),
                  pl.BlockSpec((None, 1, DIFF_V_DIM), lambda hh, i: (l, 0, 0))],
        out_specs=pl.BlockSpec((rows, LANES), lambda hh, i: (i, hh)),
        out_shape=jax.ShapeDtypeStruct((s, h * DIFF_V_DIM), BF16),
        compiler_params=_cparams(("parallel", "parallel")),
        name="diff_attention",
    )(diff_lambda, qk, qk, v, diff_subln)


def _mla_attn_kernel(q_ref, kn_ref, kr_ref, v_ref, o_ref, *, tk, nsub):
    bq = q_ref.shape[0] // nsub
    ones = jnp.ones((tk, LANES), BF16)
    k_chunk = lambda j: jnp.concatenate([kn_ref[j * tk:(j + 1) * tk, :], kr_ref[j * tk:(j + 1) * tk, :]], axis=1)
    v1_chunk = lambda j: jnp.concatenate([v_ref[j * tk:(j + 1) * tk, :], ones], axis=1)
    qs = [q_ref[r * bq:(r + 1) * bq, :] for r in range(nsub)]
    outs = _softmax_pv(qs, k_chunk, v1_chunk, kn_ref.shape[0] // tk)
    for r in range(nsub):
        o_ref[r * bq:(r + 1) * bq, :] = outs[r].astype(o_ref.dtype)


def _mla_attention(q, kv_up, qk):
    s = q.shape[0]
    bq, nsub, tk = _attn_tiles(s)
    rows = bq * nsub
    h = MLA_HEADS
    kr_blk = 2 * DIFF_HEADS
    return pl.pallas_call(
        functools.partial(_mla_attn_kernel, tk=tk, nsub=nsub),
        grid=(h, s // rows),
        in_specs=[pl.BlockSpec((rows, 2 * LANES), lambda hh, i: (i, hh)),
                  pl.BlockSpec((s, LANES), lambda hh, i: (0, 2 * hh)),
                  pl.BlockSpec((s, LANES), lambda hh, i: (0, kr_blk)),
                  pl.BlockSpec((s, LANES), lambda hh, i: (0, 2 * hh + 1))],
        out_specs=pl.BlockSpec((rows, LANES), lambda hh, i: (i, hh)),
        out_shape=jax.ShapeDtypeStruct((s, h * MLA_V), BF16),
        compiler_params=_cparams(("parallel", "parallel")),
        name="mla_attention",
    )(q, kv_up, qk, kv_up)


def _gqa_kernel(sink_ref, q_ref, kp_ref, kc_ref, kn_ref, vp_ref, vc_ref, vn_ref, o_ref, *, layer, seq, bq):
    g = pl.program_id(0)
    i = pl.program_id(1)
    nk = bq + 2 * WINDOW
    kb = jnp.concatenate([kp_ref[...], kc_ref[...], kn_ref[...]], axis=0)
    vb = jnp.concatenate([vp_ref[...], vc_ref[...], vn_ref[...]], axis=0)
    v1 = jnp.concatenate([vb, jnp.ones_like(vb)], axis=1)
    qpos = i * bq + lax.broadcasted_iota(jnp.int32, (bq, nk), 0)
    kpos = i * bq - WINDOW + lax.broadcasted_iota(jnp.int32, (bq, nk), 1)
    valid = (jnp.abs(kpos - qpos) <= WINDOW) & (kpos >= 0) & (kpos < seq)
    bias = jnp.where(valid, 0.0, NEG_INF).astype(F32)
    for hh in range(GQA_GROUP):
        q = q_ref[:, hh * GQA_DIM:(hh + 1) * GQA_DIM]
        s = lax.dot_general(q, kb, (((1,), (1,)), ((), ())), preferred_element_type=F32) + bias
        sink = sink_ref[layer * GQA_HEADS + g * GQA_GROUP + hh] * LOG2E
        m = jnp.maximum(jnp.max(s, axis=1, keepdims=True), sink)
        pv = jnp.dot(jnp.exp2(s - m).astype(BF16), v1, preferred_element_type=F32)
        denom = pv[:, GQA_DIM:] + jnp.exp2(sink - m)
        o_ref[:, hh * GQA_DIM:(hh + 1) * GQA_DIM] = (pv[:, :GQA_DIM] / denom).astype(o_ref.dtype)


def _window_gqa(gqk, vv, sink_flat, l):
    s = gqk.shape[0]
    bq = _pick(s, 256)
    assert bq % WINDOW == 0 and WINDOW == LANES
    nb = s // bq
    r = bq // WINDOW
    nh = s // WINDOW
    qw = GQA_GROUP * GQA_DIM
    kcol = GQA_HEADS
    vcol = DIFF_HEADS

    def prev(i):
        return jnp.maximum(i * r - 1, 0)

    def nxt(i):
        return jnp.minimum((i + 1) * r, nh - 1)

    return pl.pallas_call(
        functools.partial(_gqa_kernel, layer=l, seq=s, bq=bq),
        grid=(GQA_KV_HEADS, nb),
        in_specs=[pl.BlockSpec(memory_space=pltpu.SMEM),
                  pl.BlockSpec((bq, qw), lambda g, i: (i, g)),
                  pl.BlockSpec((WINDOW, LANES), lambda g, i: (prev(i), kcol + g)),
                  pl.BlockSpec((bq, LANES), lambda g, i: (i, kcol + g)),
                  pl.BlockSpec((WINDOW, LANES), lambda g, i: (nxt(i), kcol + g)),
                  pl.BlockSpec((WINDOW, LANES), lambda g, i: (prev(i), vcol + g)),
                  pl.BlockSpec((bq, LANES), lambda g, i: (i, vcol + g)),
                  pl.BlockSpec((WINDOW, LANES), lambda g, i: (nxt(i), vcol + g))],
        out_specs=pl.BlockSpec((bq, qw), lambda g, i: (i, g)),
        out_shape=jax.ShapeDtypeStruct((s, GQA_HEADS * GQA_DIM), BF16),
        compiler_params=_cparams(("parallel", "parallel")),
        name="window_gqa",
    )(sink_flat, gqk, gqk, gqk, gqk, vv, vv, vv)


def _merge_kernel(u_ref, o0_ref, o1_ref, o2_ref, wg0_ref, wg1_ref, wg2_ref, b0_ref, b1_ref, b2_ref,
                  wb0_ref, wb1_ref, wb2_ref, out_ref, wg0_s, wg1_s, wg2_s, wb0_s, wb1_s, wb2_s):
    _cast_weights_once(((wg0_ref, wg0_s), (wg1_ref, wg1_s), (wg2_ref, wg2_s),
                        (wb0_ref, wb0_s), (wb1_ref, wb1_s), (wb2_ref, wb2_s)))
    u = u_ref[...]
    acc = None
    for o_ref, wg_s, b_ref, wb_s in ((o0_ref, wg0_s, b0_ref, wb0_s),
                                     (o1_ref, wg1_s, b1_ref, wb1_s),
                                     (o2_ref, wg2_s, b2_ref, wb2_s)):
        gate = jax.nn.sigmoid(jnp.dot(u, wg_s[...], preferred_element_type=F32) + b_ref[...])
        term = jnp.dot(o_ref[...], wb_s[...], preferred_element_type=F32) * gate
        acc = term if acc is None else acc + term
    out_ref[...] = acc.astype(out_ref.dtype)


def _merge(u, o_diff, o_mla, o_gqa, w_gate, b_gate, w_branch, l):
    m, d = u.shape
    bw = o_diff.shape[1]
    bm = _pick(m, 1024)
    bn = _pick(d, 256)
    nj = d // bn
    o_spec = pl.BlockSpec((bm, bw), lambda j, i: (i, 0))

    def wg_spec(n):
        return pl.BlockSpec((None, d, bn), lambda j, i: (l, 0, n * nj + j))

    def b_spec(n):
        return pl.BlockSpec((None, 1, bn), lambda j, i: (l, 0, n * nj + j))

    def wb_spec(n):
        return pl.BlockSpec((None, None, bw, bn), lambda j, i: (l, n, 0, j))

    return pl.pallas_call(
        _merge_kernel,
        grid=(nj, m // bm),
        in_specs=[pl.BlockSpec((bm, d), lambda j, i: (i, 0)), o_spec, o_spec, o_spec,
                  wg_spec(0), wg_spec(1), wg_spec(2), b_spec(0), b_spec(1), b_spec(2),
                  wb_spec(0), wb_spec(1), wb_spec(2)],
        out_specs=pl.BlockSpec((bm, bn), lambda j, i: (i, j)),
        out_shape=jax.ShapeDtypeStruct((m, d), BF16),
        scratch_shapes=[pltpu.VMEM((d, bn), BF16)] * 3 + [pltpu.VMEM((bw, bn), BF16)] * 3,
        compiler_params=_cparams(("arbitrary", "arbitrary")),
        name="branch_merge",
    )(u, o_diff, o_mla, o_gqa, w_gate, w_gate, w_gate, b_gate, b_gate, b_gate,
      w_branch, w_branch, w_branch)


def _xattn_kernel(h_ref, g_ref, wq_ref, k_ref, v_ref, wo_ref, o_ref, ob_ref, wq_s, wo_s):
    @pl.when(pl.program_id(0) == 0)
    def _():
        wq_s[...] = wq_ref[...].astype(BF16)
        wo_s[...] = wo_ref[...].astype(BF16)

    hn = _norm_rows(h_ref[...], g_ref[...])
    q = jnp.dot(hn, wq_s[...], preferred_element_type=F32) * (XA_DIM ** -0.5 * LOG2E)
    q = q.astype(BF16)
    heads = []
    for hh in range(XA_HEADS):
        sl = slice(hh * XA_DIM, (hh + 1) * XA_DIM)
        s = lax.dot_general(q[:, sl], k_ref[:, sl], (((1,), (1,)), ((), ())), preferred_element_type=F32)
        p = jnp.exp2(s - jnp.max(s, axis=1, keepdims=True))
        denom = jnp.sum(p, axis=1, keepdims=True)
        heads.append((jnp.dot(p.astype(BF16), v_ref[:, sl], preferred_element_type=F32) / denom).astype(BF16))
    o = jnp.concatenate(heads, axis=1)
    out = h_ref[...] + jnp.dot(o, wo_s[...], preferred_element_type=F32)
    o_ref[...] = out
    ob_ref[...] = out.astype(BF16)


def _cross_attention(h, gain, xa_wq, memkv, xa_wo, l):
    m, d = h.shape
    w = XA_HEADS * XA_DIM
    ml = memkv.shape[0]
    bm = _pick(m, 512)
    return pl.pallas_call(
        _xattn_kernel,
        grid=(m // bm,),
        in_specs=[pl.BlockSpec((bm, d), lambda i: (i, 0)),
                  pl.BlockSpec((None, 1, d), lambda i: (l, 0, 0)),
                  pl.BlockSpec((None, d, w), lambda i: (l, 0, 0)),
                  pl.BlockSpec((ml, w), lambda i: (0, 0)),
                  pl.BlockSpec((ml, w), lambda i: (0, 1)),
                  pl.BlockSpec((None, w, d), lambda i: (l, 0, 0))],
        out_specs=[pl.BlockSpec((bm, d), lambda i: (i, 0))] * 2,
        out_shape=[jax.ShapeDtypeStruct((m, d), F32), jax.ShapeDtypeStruct((m, d), BF16)],
        scratch_shapes=[pltpu.VMEM((d, w), BF16), pltpu.VMEM((w, d), BF16)],
        compiler_params=_cparams(("arbitrary",)),
        name="cross_attention",
    )(h, gain, xa_wq, memkv, memkv, xa_wo)


def _pad_cols(w, width):
    return jnp.pad(w, ((0, 0), (0, width - w.shape[-1])))


def kernel(x, mem, positions, ffn1_norm, ffn1_w_gu, ffn1_w_down, mix_norm, w_in, diff_lambda, diff_subln,
           mla_q_norm, mla_kv_norm, mla_w_uq, mla_w_ukv, gqa_sink, w_branch, w_gate, b_gate, w_o, xa_norm,
           mem_norm, xa_wq, xa_wkv, xa_wo, ffn2_norm, ffn2_w_gu, ffn2_w_down, final_norm):
    b, s, d = x.shape
    assert b == 1
    depth = ffn1_norm.shape[0]
    h = x.reshape(s, d)
    mem2 = mem.reshape(mem.shape[1], d)

    pos_b = jnp.broadcast_to(positions.reshape(s, 1).astype(F32), (s, LANES))
    cos64, sin64 = _rope_tables(pos_b, DIFF_QK_DIM)
    cos128, sin128 = _rope_tables(pos_b, GQA_DIM)

    bf = lambda w: w.astype(BF16)
    xa_wkv_b = bf(xa_wkv)
    b_gate3 = b_gate.reshape(depth, 1, -1)
    norm3 = lambda g: g.reshape(depth, 1, -1)
    mix_norm3, xa_norm3, mem_norm3 = map(norm3, (mix_norm, xa_norm, mem_norm))
    ffn1_norm_col, ffn2_norm_col = ffn1_norm.reshape(depth, d, 1), ffn2_norm.reshape(depth, d, 1)
    hb = bf(h)
    diff_subln3 = diff_subln.reshape(depth, 1, -1)
    sink_flat = gqa_sink.reshape(-1)

    o_dq, o_dk, o_dv = 0, 1024, 2048
    o_cq, o_ckv, o_kr = 3072, 3584, 4096
    o_gq, o_gk, o_gv = 4160, 5184, 5440
    diff_scale = DIFF_QK_DIM ** -0.5 * LOG2E
    mla_scale = (MLA_NOPE + MLA_ROPE) ** -0.5 * LOG2E
    gqa_scale = GQA_DIM ** -0.5 * LOG2E
    cs_a = jnp.concatenate([jnp.full((1, 1024), diff_scale, F32), jnp.ones((1, 1024 + LANES), F32)], axis=1)
    cs_g = jnp.concatenate([jnp.full((1, 1024), gqa_scale, F32), jnp.ones((1, 256), F32)], axis=1)
    cs_q = jnp.full((1, MLA_HEADS * 2 * LANES), mla_scale, F32)

    for l in range(depth):
        lambda_init = 0.8 - 0.6 * math.exp(-0.3 * l)

        a = _ffn_up(hb, ffn1_norm_col, ffn1_w_gu, l)
        h = _mm_residual(a, ffn1_w_down, l, h, 0.5)

        u = _rmsnorm(h, mix_norm3, l)
        wl = w_in[l]
        w_a = bf(jnp.concatenate([wl[:, o_dq:o_dv], _pad_cols(wl[:, o_kr:o_gq], LANES)], axis=1))
        w_v = bf(jnp.concatenate([wl[:, o_dv:o_cq], wl[:, o_gv:]], axis=1))
        w_c = bf(wl[:, o_cq:o_kr])
        w_g = bf(wl[:, o_gq:o_gv])
        qk = _mm_rope(u, w_a, cs_a, cos64, sin64, DIFF_QK_DIM // 2, (True,) * 17)
        vv = _mm_plain(u, w_v)
        gains = jnp.stack([mla_q_norm[l], mla_kv_norm[l]])[:, None, :]
        cn = _mm_rmsnorm(u, w_c, gains)
        gqk = _mm_rope(u, w_g, cs_g, cos128, sin128, GQA_DIM // 2, (True,) * 10)

        o_diff = _diff_attention(qk, vv, diff_lambda, diff_subln3, l, lambda_init)

        wuq = mla_w_uq[l].reshape(MLA_Q_RANK, MLA_HEADS, MLA_NOPE + MLA_ROPE)
        wuq = jnp.pad(wuq, ((0, 0), (0, 0), (0, 2 * LANES - MLA_NOPE - MLA_ROPE)))
        wuq = bf(wuq.reshape(MLA_Q_RANK, MLA_HEADS * 2 * LANES))
        q_mla = _mm_rope(cn[0], wuq, cs_q, cos64, sin64, MLA_ROPE // 2, (False, True) * MLA_HEADS)
        kv_up = _mm_plain(cn[1], bf(mla_w_ukv[l]))
        o_mla = _mla_attention(q_mla, kv_up, qk)

        o_gqa = _window_gqa(gqk, vv, sink_flat, l)

        merged = _merge(u, o_diff, o_mla, o_gqa, w_gate, b_gate3, w_branch, l)
        h = _mm_residual(merged, w_o, l, h, 1.0)

        memkv = _mm_plain(_rmsnorm(mem2, mem_norm3, l), xa_wkv_b, l)
        h, hb = _cross_attention(h, xa_norm3, xa_wq, memkv, xa_wo, l)

        a = _ffn_up(hb, ffn2_norm_col, ffn2_w_gu, l)
        if l + 1 < depth:
            h, hb = _mm_residual(a, ffn2_w_down, l, h, 0.5, emit_bf16=True)
        else:
            h = _mm_residual(a, ffn2_w_down, l, h, 0.5)

    out = _rmsnorm(h, final_norm.reshape(1, d), None, out_dtype=F32)
    return out.reshape(b, s, d)
```

```python
import functools
import math

import jax
import jax.numpy as jnp
from jax import lax
from jax.experimental import pallas as pl
from jax.experimental.pallas import tpu as pltpu

F32 = jnp.float32
BF16 = jnp.bfloat16

DIFF_HEADS = 8
DIFF_QK_DIM = 64
DIFF_V_DIM = 128
MLA_HEADS = 8
MLA_Q_RANK = 512
MLA_KV_RANK = 512
MLA_NOPE = 128
MLA_ROPE = 64
MLA_V = 128
GQA_HEADS = 8
GQA_KV_HEADS = 2
GQA_GROUP = GQA_HEADS // GQA_KV_HEADS
GQA_DIM = 128
WINDOW = 128
N_BRANCH = 3
BRANCH_WIDTH = 1024
XA_HEADS = 4
XA_DIM = 128
D_FF = 5632
ROPE_THETA = 10000.0
EPS = 1e-6
DIFF_SUBLN_EPS = 1e-5
NEG_INF = -1e30
LOG2E = math.log2(math.e)

LANES = 128
V7X_VMEM_LIMIT_BYTES = 56 * 1024 * 1024


def _cparams(semantics):
    return pltpu.CompilerParams(dimension_semantics=semantics, vmem_limit_bytes=V7X_VMEM_LIMIT_BYTES)


def _pick(dim, pref):
    if dim <= pref:
        return dim
    b = pref
    while dim % b:
        b //= 2
    return b


def _rope_table_kernel(pos_ref, inv_ref, sign_ref, cos_ref, sin_ref):
    ang = pos_ref[...] * inv_ref[...]
    cos_ref[...] = jnp.cos(ang)
    sin_ref[...] = jnp.sin(ang) * sign_ref[...]


def _rope_tables(pos_b, dim):
    s = pos_b.shape[0]
    half = dim // 2
    inv = ROPE_THETA ** (-jnp.arange(0, dim, 2, dtype=F32) / dim)
    inv_row = jnp.tile(inv, LANES // half)[None, :]
    sign_row = jnp.tile(jnp.concatenate([-jnp.ones((half,), F32), jnp.ones((half,), F32)]), LANES // dim)[None, :]
    bm = _pick(s, 1024)
    return pl.pallas_call(
        _rope_table_kernel,
        grid=(s // bm,),
        in_specs=[pl.BlockSpec((bm, LANES), lambda i: (i, 0)),
                  pl.BlockSpec((1, LANES), lambda i: (0, 0)),
                  pl.BlockSpec((1, LANES), lambda i: (0, 0))],
        out_specs=[pl.BlockSpec((bm, LANES), lambda i: (i, 0))] * 2,
        out_shape=[jax.ShapeDtypeStruct((s, LANES), F32)] * 2,
        compiler_params=_cparams(("parallel",)),
        name="rope_tables",
    )(pos_b, inv_row, sign_row)


def _rope_block(x, cos, sin_signed, half):
    if 2 * half == LANES:
        partner = pltpu.roll(x, half, axis=1)
    else:
        lane = lax.broadcasted_iota(jnp.int32, x.shape, 1)
        first = (lane % (2 * half)) < half
        partner = jnp.where(first, pltpu.roll(x, LANES - half, axis=1), pltpu.roll(x, half, axis=1))
    return x * cos + partner * sin_signed


def _rmsnorm_kernel(x_ref, g_ref, o_ref, *, eps):
    x = x_ref[...]
    y = x * lax.rsqrt(jnp.mean(x * x, axis=-1, keepdims=True) + eps)
    o_ref[...] = (y * g_ref[...]).astype(o_ref.dtype)


def _rmsnorm(x, g_stack, l, out_dtype=BF16):
    m, d = x.shape
    bm = _pick(m, 512)
    if l is None:
        g_spec = pl.BlockSpec((1, d), lambda i: (0, 0))
    else:
        g_spec = pl.BlockSpec((None, 1, d), lambda i: (l, 0, 0))
    return pl.pallas_call(
        functools.partial(_rmsnorm_kernel, eps=EPS),
        grid=(m // bm,),
        in_specs=[pl.BlockSpec((bm, d), lambda i: (i, 0)), g_spec],
        out_specs=pl.BlockSpec((bm, d), lambda i: (i, 0)),
        out_shape=jax.ShapeDtypeStruct((m, d), out_dtype),
        compiler_params=_cparams(("parallel",)),
        name="rmsnorm",
    )(x, g_stack)


def _cast_weights_once(pairs):
    @pl.when(pl.program_id(1) == 0)
    def _():
        for w_ref, w_s in pairs:
            w_s[...] = w_ref[...].astype(BF16)


def _norm_rows(x, g, eps=EPS):
    return (x * lax.rsqrt(jnp.mean(x * x, axis=-1, keepdims=True) + eps) * g).astype(BF16)


def _mm_swiglu_kernel(x_ref, gc_ref, wg_ref, wu_ref, o_ref, wg_s, wu_s):
    @pl.when(pl.program_id(1) == 0)
    def _():
        gc = gc_ref[...]
        wg_s[...] = (wg_ref[...] * gc).astype(BF16)
        wu_s[...] = (wu_ref[...] * gc).astype(BF16)

    x = x_ref[...]
    xf = x.astype(F32)
    rs = lax.rsqrt(jnp.mean(xf * xf, axis=-1, keepdims=True) + EPS)
    g = jnp.dot(x, wg_s[...], preferred_element_type=F32) * rs
    u = jnp.dot(x, wu_s[...], preferred_element_type=F32) * rs
    o_ref[...] = (g * jax.nn.sigmoid(g) * u).astype(o_ref.dtype)


def _ffn_up(hb, gain_col, w_gu, l):
    m, d = hb.shape
    n = w_gu.shape[-1] // 2
    bm = _pick(m, 1024)
    bn = _pick(n, 512)
    nj = n // bn
    return pl.pallas_call(
        _mm_swiglu_kernel,
        grid=(nj, m // bm),
        in_specs=[pl.BlockSpec((bm, d), lambda j, i: (i, 0)),
                  pl.BlockSpec((None, d, 1), lambda j, i: (l, 0, 0)),
                  pl.BlockSpec((None, d, bn), lambda j, i: (l, 0, j)),
                  pl.BlockSpec((None, d, bn), lambda j, i: (l, 0, j + nj))],
        out_specs=pl.BlockSpec((bm, bn), lambda j, i: (i, j)),
        out_shape=jax.ShapeDtypeStruct((m, n), BF16),
        scratch_shapes=[pltpu.VMEM((d, bn), BF16), pltpu.VMEM((d, bn), BF16)],
        compiler_params=_cparams(("arbitrary", "arbitrary")),
        name="ffn_up",
    )(hb, gain_col, w_gu, w_gu)


def _mm_residual_kernel(a_ref, w_ref, r_ref, o_ref, *rest, scale):
    w_s = rest[-1]
    _cast_weights_once(((w_ref, w_s),))
    out = r_ref[...] + scale * jnp.dot(a_ref[...], w_s[...], preferred_element_type=F32)
    o_ref[...] = out
    if len(rest) == 2:
        rest[0][...] = out.astype(BF16)


def _mm_residual(a, w, l, res, scale, emit_bf16=False):
    m, k = a.shape
    n = w.shape[-1]
    bn = _pick(n, 1024)
    if k <= 2048:
        bm, w_mode = _pick(m, 1024), {}
    else:
        bm, w_mode = _pick(m, 256), dict(pipeline_mode=pl.Buffered(1))
    o_spec = pl.BlockSpec((bm, bn), lambda j, i: (i, j))
    out_shape = jax.ShapeDtypeStruct((m, n), F32)
    return pl.pallas_call(
        functools.partial(_mm_residual_kernel, scale=scale),
        grid=(n // bn, m // bm),
        in_specs=[pl.BlockSpec((bm, k), lambda j, i: (i, 0)),
                  pl.BlockSpec((None, k, bn), lambda j, i: (l, 0, j), **w_mode),
                  o_spec],
        out_specs=[o_spec, o_spec] if emit_bf16 else o_spec,
        out_shape=[out_shape, jax.ShapeDtypeStruct((m, n), BF16)] if emit_bf16 else out_shape,
        scratch_shapes=[pltpu.VMEM((k, bn), BF16)],
        compiler_params=_cparams(("arbitrary", "arbitrary")),
        name="mm_residual",
    )(a, w, res)


def _mm_plain_kernel(x_ref, w_ref, o_ref):
    o_ref[...] = jnp.dot(x_ref[...], w_ref[...], preferred_element_type=F32).astype(o_ref.dtype)


def _mm_plain(x, w, l=None):
    m, k = x.shape
    n = w.shape[-1]
    bm = _pick(m, 512)
    if l is None:
        w_spec = pl.BlockSpec((k, n), lambda i: (0, 0))
    else:
        w_spec = pl.BlockSpec((None, k, n), lambda i: (l, 0, 0))
    return pl.pallas_call(
        _mm_plain_kernel,
        grid=(m // bm,),
        in_specs=[pl.BlockSpec((bm, k), lambda i: (i, 0)), w_spec],
        out_specs=pl.BlockSpec((bm, n), lambda i: (i, 0)),
        out_shape=jax.ShapeDtypeStruct((m, n), BF16),
        compiler_params=_cparams(("parallel",)),
        name="mm_plain",
    )(x, w)


def _mm_rope_kernel(x_ref, w_ref, cs_ref, cos_ref, sin_ref, o_ref, *, half, pattern):
    acc = jnp.dot(x_ref[...], w_ref[...], preferred_element_type=F32) * cs_ref[...]
    cos = cos_ref[...]
    sin = sin_ref[...]
    for b, roped in enumerate(pattern):
        blk = acc[:, b * LANES:(b + 1) * LANES]
        if roped:
            blk = _rope_block(blk, cos, sin, half)
        o_ref[:, b * LANES:(b + 1) * LANES] = blk.astype(o_ref.dtype)


def _mm_rope(x, w, l, colscale, cos, sin, half, pattern):
    m, k = x.shape
    n = w.shape[-1]
    assert n == LANES * len(pattern)
    bm = _pick(m, 512)
    return pl.pallas_call(
        functools.partial(_mm_rope_kernel, half=half, pattern=pattern),
        grid=(m // bm,),
        in_specs=[pl.BlockSpec((bm, k), lambda i: (i, 0)),
                  pl.BlockSpec((None, k, n), lambda i: (l, 0, 0)),
                  pl.BlockSpec((1, n), lambda i: (0, 0)),
                  pl.BlockSpec((bm, LANES), lambda i: (i, 0)),
                  pl.BlockSpec((bm, LANES), lambda i: (i, 0))],
        out_specs=pl.BlockSpec((bm, n), lambda i: (i, 0)),
        out_shape=jax.ShapeDtypeStruct((m, n), BF16),
        compiler_params=_cparams(("parallel",)),
        name="mm_rope",
    )(x, w, colscale, cos, sin)


def _mm_rmsnorm_kernel(x_ref, w_ref, g_ref, o_ref):
    acc = jnp.dot(x_ref[...], w_ref[...], preferred_element_type=F32)
    y = acc * lax.rsqrt(jnp.mean(acc * acc, axis=-1, keepdims=True) + EPS)
    o_ref[...] = (y * g_ref[...]).astype(o_ref.dtype)


def _mm_rmsnorm(x, w, g, l):
    m, k = x.shape
    _, t, _, r = g.shape
    bm = _pick(m, 512)
    return pl.pallas_call(
        _mm_rmsnorm_kernel,
        grid=(t, m // bm),
        in_specs=[pl.BlockSpec((bm, k), lambda j, i: (i, 0)),
                  pl.BlockSpec((None, k, r), lambda j, i: (l, 0, j)),
                  pl.BlockSpec((None, None, 1, r), lambda j, i: (l, j, 0, 0))],
        out_specs=pl.BlockSpec((None, bm, r), lambda j, i: (j, i, 0)),
        out_shape=jax.ShapeDtypeStruct((t, m, r), BF16),
        compiler_params=_cparams(("parallel", "parallel")),
        name="mm_rmsnorm",
    )(x, w, g)


def _logits(q, k, mlane):
    s = lax.dot_general(q, k, (((1,), (1,)), ((), ())), preferred_element_type=F32)
    for b in range(s.shape[1] // LANES):
        blk = s[:, b * LANES:(b + 1) * LANES]
        mlane = blk if mlane is None else jnp.maximum(mlane, blk)
    return s, mlane


def _row_max(mlane):
    return jnp.broadcast_to(jnp.max(mlane, axis=1, keepdims=True), mlane.shape)


def _weigh(s, m, v1, acc):
    ps = [jnp.exp2(s[:, b * LANES:(b + 1) * LANES] - m).astype(BF16) for b in range(s.shape[1] // LANES)]
    pv = jnp.dot(jnp.concatenate(ps, axis=1), v1, preferred_element_type=F32)
    return pv if acc is None else acc + pv


def _softmax_pv(qs, k_chunk, v1_chunk, nchunks):
    n = len(qs)
    logits = [[None] * nchunks for _ in range(n)]
    acc = [None] * n
    mlane = None
    for j in range(nchunks):
        logits[0][j], mlane = _logits(qs[0], k_chunk(j), mlane)
    m = _row_max(mlane)
    for p in range(1, n):
        mlane = None
        for j in range(nchunks):
            acc[p - 1] = _weigh(logits[p - 1][j], m, v1_chunk(j), acc[p - 1])
            logits[p][j], mlane = _logits(qs[p], k_chunk(j), mlane)
        m = _row_max(mlane)
    for j in range(nchunks):
        acc[n - 1] = _weigh(logits[n - 1][j], m, v1_chunk(j), acc[n - 1])
    return [a[:, :a.shape[1] - LANES] / a[:, a.shape[1] - LANES:] for a in acc]


def _attn_tiles(s):
    bq = _pick(s, 256)
    return bq, (2 if s % (2 * bq) == 0 else 1), _pick(s, 1024)


def _diff_attn_kernel(lp_ref, q_ref, k_ref, v_ref, g_ref, o_ref, *, lambda_init, tk, nsub):
    bq = q_ref.shape[0] // nsub
    ones = jnp.ones((tk, LANES), BF16)
    k_chunk = lambda j: k_ref[j * tk:(j + 1) * tk, :]
    v1_chunk = lambda j: jnp.concatenate([v_ref[j * tk:(j + 1) * tk, :], ones], axis=1)
    qs = []
    for r in range(nsub):
        q = q_ref[r * bq:(r + 1) * bq, :]
        lane = lax.broadcasted_iota(jnp.int32, q.shape, 1)
        zero = jnp.zeros_like(q)
        qs.append(jnp.where(lane < DIFF_QK_DIM, q, zero))
        qs.append(jnp.where(lane >= DIFF_QK_DIM, q, zero))
    outs = _softmax_pv(qs, k_chunk, v1_chunk, k_ref.shape[0] // tk)
    lp = lp_ref[...]
    lam = (jnp.exp(jnp.sum(lp[0:1] * lp[1:2], axis=1, keepdims=True))
           - jnp.exp(jnp.sum(lp[2:3] * lp[3:4], axis=1, keepdims=True)) + lambda_init)
    for r in range(nsub):
        o = outs[2 * r] - lam * outs[2 * r + 1]
        y = o * lax.rsqrt(jnp.mean(o * o, axis=-1, keepdims=True) + DIFF_SUBLN_EPS)
        o_ref[r * bq:(r + 1) * bq, :] = (y * g_ref[...] * (1.0 - lambda_init)).astype(o_ref.dtype)


def _diff_attention(qk, v, diff_lambda, diff_subln, l, lambda_init):
    s = qk.shape[0]
    bq, nsub, tk = _attn_tiles(s)
    rows = bq * nsub
    h = DIFF_HEADS
    return pl.pallas_call(
        functools.partial(_diff_attn_kernel, lambda_init=lambda_init, tk=tk, nsub=nsub),
        grid=(h, s // rows),
        in_specs=[pl.BlockSpec((None, 4, DIFF_QK_DIM), lambda hh, i: (l, 0, 0)),
                  pl.BlockSpec((rows, LANES), lambda hh, i: (i, hh)),
                  pl.BlockSpec((s, LANES), lambda hh, i: (0, h + hh)),
                  pl.BlockSpec((s, LANES), lambda hh, i: (0, hh)),
                  pl.BlockSpec((None, 1, DIFF_V_DIM), lambda hh, i: (l, 0, 0))],
        out_specs=pl.BlockSpec((rows, LANES), lambda hh, i: (i, hh)),
        out_shape=jax.ShapeDtypeStruct((s, h * DIFF_V_DIM), BF16),
        compiler_params=_cparams(("parallel", "parallel")),
        name="diff_attention",
    )(diff_lambda, qk, qk, v, diff_subln)


def _mla_attn_kernel(q_ref, kn_ref, kr_ref, v_ref, o_ref, *, tk, nsub):
    bq = q_ref.shape[0] // nsub
    ones = jnp.ones((tk, LANES), BF16)
    k_chunk = lambda j: jnp.concatenate([kn_ref[j * tk:(j + 1) * tk, :], kr_ref[j * tk:(j + 1) * tk, :]], axis=1)
    v1_chunk = lambda j: jnp.concatenate([v_ref[j * tk:(j + 1) * tk, :], ones], axis=1)
    qs = [q_ref[r * bq:(r + 1) * bq, :] for r in range(nsub)]
    outs = _softmax_pv(qs, k_chunk, v1_chunk, kn_ref.shape[0] // tk)
    for r in range(nsub):
        o_ref[r * bq:(r + 1) * bq, :] = outs[r].astype(o_ref.dtype)


def _mla_attention(q, kv_up, qk):
    s = q.shape[0]
    bq, nsub, tk = _attn_tiles(s)
    rows = bq * nsub
    h = MLA_HEADS
    kr_blk = 2 * DIFF_HEADS
    return pl.pallas_call(
        functools.partial(_mla_attn_kernel, tk=tk, nsub=nsub),
        grid=(h, s // rows),
        in_specs=[pl.BlockSpec((rows, 2 * LANES), lambda hh, i: (i, hh)),
                  pl.BlockSpec((s, LANES), lambda hh, i: (0, 2 * hh)),
                  pl.BlockSpec((s, LANES), lambda hh, i: (0, kr_blk)),
                  pl.BlockSpec((s, LANES), lambda hh, i: (0, 2 * hh + 1))],
        out_specs=pl.BlockSpec((rows, LANES), lambda hh, i: (i, hh)),
        out_shape=jax.ShapeDtypeStruct((s, h * MLA_V), BF16),
        compiler_params=_cparams(("parallel", "parallel")),
        name="mla_attention",
    )(q, kv_up, qk, kv_up)


def _gqa_kernel(sink_ref, q_ref, kp_ref, kc_ref, kn_ref, vp_ref, vc_ref, vn_ref, o_ref, *, layer, seq, bq):
    g = pl.program_id(0)
    i = pl.program_id(1)
    nk = bq + 2 * WINDOW
    kb = jnp.concatenate([kp_ref[...], kc_ref[...], kn_ref[...]], axis=0)
    vb = jnp.concatenate([vp_ref[...], vc_ref[...], vn_ref[...]], axis=0)
    v1 = jnp.concatenate([vb, jnp.ones_like(vb)], axis=1)
    qpos = i * bq + lax.broadcasted_iota(jnp.int32, (bq, nk), 0)
    kpos = i * bq - WINDOW + lax.broadcasted_iota(jnp.int32, (bq, nk), 1)
    valid = (jnp.abs(kpos - qpos) <= WINDOW) & (kpos >= 0) & (kpos < seq)
    bias = jnp.where(valid, 0.0, NEG_INF).astype(F32)
    for hh in range(GQA_GROUP):
        q = q_ref[:, hh * GQA_DIM:(hh + 1) * GQA_DIM]
        s = lax.dot_general(q, kb, (((1,), (1,)), ((), ())), preferred_element_type=F32) + bias
        sink = sink_ref[layer * GQA_HEADS + g * GQA_GROUP + hh] * LOG2E
        m = jnp.maximum(jnp.max(s, axis=1, keepdims=True), sink)
        pv = jnp.dot(jnp.exp2(s - m).astype(BF16), v1, preferred_element_type=F32)
        denom = pv[:, GQA_DIM:] + jnp.exp2(sink - m)
        o_ref[:, hh * GQA_DIM:(hh + 1) * GQA_DIM] = (pv[:, :GQA_DIM] / denom).astype(o_ref.dtype)


def _window_gqa(gqk, vv, sink_flat, l):
    s = gqk.shape[0]
    bq = _pick(s, 256)
    assert bq % WINDOW == 0 and WINDOW == LANES
    nb = s // bq
    r = bq // WINDOW
    nh = s // WINDOW
    qw = GQA_GROUP * GQA_DIM
    kcol = GQA_HEADS
    vcol = DIFF_HEADS

    def prev(i):
        return jnp.maximum(i * r - 1, 0)

    def nxt(i):
        return jnp.minimum((i + 1) * r, nh - 1)

    return pl.pallas_call(
        functools.partial(_gqa_kernel, layer=l, seq=s, bq=bq),
        grid=(GQA_KV_HEADS, nb),
        in_specs=[pl.BlockSpec(memory_space=pltpu.SMEM),
                  pl.BlockSpec((bq, qw), lambda g, i: (i, g)),
                  pl.BlockSpec((WINDOW, LANES), lambda g, i: (prev(i), kcol + g)),
                  pl.BlockSpec((bq, LANES), lambda g, i: (i, kcol + g)),
                  pl.BlockSpec((WINDOW, LANES), lambda g, i: (nxt(i), kcol + g)),
                  pl.BlockSpec((WINDOW, LANES), lambda g, i: (prev(i), vcol + g)),
                  pl.BlockSpec((bq, LANES), lambda g, i: (i, vcol + g)),
                  pl.BlockSpec((WINDOW, LANES), lambda g, i: (nxt(i), vcol + g))],
        out_specs=pl.BlockSpec((bq, qw), lambda g, i: (i, g)),
        out_shape=jax.ShapeDtypeStruct((s, GQA_HEADS * GQA_DIM), BF16),
        compiler_params=_cparams(("parallel", "parallel")),
        name="window_gqa",
    )(sink_flat, gqk, gqk, gqk, gqk, vv, vv, vv)


def _merge_kernel(u_ref, o0_ref, o1_ref, o2_ref, wg0_ref, wg1_ref, wg2_ref, b0_ref, b1_ref, b2_ref,
                  wb0_ref, wb1_ref, wb2_ref, out_ref, wg0_s, wg1_s, wg2_s, wb0_s, wb1_s, wb2_s):
    _cast_weights_once(((wg0_ref, wg0_s), (wg1_ref, wg1_s), (wg2_ref, wg2_s),
                        (wb0_ref, wb0_s), (wb1_ref, wb1_s), (wb2_ref, wb2_s)))
    u = u_ref[...]
    acc = None
    for o_ref, wg_s, b_ref, wb_s in ((o0_ref, wg0_s, b0_ref, wb0_s),
                                     (o1_ref, wg1_s, b1_ref, wb1_s),
                                     (o2_ref, wg2_s, b2_ref, wb2_s)):
        gate = jax.nn.sigmoid(jnp.dot(u, wg_s[...], preferred_element_type=F32) + b_ref[...])
        term = jnp.dot(o_ref[...], wb_s[...], preferred_element_type=F32) * gate
        acc = term if acc is None else acc + term
    out_ref[...] = acc.astype(out_ref.dtype)


def _merge(u, o_diff, o_mla, o_gqa, w_gate, b_gate, w_branch, l):
    m, d = u.shape
    bw = o_diff.shape[1]
    bm = _pick(m, 1024)
    bn = _pick(d, 256)
    nj = d // bn
    o_spec = pl.BlockSpec((bm, bw), lambda j, i: (i, 0))

    def wg_spec(n):
        return pl.BlockSpec((None, d, bn), lambda j, i: (l, 0, n * nj + j))

    def b_spec(n):
        return pl.BlockSpec((None, 1, bn), lambda j, i: (l, 0, n * nj + j))

    def wb_spec(n):
        return pl.BlockSpec((None, None, bw, bn), lambda j, i: (l, n, 0, j))

    return pl.pallas_call(
        _merge_kernel,
        grid=(nj, m // bm),
        in_specs=[pl.BlockSpec((bm, d), lambda j, i: (i, 0)), o_spec, o_spec, o_spec,
                  wg_spec(0), wg_spec(1), wg_spec(2), b_spec(0), b_spec(1), b_spec(2),
                  wb_spec(0), wb_spec(1), wb_spec(2)],
        out_specs=pl.BlockSpec((bm, bn), lambda j, i: (i, j)),
        out_shape=jax.ShapeDtypeStruct((m, d), BF16),
        scratch_shapes=[pltpu.VMEM((d, bn), BF16)] * 3 + [pltpu.VMEM((bw, bn), BF16)] * 3,
        compiler_params=_cparams(("arbitrary", "arbitrary")),
        name="branch_merge",
    )(u, o_diff, o_mla, o_gqa, w_gate, w_gate, w_gate, b_gate, b_gate, b_gate,
      w_branch, w_branch, w_branch)


def _xattn_kernel(h_ref, g_ref, wq_ref, k_ref, v_ref, wo_ref, o_ref, ob_ref, wq_s, wo_s):
    @pl.when(pl.program_id(0) == 0)
    def _():
        wq_s[...] = wq_ref[...].astype(BF16)
        wo_s[...] = wo_ref[...].astype(BF16)

    hn = _norm_rows(h_ref[...], g_ref[...])
    q = jnp.dot(hn, wq_s[...], preferred_element_type=F32) * (XA_DIM ** -0.5 * LOG2E)
    q = q.astype(BF16)
    heads = []
    for hh in range(XA_HEADS):
        sl = slice(hh * XA_DIM, (hh + 1) * XA_DIM)
        s = lax.dot_general(q[:, sl], k_ref[:, sl], (((1,), (1,)), ((), ())), preferred_element_type=F32)
        p = jnp.exp2(s - jnp.max(s, axis=1, keepdims=True))
        denom = jnp.sum(p, axis=1, keepdims=True)
        heads.append((jnp.dot(p.astype(BF16), v_ref[:, sl], preferred_element_type=F32) / denom).astype(BF16))
    o = jnp.concatenate(heads, axis=1)
    out = h_ref[...] + jnp.dot(o, wo_s[...], preferred_element_type=F32)
    o_ref[...] = out
    ob_ref[...] = out.astype(BF16)


def _cross_attention(h, gain, xa_wq, memkv, xa_wo, l):
    m, d = h.shape
    w = XA_HEADS * XA_DIM
    ml = memkv.shape[0]
    bm = _pick(m, 512)
    return pl.pallas_call(
        _xattn_kernel,
        grid=(m // bm,),
        in_specs=[pl.BlockSpec((bm, d), lambda i: (i, 0)),
                  pl.BlockSpec((None, 1, d), lambda i: (l, 0, 0)),
                  pl.BlockSpec((None, d, w), lambda i: (l, 0, 0)),
                  pl.BlockSpec((ml, w), lambda i: (0, 0)),
                  pl.BlockSpec((ml, w), lambda i: (0, 1)),
                  pl.BlockSpec((None, w, d), lambda i: (l, 0, 0))],
        out_specs=[pl.BlockSpec((bm, d), lambda i: (i, 0))] * 2,
        out_shape=[jax.ShapeDtypeStruct((m, d), F32), jax.ShapeDtypeStruct((m, d), BF16)],
        scratch_shapes=[pltpu.VMEM((d, w), BF16), pltpu.VMEM((w, d), BF16)],
        compiler_params=_cparams(("arbitrary",)),
        name="cross_attention",
    )(h, gain, xa_wq, memkv, memkv, xa_wo)


def _pad_cols(w, width):
    return jnp.pad(w, ((0, 0),) * (w.ndim - 1) + ((0, width - w.shape[-1]),))


def kernel(x, mem, positions, ffn1_norm, ffn1_w_gu, ffn1_w_down, mix_norm, w_in, diff_lambda, diff_subln,
           mla_q_norm, mla_kv_norm, mla_w_uq, mla_w_ukv, gqa_sink, w_branch, w_gate, b_gate, w_o, xa_norm,
           mem_norm, xa_wq, xa_wkv, xa_wo, ffn2_norm, ffn2_w_gu, ffn2_w_down, final_norm):
    b, s, d = x.shape
    assert b == 1
    depth = ffn1_norm.shape[0]
    h = x.reshape(s, d)
    mem2 = mem.reshape(mem.shape[1], d)

    pos_b = jnp.broadcast_to(positions.reshape(s, 1).astype(F32), (s, LANES))
    cos64, sin64 = _rope_tables(pos_b, DIFF_QK_DIM)
    cos128, sin128 = _rope_tables(pos_b, GQA_DIM)

    bf = lambda w: w.astype(BF16)
    xa_wkv_b = bf(xa_wkv)
    b_gate3 = b_gate.reshape(depth, 1, -1)
    norm3 = lambda g: g.reshape(depth, 1, -1)
    mix_norm3, xa_norm3, mem_norm3 = map(norm3, (mix_norm, xa_norm, mem_norm))
    ffn1_norm_col, ffn2_norm_col = ffn1_norm.reshape(depth, d, 1), ffn2_norm.reshape(depth, d, 1)
    hb = bf(h)
    diff_subln3 = diff_subln.reshape(depth, 1, -1)
    sink_flat = gqa_sink.reshape(-1)

    o_dq, o_dk, o_dv = 0, 1024, 2048
    o_cq, o_ckv, o_kr = 3072, 3584, 4096
    o_gq, o_gk, o_gv = 4160, 5184, 5440
    diff_scale = DIFF_QK_DIM ** -0.5 * LOG2E
    mla_scale = (MLA_NOPE + MLA_ROPE) ** -0.5 * LOG2E
    gqa_scale = GQA_DIM ** -0.5 * LOG2E
    cs_a = jnp.concatenate([jnp.full((1, 1024), diff_scale, F32), jnp.ones((1, 1024 + LANES), F32)], axis=1)
    cs_g = jnp.concatenate([jnp.full((1, 1024), gqa_scale, F32), jnp.ones((1, 256), F32)], axis=1)
    cs_q = jnp.full((1, MLA_HEADS * 2 * LANES), mla_scale, F32)

    w_a = bf(jnp.concatenate([w_in[:, :, o_dq:o_dv], _pad_cols(w_in[:, :, o_kr:o_gq], LANES)], axis=2))
    w_v = bf(jnp.concatenate([w_in[:, :, o_dv:o_cq], w_in[:, :, o_gv:]], axis=2))
    w_c = bf(w_in[:, :, o_cq:o_kr])
    w_g = bf(w_in[:, :, o_gq:o_gv])
    wuq = mla_w_uq.reshape(depth, MLA_Q_RANK, MLA_HEADS, MLA_NOPE + MLA_ROPE)
    wuq = _pad_cols(wuq, 2 * LANES).reshape(depth, MLA_Q_RANK, MLA_HEADS * 2 * LANES)
    wuq, wukv = bf(wuq), bf(mla_w_ukv)
    mla_gains = jnp.stack([mla_q_norm, mla_kv_norm], axis=1)[:, :, None, :]

    for l in range(depth):
        lambda_init = 0.8 - 0.6 * math.exp(-0.3 * l)

        a = _ffn_up(hb, ffn1_norm_col, ffn1_w_gu, l)
        h = _mm_residual(a, ffn1_w_down, l, h, 0.5)

        u = _rmsnorm(h, mix_norm3, l)
        qk = _mm_rope(u, w_a, l, cs_a, cos64, sin64, DIFF_QK_DIM // 2, (True,) * 17)
        vv = _mm_plain(u, w_v, l)
        cn = _mm_rmsnorm(u, w_c, mla_gains, l)
        gqk = _mm_rope(u, w_g, l, cs_g, cos128, sin128, GQA_DIM // 2, (True,) * 10)

        o_diff = _diff_attention(qk, vv, diff_lambda, diff_subln3, l, lambda_init)

        q_mla = _mm_rope(cn[0], wuq, l, cs_q, cos64, sin64, MLA_ROPE // 2, (False, True) * MLA_HEADS)
        kv_up = _mm_plain(cn[1], wukv, l)
        o_mla = _mla_attention(q_mla, kv_up, qk)

        o_gqa = _window_gqa(gqk, vv, sink_flat, l)

        merged = _merge(u, o_diff, o_mla, o_gqa, w_gate, b_gate3, w_branch, l)
        h = _mm_residual(merged, w_o, l, h, 1.0)

        memkv = _mm_plain(_rmsnorm(mem2, mem_norm3, l), xa_wkv_b, l)
        h, hb = _cross_attention(h, xa_norm3, xa_wq, memkv, xa_wo, l)

        a = _ffn_up(hb, ffn2_norm_col, ffn2_w_gu, l)
        if l + 1 < depth:
            h, hb = _mm_residual(a, ffn2_w_down, l, h, 0.5, emit_bf16=True)
        else:
            h = _mm_residual(a, ffn2_w_down, l, h, 0.5)

    out = _rmsnorm(h, final_norm.reshape(1, d), None, out_dtype=F32)
    return out.reshape(b, s, d)
```

```python
import functools
import math

import jax
import jax.numpy as jnp
from jax import lax
from jax.experimental import pallas as pl
from jax.experimental.pallas import tpu as pltpu

F32 = jnp.float32
BF16 = jnp.bfloat16

DIFF_HEADS = 8
DIFF_QK_DIM = 64
DIFF_V_DIM = 128
MLA_HEADS = 8
MLA_Q_RANK = 512
MLA_KV_RANK = 512
MLA_NOPE = 128
MLA_ROPE = 64
MLA_V = 128
GQA_HEADS = 8
GQA_KV_HEADS = 2
GQA_GROUP = GQA_HEADS // GQA_KV_HEADS
GQA_DIM = 128
WINDOW = 128
N_BRANCH = 3
BRANCH_WIDTH = 1024
XA_HEADS = 4
XA_DIM = 128
D_FF = 5632
ROPE_THETA = 10000.0
EPS = 1e-6
DIFF_SUBLN_EPS = 1e-5
NEG_INF = -1e30
LOG2E = math.log2(math.e)

LANES = 128
V7X_VMEM_LIMIT_BYTES = 56 * 1024 * 1024


def _cparams(semantics):
    return pltpu.CompilerParams(dimension_semantics=semantics, vmem_limit_bytes=V7X_VMEM_LIMIT_BYTES)


def _pick(dim, pref):
    if dim <= pref:
        return dim
    b = pref
    while dim % b:
        b //= 2
    return b


def _rope_table_kernel(pos_ref, inv_ref, sign_ref, cos_ref, sin_ref):
    ang = pos_ref[...] * inv_ref[...]
    cos_ref[...] = jnp.cos(ang)
    sin_ref[...] = jnp.sin(ang) * sign_ref[...]


def _rope_tables(pos_b, dim):
    s = pos_b.shape[0]
    half = dim // 2
    inv = ROPE_THETA ** (-jnp.arange(0, dim, 2, dtype=F32) / dim)
    inv_row = jnp.tile(inv, LANES // half)[None, :]
    sign_row = jnp.tile(jnp.concatenate([-jnp.ones((half,), F32), jnp.ones((half,), F32)]), LANES // dim)[None, :]
    bm = _pick(s, 1024)
    return pl.pallas_call(
        _rope_table_kernel,
        grid=(s // bm,),
        in_specs=[pl.BlockSpec((bm, LANES), lambda i: (i, 0)),
                  pl.BlockSpec((1, LANES), lambda i: (0, 0)),
                  pl.BlockSpec((1, LANES), lambda i: (0, 0))],
        out_specs=[pl.BlockSpec((bm, LANES), lambda i: (i, 0))] * 2,
        out_shape=[jax.ShapeDtypeStruct((s, LANES), F32)] * 2,
        compiler_params=_cparams(("parallel",)),
        name="rope_tables",
    )(pos_b, inv_row, sign_row)


def _rope_block(x, cos, sin_signed, half):
    if 2 * half == LANES:
        partner = pltpu.roll(x, half, axis=1)
    else:
        lane = lax.broadcasted_iota(jnp.int32, x.shape, 1)
        first = (lane % (2 * half)) < half
        partner = jnp.where(first, pltpu.roll(x, LANES - half, axis=1), pltpu.roll(x, half, axis=1))
    return x * cos + partner * sin_signed


def _rmsnorm_kernel(x_ref, g_ref, o_ref, *, eps):
    x = x_ref[...]
    y = x * lax.rsqrt(jnp.mean(x * x, axis=-1, keepdims=True) + eps)
    o_ref[...] = (y * g_ref[...]).astype(o_ref.dtype)


def _rmsnorm(x, g_stack, l, out_dtype=BF16):
    m, d = x.shape
    bm = _pick(m, 512)
    if l is None:
        g_spec = pl.BlockSpec((1, d), lambda i: (0, 0))
    else:
        g_spec = pl.BlockSpec((None, 1, d), lambda i: (l, 0, 0))
    return pl.pallas_call(
        functools.partial(_rmsnorm_kernel, eps=EPS),
        grid=(m // bm,),
        in_specs=[pl.BlockSpec((bm, d), lambda i: (i, 0)), g_spec],
        out_specs=pl.BlockSpec((bm, d), lambda i: (i, 0)),
        out_shape=jax.ShapeDtypeStruct((m, d), out_dtype),
        compiler_params=_cparams(("parallel",)),
        name="rmsnorm",
    )(x, g_stack)


def _cast_weights_once(pairs):
    @pl.when(pl.program_id(1) == 0)
    def _():
        for w_ref, w_s in pairs:
            w_s[...] = w_ref[...].astype(BF16)


def _norm_rows(x, g, eps=EPS):
    return (x * lax.rsqrt(jnp.mean(x * x, axis=-1, keepdims=True) + eps) * g).astype(BF16)


def _mm_swiglu_kernel(x_ref, gc_ref, wg_ref, wu_ref, o_ref, wg_s, wu_s):
    @pl.when(pl.program_id(1) == 0)
    def _():
        gc = gc_ref[...]
        wg_s[...] = (wg_ref[...] * gc).astype(BF16)
        wu_s[...] = (wu_ref[...] * gc).astype(BF16)

    x = x_ref[...]
    xf = x.astype(F32)
    rs = lax.rsqrt(jnp.mean(xf * xf, axis=-1, keepdims=True) + EPS)
    g = jnp.dot(x, wg_s[...], preferred_element_type=F32) * rs
    u = jnp.dot(x, wu_s[...], preferred_element_type=F32) * rs
    o_ref[...] = (g * jax.nn.sigmoid(g) * u).astype(o_ref.dtype)


def _ffn_up(hb, gain_col, w_gu, l):
    m, d = hb.shape
    n = w_gu.shape[-1] // 2
    bm = _pick(m, 1024)
    bn = _pick(n, 512)
    nj = n // bn
    return pl.pallas_call(
        _mm_swiglu_kernel,
        grid=(nj, m // bm),
        in_specs=[pl.BlockSpec((bm, d), lambda j, i: (i, 0)),
                  pl.BlockSpec((None, d, 1), lambda j, i: (l, 0, 0)),
                  pl.BlockSpec((None, d, bn), lambda j, i: (l, 0, j)),
                  pl.BlockSpec((None, d, bn), lambda j, i: (l, 0, j + nj))],
        out_specs=pl.BlockSpec((bm, bn), lambda j, i: (i, j)),
        out_shape=jax.ShapeDtypeStruct((m, n), BF16),
        scratch_shapes=[pltpu.VMEM((d, bn), BF16), pltpu.VMEM((d, bn), BF16)],
        compiler_params=_cparams(("arbitrary", "arbitrary")),
        name="ffn_up",
    )(hb, gain_col, w_gu, w_gu)


def _mm_residual_kernel(a_ref, w_ref, r_ref, o_ref, *rest, scale):
    w_s = rest[-1]
    _cast_weights_once(((w_ref, w_s),))
    out = r_ref[...] + scale * jnp.dot(a_ref[...], w_s[...], preferred_element_type=F32)
    o_ref[...] = out
    if len(rest) == 2:
        rest[0][...] = out.astype(BF16)


def _mm_residual(a, w, l, res, scale, emit_bf16=False):
    m, k = a.shape
    n = w.shape[-1]
    bn = _pick(n, 1024)
    if k <= 2048:
        bm, w_mode = _pick(m, 1024), {}
    else:
        bm, w_mode = _pick(m, 256), dict(pipeline_mode=pl.Buffered(1))
    o_spec = pl.BlockSpec((bm, bn), lambda j, i: (i, j))
    out_shape = jax.ShapeDtypeStruct((m, n), F32)
    return pl.pallas_call(
        functools.partial(_mm_residual_kernel, scale=scale),
        grid=(n // bn, m // bm),
        in_specs=[pl.BlockSpec((bm, k), lambda j, i: (i, 0)),
                  pl.BlockSpec((None, k, bn), lambda j, i: (l, 0, j), **w_mode),
                  o_spec],
        out_specs=[o_spec, o_spec] if emit_bf16 else o_spec,
        out_shape=[out_shape, jax.ShapeDtypeStruct((m, n), BF16)] if emit_bf16 else out_shape,
        scratch_shapes=[pltpu.VMEM((k, bn), BF16)],
        compiler_params=_cparams(("arbitrary", "arbitrary")),
        name="mm_residual",
    )(a, w, res)


def _mm_plain_kernel(x_ref, w_ref, o_ref):
    o_ref[...] = jnp.dot(x_ref[...], w_ref[...], preferred_element_type=F32).astype(o_ref.dtype)


def _mm_plain(x, w, l=None):
    m, k = x.shape
    n = w.shape[-1]
    bm = _pick(m, 512)
    if l is None:
        w_spec = pl.BlockSpec((k, n), lambda i: (0, 0))
    else:
        w_spec = pl.BlockSpec((None, k, n), lambda i: (l, 0, 0))
    return pl.pallas_call(
        _mm_plain_kernel,
        grid=(m // bm,),
        in_specs=[pl.BlockSpec((bm, k), lambda i: (i, 0)), w_spec],
        out_specs=pl.BlockSpec((bm, n), lambda i: (i, 0)),
        out_shape=jax.ShapeDtypeStruct((m, n), BF16),
        compiler_params=_cparams(("parallel",)),
        name="mm_plain",
    )(x, w)


def _mm_rope_kernel(x_ref, w_ref, cs_ref, cos_ref, sin_ref, *rest, half, pattern):
    if len(rest) == 3:
        g_ref, o_ref, u_ref = rest
        x = _norm_rows(x_ref[...], g_ref[...])
        u_ref[...] = x
    else:
        (o_ref,) = rest
        x = x_ref[...]
    acc = jnp.dot(x, w_ref[...], preferred_element_type=F32) * cs_ref[...]
    cos = cos_ref[...]
    sin = sin_ref[...]
    for b, roped in enumerate(pattern):
        blk = acc[:, b * LANES:(b + 1) * LANES]
        if roped:
            blk = _rope_block(blk, cos, sin, half)
        o_ref[:, b * LANES:(b + 1) * LANES] = blk.astype(o_ref.dtype)


def _mm_rope(x, w, l, colscale, cos, sin, half, pattern, gain=None):
    m, k = x.shape
    n = w.shape[-1]
    assert n == LANES * len(pattern)
    bm = _pick(m, 512)
    x_spec = pl.BlockSpec((bm, k), lambda i: (i, 0))
    o_spec = pl.BlockSpec((bm, n), lambda i: (i, 0))
    out_shape = jax.ShapeDtypeStruct((m, n), BF16)
    in_specs = [x_spec,
                pl.BlockSpec((None, k, n), lambda i: (l, 0, 0)),
                pl.BlockSpec((1, n), lambda i: (0, 0)),
                pl.BlockSpec((bm, LANES), lambda i: (i, 0)),
                pl.BlockSpec((bm, LANES), lambda i: (i, 0))]
    args = (x, w, colscale, cos, sin)
    if gain is not None:
        in_specs.append(pl.BlockSpec((None, 1, k), lambda i: (l, 0, 0)))
        args += (gain,)
    return pl.pallas_call(
        functools.partial(_mm_rope_kernel, half=half, pattern=pattern),
        grid=(m // bm,),
        in_specs=in_specs,
        out_specs=o_spec if gain is None else [o_spec, x_spec],
        out_shape=out_shape if gain is None else [out_shape, jax.ShapeDtypeStruct((m, k), BF16)],
        compiler_params=_cparams(("parallel",)),
        name="mm_rope",
    )(*args)


def _mm_rmsnorm_kernel(x_ref, w_ref, g_ref, o_ref):
    acc = jnp.dot(x_ref[...], w_ref[...], preferred_element_type=F32)
    y = acc * lax.rsqrt(jnp.mean(acc * acc, axis=-1, keepdims=True) + EPS)
    o_ref[...] = (y * g_ref[...]).astype(o_ref.dtype)


def _mm_rmsnorm(x, w, g, l):
    m, k = x.shape
    _, t, _, r = g.shape
    bm = _pick(m, 512)
    return pl.pallas_call(
        _mm_rmsnorm_kernel,
        grid=(t, m // bm),
        in_specs=[pl.BlockSpec((bm, k), lambda j, i: (i, 0)),
                  pl.BlockSpec((None, k, r), lambda j, i: (l, 0, j)),
                  pl.BlockSpec((None, None, 1, r), lambda j, i: (l, j, 0, 0))],
        out_specs=pl.BlockSpec((None, bm, r), lambda j, i: (j, i, 0)),
        out_shape=jax.ShapeDtypeStruct((t, m, r), BF16),
        compiler_params=_cparams(("parallel", "parallel")),
        name="mm_rmsnorm",
    )(x, w, g)


def _logits(q, k, mlane):
    s = lax.dot_general(q, k, (((1,), (1,)), ((), ())), preferred_element_type=F32)
    for b in range(s.shape[1] // LANES):
        blk = s[:, b * LANES:(b + 1) * LANES]
        mlane = blk if mlane is None else jnp.maximum(mlane, blk)
    return s, mlane


def _row_max(mlane):
    return jnp.broadcast_to(jnp.max(mlane, axis=1, keepdims=True), mlane.shape)


def _weigh(s, m, v1, acc):
    ps = [jnp.exp2(s[:, b * LANES:(b + 1) * LANES] - m).astype(BF16) for b in range(s.shape[1] // LANES)]
    pv = jnp.dot(jnp.concatenate(ps, axis=1), v1, preferred_element_type=F32)
    return pv if acc is None else acc + pv


def _softmax_pv(qs, k_chunk, v1_chunk, nchunks):
    n = len(qs)
    logits = [[None] * nchunks for _ in range(n)]
    acc = [None] * n
    mlane = None
    for j in range(nchunks):
        logits[0][j], mlane = _logits(qs[0], k_chunk(j), mlane)
    m = _row_max(mlane)
    for p in range(1, n):
        mlane = None
        for j in range(nchunks):
            acc[p - 1] = _weigh(logits[p - 1][j], m, v1_chunk(j), acc[p - 1])
            logits[p][j], mlane = _logits(qs[p], k_chunk(j), mlane)
        m = _row_max(mlane)
    for j in range(nchunks):
        acc[n - 1] = _weigh(logits[n - 1][j], m, v1_chunk(j), acc[n - 1])
    return [a[:, :a.shape[1] - LANES] / a[:, a.shape[1] - LANES:] for a in acc]


def _attn_tiles(s):
    bq = _pick(s, 256)
    return bq, (2 if s % (2 * bq) == 0 else 1), _pick(s, 1024)


def _diff_attn_kernel(lp_ref, q_ref, k_ref, v_ref, g_ref, o_ref, *, lambda_init, tk, nsub):
    bq = q_ref.shape[0] // nsub
    ones = jnp.ones((tk, LANES), BF16)
    k_chunk = lambda j: k_ref[j * tk:(j + 1) * tk, :]
    v1_chunk = lambda j: jnp.concatenate([v_ref[j * tk:(j + 1) * tk, :], ones], axis=1)
    qs = []
    for r in range(nsub):
        q = q_ref[r * bq:(r + 1) * bq, :]
        lane = lax.broadcasted_iota(jnp.int32, q.shape, 1)
        zero = jnp.zeros_like(q)
        qs.append(jnp.where(lane < DIFF_QK_DIM, q, zero))
        qs.append(jnp.where(lane >= DIFF_QK_DIM, q, zero))
    outs = _softmax_pv(qs, k_chunk, v1_chunk, k_ref.shape[0] // tk)
    lp = lp_ref[...]
    lam = (jnp.exp(jnp.sum(lp[0:1] * lp[1:2], axis=1, keepdims=True))
           - jnp.exp(jnp.sum(lp[2:3] * lp[3:4], axis=1, keepdims=True)) + lambda_init)
    for r in range(nsub):
        o = outs[2 * r] - lam * outs[2 * r + 1]
        y = o * lax.rsqrt(jnp.mean(o * o, axis=-1, keepdims=True) + DIFF_SUBLN_EPS)
        o_ref[r * bq:(r + 1) * bq, :] = (y * g_ref[...] * (1.0 - lambda_init)).astype(o_ref.dtype)


def _diff_attention(qk, v, diff_lambda, diff_subln, l, lambda_init):
    s = qk.shape[0]
    bq, nsub, tk = _attn_tiles(s)
    rows = bq * nsub
    h = DIFF_HEADS
    return pl.pallas_call(
        functools.partial(_diff_attn_kernel, lambda_init=lambda_init, tk=tk, nsub=nsub),
        grid=(h, s // rows),
        in_specs=[pl.BlockSpec((None, 4, DIFF_QK_DIM), lambda hh, i: (l, 0, 0)),
                  pl.BlockSpec((rows, LANES), lambda hh, i: (i, hh)),
                  pl.BlockSpec((s, LANES), lambda hh, i: (0, h + hh)),
                  pl.BlockSpec((s, LANES), lambda hh, i: (0, hh)),
                  pl.BlockSpec((None, 1, DIFF_V_DIM), lambda hh, i: (l, 0, 0))],
        out_specs=pl.BlockSpec((rows, LANES), lambda hh, i: (i, hh)),
        out_shape=jax.ShapeDtypeStruct((s, h * DIFF_V_DIM), BF16),
        compiler_params=_cparams(("parallel", "parallel")),
        name="diff_attention",
    )(diff_lambda, qk, qk, v, diff_subln)


def _mla_attn_kernel(q_ref, kn_ref, kr_ref, v_ref, o_ref, *, tk, nsub):
    bq = q_ref.shape[0] // nsub
    ones = jnp.ones((tk, LANES), BF16)
    k_chunk = lambda j: jnp.concatenate([kn_ref[j * tk:(j + 1) * tk, :], kr_ref[j * tk:(j + 1) * tk, :]], axis=1)
    v1_chunk = lambda j: jnp.concatenate([v_ref[j * tk:(j + 1) * tk, :], ones], axis=1)
    qs = [q_ref[r * bq:(r + 1) * bq, :] for r in range(nsub)]
    outs = _softmax_pv(qs, k_chunk, v1_chunk, kn_ref.shape[0] // tk)
    for r in range(nsub):
        o_ref[r * bq:(r + 1) * bq, :] = outs[r].astype(o_ref.dtype)


def _mla_attention(q, kv_up, qk):
    s = q.shape[0]
    bq, nsub, tk = _attn_tiles(s)
    rows = bq * nsub
    h = MLA_HEADS
    kr_blk = 2 * DIFF_HEADS
    return pl.pallas_call(
        functools.partial(_mla_attn_kernel, tk=tk, nsub=nsub),
        grid=(h, s // rows),
        in_specs=[pl.BlockSpec((rows, 2 * LANES), lambda hh, i: (i, hh)),
                  pl.BlockSpec((s, LANES), lambda hh, i: (0, 2 * hh)),
                  pl.BlockSpec((s, LANES), lambda hh, i: (0, kr_blk)),
                  pl.BlockSpec((s, LANES), lambda hh, i: (0, 2 * hh + 1))],
        out_specs=pl.BlockSpec((rows, LANES), lambda hh, i: (i, hh)),
        out_shape=jax.ShapeDtypeStruct((s, h * MLA_V), BF16),
        compiler_params=_cparams(("parallel", "parallel")),
        name="mla_attention",
    )(q, kv_up, qk, kv_up)


def _gqa_kernel(sink_ref, q_ref, kp_ref, kc_ref, kn_ref, vp_ref, vc_ref, vn_ref, o_ref, *, layer, seq, bq):
    g = pl.program_id(0)
    i = pl.program_id(1)
    nk = bq + 2 * WINDOW
    kb = jnp.concatenate([kp_ref[...], kc_ref[...], kn_ref[...]], axis=0)
    vb = jnp.concatenate([vp_ref[...], vc_ref[...], vn_ref[...]], axis=0)
    v1 = jnp.concatenate([vb, jnp.ones_like(vb)], axis=1)
    qpos = i * bq + lax.broadcasted_iota(jnp.int32, (bq, nk), 0)
    kpos = i * bq - WINDOW + lax.broadcasted_iota(jnp.int32, (bq, nk), 1)
    valid = (jnp.abs(kpos - qpos) <= WINDOW) & (kpos >= 0) & (kpos < seq)
    bias = jnp.where(valid, 0.0, NEG_INF).astype(F32)
    for hh in range(GQA_GROUP):
        q = q_ref[:, hh * GQA_DIM:(hh + 1) * GQA_DIM]
        s = lax.dot_general(q, kb, (((1,), (1,)), ((), ())), preferred_element_type=F32) + bias
        sink = sink_ref[layer * GQA_HEADS + g * GQA_GROUP + hh] * LOG2E
        m = jnp.maximum(jnp.max(s, axis=1, keepdims=True), sink)
        pv = jnp.dot(jnp.exp2(s - m).astype(BF16), v1, preferred_element_type=F32)
        denom = pv[:, GQA_DIM:] + jnp.exp2(sink - m)
        o_ref[:, hh * GQA_DIM:(hh + 1) * GQA_DIM] = (pv[:, :GQA_DIM] / denom).astype(o_ref.dtype)


def _window_gqa(gqk, vv, sink_flat, l):
    s = gqk.shape[0]
    bq = _pick(s, 256)
    assert bq % WINDOW == 0 and WINDOW == LANES
    nb = s // bq
    r = bq // WINDOW
    nh = s // WINDOW
    qw = GQA_GROUP * GQA_DIM
    kcol = GQA_HEADS
    vcol = DIFF_HEADS

    def prev(i):
        return jnp.maximum(i * r - 1, 0)

    def nxt(i):
        return jnp.minimum((i + 1) * r, nh - 1)

    return pl.pallas_call(
        functools.partial(_gqa_kernel, layer=l, seq=s, bq=bq),
        grid=(GQA_KV_HEADS, nb),
        in_specs=[pl.BlockSpec(memory_space=pltpu.SMEM),
                  pl.BlockSpec((bq, qw), lambda g, i: (i, g)),
                  pl.BlockSpec((WINDOW, LANES), lambda g, i: (prev(i), kcol + g)),
                  pl.BlockSpec((bq, LANES), lambda g, i: (i, kcol + g)),
                  pl.BlockSpec((WINDOW, LANES), lambda g, i: (nxt(i), kcol + g)),
                  pl.BlockSpec((WINDOW, LANES), lambda g, i: (prev(i), vcol + g)),
                  pl.BlockSpec((bq, LANES), lambda g, i: (i, vcol + g)),
                  pl.BlockSpec((WINDOW, LANES), lambda g, i: (nxt(i), vcol + g))],
        out_specs=pl.BlockSpec((bq, qw), lambda g, i: (i, g)),
        out_shape=jax.ShapeDtypeStruct((s, GQA_HEADS * GQA_DIM), BF16),
        compiler_params=_cparams(("parallel", "parallel")),
        name="window_gqa",
    )(sink_flat, gqk, gqk, gqk, gqk, vv, vv, vv)


def _merge_kernel(u_ref, o0_ref, o1_ref, o2_ref, wg0_ref, wg1_ref, wg2_ref, b0_ref, b1_ref, b2_ref,
                  wb0_ref, wb1_ref, wb2_ref, out_ref, wg0_s, wg1_s, wg2_s, wb0_s, wb1_s, wb2_s):
    _cast_weights_once(((wg0_ref, wg0_s), (wg1_ref, wg1_s), (wg2_ref, wg2_s),
                        (wb0_ref, wb0_s), (wb1_ref, wb1_s), (wb2_ref, wb2_s)))
    u = u_ref[...]
    acc = None
    for o_ref, wg_s, b_ref, wb_s in ((o0_ref, wg0_s, b0_ref, wb0_s),
                                     (o1_ref, wg1_s, b1_ref, wb1_s),
                                     (o2_ref, wg2_s, b2_ref, wb2_s)):
        gate = jax.nn.sigmoid(jnp.dot(u, wg_s[...], preferred_element_type=F32) + b_ref[...])
        term = jnp.dot(o_ref[...], wb_s[...], preferred_element_type=F32) * gate
        acc = term if acc is None else acc + term
    out_ref[...] = acc.astype(out_ref.dtype)


def _merge(u, o_diff, o_mla, o_gqa, w_gate, b_gate, w_branch, l):
    m, d = u.shape
    bw = o_diff.shape[1]
    bm = _pick(m, 1024)
    bn = _pick(d, 256)
    nj = d // bn
    o_spec = pl.BlockSpec((bm, bw), lambda j, i: (i, 0))

    def wg_spec(n):
        return pl.BlockSpec((None, d, bn), lambda j, i: (l, 0, n * nj + j))

    def b_spec(n):
        return pl.BlockSpec((None, 1, bn), lambda j, i: (l, 0, n * nj + j))

    def wb_spec(n):
        return pl.BlockSpec((None, None, bw, bn), lambda j, i: (l, n, 0, j))

    return pl.pallas_call(
        _merge_kernel,
        grid=(nj, m // bm),
        in_specs=[pl.BlockSpec((bm, d), lambda j, i: (i, 0)), o_spec, o_spec, o_spec,
                  wg_spec(0), wg_spec(1), wg_spec(2), b_spec(0), b_spec(1), b_spec(2),
                  wb_spec(0), wb_spec(1), wb_spec(2)],
        out_specs=pl.BlockSpec((bm, bn), lambda j, i: (i, j)),
        out_shape=jax.ShapeDtypeStruct((m, d), BF16),
        scratch_shapes=[pltpu.VMEM((d, bn), BF16)] * 3 + [pltpu.VMEM((bw, bn), BF16)] * 3,
        compiler_params=_cparams(("arbitrary", "arbitrary")),
        name="branch_merge",
    )(u, o_diff, o_mla, o_gqa, w_gate, w_gate, w_gate, b_gate, b_gate, b_gate,
      w_branch, w_branch, w_branch)


def _xattn_kernel(mg_ref, h_ref, wm_ref, g_ref, wq_ref, k_ref, v_ref, wo_ref, o_ref, ob_ref, wm_s, wq_s, wo_s):
    @pl.when(pl.program_id(0) == 0)
    def _():
        wm_s[...] = wm_ref[...].astype(BF16)
        wq_s[...] = wq_ref[...].astype(BF16)
        wo_s[...] = wo_ref[...].astype(BF16)

    h1 = h_ref[...] + jnp.dot(mg_ref[...], wm_s[...], preferred_element_type=F32)
    hn = _norm_rows(h1, g_ref[...])
    q = jnp.dot(hn, wq_s[...], preferred_element_type=F32) * (XA_DIM ** -0.5 * LOG2E)
    q = q.astype(BF16)
    heads = []
    for hh in range(XA_HEADS):
        sl = slice(hh * XA_DIM, (hh + 1) * XA_DIM)
        s = lax.dot_general(q[:, sl], k_ref[:, sl], (((1,), (1,)), ((), ())), preferred_element_type=F32)
        p = jnp.exp2(s - jnp.max(s, axis=1, keepdims=True))
        denom = jnp.sum(p, axis=1, keepdims=True)
        heads.append((jnp.dot(p.astype(BF16), v_ref[:, sl], preferred_element_type=F32) / denom).astype(BF16))
    o = jnp.concatenate(heads, axis=1)
    out = h1 + jnp.dot(o, wo_s[...], preferred_element_type=F32)
    o_ref[...] = out
    ob_ref[...] = out.astype(BF16)


def _cross_attention(merged, h, w_o, gain, xa_wq, memkv, xa_wo, l):
    m, d = h.shape
    w = XA_HEADS * XA_DIM
    ml = memkv.shape[0]
    bm = _pick(m, 256)
    once = dict(pipeline_mode=pl.Buffered(1))
    return pl.pallas_call(
        _xattn_kernel,
        grid=(m // bm,),
        in_specs=[pl.BlockSpec((bm, d), lambda i: (i, 0)),
                  pl.BlockSpec((bm, d), lambda i: (i, 0)),
                  pl.BlockSpec((None, d, d), lambda i: (l, 0, 0), **once),
                  pl.BlockSpec((None, 1, d), lambda i: (l, 0, 0)),
                  pl.BlockSpec((None, d, w), lambda i: (l, 0, 0), **once),
                  pl.BlockSpec((ml, w), lambda i: (0, 0)),
                  pl.BlockSpec((ml, w), lambda i: (0, 1)),
                  pl.BlockSpec((None, w, d), lambda i: (l, 0, 0), **once)],
        out_specs=[pl.BlockSpec((bm, d), lambda i: (i, 0))] * 2,
        out_shape=[jax.ShapeDtypeStruct((m, d), F32), jax.ShapeDtypeStruct((m, d), BF16)],
        scratch_shapes=[pltpu.VMEM((d, d), BF16), pltpu.VMEM((d, w), BF16), pltpu.VMEM((w, d), BF16)],
        compiler_params=_cparams(("arbitrary",)),
        name="cross_attention",
    )(merged, h, w_o, gain, xa_wq, memkv, memkv, xa_wo)


def _pad_cols(w, width):
    return jnp.pad(w, ((0, 0),) * (w.ndim - 1) + ((0, width - w.shape[-1]),))


def kernel(x, mem, positions, ffn1_norm, ffn1_w_gu, ffn1_w_down, mix_norm, w_in, diff_lambda, diff_subln,
           mla_q_norm, mla_kv_norm, mla_w_uq, mla_w_ukv, gqa_sink, w_branch, w_gate, b_gate, w_o, xa_norm,
           mem_norm, xa_wq, xa_wkv, xa_wo, ffn2_norm, ffn2_w_gu, ffn2_w_down, final_norm):
    b, s, d = x.shape
    assert b == 1
    depth = ffn1_norm.shape[0]
    h = x.reshape(s, d)
    mem2 = mem.reshape(mem.shape[1], d)

    pos_b = jnp.broadcast_to(positions.reshape(s, 1).astype(F32), (s, LANES))
    cos64, sin64 = _rope_tables(pos_b, DIFF_QK_DIM)
    cos128, sin128 = _rope_tables(pos_b, GQA_DIM)

    bf = lambda w: w.astype(BF16)
    xa_wkv_b = bf(xa_wkv)
    b_gate3 = b_gate.reshape(depth, 1, -1)
    norm3 = lambda g: g.reshape(depth, 1, -1)
    mix_norm3, xa_norm3, mem_norm3 = map(norm3, (mix_norm, xa_norm, mem_norm))
    ffn1_norm_col, ffn2_norm_col = ffn1_norm.reshape(depth, d, 1), ffn2_norm.reshape(depth, d, 1)
    hb = bf(h)
    diff_subln3 = diff_subln.reshape(depth, 1, -1)
    sink_flat = gqa_sink.reshape(-1)

    o_dq, o_dk, o_dv = 0, 1024, 2048
    o_cq, o_ckv, o_kr = 3072, 3584, 4096
    o_gq, o_gk, o_gv = 4160, 5184, 5440
    diff_scale = DIFF_QK_DIM ** -0.5 * LOG2E
    mla_scale = (MLA_NOPE + MLA_ROPE) ** -0.5 * LOG2E
    gqa_scale = GQA_DIM ** -0.5 * LOG2E
    cs_a = jnp.concatenate([jnp.full((1, 1024), diff_scale, F32), jnp.ones((1, 1024 + LANES), F32)], axis=1)
    cs_g = jnp.concatenate([jnp.full((1, 1024), gqa_scale, F32), jnp.ones((1, 256), F32)], axis=1)
    cs_q = jnp.full((1, MLA_HEADS * 2 * LANES), mla_scale, F32)

    w_a = bf(jnp.concatenate([w_in[:, :, o_dq:o_dv], _pad_cols(w_in[:, :, o_kr:o_gq], LANES)], axis=2))
    w_v = bf(jnp.concatenate([w_in[:, :, o_dv:o_cq], w_in[:, :, o_gv:]], axis=2))
    w_c = bf(w_in[:, :, o_cq:o_kr])
    w_g = bf(w_in[:, :, o_gq:o_gv])
    wuq = mla_w_uq.reshape(depth, MLA_Q_RANK, MLA_HEADS, MLA_NOPE + MLA_ROPE)
    wuq = _pad_cols(wuq, 2 * LANES).reshape(depth, MLA_Q_RANK, MLA_HEADS * 2 * LANES)
    wuq, wukv = bf(wuq), bf(mla_w_ukv)
    mla_gains = jnp.stack([mla_q_norm, mla_kv_norm], axis=1)[:, :, None, :]

    for l in range(depth):
        lambda_init = 0.8 - 0.6 * math.exp(-0.3 * l)

        a = _ffn_up(hb, ffn1_norm_col, ffn1_w_gu, l)
        h = _mm_residual(a, ffn1_w_down, l, h, 0.5)

        qk, u = _mm_rope(h, w_a, l, cs_a, cos64, sin64, DIFF_QK_DIM // 2, (True,) * 17, gain=mix_norm3)
        vv = _mm_plain(u, w_v, l)
        cn = _mm_rmsnorm(u, w_c, mla_gains, l)
        gqk = _mm_rope(u, w_g, l, cs_g, cos128, sin128, GQA_DIM // 2, (True,) * 10)

        o_diff = _diff_attention(qk, vv, diff_lambda, diff_subln3, l, lambda_init)

        q_mla = _mm_rope(cn[0], wuq, l, cs_q, cos64, sin64, MLA_ROPE // 2, (False, True) * MLA_HEADS)
        kv_up = _mm_plain(cn[1], wukv, l)
        o_mla = _mla_attention(q_mla, kv_up, qk)

        o_gqa = _window_gqa(gqk, vv, sink_flat, l)

        merged = _merge(u, o_diff, o_mla, o_gqa, w_gate, b_gate3, w_branch, l)
        memkv = _mm_plain(_rmsnorm(mem2, mem_norm3, l), xa_wkv_b, l)
        h, hb = _cross_attention(merged, h, w_o, xa_norm3, xa_wq, memkv, xa_wo, l)

        a = _ffn_up(hb, ffn2_norm_col, ffn2_w_gu, l)
        if l + 1 < depth:
            h, hb = _mm_residual(a, ffn2_w_down, l, h, 0.5, emit_bf16=True)
        else:
            h = _mm_residual(a, ffn2_w_down, l, h, 0.5)

    out = _rmsnorm(h, final_norm.reshape(1, d), None, out_dtype=F32)
    return out.reshape(b, s, d)
```

```python
import functools
import math

import jax
import jax.numpy as jnp
from jax import lax
from jax.experimental import pallas as pl
from jax.experimental.pallas import tpu as pltpu

F32 = jnp.float32
BF16 = jnp.bfloat16

DIFF_HEADS = 8
DIFF_QK_DIM = 64
DIFF_V_DIM = 128
MLA_HEADS = 8
MLA_Q_RANK = 512
MLA_KV_RANK = 512
MLA_NOPE = 128
MLA_ROPE = 64
MLA_V = 128
GQA_HEADS = 8
GQA_KV_HEADS = 2
GQA_GROUP = GQA_HEADS // GQA_KV_HEADS
GQA_DIM = 128
WINDOW = 128
N_BRANCH = 3
BRANCH_WIDTH = 1024
XA_HEADS = 4
XA_DIM = 128
D_FF = 5632
ROPE_THETA = 10000.0
EPS = 1e-6
DIFF_SUBLN_EPS = 1e-5
NEG_INF = -1e30
LOG2E = math.log2(math.e)

LANES = 128
V7X_VMEM_LIMIT_BYTES = 56 * 1024 * 1024


def _cparams(semantics):
    return pltpu.CompilerParams(dimension_semantics=semantics, vmem_limit_bytes=V7X_VMEM_LIMIT_BYTES)


def _pick(dim, pref):
    if dim <= pref:
        return dim
    b = pref
    while dim % b:
        b //= 2
    return b


def _rope_table_kernel(pos_ref, inv_ref, sign_ref, cos_ref, sin_ref):
    ang = pos_ref[...] * inv_ref[...]
    cos_ref[...] = jnp.cos(ang)
    sin_ref[...] = jnp.sin(ang) * sign_ref[...]


def _rope_tables(pos_b, dim):
    s = pos_b.shape[0]
    half = dim // 2
    inv = ROPE_THETA ** (-jnp.arange(0, dim, 2, dtype=F32) / dim)
    inv_row = jnp.tile(inv, LANES // half)[None, :]
    sign_row = jnp.tile(jnp.concatenate([-jnp.ones((half,), F32), jnp.ones((half,), F32)]), LANES // dim)[None, :]
    bm = _pick(s, 1024)
    return pl.pallas_call(
        _rope_table_kernel,
        grid=(s // bm,),
        in_specs=[pl.BlockSpec((bm, LANES), lambda i: (i, 0)),
                  pl.BlockSpec((1, LANES), lambda i: (0, 0)),
                  pl.BlockSpec((1, LANES), lambda i: (0, 0))],
        out_specs=[pl.BlockSpec((bm, LANES), lambda i: (i, 0))] * 2,
        out_shape=[jax.ShapeDtypeStruct((s, LANES), F32)] * 2,
        compiler_params=_cparams(("parallel",)),
        name="rope_tables",
    )(pos_b, inv_row, sign_row)


def _rope_block(x, cos, sin_signed, half):
    if 2 * half == LANES:
        partner = pltpu.roll(x, half, axis=1)
    else:
        lane = lax.broadcasted_iota(jnp.int32, x.shape, 1)
        first = (lane % (2 * half)) < half
        partner = jnp.where(first, pltpu.roll(x, LANES - half, axis=1), pltpu.roll(x, half, axis=1))
    return x * cos + partner * sin_signed


def _rmsnorm_kernel(x_ref, g_ref, o_ref, *, eps):
    x = x_ref[...]
    y = x * lax.rsqrt(jnp.mean(x * x, axis=-1, keepdims=True) + eps)
    o_ref[...] = (y * g_ref[...]).astype(o_ref.dtype)


def _rmsnorm(x, g_stack, l, out_dtype=BF16):
    m, d = x.shape
    bm = _pick(m, 512)
    if l is None:
        g_spec = pl.BlockSpec((1, d), lambda i: (0, 0))
    else:
        g_spec = pl.BlockSpec((None, 1, d), lambda i: (l, 0, 0))
    return pl.pallas_call(
        functools.partial(_rmsnorm_kernel, eps=EPS),
        grid=(m // bm,),
        in_specs=[pl.BlockSpec((bm, d), lambda i: (i, 0)), g_spec],
        out_specs=pl.BlockSpec((bm, d), lambda i: (i, 0)),
        out_shape=jax.ShapeDtypeStruct((m, d), out_dtype),
        compiler_params=_cparams(("parallel",)),
        name="rmsnorm",
    )(x, g_stack)


def _cast_weights_once(pairs):
    @pl.when(pl.program_id(1) == 0)
    def _():
        for w_ref, w_s in pairs:
            w_s[...] = w_ref[...].astype(BF16)


def _norm_rows(x, g, eps=EPS):
    return (x * lax.rsqrt(jnp.mean(x * x, axis=-1, keepdims=True) + eps) * g).astype(BF16)


def _mm_swiglu_kernel(x_ref, gc_ref, wg_ref, wu_ref, o_ref, wg_s, wu_s, *, sub):
    @pl.when(pl.program_id(1) == 0)
    def _():
        gc = gc_ref[...]
        wg_s[...] = (wg_ref[...] * gc).astype(BF16)
        wu_s[...] = (wu_ref[...] * gc).astype(BF16)

    def rows_tile(t, carry):
        rows = pl.ds(pl.multiple_of(t * sub, sub), sub)
        x = x_ref[rows, :]
        xf = x.astype(F32)
        rs = lax.rsqrt(jnp.mean(xf * xf, axis=-1, keepdims=True) + EPS)
        g = jnp.dot(x, wg_s[...], preferred_element_type=F32) * rs
        u = jnp.dot(x, wu_s[...], preferred_element_type=F32) * rs
        o_ref[rows, :] = (g * jax.nn.sigmoid(g) * u).astype(o_ref.dtype)
        return carry

    lax.fori_loop(0, x_ref.shape[0] // sub, rows_tile, 0)


def _ffn_up(hb, gain_col, w_gu, l):
    m, d = hb.shape
    n = w_gu.shape[-1] // 2
    bm = _pick(m, 2048)
    bn = _pick(n, 512)
    nj = n // bn
    return pl.pallas_call(
        functools.partial(_mm_swiglu_kernel, sub=_pick(bm, 1024)),
        grid=(nj, m // bm),
        in_specs=[pl.BlockSpec((bm, d), lambda j, i: (i, 0)),
                  pl.BlockSpec((None, d, 1), lambda j, i: (l, 0, 0)),
                  pl.BlockSpec((None, d, bn), lambda j, i: (l, 0, j)),
                  pl.BlockSpec((None, d, bn), lambda j, i: (l, 0, j + nj))],
        out_specs=pl.BlockSpec((bm, bn), lambda j, i: (i, j)),
        out_shape=jax.ShapeDtypeStruct((m, n), BF16),
        scratch_shapes=[pltpu.VMEM((d, bn), BF16), pltpu.VMEM((d, bn), BF16)],
        compiler_params=_cparams(("arbitrary", "arbitrary")),
        name="ffn_up",
    )(hb, gain_col, w_gu, w_gu)


def _mm_residual_kernel(a_ref, w_ref, r_ref, o_ref, *rest, scale):
    w_s = rest[-1]
    _cast_weights_once(((w_ref, w_s),))
    out = r_ref[...] + scale * jnp.dot(a_ref[...], w_s[...], preferred_element_type=F32)
    o_ref[...] = out
    if len(rest) == 2:
        rest[0][...] = out.astype(BF16)


def _mm_residual(a, w, l, res, scale, emit_bf16=False):
    m, k = a.shape
    n = w.shape[-1]
    bm, bn = _pick(m, 256), _pick(n, 1024)
    o_spec = pl.BlockSpec((bm, bn), lambda j, i: (i, j))
    out_shape = jax.ShapeDtypeStruct((m, n), F32)
    return pl.pallas_call(
        functools.partial(_mm_residual_kernel, scale=scale),
        grid=(n // bn, m // bm),
        in_specs=[pl.BlockSpec((bm, k), lambda j, i: (i, 0)),
                  pl.BlockSpec((None, k, bn), lambda j, i: (l, 0, j), pipeline_mode=pl.Buffered(1)),
                  o_spec],
        out_specs=[o_spec, o_spec] if emit_bf16 else o_spec,
        out_shape=[out_shape, jax.ShapeDtypeStruct((m, n), BF16)] if emit_bf16 else out_shape,
        scratch_shapes=[pltpu.VMEM((k, bn), BF16)],
        compiler_params=_cparams(("arbitrary", "arbitrary")),
        name="mm_residual",
    )(a, w, res)


def _mm_plain_kernel(x_ref, w_ref, o_ref):
    o_ref[...] = jnp.dot(x_ref[...], w_ref[...], preferred_element_type=F32).astype(o_ref.dtype)


def _mm_plain(x, w, l=None):
    m, k = x.shape
    n = w.shape[-1]
    bm = _pick(m, 512)
    if l is None:
        w_spec = pl.BlockSpec((k, n), lambda i: (0, 0))
    else:
        w_spec = pl.BlockSpec((None, k, n), lambda i: (l, 0, 0))
    return pl.pallas_call(
        _mm_plain_kernel,
        grid=(m // bm,),
        in_specs=[pl.BlockSpec((bm, k), lambda i: (i, 0)), w_spec],
        out_specs=pl.BlockSpec((bm, n), lambda i: (i, 0)),
        out_shape=jax.ShapeDtypeStruct((m, n), BF16),
        compiler_params=_cparams(("parallel",)),
        name="mm_plain",
    )(x, w)


def _mm_rope_kernel(x_ref, w_ref, cs_ref, cos_ref, sin_ref, *rest, half, pattern):
    if len(rest) == 3:
        g_ref, o_ref, u_ref = rest
        x = _norm_rows(x_ref[...], g_ref[...])
        u_ref[...] = x
    else:
        (o_ref,) = rest
        x = x_ref[...]
    acc = jnp.dot(x, w_ref[...], preferred_element_type=F32) * cs_ref[...]
    cos = cos_ref[...]
    sin = sin_ref[...]
    for b, roped in enumerate(pattern):
        blk = acc[:, b * LANES:(b + 1) * LANES]
        if roped:
            blk = _rope_block(blk, cos, sin, half)
        o_ref[:, b * LANES:(b + 1) * LANES] = blk.astype(o_ref.dtype)


def _mm_rope(x, w, l, colscale, cos, sin, half, pattern, gain=None):
    m, k = x.shape
    n = w.shape[-1]
    assert n == LANES * len(pattern)
    bm = _pick(m, 512)
    x_spec = pl.BlockSpec((bm, k), lambda i: (i, 0))
    o_spec = pl.BlockSpec((bm, n), lambda i: (i, 0))
    out_shape = jax.ShapeDtypeStruct((m, n), BF16)
    in_specs = [x_spec,
                pl.BlockSpec((None, k, n), lambda i: (l, 0, 0)),
                pl.BlockSpec((1, n), lambda i: (0, 0)),
                pl.BlockSpec((bm, LANES), lambda i: (i, 0)),
                pl.BlockSpec((bm, LANES), lambda i: (i, 0))]
    args = (x, w, colscale, cos, sin)
    if gain is not None:
        in_specs.append(pl.BlockSpec((None, 1, k), lambda i: (l, 0, 0)))
        args += (gain,)
    return pl.pallas_call(
        functools.partial(_mm_rope_kernel, half=half, pattern=pattern),
        grid=(m // bm,),
        in_specs=in_specs,
        out_specs=o_spec if gain is None else [o_spec, x_spec],
        out_shape=out_shape if gain is None else [out_shape, jax.ShapeDtypeStruct((m, k), BF16)],
        compiler_params=_cparams(("parallel",)),
        name="mm_rope",
    )(*args)


def _mm_rmsnorm_kernel(x_ref, w_ref, g_ref, o_ref):
    acc = jnp.dot(x_ref[...], w_ref[...], preferred_element_type=F32)
    y = acc * lax.rsqrt(jnp.mean(acc * acc, axis=-1, keepdims=True) + EPS)
    o_ref[...] = (y * g_ref[...]).astype(o_ref.dtype)


def _mm_rmsnorm(x, w, g, l):
    m, k = x.shape
    _, t, _, r = g.shape
    bm = _pick(m, 512)
    return pl.pallas_call(
        _mm_rmsnorm_kernel,
        grid=(t, m // bm),
        in_specs=[pl.BlockSpec((bm, k), lambda j, i: (i, 0)),
                  pl.BlockSpec((None, k, r), lambda j, i: (l, 0, j)),
                  pl.BlockSpec((None, None, 1, r), lambda j, i: (l, j, 0, 0))],
        out_specs=pl.BlockSpec((None, bm, r), lambda j, i: (j, i, 0)),
        out_shape=jax.ShapeDtypeStruct((t, m, r), BF16),
        compiler_params=_cparams(("parallel", "parallel")),
        name="mm_rmsnorm",
    )(x, w, g)


def _logits(q, k, mlane):
    s = lax.dot_general(q, k, (((1,), (1,)), ((), ())), preferred_element_type=F32)
    for b in range(s.shape[1] // LANES):
        blk = s[:, b * LANES:(b + 1) * LANES]
        mlane = blk if mlane is None else jnp.maximum(mlane, blk)
    return s, mlane


def _row_max(mlane):
    return jnp.broadcast_to(jnp.max(mlane, axis=1, keepdims=True), mlane.shape)


def _weigh(s, m, v1, acc):
    ps = [jnp.exp2(s[:, b * LANES:(b + 1) * LANES] - m).astype(BF16) for b in range(s.shape[1] // LANES)]
    pv = jnp.dot(jnp.concatenate(ps, axis=1), v1, preferred_element_type=F32)
    return pv if acc is None else acc + pv


def _softmax_pv(qs, k_chunk, v1_chunk, nchunks):
    n = len(qs)
    logits = [[None] * nchunks for _ in range(n)]
    acc = [None] * n
    mlane = None
    for j in range(nchunks):
        logits[0][j], mlane = _logits(qs[0], k_chunk(j), mlane)
    m = _row_max(mlane)
    for p in range(1, n):
        mlane = None
        for j in range(nchunks):
            acc[p - 1] = _weigh(logits[p - 1][j], m, v1_chunk(j), acc[p - 1])
            logits[p][j], mlane = _logits(qs[p], k_chunk(j), mlane)
        m = _row_max(mlane)
    for j in range(nchunks):
        acc[n - 1] = _weigh(logits[n - 1][j], m, v1_chunk(j), acc[n - 1])
    return [a[:, :a.shape[1] - LANES] / a[:, a.shape[1] - LANES:] for a in acc]


def _attn_tiles(s):
    rows = _pick(s, 512)
    return rows, (2 if s % (2 * rows) == 0 else 1), _pick(s, 1024)


def _diff_attn_kernel(lp_ref, q_ref, k_ref, v_ref, g_ref, o_ref, *, lambda_init, tk, nsub):
    bq = q_ref.shape[0] // nsub
    ones = jnp.ones((tk, LANES), BF16)
    k_chunk = lambda j: k_ref[j * tk:(j + 1) * tk, :]
    v1_chunk = lambda j: jnp.concatenate([v_ref[j * tk:(j + 1) * tk, :], ones], axis=1)
    qs = []
    for r in range(nsub):
        q = q_ref[r * bq:(r + 1) * bq, :]
        lane = lax.broadcasted_iota(jnp.int32, q.shape, 1)
        zero = jnp.zeros_like(q)
        qs.append(jnp.concatenate([jnp.where(lane < DIFF_QK_DIM, q, zero),
                                   jnp.where(lane >= DIFF_QK_DIM, q, zero)], axis=0))
    outs = _softmax_pv(qs, k_chunk, v1_chunk, k_ref.shape[0] // tk)
    lp = lp_ref[...]
    lam = (jnp.exp(jnp.sum(lp[0:1] * lp[1:2], axis=1, keepdims=True))
           - jnp.exp(jnp.sum(lp[2:3] * lp[3:4], axis=1, keepdims=True)) + lambda_init)
    for r in range(nsub):
        o = outs[r][:bq] - lam * outs[r][bq:]
        y = o * lax.rsqrt(jnp.mean(o * o, axis=-1, keepdims=True) + DIFF_SUBLN_EPS)
        o_ref[r * bq:(r + 1) * bq, :] = (y * g_ref[...] * (1.0 - lambda_init)).astype(o_ref.dtype)


def _diff_attention(qk, v, diff_lambda, diff_subln, l, lambda_init):
    s = qk.shape[0]
    prob_rows, nsub, tk = _attn_tiles(s)
    rows = prob_rows // 2 * nsub
    h = DIFF_HEADS
    return pl.pallas_call(
        functools.partial(_diff_attn_kernel, lambda_init=lambda_init, tk=tk, nsub=nsub),
        grid=(h, s // rows),
        in_specs=[pl.BlockSpec((None, 4, DIFF_QK_DIM), lambda hh, i: (l, 0, 0)),
                  pl.BlockSpec((rows, LANES), lambda hh, i: (i, hh)),
                  pl.BlockSpec((s, LANES), lambda hh, i: (0, h + hh)),
                  pl.BlockSpec((s, LANES), lambda hh, i: (0, hh)),
                  pl.BlockSpec((None, 1, DIFF_V_DIM), lambda hh, i: (l, 0, 0))],
        out_specs=pl.BlockSpec((rows, LANES), lambda hh, i: (i, hh)),
        out_shape=jax.ShapeDtypeStruct((s, h * DIFF_V_DIM), BF16),
        compiler_params=_cparams(("parallel", "parallel")),
        name="diff_attention",
    )(diff_lambda, qk, qk, v, diff_subln)


def _mla_attn_kernel(q_ref, kn_ref, kr_ref, v_ref, o_ref, *, tk, nsub):
    bq = q_ref.shape[0] // nsub
    ones = jnp.ones((tk, LANES), BF16)
    k_chunk = lambda j: jnp.concatenate([kn_ref[j * tk:(j + 1) * tk, :], kr_ref[j * tk:(j + 1) * tk, :]], axis=1)
    v1_chunk = lambda j: jnp.concatenate([v_ref[j * tk:(j + 1) * tk, :], ones], axis=1)
    qs = [q_ref[r * bq:(r + 1) * bq, :] for r in range(nsub)]
    outs = _softmax_pv(qs, k_chunk, v1_chunk, kn_ref.shape[0] // tk)
    for r in range(nsub):
        o_ref[r * bq:(r + 1) * bq, :] = outs[r].astype(o_ref.dtype)


def _mla_attention(q, kv_up, qk):
    s = q.shape[0]
    bq, nsub, tk = _attn_tiles(s)
    rows = bq * nsub
    h = MLA_HEADS
    kr_blk = 2 * DIFF_HEADS
    return pl.pallas_call(
        functools.partial(_mla_attn_kernel, tk=tk, nsub=nsub),
        grid=(h, s // rows),
        in_specs=[pl.BlockSpec((rows, 2 * LANES), lambda hh, i: (i, hh)),
                  pl.BlockSpec((s, LANES), lambda hh, i: (0, 2 * hh)),
                  pl.BlockSpec((s, LANES), lambda hh, i: (0, kr_blk)),
                  pl.BlockSpec((s, LANES), lambda hh, i: (0, 2 * hh + 1))],
        out_specs=pl.BlockSpec((rows, LANES), lambda hh, i: (i, hh)),
        out_shape=jax.ShapeDtypeStruct((s, h * MLA_V), BF16),
        compiler_params=_cparams(("parallel", "parallel")),
        name="mla_attention",
    )(q, kv_up, qk, kv_up)


def _gqa_kernel(sink_ref, q_ref, kp_ref, kc_ref, kn_ref, vp_ref, vc_ref, vn_ref, o_ref, *, layer, seq, bq):
    g = pl.program_id(0)
    i = pl.program_id(1)
    nk = bq + 2 * WINDOW
    kb = jnp.concatenate([kp_ref[...], kc_ref[...], kn_ref[...]], axis=0)
    vb = jnp.concatenate([vp_ref[...], vc_ref[...], vn_ref[...]], axis=0)
    v1 = jnp.concatenate([vb, jnp.ones_like(vb)], axis=1)
    qpos = i * bq + lax.broadcasted_iota(jnp.int32, (bq, nk), 0)
    kpos = i * bq - WINDOW + lax.broadcasted_iota(jnp.int32, (bq, nk), 1)
    valid = (jnp.abs(kpos - qpos) <= WINDOW) & (kpos >= 0) & (kpos < seq)
    bias = jnp.where(valid, 0.0, NEG_INF).astype(F32)
    for hh in range(GQA_GROUP):
        q = q_ref[:, hh * GQA_DIM:(hh + 1) * GQA_DIM]
        s = lax.dot_general(q, kb, (((1,), (1,)), ((), ())), preferred_element_type=F32) + bias
        sink = sink_ref[layer * GQA_HEADS + g * GQA_GROUP + hh] * LOG2E
        m = jnp.maximum(jnp.max(s, axis=1, keepdims=True), sink)
        pv = jnp.dot(jnp.exp2(s - m).astype(BF16), v1, preferred_element_type=F32)
        denom = pv[:, GQA_DIM:] + jnp.exp2(sink - m)
        o_ref[:, hh * GQA_DIM:(hh + 1) * GQA_DIM] = (pv[:, :GQA_DIM] / denom).astype(o_ref.dtype)


def _window_gqa(gqk, vv, sink_flat, l):
    s = gqk.shape[0]
    bq = _pick(s, 256)
    assert bq % WINDOW == 0 and WINDOW == LANES
    nb = s // bq
    r = bq // WINDOW
    nh = s // WINDOW
    qw = GQA_GROUP * GQA_DIM
    kcol = GQA_HEADS
    vcol = DIFF_HEADS

    def prev(i):
        return jnp.maximum(i * r - 1, 0)

    def nxt(i):
        return jnp.minimum((i + 1) * r, nh - 1)

    return pl.pallas_call(
        functools.partial(_gqa_kernel, layer=l, seq=s, bq=bq),
        grid=(GQA_KV_HEADS, nb),
        in_specs=[pl.BlockSpec(memory_space=pltpu.SMEM),
                  pl.BlockSpec((bq, qw), lambda g, i: (i, g)),
                  pl.BlockSpec((WINDOW, LANES), lambda g, i: (prev(i), kcol + g)),
                  pl.BlockSpec((bq, LANES), lambda g, i: (i, kcol + g)),
                  pl.BlockSpec((WINDOW, LANES), lambda g, i: (nxt(i), kcol + g)),
                  pl.BlockSpec((WINDOW, LANES), lambda g, i: (prev(i), vcol + g)),
                  pl.BlockSpec((bq, LANES), lambda g, i: (i, vcol + g)),
                  pl.BlockSpec((WINDOW, LANES), lambda g, i: (nxt(i), vcol + g))],
        out_specs=pl.BlockSpec((bq, qw), lambda g, i: (i, g)),
        out_shape=jax.ShapeDtypeStruct((s, GQA_HEADS * GQA_DIM), BF16),
        compiler_params=_cparams(("parallel", "parallel")),
        name="window_gqa",
    )(sink_flat, gqk, gqk, gqk, gqk, vv, vv, vv)


def _merge_kernel(u_ref, o0_ref, o1_ref, o2_ref, wg0_ref, wg1_ref, wg2_ref, b0_ref, b1_ref, b2_ref,
                  wb0_ref, wb1_ref, wb2_ref, out_ref, wg0_s, wg1_s, wg2_s, wb0_s, wb1_s, wb2_s):
    _cast_weights_once(((wg0_ref, wg0_s), (wg1_ref, wg1_s), (wg2_ref, wg2_s),
                        (wb0_ref, wb0_s), (wb1_ref, wb1_s), (wb2_ref, wb2_s)))
    u = u_ref[...]
    acc = None
    for o_ref, wg_s, b_ref, wb_s in ((o0_ref, wg0_s, b0_ref, wb0_s),
                                     (o1_ref, wg1_s, b1_ref, wb1_s),
                                     (o2_ref, wg2_s, b2_ref, wb2_s)):
        gate = jax.nn.sigmoid(jnp.dot(u, wg_s[...], preferred_element_type=F32) + b_ref[...])
        term = jnp.dot(o_ref[...], wb_s[...], preferred_element_type=F32) * gate
        acc = term if acc is None else acc + term
    out_ref[...] = acc.astype(out_ref.dtype)


def _merge(u, o_diff, o_mla, o_gqa, w_gate, b_gate, w_branch, l):
    m, d = u.shape
    bw = o_diff.shape[1]
    bm = _pick(m, 1024)
    bn = _pick(d, 256)
    nj = d // bn
    o_spec = pl.BlockSpec((bm, bw), lambda j, i: (i, 0))

    def wg_spec(n):
        return pl.BlockSpec((None, d, bn), lambda j, i: (l, 0, n * nj + j))

    def b_spec(n):
        return pl.BlockSpec((None, 1, bn), lambda j, i: (l, 0, n * nj + j))

    def wb_spec(n):
        return pl.BlockSpec((None, None, bw, bn), lambda j, i: (l, n, 0, j))

    return pl.pallas_call(
        _merge_kernel,
        grid=(nj, m // bm),
        in_specs=[pl.BlockSpec((bm, d), lambda j, i: (i, 0)), o_spec, o_spec, o_spec,
                  wg_spec(0), wg_spec(1), wg_spec(2), b_spec(0), b_spec(1), b_spec(2),
                  wb_spec(0), wb_spec(1), wb_spec(2)],
        out_specs=pl.BlockSpec((bm, bn), lambda j, i: (i, j)),
        out_shape=jax.ShapeDtypeStruct((m, d), BF16),
        scratch_shapes=[pltpu.VMEM((d, bn), BF16)] * 3 + [pltpu.VMEM((bw, bn), BF16)] * 3,
        compiler_params=_cparams(("arbitrary", "arbitrary")),
        name="branch_merge",
    )(u, o_diff, o_mla, o_gqa, w_gate, w_gate, w_gate, b_gate, b_gate, b_gate,
      w_branch, w_branch, w_branch)


def _xattn_kernel(mg_ref, h_ref, wm_ref, g_ref, wq_ref, k_ref, v_ref, wo_ref, o_ref, ob_ref, wm_s, wq_s, wo_s):
    @pl.when(pl.program_id(0) == 0)
    def _():
        wm_s[...] = wm_ref[...].astype(BF16)
        wq_s[...] = wq_ref[...].astype(BF16)
        wo_s[...] = wo_ref[...].astype(BF16)

    h1 = h_ref[...] + jnp.dot(mg_ref[...], wm_s[...], preferred_element_type=F32)
    hn = _norm_rows(h1, g_ref[...])
    q = jnp.dot(hn, wq_s[...], preferred_element_type=F32) * (XA_DIM ** -0.5 * LOG2E)
    q = q.astype(BF16)
    heads = []
    for hh in range(XA_HEADS):
        sl = slice(hh * XA_DIM, (hh + 1) * XA_DIM)
        s = lax.dot_general(q[:, sl], k_ref[:, sl], (((1,), (1,)), ((), ())), preferred_element_type=F32)
        p = jnp.exp2(s - jnp.max(s, axis=1, keepdims=True))
        denom = jnp.sum(p, axis=1, keepdims=True)
        heads.append((jnp.dot(p.astype(BF16), v_ref[:, sl], preferred_element_type=F32) / denom).astype(BF16))
    o = jnp.concatenate(heads, axis=1)
    out = h1 + jnp.dot(o, wo_s[...], preferred_element_type=F32)
    o_ref[...] = out
    ob_ref[...] = out.astype(BF16)


def _cross_attention(merged, h, w_o, gain, xa_wq, memkv, xa_wo, l):
    m, d = h.shape
    w = XA_HEADS * XA_DIM
    ml = memkv.shape[0]
    bm = _pick(m, 256)
    once = dict(pipeline_mode=pl.Buffered(1))
    return pl.pallas_call(
        _xattn_kernel,
        grid=(m // bm,),
        in_specs=[pl.BlockSpec((bm, d), lambda i: (i, 0)),
                  pl.BlockSpec((bm, d), lambda i: (i, 0)),
                  pl.BlockSpec((None, d, d), lambda i: (l, 0, 0), **once),
                  pl.BlockSpec((None, 1, d), lambda i: (l, 0, 0)),
                  pl.BlockSpec((None, d, w), lambda i: (l, 0, 0), **once),
                  pl.BlockSpec((ml, w), lambda i: (0, 0)),
                  pl.BlockSpec((ml, w), lambda i: (0, 1)),
                  pl.BlockSpec((None, w, d), lambda i: (l, 0, 0), **once)],
        out_specs=[pl.BlockSpec((bm, d), lambda i: (i, 0))] * 2,
        out_shape=[jax.ShapeDtypeStruct((m, d), F32), jax.ShapeDtypeStruct((m, d), BF16)],
        scratch_shapes=[pltpu.VMEM((d, d), BF16), pltpu.VMEM((d, w), BF16), pltpu.VMEM((w, d), BF16)],
        compiler_params=_cparams(("arbitrary",)),
        name="cross_attention",
    )(merged, h, w_o, gain, xa_wq, memkv, memkv, xa_wo)


def _pad_cols(w, width):
    return jnp.pad(w, ((0, 0),) * (w.ndim - 1) + ((0, width - w.shape[-1]),))


def kernel(x, mem, positions, ffn1_norm, ffn1_w_gu, ffn1_w_down, mix_norm, w_in, diff_lambda, diff_subln,
           mla_q_norm, mla_kv_norm, mla_w_uq, mla_w_ukv, gqa_sink, w_branch, w_gate, b_gate, w_o, xa_norm,
           mem_norm, xa_wq, xa_wkv, xa_wo, ffn2_norm, ffn2_w_gu, ffn2_w_down, final_norm):
    b, s, d = x.shape
    assert b == 1
    depth = ffn1_norm.shape[0]
    h = x.reshape(s, d)
    mem2 = mem.reshape(mem.shape[1], d)

    pos_b = jnp.broadcast_to(positions.reshape(s, 1).astype(F32), (s, LANES))
    cos64, sin64 = _rope_tables(pos_b, DIFF_QK_DIM)
    cos128, sin128 = _rope_tables(pos_b, GQA_DIM)

    bf = lambda w: w.astype(BF16)
    xa_wkv_b = bf(xa_wkv)
    b_gate3 = b_gate.reshape(depth, 1, -1)
    norm3 = lambda g: g.reshape(depth, 1, -1)
    mix_norm3, xa_norm3, mem_norm3 = map(norm3, (mix_norm, xa_norm, mem_norm))
    ffn1_norm_col, ffn2_norm_col = ffn1_norm.reshape(depth, d, 1), ffn2_norm.reshape(depth, d, 1)
    hb = bf(h)
    diff_subln3 = diff_subln.reshape(depth, 1, -1)
    sink_flat = gqa_sink.reshape(-1)

    o_dq, o_dk, o_dv = 0, 1024, 2048
    o_cq, o_ckv, o_kr = 3072, 3584, 4096
    o_gq, o_gk, o_gv = 4160, 5184, 5440
    diff_scale = DIFF_QK_DIM ** -0.5 * LOG2E
    mla_scale = (MLA_NOPE + MLA_ROPE) ** -0.5 * LOG2E
    gqa_scale = GQA_DIM ** -0.5 * LOG2E
    cs_a = jnp.concatenate([jnp.full((1, 1024), diff_scale, F32), jnp.ones((1, 1024 + LANES), F32)], axis=1)
    cs_g = jnp.concatenate([jnp.full((1, 1024), gqa_scale, F32), jnp.ones((1, 256), F32)], axis=1)
    cs_q = jnp.full((1, MLA_HEADS * 2 * LANES), mla_scale, F32)

    w_a = bf(jnp.concatenate([w_in[:, :, o_dq:o_dv], _pad_cols(w_in[:, :, o_kr:o_gq], LANES)], axis=2))
    w_v = bf(jnp.concatenate([w_in[:, :, o_dv:o_cq], w_in[:, :, o_gv:]], axis=2))
    w_c = bf(w_in[:, :, o_cq:o_kr])
    w_g = bf(w_in[:, :, o_gq:o_gv])
    wuq = mla_w_uq.reshape(depth, MLA_Q_RANK, MLA_HEADS, MLA_NOPE + MLA_ROPE)
    wuq = _pad_cols(wuq, 2 * LANES).reshape(depth, MLA_Q_RANK, MLA_HEADS * 2 * LANES)
    wuq, wukv = bf(wuq), bf(mla_w_ukv)
    mla_gains = jnp.stack([mla_q_norm, mla_kv_norm], axis=1)[:, :, None, :]

    for l in range(depth):
        lambda_init = 0.8 - 0.6 * math.exp(-0.3 * l)

        a = _ffn_up(hb, ffn1_norm_col, ffn1_w_gu, l)
        h = _mm_residual(a, ffn1_w_down, l, h, 0.5)

        qk, u = _mm_rope(h, w_a, l, cs_a, cos64, sin64, DIFF_QK_DIM // 2, (True,) * 17, gain=mix_norm3)
        vv = _mm_plain(u, w_v, l)
        cn = _mm_rmsnorm(u, w_c, mla_gains, l)
        gqk = _mm_rope(u, w_g, l, cs_g, cos128, sin128, GQA_DIM // 2, (True,) * 10)

        o_diff = _diff_attention(qk, vv, diff_lambda, diff_subln3, l, lambda_init)

        q_mla = _mm_rope(cn[0], wuq, l, cs_q, cos64, sin64, MLA_ROPE // 2, (False, True) * MLA_HEADS)
        kv_up = _mm_plain(cn[1], wukv, l)
        o_mla = _mla_attention(q_mla, kv_up, qk)

        o_gqa = _window_gqa(gqk, vv, sink_flat, l)

        merged = _merge(u, o_diff, o_mla, o_gqa, w_gate, b_gate3, w_branch, l)
        memkv = _mm_plain(_rmsnorm(mem2, mem_norm3, l), xa_wkv_b, l)
        h, hb = _cross_attention(merged, h, w_o, xa_norm3, xa_wq, memkv, xa_wo, l)

        a = _ffn_up(hb, ffn2_norm_col, ffn2_w_gu, l)
        if l + 1 < depth:
            h, hb = _mm_residual(a, ffn2_w_down, l, h, 0.5, emit_bf16=True)
        else:
            h = _mm_residual(a, ffn2_w_down, l, h, 0.5)

    out = _rmsnorm(h, final_norm.reshape(1, d), None, out_dtype=F32)
    return out.reshape(b, s, d)
```

```python
import functools
import math

import jax
import jax.numpy as jnp
from jax import lax
from jax.experimental import pallas as pl
from jax.experimental.pallas import tpu as pltpu

F32 = jnp.float32
BF16 = jnp.bfloat16

DIFF_HEADS = 8
DIFF_QK_DIM = 64
DIFF_V_DIM = 128
MLA_HEADS = 8
MLA_Q_RANK = 512
MLA_KV_RANK = 512
MLA_NOPE = 128
MLA_ROPE = 64
MLA_V = 128
GQA_HEADS = 8
GQA_KV_HEADS = 2
GQA_GROUP = GQA_HEADS // GQA_KV_HEADS
GQA_DIM = 128
WINDOW = 128
N_BRANCH = 3
BRANCH_WIDTH = 1024
XA_HEADS = 4
XA_DIM = 128
D_FF = 5632
ROPE_THETA = 10000.0
EPS = 1e-6
DIFF_SUBLN_EPS = 1e-5
NEG_INF = -1e30
LOG2E = math.log2(math.e)

LANES = 128
V7X_VMEM_LIMIT_BYTES = 56 * 1024 * 1024


def _cparams(semantics):
    return pltpu.CompilerParams(dimension_semantics=semantics, vmem_limit_bytes=V7X_VMEM_LIMIT_BYTES)


def _pick(dim, pref):
    if dim <= pref:
        return dim
    b = pref
    while dim % b:
        b //= 2
    return b


def _rope_table_kernel(pos_ref, inv_ref, sign_ref, cos_ref, sin_ref):
    ang = pos_ref[...] * inv_ref[...]
    cos_ref[...] = jnp.cos(ang)
    sin_ref[...] = jnp.sin(ang) * sign_ref[...]


def _rope_tables(pos_b, dim):
    s = pos_b.shape[0]
    half = dim // 2
    inv = ROPE_THETA ** (-jnp.arange(0, dim, 2, dtype=F32) / dim)
    inv_row = jnp.tile(inv, LANES // half)[None, :]
    sign_row = jnp.tile(jnp.concatenate([-jnp.ones((half,), F32), jnp.ones((half,), F32)]), LANES // dim)[None, :]
    bm = _pick(s, 1024)
    return pl.pallas_call(
        _rope_table_kernel,
        grid=(s // bm,),
        in_specs=[pl.BlockSpec((bm, LANES), lambda i: (i, 0)),
                  pl.BlockSpec((1, LANES), lambda i: (0, 0)),
                  pl.BlockSpec((1, LANES), lambda i: (0, 0))],
        out_specs=[pl.BlockSpec((bm, LANES), lambda i: (i, 0))] * 2,
        out_shape=[jax.ShapeDtypeStruct((s, LANES), F32)] * 2,
        compiler_params=_cparams(("parallel",)),
        name="rope_tables",
    )(pos_b, inv_row, sign_row)


def _rope_block(x, cos, sin_signed, half):
    if 2 * half == LANES:
        partner = pltpu.roll(x, half, axis=1)
    else:
        lane = lax.broadcasted_iota(jnp.int32, x.shape, 1)
        first = (lane % (2 * half)) < half
        partner = jnp.where(first, pltpu.roll(x, LANES - half, axis=1), pltpu.roll(x, half, axis=1))
    return x * cos + partner * sin_signed


def _rmsnorm_kernel(x_ref, g_ref, o_ref, *, eps):
    x = x_ref[...]
    y = x * lax.rsqrt(jnp.mean(x * x, axis=-1, keepdims=True) + eps)
    o_ref[...] = (y * g_ref[...]).astype(o_ref.dtype)


def _rmsnorm(x, g_stack, l, out_dtype=BF16):
    m, d = x.shape
    bm = _pick(m, 512)
    if l is None:
        g_spec = pl.BlockSpec((1, d), lambda i: (0, 0))
    else:
        g_spec = pl.BlockSpec((None, 1, d), lambda i: (l, 0, 0))
    return pl.pallas_call(
        functools.partial(_rmsnorm_kernel, eps=EPS),
        grid=(m // bm,),
        in_specs=[pl.BlockSpec((bm, d), lambda i: (i, 0)), g_spec],
        out_specs=pl.BlockSpec((bm, d), lambda i: (i, 0)),
        out_shape=jax.ShapeDtypeStruct((m, d), out_dtype),
        compiler_params=_cparams(("parallel",)),
        name="rmsnorm",
    )(x, g_stack)


def _cast_weights_once(pairs):
    @pl.when(pl.program_id(1) == 0)
    def _():
        for w_ref, w_s in pairs:
            w_s[...] = w_ref[...].astype(BF16)


def _norm_rows(x, g, eps=EPS):
    return (x * lax.rsqrt(jnp.mean(x * x, axis=-1, keepdims=True) + eps) * g).astype(BF16)


def _mm_swiglu_kernel(x_ref, gc_ref, wg_ref, wu_ref, o_ref, wg_s, wu_s, *, sub):
    @pl.when(pl.program_id(1) == 0)
    def _():
        gc = gc_ref[...]
        wg_s[...] = (wg_ref[...] * gc).astype(BF16)
        wu_s[...] = (wu_ref[...] * gc).astype(BF16)

    def rows_tile(t, carry):
        rows = pl.ds(pl.multiple_of(t * sub, sub), sub)
        x = x_ref[rows, :]
        xf = x.astype(F32)
        rs = lax.rsqrt(jnp.mean(xf * xf, axis=-1, keepdims=True) + EPS)
        g = jnp.dot(x, wg_s[...], preferred_element_type=F32) * rs
        u = jnp.dot(x, wu_s[...], preferred_element_type=F32) * rs
        o_ref[rows, :] = (g * jax.nn.sigmoid(g) * u).astype(o_ref.dtype)
        return carry

    lax.fori_loop(0, x_ref.shape[0] // sub, rows_tile, 0)


def _ffn_up(hb, gain_col, w_gu, l):
    m, d = hb.shape
    n = w_gu.shape[-1] // 2
    bm = _pick(m, 2048)
    bn = _pick(n, 512)
    nj = n // bn
    return pl.pallas_call(
        functools.partial(_mm_swiglu_kernel, sub=_pick(bm, 1024)),
        grid=(nj, m // bm),
        in_specs=[pl.BlockSpec((bm, d), lambda j, i: (i, 0)),
                  pl.BlockSpec((None, d, 1), lambda j, i: (l, 0, 0)),
                  pl.BlockSpec((None, d, bn), lambda j, i: (l, 0, j)),
                  pl.BlockSpec((None, d, bn), lambda j, i: (l, 0, j + nj))],
        out_specs=pl.BlockSpec((bm, bn), lambda j, i: (i, j)),
        out_shape=jax.ShapeDtypeStruct((m, n), BF16),
        scratch_shapes=[pltpu.VMEM((d, bn), BF16), pltpu.VMEM((d, bn), BF16)],
        compiler_params=_cparams(("arbitrary", "arbitrary")),
        name="ffn_up",
    )(hb, gain_col, w_gu, w_gu)


def _mm_residual_kernel(a_ref, w_ref, r_ref, o_ref, *rest, scale):
    w_s = rest[-1]
    _cast_weights_once(((w_ref, w_s),))
    out = r_ref[...] + scale * jnp.dot(a_ref[...], w_s[...], preferred_element_type=F32)
    o_ref[...] = out
    if len(rest) == 2:
        rest[0][...] = out.astype(BF16)


def _mm_residual(a, w, l, res, scale, emit_bf16=False):
    m, k = a.shape
    n = w.shape[-1]
    bm, bn = _pick(m, 256), _pick(n, 1024)
    o_spec = pl.BlockSpec((bm, bn), lambda j, i: (i, j))
    out_shape = jax.ShapeDtypeStruct((m, n), F32)
    return pl.pallas_call(
        functools.partial(_mm_residual_kernel, scale=scale),
        grid=(n // bn, m // bm),
        in_specs=[pl.BlockSpec((bm, k), lambda j, i: (i, 0)),
                  pl.BlockSpec((None, k, bn), lambda j, i: (l, 0, j), pipeline_mode=pl.Buffered(1)),
                  o_spec],
        out_specs=[o_spec, o_spec] if emit_bf16 else o_spec,
        out_shape=[out_shape, jax.ShapeDtypeStruct((m, n), BF16)] if emit_bf16 else out_shape,
        scratch_shapes=[pltpu.VMEM((k, bn), BF16)],
        compiler_params=_cparams(("arbitrary", "arbitrary")),
        name="mm_residual",
    )(a, w, res)


def _mm_plain_kernel(x_ref, w_ref, o_ref):
    o_ref[...] = jnp.dot(x_ref[...], w_ref[...], preferred_element_type=F32).astype(o_ref.dtype)


def _mm_plain(x, w, l=None):
    m, k = x.shape
    n = w.shape[-1]
    bm = _pick(m, 512)
    if l is None:
        w_spec = pl.BlockSpec((k, n), lambda i: (0, 0))
    else:
        w_spec = pl.BlockSpec((None, k, n), lambda i: (l, 0, 0))
    return pl.pallas_call(
        _mm_plain_kernel,
        grid=(m // bm,),
        in_specs=[pl.BlockSpec((bm, k), lambda i: (i, 0)), w_spec],
        out_specs=pl.BlockSpec((bm, n), lambda i: (i, 0)),
        out_shape=jax.ShapeDtypeStruct((m, n), BF16),
        compiler_params=_cparams(("parallel",)),
        name="mm_plain",
    )(x, w)


def _mm_rope_kernel(x_ref, w_ref, cs_ref, cos_ref, sin_ref, *rest, half, pattern):
    if len(rest) == 3:
        g_ref, o_ref, u_ref = rest
        x = _norm_rows(x_ref[...], g_ref[...])
        u_ref[...] = x
    else:
        (o_ref,) = rest
        x = x_ref[...]
    acc = jnp.dot(x, w_ref[...], preferred_element_type=F32) * cs_ref[...]
    cos = cos_ref[...]
    sin = sin_ref[...]
    for b, roped in enumerate(pattern):
        blk = acc[:, b * LANES:(b + 1) * LANES]
        if roped:
            blk = _rope_block(blk, cos, sin, half)
        o_ref[:, b * LANES:(b + 1) * LANES] = blk.astype(o_ref.dtype)


def _mm_rope(x, w, l, colscale, cos, sin, half, pattern, gain=None):
    m, k = x.shape
    n = w.shape[-1]
    assert n == LANES * len(pattern)
    bm = _pick(m, 512)
    x_spec = pl.BlockSpec((bm, k), lambda i: (i, 0))
    o_spec = pl.BlockSpec((bm, n), lambda i: (i, 0))
    out_shape = jax.ShapeDtypeStruct((m, n), BF16)
    in_specs = [x_spec,
                pl.BlockSpec((None, k, n), lambda i: (l, 0, 0)),
                pl.BlockSpec((1, n), lambda i: (0, 0)),
                pl.BlockSpec((bm, LANES), lambda i: (i, 0)),
                pl.BlockSpec((bm, LANES), lambda i: (i, 0))]
    args = (x, w, colscale, cos, sin)
    if gain is not None:
        in_specs.append(pl.BlockSpec((None, 1, k), lambda i: (l, 0, 0)))
        args += (gain,)
    return pl.pallas_call(
        functools.partial(_mm_rope_kernel, half=half, pattern=pattern),
        grid=(m // bm,),
        in_specs=in_specs,
        out_specs=o_spec if gain is None else [o_spec, x_spec],
        out_shape=out_shape if gain is None else [out_shape, jax.ShapeDtypeStruct((m, k), BF16)],
        compiler_params=_cparams(("parallel",)),
        name="mm_rope",
    )(*args)


def _mm_rmsnorm_kernel(x_ref, w_ref, g_ref, o_ref):
    acc = jnp.dot(x_ref[...], w_ref[...], preferred_element_type=F32)
    y = acc * lax.rsqrt(jnp.mean(acc * acc, axis=-1, keepdims=True) + EPS)
    o_ref[...] = (y * g_ref[...]).astype(o_ref.dtype)


def _mm_rmsnorm(x, w, g, l):
    m, k = x.shape
    _, t, _, r = g.shape
    bm = _pick(m, 512)
    return pl.pallas_call(
        _mm_rmsnorm_kernel,
        grid=(t, m // bm),
        in_specs=[pl.BlockSpec((bm, k), lambda j, i: (i, 0)),
                  pl.BlockSpec((None, k, r), lambda j, i: (l, 0, j)),
                  pl.BlockSpec((None, None, 1, r), lambda j, i: (l, j, 0, 0))],
        out_specs=pl.BlockSpec((None, bm, r), lambda j, i: (j, i, 0)),
        out_shape=jax.ShapeDtypeStruct((t, m, r), BF16),
        compiler_params=_cparams(("parallel", "parallel")),
        name="mm_rmsnorm",
    )(x, w, g)


def _logits(q, k, mlane):
    s = lax.dot_general(q, k, (((1,), (1,)), ((), ())), preferred_element_type=F32)
    for b in range(s.shape[1] // LANES):
        blk = s[:, b * LANES:(b + 1) * LANES]
        mlane = blk if mlane is None else jnp.maximum(mlane, blk)
    return s, mlane


def _row_max(mlane):
    return jnp.broadcast_to(jnp.max(mlane, axis=1, keepdims=True), mlane.shape)


def _weigh(s, m, v1, acc):
    ps = [jnp.exp2(s[:, b * LANES:(b + 1) * LANES] - m).astype(BF16) for b in range(s.shape[1] // LANES)]
    pv = jnp.dot(jnp.concatenate(ps, axis=1), v1, preferred_element_type=F32)
    return pv if acc is None else acc + pv


def _softmax_pv(qs, k_chunk, v1_chunk, nchunks):
    n = len(qs)
    logits = [[None] * nchunks for _ in range(n)]
    acc = [None] * n
    mlane = None
    for j in range(nchunks):
        logits[0][j], mlane = _logits(qs[0], k_chunk(j), mlane)
    m = _row_max(mlane)
    for p in range(1, n):
        mlane = None
        for j in range(nchunks):
            acc[p - 1] = _weigh(logits[p - 1][j], m, v1_chunk(j), acc[p - 1])
            logits[p][j], mlane = _logits(qs[p], k_chunk(j), mlane)
        m = _row_max(mlane)
    for j in range(nchunks):
        acc[n - 1] = _weigh(logits[n - 1][j], m, v1_chunk(j), acc[n - 1])
    return [a[:, :a.shape[1] - LANES] / a[:, a.shape[1] - LANES:] for a in acc]


def _attn_tiles(s):
    rows = _pick(s, 512)
    return rows, (2 if s % (2 * rows) == 0 else 1), _pick(s, 1024)


def _diff_attn_kernel(lp_ref, q_ref, k_ref, v_ref, g_ref, o_ref, *, lambda_init, tk, nsub):
    bq = q_ref.shape[0] // nsub
    ones = jnp.ones((tk, LANES), BF16)
    k_chunk = lambda j: k_ref[j * tk:(j + 1) * tk, :]
    v1_chunk = lambda j: jnp.concatenate([v_ref[j * tk:(j + 1) * tk, :], ones], axis=1)
    qs = []
    for r in range(nsub):
        q = q_ref[r * bq:(r + 1) * bq, :]
        lane = lax.broadcasted_iota(jnp.int32, q.shape, 1)
        zero = jnp.zeros_like(q)
        qs.append(jnp.concatenate([jnp.where(lane < DIFF_QK_DIM, q, zero),
                                   jnp.where(lane >= DIFF_QK_DIM, q, zero)], axis=0))
    outs = _softmax_pv(qs, k_chunk, v1_chunk, k_ref.shape[0] // tk)
    lp = lp_ref[...]
    lam = (jnp.exp(jnp.sum(lp[0:1] * lp[1:2], axis=1, keepdims=True))
           - jnp.exp(jnp.sum(lp[2:3] * lp[3:4], axis=1, keepdims=True)) + lambda_init)
    for r in range(nsub):
        o = outs[r][:bq] - lam * outs[r][bq:]
        y = o * lax.rsqrt(jnp.mean(o * o, axis=-1, keepdims=True) + DIFF_SUBLN_EPS)
        o_ref[r * bq:(r + 1) * bq, :] = (y * g_ref[...] * (1.0 - lambda_init)).astype(o_ref.dtype)


def _diff_attention(qk, v, diff_lambda, diff_subln, l, lambda_init):
    s = qk.shape[0]
    prob_rows, nsub, tk = _attn_tiles(s)
    rows = prob_rows // 2 * nsub
    h = DIFF_HEADS
    return pl.pallas_call(
        functools.partial(_diff_attn_kernel, lambda_init=lambda_init, tk=tk, nsub=nsub),
        grid=(h, s // rows),
        in_specs=[pl.BlockSpec((None, 4, DIFF_QK_DIM), lambda hh, i: (l, 0, 0)),
                  pl.BlockSpec((rows, LANES), lambda hh, i: (i, hh)),
                  pl.BlockSpec((s, LANES), lambda hh, i: (0, h + hh)),
                  pl.BlockSpec((s, LANES), lambda hh, i: (0, hh)),
                  pl.BlockSpec((None, 1, DIFF_V_DIM), lambda hh, i: (l, 0, 0))],
        out_specs=pl.BlockSpec((rows, LANES), lambda hh, i: (i, hh)),
        out_shape=jax.ShapeDtypeStruct((s, h * DIFF_V_DIM), BF16),
        compiler_params=_cparams(("parallel", "parallel")),
        name="diff_attention",
    )(diff_lambda, qk, qk, v, diff_subln)


def _mla_attn_kernel(q_ref, kn_ref, kr_ref, v_ref, o_ref, *, tk, nsub):
    bq = q_ref.shape[0] // nsub
    ones = jnp.ones((tk, LANES), BF16)
    k_chunk = lambda j: jnp.concatenate([kn_ref[j * tk:(j + 1) * tk, :], kr_ref[j * tk:(j + 1) * tk, :]], axis=1)
    v1_chunk = lambda j: jnp.concatenate([v_ref[j * tk:(j + 1) * tk, :], ones], axis=1)
    qs = [q_ref[r * bq:(r + 1) * bq, :] for r in range(nsub)]
    outs = _softmax_pv(qs, k_chunk, v1_chunk, kn_ref.shape[0] // tk)
    for r in range(nsub):
        o_ref[r * bq:(r + 1) * bq, :] = outs[r].astype(o_ref.dtype)


def _mla_attention(q, kv_up, qk):
    s = q.shape[0]
    bq, nsub, tk = _attn_tiles(s)
    rows = bq * nsub
    h = MLA_HEADS
    kr_blk = 2 * DIFF_HEADS
    return pl.pallas_call(
        functools.partial(_mla_attn_kernel, tk=tk, nsub=nsub),
        grid=(h, s // rows),
        in_specs=[pl.BlockSpec((rows, 2 * LANES), lambda hh, i: (i, hh)),
                  pl.BlockSpec((s, LANES), lambda hh, i: (0, 2 * hh)),
                  pl.BlockSpec((s, LANES), lambda hh, i: (0, kr_blk)),
                  pl.BlockSpec((s, LANES), lambda hh, i: (0, 2 * hh + 1))],
        out_specs=pl.BlockSpec((rows, LANES), lambda hh, i: (i, hh)),
        out_shape=jax.ShapeDtypeStruct((s, h * MLA_V), BF16),
        compiler_params=_cparams(("parallel", "parallel")),
        name="mla_attention",
    )(q, kv_up, qk, kv_up)


def _gqa_kernel(sink_ref, q_ref, kp_ref, kc_ref, kn_ref, vp_ref, vc_ref, vn_ref, o_ref, *, layer, seq, bq):
    g = pl.program_id(0)
    i = pl.program_id(1)
    nk = bq + 2 * WINDOW
    kb = jnp.concatenate([kp_ref[...], kc_ref[...], kn_ref[...]], axis=0)
    vb = jnp.concatenate([vp_ref[...], vc_ref[...], vn_ref[...]], axis=0)
    v1 = jnp.concatenate([vb, jnp.ones_like(vb)], axis=1)
    qpos = i * bq + lax.broadcasted_iota(jnp.int32, (bq, nk), 0)
    kpos = i * bq - WINDOW + lax.broadcasted_iota(jnp.int32, (bq, nk), 1)
    valid = (jnp.abs(kpos - qpos) <= WINDOW) & (kpos >= 0) & (kpos < seq)
    bias = jnp.where(valid, 0.0, NEG_INF).astype(F32)
    for hh in range(GQA_GROUP):
        q = q_ref[:, hh * GQA_DIM:(hh + 1) * GQA_DIM]
        s = lax.dot_general(q, kb, (((1,), (1,)), ((), ())), preferred_element_type=F32) + bias
        sink = sink_ref[layer * GQA_HEADS + g * GQA_GROUP + hh] * LOG2E
        m = jnp.maximum(jnp.max(s, axis=1, keepdims=True), sink)
        pv = jnp.dot(jnp.exp2(s - m).astype(BF16), v1, preferred_element_type=F32)
        denom = pv[:, GQA_DIM:] + jnp.exp2(sink - m)
        o_ref[:, hh * GQA_DIM:(hh + 1) * GQA_DIM] = (pv[:, :GQA_DIM] / denom).astype(o_ref.dtype)


def _window_gqa(gqk, vv, sink_flat, l):
    s = gqk.shape[0]
    bq = _pick(s, 256)
    assert bq % WINDOW == 0 and WINDOW == LANES
    nb = s // bq
    r = bq // WINDOW
    nh = s // WINDOW
    qw = GQA_GROUP * GQA_DIM
    kcol = GQA_HEADS
    vcol = DIFF_HEADS

    def prev(i):
        return jnp.maximum(i * r - 1, 0)

    def nxt(i):
        return jnp.minimum((i + 1) * r, nh - 1)

    return pl.pallas_call(
        functools.partial(_gqa_kernel, layer=l, seq=s, bq=bq),
        grid=(GQA_KV_HEADS, nb),
        in_specs=[pl.BlockSpec(memory_space=pltpu.SMEM),
                  pl.BlockSpec((bq, qw), lambda g, i: (i, g)),
                  pl.BlockSpec((WINDOW, LANES), lambda g, i: (prev(i), kcol + g)),
                  pl.BlockSpec((bq, LANES), lambda g, i: (i, kcol + g)),
                  pl.BlockSpec((WINDOW, LANES), lambda g, i: (nxt(i), kcol + g)),
                  pl.BlockSpec((WINDOW, LANES), lambda g, i: (prev(i), vcol + g)),
                  pl.BlockSpec((bq, LANES), lambda g, i: (i, vcol + g)),
                  pl.BlockSpec((WINDOW, LANES), lambda g, i: (nxt(i), vcol + g))],
        out_specs=pl.BlockSpec((bq, qw), lambda g, i: (i, g)),
        out_shape=jax.ShapeDtypeStruct((s, GQA_HEADS * GQA_DIM), BF16),
        compiler_params=_cparams(("parallel", "parallel")),
        name="window_gqa",
    )(sink_flat, gqk, gqk, gqk, gqk, vv, vv, vv)


def _merge_kernel(u_ref, o0_ref, o1_ref, o2_ref, wg0_ref, wg1_ref, wg2_ref, b0_ref, b1_ref, b2_ref,
                  wb0_ref, wb1_ref, wb2_ref, out_ref, wg0_s, wg1_s, wg2_s, wb0_s, wb1_s, wb2_s):
    _cast_weights_once(((wg0_ref, wg0_s), (wg1_ref, wg1_s), (wg2_ref, wg2_s),
                        (wb0_ref, wb0_s), (wb1_ref, wb1_s), (wb2_ref, wb2_s)))
    u = u_ref[...]
    acc = None
    for o_ref, wg_s, b_ref, wb_s in ((o0_ref, wg0_s, b0_ref, wb0_s),
                                     (o1_ref, wg1_s, b1_ref, wb1_s),
                                     (o2_ref, wg2_s, b2_ref, wb2_s)):
        gate = jax.nn.sigmoid(jnp.dot(u, wg_s[...], preferred_element_type=F32) + b_ref[...])
        term = jnp.dot(o_ref[...], wb_s[...], preferred_element_type=F32) * gate
        acc = term if acc is None else acc + term
    out_ref[...] = acc.astype(out_ref.dtype)


def _merge(u, o_diff, o_mla, o_gqa, w_gate, b_gate, w_branch, l):
    m, d = u.shape
    bw = o_diff.shape[1]
    bm = _pick(m, 1024)
    bn = _pick(d, 256)
    nj = d // bn
    o_spec = pl.BlockSpec((bm, bw), lambda j, i: (i, 0))

    def wg_spec(n):
        return pl.BlockSpec((None, d, bn), lambda j, i: (l, 0, n * nj + j))

    def b_spec(n):
        return pl.BlockSpec((None, 1, bn), lambda j, i: (l, 0, n * nj + j))

    def wb_spec(n):
        return pl.BlockSpec((None, None, bw, bn), lambda j, i: (l, n, 0, j))

    return pl.pallas_call(
        _merge_kernel,
        grid=(nj, m // bm),
        in_specs=[pl.BlockSpec((bm, d), lambda j, i: (i, 0)), o_spec, o_spec, o_spec,
                  wg_spec(0), wg_spec(1), wg_spec(2), b_spec(0), b_spec(1), b_spec(2),
                  wb_spec(0), wb_spec(1), wb_spec(2)],
        out_specs=pl.BlockSpec((bm, bn), lambda j, i: (i, j)),
        out_shape=jax.ShapeDtypeStruct((m, d), BF16),
        scratch_shapes=[pltpu.VMEM((d, bn), BF16)] * 3 + [pltpu.VMEM((bw, bn), BF16)] * 3,
        compiler_params=_cparams(("arbitrary", "arbitrary")),
        name="branch_merge",
    )(u, o_diff, o_mla, o_gqa, w_gate, w_gate, w_gate, b_gate, b_gate, b_gate,
      w_branch, w_branch, w_branch)


def _xattn_kernel(mg_ref, h_ref, wm_ref, g_ref, wq_ref, k_ref, v_ref, wo_ref, o_ref, ob_ref, wm_s, wq_s, wo_s):
    @pl.when(pl.program_id(0) == 0)
    def _():
        wm_s[...] = wm_ref[...].astype(BF16)
        wq_s[...] = wq_ref[...].astype(BF16)
        wo_s[...] = wo_ref[...].astype(BF16)

    h1 = h_ref[...] + jnp.dot(mg_ref[...], wm_s[...], preferred_element_type=F32)
    hn = _norm_rows(h1, g_ref[...])
    q = jnp.dot(hn, wq_s[...], preferred_element_type=F32) * (XA_DIM ** -0.5 * LOG2E)
    q = q.astype(BF16)
    heads = []
    for hh in range(XA_HEADS):
        sl = slice(hh * XA_DIM, (hh + 1) * XA_DIM)
        s = lax.dot_general(q[:, sl], k_ref[:, sl], (((1,), (1,)), ((), ())), preferred_element_type=F32)
        p = jnp.exp2(s - jnp.max(s, axis=1, keepdims=True))
        denom = jnp.sum(p, axis=1, keepdims=True)
        heads.append((jnp.dot(p.astype(BF16), v_ref[:, sl], preferred_element_type=F32) / denom).astype(BF16))
    o = jnp.concatenate(heads, axis=1)
    out = h1 + jnp.dot(o, wo_s[...], preferred_element_type=F32)
    o_ref[...] = out
    ob_ref[...] = out.astype(BF16)


def _cross_attention(merged, h, w_o, gain, xa_wq, memkv, xa_wo, l):
    m, d = h.shape
    w = XA_HEADS * XA_DIM
    ml = memkv.shape[0]
    bm = _pick(m, 256)
    once = dict(pipeline_mode=pl.Buffered(1))
    return pl.pallas_call(
        _xattn_kernel,
        grid=(m // bm,),
        in_specs=[pl.BlockSpec((bm, d), lambda i: (i, 0)),
                  pl.BlockSpec((bm, d), lambda i: (i, 0)),
                  pl.BlockSpec((None, d, d), lambda i: (l, 0, 0), **once),
                  pl.BlockSpec((None, 1, d), lambda i: (l, 0, 0)),
                  pl.BlockSpec((None, d, w), lambda i: (l, 0, 0), **once),
                  pl.BlockSpec((ml, w), lambda i: (0, 0)),
                  pl.BlockSpec((ml, w), lambda i: (0, 1)),
                  pl.BlockSpec((None, w, d), lambda i: (l, 0, 0), **once)],
        out_specs=[pl.BlockSpec((bm, d), lambda i: (i, 0))] * 2,
        out_shape=[jax.ShapeDtypeStruct((m, d), F32), jax.ShapeDtypeStruct((m, d), BF16)],
        scratch_shapes=[pltpu.VMEM((d, d), BF16), pltpu.VMEM((d, w), BF16), pltpu.VMEM((w, d), BF16)],
        compiler_params=_cparams(("arbitrary",)),
        name="cross_attention",
    )(merged, h, w_o, gain, xa_wq, memkv, memkv, xa_wo)


def _pad_cols(w, width):
    return jnp.pad(w, ((0, 0),) * (w.ndim - 1) + ((0, width - w.shape[-1]),))


_DQ, _DV, _CQ, _KR, _GQ, _GV, _END = 0, 2048, 3072, 4096, 4160, 5440, 5696


def _regroup_kernel(w_ref, a_ref, v_ref, c_ref, g_ref):
    x = w_ref[...]
    pad = jnp.zeros((x.shape[0], LANES - (_GQ - _KR)), x.dtype)
    a_ref[...] = jnp.concatenate([x[:, _DQ:_DV], x[:, _KR:_GQ], pad], axis=1).astype(BF16)
    v_ref[...] = jnp.concatenate([x[:, _DV:_CQ], x[:, _GV:_END]], axis=1).astype(BF16)
    c_ref[...] = x[:, _CQ:_KR].astype(BF16)
    g_ref[...] = x[:, _GQ:_GV].astype(BF16)


def _regroup_w_in(w_in):
    depth, k, n = w_in.shape
    assert n == _END
    bk = _pick(k, 256)
    widths = (_DV - _DQ + LANES, (_CQ - _DV) + (_END - _GV), _KR - _CQ, _GV - _GQ)
    return pl.pallas_call(
        _regroup_kernel,
        grid=(depth, k // bk),
        in_specs=[pl.BlockSpec((None, bk, n), lambda l, i: (l, i, 0))],
        out_specs=[pl.BlockSpec((None, bk, w), lambda l, i: (l, i, 0)) for w in widths],
        out_shape=[jax.ShapeDtypeStruct((depth, k, w), BF16) for w in widths],
        compiler_params=_cparams(("parallel", "parallel")),
        name="regroup_w_in",
    )(w_in)


def kernel(x, mem, positions, ffn1_norm, ffn1_w_gu, ffn1_w_down, mix_norm, w_in, diff_lambda, diff_subln,
           mla_q_norm, mla_kv_norm, mla_w_uq, mla_w_ukv, gqa_sink, w_branch, w_gate, b_gate, w_o, xa_norm,
           mem_norm, xa_wq, xa_wkv, xa_wo, ffn2_norm, ffn2_w_gu, ffn2_w_down, final_norm):
    b, s, d = x.shape
    assert b == 1
    depth = ffn1_norm.shape[0]
    h = x.reshape(s, d)
    mem2 = mem.reshape(mem.shape[1], d)

    pos_b = jnp.broadcast_to(positions.reshape(s, 1).astype(F32), (s, LANES))
    cos64, sin64 = _rope_tables(pos_b, DIFF_QK_DIM)
    cos128, sin128 = _rope_tables(pos_b, GQA_DIM)

    bf = lambda w: w.astype(BF16)
    xa_wkv_b = bf(xa_wkv)
    b_gate3 = b_gate.reshape(depth, 1, -1)
    norm3 = lambda g: g.reshape(depth, 1, -1)
    mix_norm3, xa_norm3, mem_norm3 = map(norm3, (mix_norm, xa_norm, mem_norm))
    ffn1_norm_col, ffn2_norm_col = ffn1_norm.reshape(depth, d, 1), ffn2_norm.reshape(depth, d, 1)
    hb = bf(h)
    diff_subln3 = diff_subln.reshape(depth, 1, -1)
    sink_flat = gqa_sink.reshape(-1)

    diff_scale = DIFF_QK_DIM ** -0.5 * LOG2E
    mla_scale = (MLA_NOPE + MLA_ROPE) ** -0.5 * LOG2E
    gqa_scale = GQA_DIM ** -0.5 * LOG2E
    cs_a = jnp.concatenate([jnp.full((1, 1024), diff_scale, F32), jnp.ones((1, 1024 + LANES), F32)], axis=1)
    cs_g = jnp.concatenate([jnp.full((1, 1024), gqa_scale, F32), jnp.ones((1, 256), F32)], axis=1)
    cs_q = jnp.full((1, MLA_HEADS * 2 * LANES), mla_scale, F32)

    w_a, w_v, w_c, w_g = _regroup_w_in(w_in)
    wuq = mla_w_uq.reshape(depth, MLA_Q_RANK, MLA_HEADS, MLA_NOPE + MLA_ROPE)
    wuq = _pad_cols(wuq, 2 * LANES).reshape(depth, MLA_Q_RANK, MLA_HEADS * 2 * LANES)
    wuq, wukv = bf(wuq), bf(mla_w_ukv)
    mla_gains = jnp.stack([mla_q_norm, mla_kv_norm], axis=1)[:, :, None, :]

    for l in range(depth):
        lambda_init = 0.8 - 0.6 * math.exp(-0.3 * l)

        a = _ffn_up(hb, ffn1_norm_col, ffn1_w_gu, l)
        h = _mm_residual(a, ffn1_w_down, l, h, 0.5)

        qk, u = _mm_rope(h, w_a, l, cs_a, cos64, sin64, DIFF_QK_DIM // 2, (True,) * 17, gain=mix_norm3)
        vv = _mm_plain(u, w_v, l)
        cn = _mm_rmsnorm(u, w_c, mla_gains, l)
        gqk = _mm_rope(u, w_g, l, cs_g, cos128, sin128, GQA_DIM // 2, (True,) * 10)

        o_diff = _diff_attention(qk, vv, diff_lambda, diff_subln3, l, lambda_init)

        q_mla = _mm_rope(cn[0], wuq, l, cs_q, cos64, sin64, MLA_ROPE // 2, (False, True) * MLA_HEADS)
        kv_up = _mm_plain(cn[1], wukv, l)
        o_mla = _mla_attention(q_mla, kv_up, qk)

        o_gqa = _window_gqa(gqk, vv, sink_flat, l)

        merged = _merge(u, o_diff, o_mla, o_gqa, w_gate, b_gate3, w_branch, l)
        memkv = _mm_plain(_rmsnorm(mem2, mem_norm3, l), xa_wkv_b, l)
        h, hb = _cross_attention(merged, h, w_o, xa_norm3, xa_wq, memkv, xa_wo, l)

        a = _ffn_up(hb, ffn2_norm_col, ffn2_w_gu, l)
        if l + 1 < depth:
            h, hb = _mm_residual(a, ffn2_w_down, l, h, 0.5, emit_bf16=True)
        else:
            h = _mm_residual(a, ffn2_w_down, l, h, 0.5)

    out = _rmsnorm(h, final_norm.reshape(1, d), None, out_dtype=F32)
    return out.reshape(b, s, d)
```

```python
import functools
import math

import jax
import jax.numpy as jnp
from jax import lax
from jax.experimental import pallas as pl
from jax.experimental.pallas import tpu as pltpu

F32 = jnp.float32
BF16 = jnp.bfloat16

DIFF_HEADS = 8
DIFF_QK_DIM = 64
DIFF_V_DIM = 128
MLA_HEADS = 8
MLA_Q_RANK = 512
MLA_KV_RANK = 512
MLA_NOPE = 128
MLA_ROPE = 64
MLA_V = 128
GQA_HEADS = 8
GQA_KV_HEADS = 2
GQA_GROUP = GQA_HEADS // GQA_KV_HEADS
GQA_DIM = 128
WINDOW = 128
N_BRANCH = 3
BRANCH_WIDTH = 1024
XA_HEADS = 4
XA_DIM = 128
D_FF = 5632
ROPE_THETA = 10000.0
EPS = 1e-6
DIFF_SUBLN_EPS = 1e-5
NEG_INF = -1e30
LOG2E = math.log2(math.e)

LANES = 128
V7X_VMEM_LIMIT_BYTES = 56 * 1024 * 1024


def _cparams(semantics):
    return pltpu.CompilerParams(dimension_semantics=semantics, vmem_limit_bytes=V7X_VMEM_LIMIT_BYTES)


def _pick(dim, pref):
    if dim <= pref:
        return dim
    b = pref
    while dim % b:
        b //= 2
    return b


def _rope_table_kernel(pos_ref, inv_ref, sign_ref, cos_ref, sin_ref):
    ang = pos_ref[...] * inv_ref[...]
    cos_ref[...] = jnp.cos(ang)
    sin_ref[...] = jnp.sin(ang) * sign_ref[...]


def _rope_tables(pos_b, dim):
    s = pos_b.shape[0]
    half = dim // 2
    inv = ROPE_THETA ** (-jnp.arange(0, dim, 2, dtype=F32) / dim)
    inv_row = jnp.tile(inv, LANES // half)[None, :]
    sign_row = jnp.tile(jnp.concatenate([-jnp.ones((half,), F32), jnp.ones((half,), F32)]), LANES // dim)[None, :]
    bm = _pick(s, 1024)
    return pl.pallas_call(
        _rope_table_kernel,
        grid=(s // bm,),
        in_specs=[pl.BlockSpec((bm, LANES), lambda i: (i, 0)),
                  pl.BlockSpec((1, LANES), lambda i: (0, 0)),
                  pl.BlockSpec((1, LANES), lambda i: (0, 0))],
        out_specs=[pl.BlockSpec((bm, LANES), lambda i: (i, 0))] * 2,
        out_shape=[jax.ShapeDtypeStruct((s, LANES), F32)] * 2,
        compiler_params=_cparams(("parallel",)),
        name="rope_tables",
    )(pos_b, inv_row, sign_row)


def _rope_block(x, cos, sin_signed, half):
    if 2 * half == LANES:
        partner = pltpu.roll(x, half, axis=1)
    else:
        lane = lax.broadcasted_iota(jnp.int32, x.shape, 1)
        first = (lane % (2 * half)) < half
        partner = jnp.where(first, pltpu.roll(x, LANES - half, axis=1), pltpu.roll(x, half, axis=1))
    return x * cos + partner * sin_signed


def _rmsnorm_kernel(x_ref, g_ref, o_ref, *, eps):
    x = x_ref[...]
    y = x * lax.rsqrt(jnp.mean(x * x, axis=-1, keepdims=True) + eps)
    o_ref[...] = (y * g_ref[...]).astype(o_ref.dtype)


def _rmsnorm(x, g_stack, l, out_dtype=BF16):
    m, d = x.shape
    bm = _pick(m, 512)
    if l is None:
        g_spec = pl.BlockSpec((1, d), lambda i: (0, 0))
    else:
        g_spec = pl.BlockSpec((None, 1, d), lambda i: (l, 0, 0))
    return pl.pallas_call(
        functools.partial(_rmsnorm_kernel, eps=EPS),
        grid=(m // bm,),
        in_specs=[pl.BlockSpec((bm, d), lambda i: (i, 0)), g_spec],
        out_specs=pl.BlockSpec((bm, d), lambda i: (i, 0)),
        out_shape=jax.ShapeDtypeStruct((m, d), out_dtype),
        compiler_params=_cparams(("parallel",)),
        name="rmsnorm",
    )(x, g_stack)


def _cast_weights_once(pairs):
    @pl.when(pl.program_id(1) == 0)
    def _():
        for w_ref, w_s in pairs:
            w_s[...] = w_ref[...].astype(BF16)


def _norm_rows(x, g, eps=EPS):
    return (x * lax.rsqrt(jnp.mean(x * x, axis=-1, keepdims=True) + eps) * g).astype(BF16)


def _mm_swiglu_kernel(x_ref, gc_ref, wg_ref, wu_ref, o_ref, wg_s, wu_s, *, sub):
    @pl.when(pl.program_id(1) == 0)
    def _():
        gc = gc_ref[...]
        wg_s[...] = (wg_ref[...] * gc).astype(BF16)
        wu_s[...] = (wu_ref[...] * gc).astype(BF16)

    def rows_tile(t, carry):
        rows = pl.ds(pl.multiple_of(t * sub, sub), sub)
        x = x_ref[rows, :]
        xf = x.astype(F32)
        rs = lax.rsqrt(jnp.mean(xf * xf, axis=-1, keepdims=True) + EPS)
        g = jnp.dot(x, wg_s[...], preferred_element_type=F32) * rs
        u = jnp.dot(x, wu_s[...], preferred_element_type=F32) * rs
        o_ref[rows, :] = (g * jax.nn.sigmoid(g) * u).astype(o_ref.dtype)
        return carry

    lax.fori_loop(0, x_ref.shape[0] // sub, rows_tile, 0)


def _ffn_up(hb, gain_col, w_gu, l):
    m, d = hb.shape
    n = w_gu.shape[-1] // 2
    bm = _pick(m, 2048)
    bn = _pick(n, 512)
    nj = n // bn
    return pl.pallas_call(
        functools.partial(_mm_swiglu_kernel, sub=_pick(bm, 1024)),
        grid=(nj, m // bm),
        in_specs=[pl.BlockSpec((bm, d), lambda j, i: (i, 0)),
                  pl.BlockSpec((None, d, 1), lambda j, i: (l, 0, 0)),
                  pl.BlockSpec((None, d, bn), lambda j, i: (l, 0, j)),
                  pl.BlockSpec((None, d, bn), lambda j, i: (l, 0, j + nj))],
        out_specs=pl.BlockSpec((bm, bn), lambda j, i: (i, j)),
        out_shape=jax.ShapeDtypeStruct((m, n), BF16),
        scratch_shapes=[pltpu.VMEM((d, bn), BF16), pltpu.VMEM((d, bn), BF16)],
        compiler_params=_cparams(("arbitrary", "arbitrary")),
        name="ffn_up",
    )(hb, gain_col, w_gu, w_gu)


def _mm_residual_kernel(a_ref, w_ref, r_ref, o_ref, *rest, scale):
    w_s = rest[-1]
    _cast_weights_once(((w_ref, w_s),))
    out = r_ref[...] + scale * jnp.dot(a_ref[...], w_s[...], preferred_element_type=F32)
    o_ref[...] = out
    if len(rest) == 2:
        rest[0][...] = out.astype(BF16)


def _mm_residual(a, w, l, res, scale, emit_bf16=False):
    m, k = a.shape
    n = w.shape[-1]
    bm, bn = _pick(m, 256), _pick(n, 1024)
    o_spec = pl.BlockSpec((bm, bn), lambda j, i: (i, j))
    out_shape = jax.ShapeDtypeStruct((m, n), F32)
    return pl.pallas_call(
        functools.partial(_mm_residual_kernel, scale=scale),
        grid=(n // bn, m // bm),
        in_specs=[pl.BlockSpec((bm, k), lambda j, i: (i, 0)),
                  pl.BlockSpec((None, k, bn), lambda j, i: (l, 0, j), pipeline_mode=pl.Buffered(1)),
                  o_spec],
        out_specs=[o_spec, o_spec] if emit_bf16 else o_spec,
        out_shape=[out_shape, jax.ShapeDtypeStruct((m, n), BF16)] if emit_bf16 else out_shape,
        scratch_shapes=[pltpu.VMEM((k, bn), BF16)],
        compiler_params=_cparams(("arbitrary", "arbitrary")),
        name="mm_residual",
    )(a, w, res)


def _mm_plain_kernel(x_ref, w_ref, o_ref):
    o_ref[...] = jnp.dot(x_ref[...], w_ref[...], preferred_element_type=F32).astype(o_ref.dtype)


def _rows_spec(x, bm, lead):
    k = x.shape[-1]
    if lead is None:
        return pl.BlockSpec((bm, k), lambda i: (i, 0))
    return pl.BlockSpec((None, bm, k), lambda i: (lead, i, 0))


def _mm_plain(x, w, l, lead=None):
    m, k = x.shape[-2:]
    n = w.shape[-1]
    bm = _pick(m, 512)
    w_spec = pl.BlockSpec((None, k, n), lambda i: (l, 0, 0))
    return pl.pallas_call(
        _mm_plain_kernel,
        grid=(m // bm,),
        in_specs=[_rows_spec(x, bm, lead), w_spec],
        out_specs=pl.BlockSpec((bm, n), lambda i: (i, 0)),
        out_shape=jax.ShapeDtypeStruct((m, n), BF16),
        compiler_params=_cparams(("parallel",)),
        name="mm_plain",
    )(x, w)


def _mm_rope_kernel(x_ref, w_ref, cs_ref, cos_ref, sin_ref, *rest, half, pattern):
    if len(rest) == 3:
        g_ref, o_ref, u_ref = rest
        x = _norm_rows(x_ref[...], g_ref[...])
        u_ref[...] = x
    else:
        (o_ref,) = rest
        x = x_ref[...]
    acc = jnp.dot(x, w_ref[...], preferred_element_type=F32) * cs_ref[...]
    cos = cos_ref[...]
    sin = sin_ref[...]
    for b, roped in enumerate(pattern):
        blk = acc[:, b * LANES:(b + 1) * LANES]
        if roped:
            blk = _rope_block(blk, cos, sin, half)
        o_ref[:, b * LANES:(b + 1) * LANES] = blk.astype(o_ref.dtype)


def _mm_rope(x, w, l, colscale, cos, sin, half, pattern, gain=None, lead=None):
    m, k = x.shape[-2:]
    n = w.shape[-1]
    assert n == LANES * len(pattern)
    bm = _pick(m, 512)
    x_spec = pl.BlockSpec((bm, k), lambda i: (i, 0))
    o_spec = pl.BlockSpec((bm, n), lambda i: (i, 0))
    out_shape = jax.ShapeDtypeStruct((m, n), BF16)
    in_specs = [_rows_spec(x, bm, lead),
                pl.BlockSpec((None, k, n), lambda i: (l, 0, 0)),
                pl.BlockSpec((1, n), lambda i: (0, 0)),
                pl.BlockSpec((bm, LANES), lambda i: (i, 0)),
                pl.BlockSpec((bm, LANES), lambda i: (i, 0))]
    args = (x, w, colscale, cos, sin)
    if gain is not None:
        in_specs.append(pl.BlockSpec((None, 1, k), lambda i: (l, 0, 0)))
        args += (gain,)
    return pl.pallas_call(
        functools.partial(_mm_rope_kernel, half=half, pattern=pattern),
        grid=(m // bm,),
        in_specs=in_specs,
        out_specs=o_spec if gain is None else [o_spec, x_spec],
        out_shape=out_shape if gain is None else [out_shape, jax.ShapeDtypeStruct((m, k), BF16)],
        compiler_params=_cparams(("parallel",)),
        name="mm_rope",
    )(*args)


def _mm_rmsnorm_kernel(x_ref, w_ref, g_ref, o_ref):
    acc = jnp.dot(x_ref[...], w_ref[...], preferred_element_type=F32)
    y = acc * lax.rsqrt(jnp.mean(acc * acc, axis=-1, keepdims=True) + EPS)
    o_ref[...] = (y * g_ref[...]).astype(o_ref.dtype)


def _mm_rmsnorm(x, w, g, l):
    m, k = x.shape
    _, t, _, r = g.shape
    bm = _pick(m, 512)
    return pl.pallas_call(
        _mm_rmsnorm_kernel,
        grid=(t, m // bm),
        in_specs=[pl.BlockSpec((bm, k), lambda j, i: (i, 0)),
                  pl.BlockSpec((None, k, r), lambda j, i: (l, 0, j)),
                  pl.BlockSpec((None, None, 1, r), lambda j, i: (l, j, 0, 0))],
        out_specs=pl.BlockSpec((None, bm, r), lambda j, i: (j, i, 0)),
        out_shape=jax.ShapeDtypeStruct((t, m, r), BF16),
        compiler_params=_cparams(("parallel", "parallel")),
        name="mm_rmsnorm",
    )(x, w, g)


def _logits(q, k, mlane):
    s = lax.dot_general(q, k, (((1,), (1,)), ((), ())), preferred_element_type=F32)
    for b in range(s.shape[1] // LANES):
        blk = s[:, b * LANES:(b + 1) * LANES]
        mlane = blk if mlane is None else jnp.maximum(mlane, blk)
    return s, mlane


def _row_max(mlane):
    return jnp.broadcast_to(jnp.max(mlane, axis=1, keepdims=True), mlane.shape)


def _weigh(s, m, v1, acc):
    ps = [jnp.exp2(s[:, b * LANES:(b + 1) * LANES] - m).astype(BF16) for b in range(s.shape[1] // LANES)]
    pv = jnp.dot(jnp.concatenate(ps, axis=1), v1, preferred_element_type=F32)
    return pv if acc is None else acc + pv


def _softmax_pv(qs, k_chunk, v1_chunk, nchunks):
    n = len(qs)
    logits = [[None] * nchunks for _ in range(n)]
    acc = [None] * n
    mlane = None
    for j in range(nchunks):
        logits[0][j], mlane = _logits(qs[0], k_chunk(j), mlane)
    m = _row_max(mlane)
    for p in range(1, n):
        mlane = None
        for j in range(nchunks):
            acc[p - 1] = _weigh(logits[p - 1][j], m, v1_chunk(j), acc[p - 1])
            logits[p][j], mlane = _logits(qs[p], k_chunk(j), mlane)
        m = _row_max(mlane)
    for j in range(nchunks):
        acc[n - 1] = _weigh(logits[n - 1][j], m, v1_chunk(j), acc[n - 1])
    return [a[:, :a.shape[1] - LANES] / a[:, a.shape[1] - LANES:] for a in acc]


def _attn_tiles(s):
    rows = _pick(s, 512)
    return rows, (2 if s % (2 * rows) == 0 else 1), _pick(s, 1024)


def _diff_attn_kernel(lp_ref, q_ref, k_ref, v_ref, g_ref, o_ref, *, lambda_init, tk, nsub):
    bq = q_ref.shape[0] // nsub
    ones = jnp.ones((tk, LANES), BF16)
    k_chunk = lambda j: k_ref[j * tk:(j + 1) * tk, :]
    v1_chunk = lambda j: jnp.concatenate([v_ref[j * tk:(j + 1) * tk, :], ones], axis=1)
    qs = []
    for r in range(nsub):
        q = q_ref[r * bq:(r + 1) * bq, :]
        lane = lax.broadcasted_iota(jnp.int32, q.shape, 1)
        zero = jnp.zeros_like(q)
        qs.append(jnp.concatenate([jnp.where(lane < DIFF_QK_DIM, q, zero),
                                   jnp.where(lane >= DIFF_QK_DIM, q, zero)], axis=0))
    outs = _softmax_pv(qs, k_chunk, v1_chunk, k_ref.shape[0] // tk)
    lp = lp_ref[...]
    lam = (jnp.exp(jnp.sum(lp[0:1] * lp[1:2], axis=1, keepdims=True))
           - jnp.exp(jnp.sum(lp[2:3] * lp[3:4], axis=1, keepdims=True)) + lambda_init)
    for r in range(nsub):
        o = outs[r][:bq] - lam * outs[r][bq:]
        y = o * lax.rsqrt(jnp.mean(o * o, axis=-1, keepdims=True) + DIFF_SUBLN_EPS)
        o_ref[r * bq:(r + 1) * bq, :] = (y * g_ref[...] * (1.0 - lambda_init)).astype(o_ref.dtype)


def _diff_attention(qk, v, diff_lambda, diff_subln, l, lambda_init):
    s = qk.shape[0]
    prob_rows, nsub, tk = _attn_tiles(s)
    rows = prob_rows // 2 * nsub
    h = DIFF_HEADS
    return pl.pallas_call(
        functools.partial(_diff_attn_kernel, lambda_init=lambda_init, tk=tk, nsub=nsub),
        grid=(h, s // rows),
        in_specs=[pl.BlockSpec((None, 4, DIFF_QK_DIM), lambda hh, i: (l, 0, 0)),
                  pl.BlockSpec((rows, LANES), lambda hh, i: (i, hh)),
                  pl.BlockSpec((s, LANES), lambda hh, i: (0, h + hh)),
                  pl.BlockSpec((s, LANES), lambda hh, i: (0, hh)),
                  pl.BlockSpec((None, 1, DIFF_V_DIM), lambda hh, i: (l, 0, 0))],
        out_specs=pl.BlockSpec((rows, LANES), lambda hh, i: (i, hh)),
        out_shape=jax.ShapeDtypeStruct((s, h * DIFF_V_DIM), BF16),
        compiler_params=_cparams(("parallel", "parallel")),
        name="diff_attention",
    )(diff_lambda, qk, qk, v, diff_subln)


def _mla_attn_kernel(q_ref, kn_ref, kr_ref, v_ref, o_ref, *, tk, nsub):
    bq = q_ref.shape[0] // nsub
    ones = jnp.ones((tk, LANES), BF16)
    k_chunk = lambda j: jnp.concatenate([kn_ref[j * tk:(j + 1) * tk, :], kr_ref[j * tk:(j + 1) * tk, :]], axis=1)
    v1_chunk = lambda j: jnp.concatenate([v_ref[j * tk:(j + 1) * tk, :], ones], axis=1)
    qs = [q_ref[r * bq:(r + 1) * bq, :] for r in range(nsub)]
    outs = _softmax_pv(qs, k_chunk, v1_chunk, kn_ref.shape[0] // tk)
    for r in range(nsub):
        o_ref[r * bq:(r + 1) * bq, :] = outs[r].astype(o_ref.dtype)


def _mla_attention(q, kv_up, qk):
    s = q.shape[0]
    bq, nsub, tk = _attn_tiles(s)
    rows = bq * nsub
    h = MLA_HEADS
    kr_blk = 2 * DIFF_HEADS
    return pl.pallas_call(
        functools.partial(_mla_attn_kernel, tk=tk, nsub=nsub),
        grid=(h, s // rows),
        in_specs=[pl.BlockSpec((rows, 2 * LANES), lambda hh, i: (i, hh)),
                  pl.BlockSpec((s, LANES), lambda hh, i: (0, 2 * hh)),
                  pl.BlockSpec((s, LANES), lambda hh, i: (0, kr_blk)),
                  pl.BlockSpec((s, LANES), lambda hh, i: (0, 2 * hh + 1))],
        out_specs=pl.BlockSpec((rows, LANES), lambda hh, i: (i, hh)),
        out_shape=jax.ShapeDtypeStruct((s, h * MLA_V), BF16),
        compiler_params=_cparams(("parallel", "parallel")),
        name="mla_attention",
    )(q, kv_up, qk, kv_up)


def _gqa_kernel(sink_ref, q_ref, kp_ref, kc_ref, kn_ref, vp_ref, vc_ref, vn_ref, o_ref, *, layer, seq, bq):
    g = pl.program_id(0)
    i = pl.program_id(1)
    nk = bq + 2 * WINDOW
    kb = jnp.concatenate([kp_ref[...], kc_ref[...], kn_ref[...]], axis=0)
    vb = jnp.concatenate([vp_ref[...], vc_ref[...], vn_ref[...]], axis=0)
    v1 = jnp.concatenate([vb, jnp.ones_like(vb)], axis=1)
    qpos = i * bq + lax.broadcasted_iota(jnp.int32, (bq, nk), 0)
    kpos = i * bq - WINDOW + lax.broadcasted_iota(jnp.int32, (bq, nk), 1)
    valid = (jnp.abs(kpos - qpos) <= WINDOW) & (kpos >= 0) & (kpos < seq)
    bias = jnp.where(valid, 0.0, NEG_INF).astype(F32)
    for hh in range(GQA_GROUP):
        q = q_ref[:, hh * GQA_DIM:(hh + 1) * GQA_DIM]
        s = lax.dot_general(q, kb, (((1,), (1,)), ((), ())), preferred_element_type=F32) + bias
        sink = sink_ref[layer * GQA_HEADS + g * GQA_GROUP + hh] * LOG2E
        m = jnp.maximum(jnp.max(s, axis=1, keepdims=True), sink)
        pv = jnp.dot(jnp.exp2(s - m).astype(BF16), v1, preferred_element_type=F32)
        denom = pv[:, GQA_DIM:] + jnp.exp2(sink - m)
        o_ref[:, hh * GQA_DIM:(hh + 1) * GQA_DIM] = (pv[:, :GQA_DIM] / denom).astype(o_ref.dtype)


def _window_gqa(gqk, vv, sink_flat, l):
    s = gqk.shape[0]
    bq = _pick(s, 256)
    assert bq % WINDOW == 0 and WINDOW == LANES
    nb = s // bq
    r = bq // WINDOW
    nh = s // WINDOW
    qw = GQA_GROUP * GQA_DIM
    kcol = GQA_HEADS
    vcol = DIFF_HEADS

    def prev(i):
        return jnp.maximum(i * r - 1, 0)

    def nxt(i):
        return jnp.minimum((i + 1) * r, nh - 1)

    return pl.pallas_call(
        functools.partial(_gqa_kernel, layer=l, seq=s, bq=bq),
        grid=(GQA_KV_HEADS, nb),
        in_specs=[pl.BlockSpec(memory_space=pltpu.SMEM),
                  pl.BlockSpec((bq, qw), lambda g, i: (i, g)),
                  pl.BlockSpec((WINDOW, LANES), lambda g, i: (prev(i), kcol + g)),
                  pl.BlockSpec((bq, LANES), lambda g, i: (i, kcol + g)),
                  pl.BlockSpec((WINDOW, LANES), lambda g, i: (nxt(i), kcol + g)),
                  pl.BlockSpec((WINDOW, LANES), lambda g, i: (prev(i), vcol + g)),
                  pl.BlockSpec((bq, LANES), lambda g, i: (i, vcol + g)),
                  pl.BlockSpec((WINDOW, LANES), lambda g, i: (nxt(i), vcol + g))],
        out_specs=pl.BlockSpec((bq, qw), lambda g, i: (i, g)),
        out_shape=jax.ShapeDtypeStruct((s, GQA_HEADS * GQA_DIM), BF16),
        compiler_params=_cparams(("parallel", "parallel")),
        name="window_gqa",
    )(sink_flat, gqk, gqk, gqk, gqk, vv, vv, vv)


def _merge_kernel(u_ref, o0_ref, o1_ref, o2_ref, wg0_ref, wg1_ref, wg2_ref, b0_ref, b1_ref, b2_ref,
                  wb0_ref, wb1_ref, wb2_ref, out_ref, wg0_s, wg1_s, wg2_s, wb0_s, wb1_s, wb2_s):
    _cast_weights_once(((wg0_ref, wg0_s), (wg1_ref, wg1_s), (wg2_ref, wg2_s),
                        (wb0_ref, wb0_s), (wb1_ref, wb1_s), (wb2_ref, wb2_s)))
    u = u_ref[...]
    acc = None
    for o_ref, wg_s, b_ref, wb_s in ((o0_ref, wg0_s, b0_ref, wb0_s),
                                     (o1_ref, wg1_s, b1_ref, wb1_s),
                                     (o2_ref, wg2_s, b2_ref, wb2_s)):
        gate = jax.nn.sigmoid(jnp.dot(u, wg_s[...], preferred_element_type=F32) + b_ref[...])
        term = jnp.dot(o_ref[...], wb_s[...], preferred_element_type=F32) * gate
        acc = term if acc is None else acc + term
    out_ref[...] = acc.astype(out_ref.dtype)


def _merge(u, o_diff, o_mla, o_gqa, w_gate, b_gate, w_branch, l):
    m, d = u.shape
    bw = o_diff.shape[1]
    bm = _pick(m, 1024)
    bn = _pick(d, 256)
    nj = d // bn
    o_spec = pl.BlockSpec((bm, bw), lambda j, i: (i, 0))

    def wg_spec(n):
        return pl.BlockSpec((None, d, bn), lambda j, i: (l, 0, n * nj + j))

    def b_spec(n):
        return pl.BlockSpec((None, 1, bn), lambda j, i: (l, 0, n * nj + j))

    def wb_spec(n):
        return pl.BlockSpec((None, None, bw, bn), lambda j, i: (l, n, 0, j))

    return pl.pallas_call(
        _merge_kernel,
        grid=(nj, m // bm),
        in_specs=[pl.BlockSpec((bm, d), lambda j, i: (i, 0)), o_spec, o_spec, o_spec,
                  wg_spec(0), wg_spec(1), wg_spec(2), b_spec(0), b_spec(1), b_spec(2),
                  wb_spec(0), wb_spec(1), wb_spec(2)],
        out_specs=pl.BlockSpec((bm, bn), lambda j, i: (i, j)),
        out_shape=jax.ShapeDtypeStruct((m, d), BF16),
        scratch_shapes=[pltpu.VMEM((d, bn), BF16)] * 3 + [pltpu.VMEM((bw, bn), BF16)] * 3,
        compiler_params=_cparams(("arbitrary", "arbitrary")),
        name="branch_merge",
    )(u, o_diff, o_mla, o_gqa, w_gate, w_gate, w_gate, b_gate, b_gate, b_gate,
      w_branch, w_branch, w_branch)


def _xattn_kernel(mg_ref, h_ref, wm_ref, g_ref, wq_ref, k_ref, v_ref, wo_ref, o_ref, ob_ref, wm_s, wq_s, wo_s):
    @pl.when(pl.program_id(0) == 0)
    def _():
        wm_s[...] = wm_ref[...].astype(BF16)
        wq_s[...] = wq_ref[...].astype(BF16)
        wo_s[...] = wo_ref[...].astype(BF16)

    h1 = h_ref[...] + jnp.dot(mg_ref[...], wm_s[...], preferred_element_type=F32)
    hn = _norm_rows(h1, g_ref[...])
    q = jnp.dot(hn, wq_s[...], preferred_element_type=F32) * (XA_DIM ** -0.5 * LOG2E)
    q = q.astype(BF16)
    heads = []
    for hh in range(XA_HEADS):
        sl = slice(hh * XA_DIM, (hh + 1) * XA_DIM)
        s = lax.dot_general(q[:, sl], k_ref[:, sl], (((1,), (1,)), ((), ())), preferred_element_type=F32)
        p = jnp.exp2(s - jnp.max(s, axis=1, keepdims=True))
        denom = jnp.sum(p, axis=1, keepdims=True)
        heads.append((jnp.dot(p.astype(BF16), v_ref[:, sl], preferred_element_type=F32) / denom).astype(BF16))
    o = jnp.concatenate(heads, axis=1)
    out = h1 + jnp.dot(o, wo_s[...], preferred_element_type=F32)
    o_ref[...] = out
    ob_ref[...] = out.astype(BF16)


def _cross_attention(merged, h, w_o, gain, xa_wq, memkv, xa_wo, l):
    m, d = h.shape
    w = XA_HEADS * XA_DIM
    ml = memkv.shape[0]
    bm = _pick(m, 256)
    once = dict(pipeline_mode=pl.Buffered(1))
    return pl.pallas_call(
        _xattn_kernel,
        grid=(m // bm,),
        in_specs=[pl.BlockSpec((bm, d), lambda i: (i, 0)),
                  pl.BlockSpec((bm, d), lambda i: (i, 0)),
                  pl.BlockSpec((None, d, d), lambda i: (l, 0, 0), **once),
                  pl.BlockSpec((None, 1, d), lambda i: (l, 0, 0)),
                  pl.BlockSpec((None, d, w), lambda i: (l, 0, 0), **once),
                  pl.BlockSpec((ml, w), lambda i: (0, 0)),
                  pl.BlockSpec((ml, w), lambda i: (0, 1)),
                  pl.BlockSpec((None, w, d), lambda i: (l, 0, 0), **once)],
        out_specs=[pl.BlockSpec((bm, d), lambda i: (i, 0))] * 2,
        out_shape=[jax.ShapeDtypeStruct((m, d), F32), jax.ShapeDtypeStruct((m, d), BF16)],
        scratch_shapes=[pltpu.VMEM((d, d), BF16), pltpu.VMEM((d, w), BF16), pltpu.VMEM((w, d), BF16)],
        compiler_params=_cparams(("arbitrary",)),
        name="cross_attention",
    )(merged, h, w_o, gain, xa_wq, memkv, memkv, xa_wo)


def _pad_cols(w, width):
    return jnp.pad(w, ((0, 0),) * (w.ndim - 1) + ((0, width - w.shape[-1]),))


_DQ, _DV, _CQ, _KR, _GQ, _GV, _END = 0, 2048, 3072, 4096, 4160, 5440, 5696


def _regroup_kernel(wt_ref, a_ref, v_ref, c_ref, g_ref):
    def cols(lo, hi):
        return wt_ref[lo:hi, :].T

    kr = jnp.concatenate([wt_ref[_KR:_GQ, :], jnp.zeros((LANES - (_GQ - _KR), wt_ref.shape[1]), wt_ref.dtype)], axis=0)
    a_ref[...] = jnp.concatenate([cols(_DQ, _DV), kr.T], axis=1).astype(BF16)
    v_ref[...] = jnp.concatenate([cols(_DV, _CQ), cols(_GV, _END)], axis=1).astype(BF16)
    c_ref[...] = cols(_CQ, _KR).astype(BF16)
    g_ref[...] = cols(_GQ, _GV).astype(BF16)


def _regroup_w_in(w_in):
    depth, k, n = w_in.shape
    assert n == _END
    bk = _pick(k, 256)
    widths = (_DV - _DQ + LANES, (_CQ - _DV) + (_END - _GV), _KR - _CQ, _GV - _GQ)
    return pl.pallas_call(
        _regroup_kernel,
        grid=(depth, k // bk),
        in_specs=[pl.BlockSpec((None, n, bk), lambda l, i: (l, 0, i))],
        out_specs=[pl.BlockSpec((None, bk, w), lambda l, i: (l, i, 0)) for w in widths],
        out_shape=[jax.ShapeDtypeStruct((depth, k, w), BF16) for w in widths],
        compiler_params=_cparams(("parallel", "parallel")),
        name="regroup_w_in",
    )(jnp.swapaxes(w_in, 1, 2))


def kernel(x, mem, positions, ffn1_norm, ffn1_w_gu, ffn1_w_down, mix_norm, w_in, diff_lambda, diff_subln,
           mla_q_norm, mla_kv_norm, mla_w_uq, mla_w_ukv, gqa_sink, w_branch, w_gate, b_gate, w_o, xa_norm,
           mem_norm, xa_wq, xa_wkv, xa_wo, ffn2_norm, ffn2_w_gu, ffn2_w_down, final_norm):
    b, s, d = x.shape
    assert b == 1
    depth = ffn1_norm.shape[0]
    h = x.reshape(s, d)
    mem2 = mem.reshape(mem.shape[1], d)

    pos_b = jnp.broadcast_to(positions.reshape(s, 1).astype(F32), (s, LANES))
    cos64, sin64 = _rope_tables(pos_b, DIFF_QK_DIM)
    cos128, sin128 = _rope_tables(pos_b, GQA_DIM)

    bf = lambda w: w.astype(BF16)
    xa_wkv_b = bf(xa_wkv)
    b_gate3 = b_gate.reshape(depth, 1, -1)
    norm3 = lambda g: g.reshape(depth, 1, -1)
    mix_norm3, xa_norm3, mem_norm3 = map(norm3, (mix_norm, xa_norm, mem_norm))
    ffn1_norm_col, ffn2_norm_col = ffn1_norm.reshape(depth, d, 1), ffn2_norm.reshape(depth, d, 1)
    hb = bf(h)
    diff_subln3 = diff_subln.reshape(depth, 1, -1)
    sink_flat = gqa_sink.reshape(-1)

    diff_scale = DIFF_QK_DIM ** -0.5 * LOG2E
    mla_scale = (MLA_NOPE + MLA_ROPE) ** -0.5 * LOG2E
    gqa_scale = GQA_DIM ** -0.5 * LOG2E
    cs_a = jnp.concatenate([jnp.full((1, 1024), diff_scale, F32), jnp.ones((1, 1024 + LANES), F32)], axis=1)
    cs_g = jnp.concatenate([jnp.full((1, 1024), gqa_scale, F32), jnp.ones((1, 256), F32)], axis=1)
    cs_q = jnp.full((1, MLA_HEADS * 2 * LANES), mla_scale, F32)

    w_a, w_v, w_c, w_g = _regroup_w_in(w_in)
    wuq = mla_w_uq.reshape(depth, MLA_Q_RANK, MLA_HEADS, MLA_NOPE + MLA_ROPE)
    wuq = _pad_cols(wuq, 2 * LANES).reshape(depth, MLA_Q_RANK, MLA_HEADS * 2 * LANES)
    wuq, wukv = bf(wuq), bf(mla_w_ukv)
    mla_gains = jnp.stack([mla_q_norm, mla_kv_norm], axis=1)[:, :, None, :]

    for l in range(depth):
        lambda_init = 0.8 - 0.6 * math.exp(-0.3 * l)

        a = _ffn_up(hb, ffn1_norm_col, ffn1_w_gu, l)
        h = _mm_residual(a, ffn1_w_down, l, h, 0.5)

        qk, u = _mm_rope(h, w_a, l, cs_a, cos64, sin64, DIFF_QK_DIM // 2, (True,) * 17, gain=mix_norm3)
        vv = _mm_plain(u, w_v, l)
        cn = _mm_rmsnorm(u, w_c, mla_gains, l)
        gqk = _mm_rope(u, w_g, l, cs_g, cos128, sin128, GQA_DIM // 2, (True,) * 10)

        o_diff = _diff_attention(qk, vv, diff_lambda, diff_subln3, l, lambda_init)

        q_mla = _mm_rope(cn, wuq, l, cs_q, cos64, sin64, MLA_ROPE // 2, (False, True) * MLA_HEADS, lead=0)
        kv_up = _mm_plain(cn, wukv, l, lead=1)
        o_mla = _mla_attention(q_mla, kv_up, qk)

        o_gqa = _window_gqa(gqk, vv, sink_flat, l)

        merged = _merge(u, o_diff, o_mla, o_gqa, w_gate, b_gate3, w_branch, l)
        memkv = _mm_plain(_rmsnorm(mem2, mem_norm3, l), xa_wkv_b, l)
        h, hb = _cross_attention(merged, h, w_o, xa_norm3, xa_wq, memkv, xa_wo, l)

        a = _ffn_up(hb, ffn2_norm_col, ffn2_w_gu, l)
        if l + 1 < depth:
            h, hb = _mm_residual(a, ffn2_w_down, l, h, 0.5, emit_bf16=True)
        else:
            h = _mm_residual(a, ffn2_w_down, l, h, 0.5)

    out = _rmsnorm(h, final_norm.reshape(1, d), None, out_dtype=F32)
    return out.reshape(b, s, d)
```

```python
import functools
import math

import jax
import jax.numpy as jnp
from jax import lax
from jax.experimental import pallas as pl
from jax.experimental.pallas import tpu as pltpu

F32 = jnp.float32
BF16 = jnp.bfloat16

DIFF_HEADS = 8
DIFF_QK_DIM = 64
DIFF_V_DIM = 128
MLA_HEADS = 8
MLA_Q_RANK = 512
MLA_KV_RANK = 512
MLA_NOPE = 128
MLA_ROPE = 64
MLA_V = 128
GQA_HEADS = 8
GQA_KV_HEADS = 2
GQA_GROUP = GQA_HEADS // GQA_KV_HEADS
GQA_DIM = 128
WINDOW = 128
N_BRANCH = 3
BRANCH_WIDTH = 1024
XA_HEADS = 4
XA_DIM = 128
D_FF = 5632
ROPE_THETA = 10000.0
EPS = 1e-6
DIFF_SUBLN_EPS = 1e-5
NEG_INF = -1e30
LOG2E = math.log2(math.e)

LANES = 128
V7X_VMEM_LIMIT_BYTES = 56 * 1024 * 1024


def _cparams(semantics):
    return pltpu.CompilerParams(dimension_semantics=semantics, vmem_limit_bytes=V7X_VMEM_LIMIT_BYTES)


def _pick(dim, pref):
    if dim <= pref:
        return dim
    b = pref
    while dim % b:
        b //= 2
    return b


def _rope_table_kernel(pos_ref, inv_ref, sign_ref, cos_ref, sin_ref):
    ang = pos_ref[...] * inv_ref[...]
    cos_ref[...] = jnp.cos(ang)
    sin_ref[...] = jnp.sin(ang) * sign_ref[...]


def _rope_tables(pos_b, dim):
    s = pos_b.shape[0]
    half = dim // 2
    inv = ROPE_THETA ** (-jnp.arange(0, dim, 2, dtype=F32) / dim)
    inv_row = jnp.tile(inv, LANES // half)[None, :]
    sign_row = jnp.tile(jnp.concatenate([-jnp.ones((half,), F32), jnp.ones((half,), F32)]), LANES // dim)[None, :]
    bm = _pick(s, 1024)
    return pl.pallas_call(
        _rope_table_kernel,
        grid=(s // bm,),
        in_specs=[pl.BlockSpec((bm, LANES), lambda i: (i, 0)),
                  pl.BlockSpec((1, LANES), lambda i: (0, 0)),
                  pl.BlockSpec((1, LANES), lambda i: (0, 0))],
        out_specs=[pl.BlockSpec((bm, LANES), lambda i: (i, 0))] * 2,
        out_shape=[jax.ShapeDtypeStruct((s, LANES), F32)] * 2,
        compiler_params=_cparams(("parallel",)),
        name="rope_tables",
    )(pos_b, inv_row, sign_row)


def _rope_block(x, cos, sin_signed, half):
    if 2 * half == LANES:
        partner = pltpu.roll(x, half, axis=1)
    else:
        lane = lax.broadcasted_iota(jnp.int32, x.shape, 1)
        first = (lane % (2 * half)) < half
        partner = jnp.where(first, pltpu.roll(x, LANES - half, axis=1), pltpu.roll(x, half, axis=1))
    return x * cos + partner * sin_signed


def _rmsnorm_kernel(x_ref, g_ref, o_ref, *, eps):
    x = x_ref[...]
    y = x * lax.rsqrt(jnp.mean(x * x, axis=-1, keepdims=True) + eps)
    o_ref[...] = (y * g_ref[...]).astype(o_ref.dtype)


def _rmsnorm(x, g_stack, l, out_dtype=BF16):
    m, d = x.shape
    bm = _pick(m, 512)
    if l is None:
        g_spec = pl.BlockSpec((1, d), lambda i: (0, 0))
    else:
        g_spec = pl.BlockSpec((None, 1, d), lambda i: (l, 0, 0))
    return pl.pallas_call(
        functools.partial(_rmsnorm_kernel, eps=EPS),
        grid=(m // bm,),
        in_specs=[pl.BlockSpec((bm, d), lambda i: (i, 0)), g_spec],
        out_specs=pl.BlockSpec((bm, d), lambda i: (i, 0)),
        out_shape=jax.ShapeDtypeStruct((m, d), out_dtype),
        compiler_params=_cparams(("parallel",)),
        name="rmsnorm",
    )(x, g_stack)


def _cast_weights_once(pairs):
    @pl.when(pl.program_id(1) == 0)
    def _():
        for w_ref, w_s in pairs:
            w_s[...] = w_ref[...].astype(BF16)


def _norm_rows(x, g, eps=EPS):
    return (x * lax.rsqrt(jnp.mean(x * x, axis=-1, keepdims=True) + eps) * g).astype(BF16)


def _mm_swiglu_kernel(x_ref, gc_ref, wg_ref, wu_ref, o_ref, wg_s, wu_s, *, sub):
    @pl.when(pl.program_id(1) == 0)
    def _():
        gc = gc_ref[...]
        wg_s[...] = (wg_ref[...] * gc).astype(BF16)
        wu_s[...] = (wu_ref[...] * gc).astype(BF16)

    def rows_tile(t, carry):
        rows = pl.ds(pl.multiple_of(t * sub, sub), sub)
        x = x_ref[rows, :]
        xf = x.astype(F32)
        rs = lax.rsqrt(jnp.mean(xf * xf, axis=-1, keepdims=True) + EPS)
        g = jnp.dot(x, wg_s[...], preferred_element_type=F32) * rs
        u = jnp.dot(x, wu_s[...], preferred_element_type=F32) * rs
        o_ref[rows, :] = (g * jax.nn.sigmoid(g) * u).astype(o_ref.dtype)
        return carry

    lax.fori_loop(0, x_ref.shape[0] // sub, rows_tile, 0)


def _ffn_up(hb, gain_col, w_gu, l):
    m, d = hb.shape
    n = w_gu.shape[-1] // 2
    bm = _pick(m, 2048)
    bn = _pick(n, 512)
    nj = n // bn
    return pl.pallas_call(
        functools.partial(_mm_swiglu_kernel, sub=_pick(bm, 1024)),
        grid=(nj, m // bm),
        in_specs=[pl.BlockSpec((bm, d), lambda j, i: (i, 0)),
                  pl.BlockSpec((None, d, 1), lambda j, i: (l, 0, 0)),
                  pl.BlockSpec((None, d, bn), lambda j, i: (l, 0, j)),
                  pl.BlockSpec((None, d, bn), lambda j, i: (l, 0, j + nj))],
        out_specs=pl.BlockSpec((bm, bn), lambda j, i: (i, j)),
        out_shape=jax.ShapeDtypeStruct((m, n), BF16),
        scratch_shapes=[pltpu.VMEM((d, bn), BF16), pltpu.VMEM((d, bn), BF16)],
        compiler_params=_cparams(("arbitrary", "arbitrary")),
        name="ffn_up",
    )(hb, gain_col, w_gu, w_gu)


def _mm_residual_kernel(a_ref, w_ref, r_ref, o_ref, *rest, scale):
    w_s = rest[-1]
    _cast_weights_once(((w_ref, w_s),))
    out = r_ref[...] + scale * jnp.dot(a_ref[...], w_s[...], preferred_element_type=F32)
    o_ref[...] = out
    if len(rest) == 2:
        rest[0][...] = out.astype(BF16)


def _mm_residual(a, w, l, res, scale, emit_bf16=False):
    m, k = a.shape
    n = w.shape[-1]
    bm, bn = _pick(m, 256), _pick(n, 1024)
    o_spec = pl.BlockSpec((bm, bn), lambda j, i: (i, j))
    out_shape = jax.ShapeDtypeStruct((m, n), F32)
    return pl.pallas_call(
        functools.partial(_mm_residual_kernel, scale=scale),
        grid=(n // bn, m // bm),
        in_specs=[pl.BlockSpec((bm, k), lambda j, i: (i, 0)),
                  pl.BlockSpec((None, k, bn), lambda j, i: (l, 0, j), pipeline_mode=pl.Buffered(1)),
                  o_spec],
        out_specs=[o_spec, o_spec] if emit_bf16 else o_spec,
        out_shape=[out_shape, jax.ShapeDtypeStruct((m, n), BF16)] if emit_bf16 else out_shape,
        scratch_shapes=[pltpu.VMEM((k, bn), BF16)],
        compiler_params=_cparams(("arbitrary", "arbitrary")),
        name="mm_residual",
    )(a, w, res)


def _mm_plain_kernel(x_ref, w_ref, o_ref):
    o_ref[...] = jnp.dot(x_ref[...], w_ref[...], preferred_element_type=F32).astype(o_ref.dtype)


def _rows_spec(x, bm, lead):
    k = x.shape[-1]
    if lead is None:
        return pl.BlockSpec((bm, k), lambda i: (i, 0))
    return pl.BlockSpec((None, bm, k), lambda i: (lead, i, 0))


def _mm_plain(x, w, l, lead=None):
    m, k = x.shape[-2:]
    n = w.shape[-1]
    bm = _pick(m, 1024)
    w_spec = pl.BlockSpec((None, k, n), lambda i: (l, 0, 0))
    return pl.pallas_call(
        _mm_plain_kernel,
        grid=(m // bm,),
        in_specs=[_rows_spec(x, bm, lead), w_spec],
        out_specs=pl.BlockSpec((bm, n), lambda i: (i, 0)),
        out_shape=jax.ShapeDtypeStruct((m, n), BF16),
        compiler_params=_cparams(("parallel",)),
        name="mm_plain",
    )(x, w)


def _mm_rope_kernel(x_ref, w_ref, cs_ref, cos_ref, sin_ref, *rest, half, pattern):
    if len(rest) == 3:
        g_ref, o_ref, u_ref = rest
        x = _norm_rows(x_ref[...], g_ref[...])
        u_ref[...] = x
    else:
        (o_ref,) = rest
        x = x_ref[...]
    acc = jnp.dot(x, w_ref[...], preferred_element_type=F32) * cs_ref[...]
    cos = cos_ref[...]
    sin = sin_ref[...]
    for b, roped in enumerate(pattern):
        blk = acc[:, b * LANES:(b + 1) * LANES]
        if roped:
            blk = _rope_block(blk, cos, sin, half)
        o_ref[:, b * LANES:(b + 1) * LANES] = blk.astype(o_ref.dtype)


def _mm_rope(x, w, l, colscale, cos, sin, half, pattern, gain=None, lead=None):
    m, k = x.shape[-2:]
    n = w.shape[-1]
    assert n == LANES * len(pattern)
    bm = _pick(m, 512 if gain is not None else 1024)
    x_spec = pl.BlockSpec((bm, k), lambda i: (i, 0))
    o_spec = pl.BlockSpec((bm, n), lambda i: (i, 0))
    out_shape = jax.ShapeDtypeStruct((m, n), BF16)
    in_specs = [_rows_spec(x, bm, lead),
                pl.BlockSpec((None, k, n), lambda i: (l, 0, 0)),
                pl.BlockSpec((1, n), lambda i: (0, 0)),
                pl.BlockSpec((bm, LANES), lambda i: (i, 0)),
                pl.BlockSpec((bm, LANES), lambda i: (i, 0))]
    args = (x, w, colscale, cos, sin)
    if gain is not None:
        in_specs.append(pl.BlockSpec((None, 1, k), lambda i: (l, 0, 0)))
        args += (gain,)
    return pl.pallas_call(
        functools.partial(_mm_rope_kernel, half=half, pattern=pattern),
        grid=(m // bm,),
        in_specs=in_specs,
        out_specs=o_spec if gain is None else [o_spec, x_spec],
        out_shape=out_shape if gain is None else [out_shape, jax.ShapeDtypeStruct((m, k), BF16)],
        compiler_params=_cparams(("parallel",)),
        name="mm_rope",
    )(*args)


def _mm_rmsnorm_kernel(x_ref, w_ref, g_ref, o_ref):
    acc = jnp.dot(x_ref[...], w_ref[...], preferred_element_type=F32)
    y = acc * lax.rsqrt(jnp.mean(acc * acc, axis=-1, keepdims=True) + EPS)
    o_ref[...] = (y * g_ref[...]).astype(o_ref.dtype)


def _mm_rmsnorm(x, w, g, l):
    m, k = x.shape
    _, t, _, r = g.shape
    bm = _pick(m, 1024)
    return pl.pallas_call(
        _mm_rmsnorm_kernel,
        grid=(t, m // bm),
        in_specs=[pl.BlockSpec((bm, k), lambda j, i: (i, 0)),
                  pl.BlockSpec((None, k, r), lambda j, i: (l, 0, j)),
                  pl.BlockSpec((None, None, 1, r), lambda j, i: (l, j, 0, 0))],
        out_specs=pl.BlockSpec((None, bm, r), lambda j, i: (j, i, 0)),
        out_shape=jax.ShapeDtypeStruct((t, m, r), BF16),
        compiler_params=_cparams(("parallel", "parallel")),
        name="mm_rmsnorm",
    )(x, w, g)


def _logits(q, k, mlane):
    s = lax.dot_general(q, k, (((1,), (1,)), ((), ())), preferred_element_type=F32)
    for b in range(s.shape[1] // LANES):
        blk = s[:, b * LANES:(b + 1) * LANES]
        mlane = blk if mlane is None else jnp.maximum(mlane, blk)
    return s, mlane


def _row_max(mlane):
    return jnp.broadcast_to(jnp.max(mlane, axis=1, keepdims=True), mlane.shape)


def _weigh(s, m, v1, acc):
    ps = [jnp.exp2(s[:, b * LANES:(b + 1) * LANES] - m).astype(BF16) for b in range(s.shape[1] // LANES)]
    pv = jnp.dot(jnp.concatenate(ps, axis=1), v1, preferred_element_type=F32)
    return pv if acc is None else acc + pv


def _softmax_pv(qs, k_chunk, v1_chunk, nchunks):
    n = len(qs)
    logits = [[None] * nchunks for _ in range(n)]
    acc = [None] * n
    mlane = None
    for j in range(nchunks):
        logits[0][j], mlane = _logits(qs[0], k_chunk(j), mlane)
    m = _row_max(mlane)
    for p in range(1, n):
        mlane = None
        for j in range(nchunks):
            acc[p - 1] = _weigh(logits[p - 1][j], m, v1_chunk(j), acc[p - 1])
            logits[p][j], mlane = _logits(qs[p], k_chunk(j), mlane)
        m = _row_max(mlane)
    for j in range(nchunks):
        acc[n - 1] = _weigh(logits[n - 1][j], m, v1_chunk(j), acc[n - 1])
    return [a[:, :a.shape[1] - LANES] / a[:, a.shape[1] - LANES:] for a in acc]


def _attn_tiles(s):
    rows = _pick(s, 512)
    return rows, (2 if s % (2 * rows) == 0 else 1), _pick(s, 1024)


def _diff_attn_kernel(lp_ref, q_ref, k_ref, v_ref, g_ref, o_ref, *, lambda_init, tk, nsub):
    bq = q_ref.shape[0] // nsub
    ones = jnp.ones((tk, LANES), BF16)
    k_chunk = lambda j: k_ref[j * tk:(j + 1) * tk, :]
    v1_chunk = lambda j: jnp.concatenate([v_ref[j * tk:(j + 1) * tk, :], ones], axis=1)
    qs = []
    for r in range(nsub):
        q = q_ref[r * bq:(r + 1) * bq, :]
        lane = lax.broadcasted_iota(jnp.int32, q.shape, 1)
        zero = jnp.zeros_like(q)
        qs.append(jnp.concatenate([jnp.where(lane < DIFF_QK_DIM, q, zero),
                                   jnp.where(lane >= DIFF_QK_DIM, q, zero)], axis=0))
    outs = _softmax_pv(qs, k_chunk, v1_chunk, k_ref.shape[0] // tk)
    lp = lp_ref[...]
    lam = (jnp.exp(jnp.sum(lp[0:1] * lp[1:2], axis=1, keepdims=True))
           - jnp.exp(jnp.sum(lp[2:3] * lp[3:4], axis=1, keepdims=True)) + lambda_init)
    for r in range(nsub):
        o = outs[r][:bq] - lam * outs[r][bq:]
        y = o * lax.rsqrt(jnp.mean(o * o, axis=-1, keepdims=True) + DIFF_SUBLN_EPS)
        o_ref[r * bq:(r + 1) * bq, :] = (y * g_ref[...] * (1.0 - lambda_init)).astype(o_ref.dtype)


def _diff_attention(qk, v, diff_lambda, diff_subln, l, lambda_init):
    s = qk.shape[0]
    prob_rows, nsub, tk = _attn_tiles(s)
    rows = prob_rows // 2 * nsub
    h = DIFF_HEADS
    return pl.pallas_call(
        functools.partial(_diff_attn_kernel, lambda_init=lambda_init, tk=tk, nsub=nsub),
        grid=(h, s // rows),
        in_specs=[pl.BlockSpec((None, 4, DIFF_QK_DIM), lambda hh, i: (l, 0, 0)),
                  pl.BlockSpec((rows, LANES), lambda hh, i: (i, hh)),
                  pl.BlockSpec((s, LANES), lambda hh, i: (0, h + hh)),
                  pl.BlockSpec((s, LANES), lambda hh, i: (0, hh)),
                  pl.BlockSpec((None, 1, DIFF_V_DIM), lambda hh, i: (l, 0, 0))],
        out_specs=pl.BlockSpec((rows, LANES), lambda hh, i: (i, hh)),
        out_shape=jax.ShapeDtypeStruct((s, h * DIFF_V_DIM), BF16),
        compiler_params=_cparams(("parallel", "parallel")),
        name="diff_attention",
    )(diff_lambda, qk, qk, v, diff_subln)


def _mla_attn_kernel(q_ref, kn_ref, kr_ref, v_ref, o_ref, *, tk, nsub):
    bq = q_ref.shape[0] // nsub
    ones = jnp.ones((tk, LANES), BF16)
    k_chunk = lambda j: jnp.concatenate([kn_ref[j * tk:(j + 1) * tk, :], kr_ref[j * tk:(j + 1) * tk, :]], axis=1)
    v1_chunk = lambda j: jnp.concatenate([v_ref[j * tk:(j + 1) * tk, :], ones], axis=1)
    qs = [q_ref[r * bq:(r + 1) * bq, :] for r in range(nsub)]
    outs = _softmax_pv(qs, k_chunk, v1_chunk, kn_ref.shape[0] // tk)
    for r in range(nsub):
        o_ref[r * bq:(r + 1) * bq, :] = outs[r].astype(o_ref.dtype)


def _mla_attention(q, kv_up, qk):
    s = q.shape[0]
    bq, nsub, tk = _attn_tiles(s)
    rows = bq * nsub
    h = MLA_HEADS
    kr_blk = 2 * DIFF_HEADS
    return pl.pallas_call(
        functools.partial(_mla_attn_kernel, tk=tk, nsub=nsub),
        grid=(h, s // rows),
        in_specs=[pl.BlockSpec((rows, 2 * LANES), lambda hh, i: (i, hh)),
                  pl.BlockSpec((s, LANES), lambda hh, i: (0, 2 * hh)),
                  pl.BlockSpec((s, LANES), lambda hh, i: (0, kr_blk)),
                  pl.BlockSpec((s, LANES), lambda hh, i: (0, 2 * hh + 1))],
        out_specs=pl.BlockSpec((rows, LANES), lambda hh, i: (i, hh)),
        out_shape=jax.ShapeDtypeStruct((s, h * MLA_V), BF16),
        compiler_params=_cparams(("parallel", "parallel")),
        name="mla_attention",
    )(q, kv_up, qk, kv_up)


def _gqa_kernel(sink_ref, q_ref, kp_ref, kc_ref, kn_ref, vp_ref, vc_ref, vn_ref, o_ref, *, layer, seq, bq):
    g = pl.program_id(0)
    i = pl.program_id(1)
    nk = bq + 2 * WINDOW
    kb = jnp.concatenate([kp_ref[...], kc_ref[...], kn_ref[...]], axis=0)
    vb = jnp.concatenate([vp_ref[...], vc_ref[...], vn_ref[...]], axis=0)
    v1 = jnp.concatenate([vb, jnp.ones_like(vb)], axis=1)
    qpos = i * bq + lax.broadcasted_iota(jnp.int32, (bq, nk), 0)
    kpos = i * bq - WINDOW + lax.broadcasted_iota(jnp.int32, (bq, nk), 1)
    valid = (jnp.abs(kpos - qpos) <= WINDOW) & (kpos >= 0) & (kpos < seq)
    bias = jnp.where(valid, 0.0, NEG_INF).astype(F32)
    for hh in range(GQA_GROUP):
        q = q_ref[:, hh * GQA_DIM:(hh + 1) * GQA_DIM]
        s = lax.dot_general(q, kb, (((1,), (1,)), ((), ())), preferred_element_type=F32) + bias
        sink = sink_ref[layer * GQA_HEADS + g * GQA_GROUP + hh] * LOG2E
        m = jnp.maximum(jnp.max(s, axis=1, keepdims=True), sink)
        pv = jnp.dot(jnp.exp2(s - m).astype(BF16), v1, preferred_element_type=F32)
        denom = pv[:, GQA_DIM:] + jnp.exp2(sink - m)
        o_ref[:, hh * GQA_DIM:(hh + 1) * GQA_DIM] = (pv[:, :GQA_DIM] / denom).astype(o_ref.dtype)


def _window_gqa(gqk, vv, sink_flat, l):
    s = gqk.shape[0]
    bq = _pick(s, 256)
    assert bq % WINDOW == 0 and WINDOW == LANES
    nb = s // bq
    r = bq // WINDOW
    nh = s // WINDOW
    qw = GQA_GROUP * GQA_DIM
    kcol = GQA_HEADS
    vcol = DIFF_HEADS

    def prev(i):
        return jnp.maximum(i * r - 1, 0)

    def nxt(i):
        return jnp.minimum((i + 1) * r, nh - 1)

    return pl.pallas_call(
        functools.partial(_gqa_kernel, layer=l, seq=s, bq=bq),
        grid=(GQA_KV_HEADS, nb),
        in_specs=[pl.BlockSpec(memory_space=pltpu.SMEM),
                  pl.BlockSpec((bq, qw), lambda g, i: (i, g)),
                  pl.BlockSpec((WINDOW, LANES), lambda g, i: (prev(i), kcol + g)),
                  pl.BlockSpec((bq, LANES), lambda g, i: (i, kcol + g)),
                  pl.BlockSpec((WINDOW, LANES), lambda g, i: (nxt(i), kcol + g)),
                  pl.BlockSpec((WINDOW, LANES), lambda g, i: (prev(i), vcol + g)),
                  pl.BlockSpec((bq, LANES), lambda g, i: (i, vcol + g)),
                  pl.BlockSpec((WINDOW, LANES), lambda g, i: (nxt(i), vcol + g))],
        out_specs=pl.BlockSpec((bq, qw), lambda g, i: (i, g)),
        out_shape=jax.ShapeDtypeStruct((s, GQA_HEADS * GQA_DIM), BF16),
        compiler_params=_cparams(("parallel", "parallel")),
        name="window_gqa",
    )(sink_flat, gqk, gqk, gqk, gqk, vv, vv, vv)


def _merge_kernel(u_ref, o0_ref, o1_ref, o2_ref, wg0_ref, wg1_ref, wg2_ref, b0_ref, b1_ref, b2_ref,
                  wb0_ref, wb1_ref, wb2_ref, out_ref, wg0_s, wg1_s, wg2_s, wb0_s, wb1_s, wb2_s):
    _cast_weights_once(((wg0_ref, wg0_s), (wg1_ref, wg1_s), (wg2_ref, wg2_s),
                        (wb0_ref, wb0_s), (wb1_ref, wb1_s), (wb2_ref, wb2_s)))
    u = u_ref[...]
    acc = None
    for o_ref, wg_s, b_ref, wb_s in ((o0_ref, wg0_s, b0_ref, wb0_s),
                                     (o1_ref, wg1_s, b1_ref, wb1_s),
                                     (o2_ref, wg2_s, b2_ref, wb2_s)):
        gate = jax.nn.sigmoid(jnp.dot(u, wg_s[...], preferred_element_type=F32) + b_ref[...])
        term = jnp.dot(o_ref[...], wb_s[...], preferred_element_type=F32) * gate
        acc = term if acc is None else acc + term
    out_ref[...] = acc.astype(out_ref.dtype)


def _merge(u, o_diff, o_mla, o_gqa, w_gate, b_gate, w_branch, l):
    m, d = u.shape
    bw = o_diff.shape[1]
    bm = _pick(m, 1024)
    bn = _pick(d, 256)
    nj = d // bn
    o_spec = pl.BlockSpec((bm, bw), lambda j, i: (i, 0))

    def wg_spec(n):
        return pl.BlockSpec((None, d, bn), lambda j, i: (l, 0, n * nj + j))

    def b_spec(n):
        return pl.BlockSpec((None, 1, bn), lambda j, i: (l, 0, n * nj + j))

    def wb_spec(n):
        return pl.BlockSpec((None, None, bw, bn), lambda j, i: (l, n, 0, j))

    return pl.pallas_call(
        _merge_kernel,
        grid=(nj, m // bm),
        in_specs=[pl.BlockSpec((bm, d), lambda j, i: (i, 0)), o_spec, o_spec, o_spec,
                  wg_spec(0), wg_spec(1), wg_spec(2), b_spec(0), b_spec(1), b_spec(2),
                  wb_spec(0), wb_spec(1), wb_spec(2)],
        out_specs=pl.BlockSpec((bm, bn), lambda j, i: (i, j)),
        out_shape=jax.ShapeDtypeStruct((m, d), BF16),
        scratch_shapes=[pltpu.VMEM((d, bn), BF16)] * 3 + [pltpu.VMEM((bw, bn), BF16)] * 3,
        compiler_params=_cparams(("arbitrary", "arbitrary")),
        name="branch_merge",
    )(u, o_diff, o_mla, o_gqa, w_gate, w_gate, w_gate, b_gate, b_gate, b_gate,
      w_branch, w_branch, w_branch)


def _xattn_kernel(mg_ref, h_ref, wm_ref, g_ref, wq_ref, k_ref, v_ref, wo_ref, o_ref, ob_ref, wm_s, wq_s, wo_s):
    @pl.when(pl.program_id(0) == 0)
    def _():
        wm_s[...] = wm_ref[...].astype(BF16)
        wq_s[...] = wq_ref[...].astype(BF16)
        wo_s[...] = wo_ref[...].astype(BF16)

    h1 = h_ref[...] + jnp.dot(mg_ref[...], wm_s[...], preferred_element_type=F32)
    hn = _norm_rows(h1, g_ref[...])
    q = jnp.dot(hn, wq_s[...], preferred_element_type=F32) * (XA_DIM ** -0.5 * LOG2E)
    q = q.astype(BF16)
    heads = []
    for hh in range(XA_HEADS):
        sl = slice(hh * XA_DIM, (hh + 1) * XA_DIM)
        s = lax.dot_general(q[:, sl], k_ref[:, sl], (((1,), (1,)), ((), ())), preferred_element_type=F32)
        p = jnp.exp2(s - jnp.max(s, axis=1, keepdims=True))
        denom = jnp.sum(p, axis=1, keepdims=True)
        heads.append((jnp.dot(p.astype(BF16), v_ref[:, sl], preferred_element_type=F32) / denom).astype(BF16))
    o = jnp.concatenate(heads, axis=1)
    out = h1 + jnp.dot(o, wo_s[...], preferred_element_type=F32)
    o_ref[...] = out
    ob_ref[...] = out.astype(BF16)


def _cross_attention(merged, h, w_o, gain, xa_wq, memkv, xa_wo, l):
    m, d = h.shape
    w = XA_HEADS * XA_DIM
    ml = memkv.shape[0]
    bm = _pick(m, 256)
    once = dict(pipeline_mode=pl.Buffered(1))
    return pl.pallas_call(
        _xattn_kernel,
        grid=(m // bm,),
        in_specs=[pl.BlockSpec((bm, d), lambda i: (i, 0)),
                  pl.BlockSpec((bm, d), lambda i: (i, 0)),
                  pl.BlockSpec((None, d, d), lambda i: (l, 0, 0), **once),
                  pl.BlockSpec((None, 1, d), lambda i: (l, 0, 0)),
                  pl.BlockSpec((None, d, w), lambda i: (l, 0, 0), **once),
                  pl.BlockSpec((ml, w), lambda i: (0, 0)),
                  pl.BlockSpec((ml, w), lambda i: (0, 1)),
                  pl.BlockSpec((None, w, d), lambda i: (l, 0, 0), **once)],
        out_specs=[pl.BlockSpec((bm, d), lambda i: (i, 0))] * 2,
        out_shape=[jax.ShapeDtypeStruct((m, d), F32), jax.ShapeDtypeStruct((m, d), BF16)],
        scratch_shapes=[pltpu.VMEM((d, d), BF16), pltpu.VMEM((d, w), BF16), pltpu.VMEM((w, d), BF16)],
        compiler_params=_cparams(("arbitrary",)),
        name="cross_attention",
    )(merged, h, w_o, gain, xa_wq, memkv, memkv, xa_wo)


def _pad_cols(w, width):
    return jnp.pad(w, ((0, 0),) * (w.ndim - 1) + ((0, width - w.shape[-1]),))


_DQ, _DV, _CQ, _KR, _GQ, _GV, _END = 0, 2048, 3072, 4096, 4160, 5440, 5696


def _regroup_kernel(wt_ref, a_ref, v_ref, c_ref, g_ref):
    def cols(lo, hi):
        return wt_ref[lo:hi, :].T

    kr = jnp.concatenate([wt_ref[_KR:_GQ, :], jnp.zeros((LANES - (_GQ - _KR), wt_ref.shape[1]), wt_ref.dtype)], axis=0)
    a_ref[...] = jnp.concatenate([cols(_DQ, _DV), kr.T], axis=1).astype(BF16)
    v_ref[...] = jnp.concatenate([cols(_DV, _CQ), cols(_GV, _END)], axis=1).astype(BF16)
    c_ref[...] = cols(_CQ, _KR).astype(BF16)
    g_ref[...] = cols(_GQ, _GV).astype(BF16)


def _regroup_w_in(w_in):
    depth, k, n = w_in.shape
    assert n == _END
    bk = _pick(k, 256)
    widths = (_DV - _DQ + LANES, (_CQ - _DV) + (_END - _GV), _KR - _CQ, _GV - _GQ)
    return pl.pallas_call(
        _regroup_kernel,
        grid=(depth, k // bk),
        in_specs=[pl.BlockSpec((None, n, bk), lambda l, i: (l, 0, i))],
        out_specs=[pl.BlockSpec((None, bk, w), lambda l, i: (l, i, 0)) for w in widths],
        out_shape=[jax.ShapeDtypeStruct((depth, k, w), BF16) for w in widths],
        compiler_params=_cparams(("parallel", "parallel")),
        name="regroup_w_in",
    )(jnp.swapaxes(w_in, 1, 2))


def kernel(x, mem, positions, ffn1_norm, ffn1_w_gu, ffn1_w_down, mix_norm, w_in, diff_lambda, diff_subln,
           mla_q_norm, mla_kv_norm, mla_w_uq, mla_w_ukv, gqa_sink, w_branch, w_gate, b_gate, w_o, xa_norm,
           mem_norm, xa_wq, xa_wkv, xa_wo, ffn2_norm, ffn2_w_gu, ffn2_w_down, final_norm):
    b, s, d = x.shape
    assert b == 1
    depth = ffn1_norm.shape[0]
    h = x.reshape(s, d)
    mem2 = mem.reshape(mem.shape[1], d)

    pos_b = jnp.broadcast_to(positions.reshape(s, 1).astype(F32), (s, LANES))
    cos64, sin64 = _rope_tables(pos_b, DIFF_QK_DIM)
    cos128, sin128 = _rope_tables(pos_b, GQA_DIM)

    bf = lambda w: w.astype(BF16)
    xa_wkv_b = bf(xa_wkv)
    b_gate3 = b_gate.reshape(depth, 1, -1)
    norm3 = lambda g: g.reshape(depth, 1, -1)
    mix_norm3, xa_norm3, mem_norm3 = map(norm3, (mix_norm, xa_norm, mem_norm))
    ffn1_norm_col, ffn2_norm_col = ffn1_norm.reshape(depth, d, 1), ffn2_norm.reshape(depth, d, 1)
    hb = bf(h)
    diff_subln3 = diff_subln.reshape(depth, 1, -1)
    sink_flat = gqa_sink.reshape(-1)

    diff_scale = DIFF_QK_DIM ** -0.5 * LOG2E
    mla_scale = (MLA_NOPE + MLA_ROPE) ** -0.5 * LOG2E
    gqa_scale = GQA_DIM ** -0.5 * LOG2E
    cs_a = jnp.concatenate([jnp.full((1, 1024), diff_scale, F32), jnp.ones((1, 1024 + LANES), F32)], axis=1)
    cs_g = jnp.concatenate([jnp.full((1, 1024), gqa_scale, F32), jnp.ones((1, 256), F32)], axis=1)
    cs_q = jnp.full((1, MLA_HEADS * 2 * LANES), mla_scale, F32)

    w_a, w_v, w_c, w_g = _regroup_w_in(w_in)
    wuq = mla_w_uq.reshape(depth, MLA_Q_RANK, MLA_HEADS, MLA_NOPE + MLA_ROPE)
    wuq = _pad_cols(wuq, 2 * LANES).reshape(depth, MLA_Q_RANK, MLA_HEADS * 2 * LANES)
    wuq, wukv = bf(wuq), bf(mla_w_ukv)
    mla_gains = jnp.stack([mla_q_norm, mla_kv_norm], axis=1)[:, :, None, :]

    for l in range(depth):
        lambda_init = 0.8 - 0.6 * math.exp(-0.3 * l)

        a = _ffn_up(hb, ffn1_norm_col, ffn1_w_gu, l)
        h = _mm_residual(a, ffn1_w_down, l, h, 0.5)

        qk, u = _mm_rope(h, w_a, l, cs_a, cos64, sin64, DIFF_QK_DIM // 2, (True,) * 17, gain=mix_norm3)
        vv = _mm_plain(u, w_v, l)
        cn = _mm_rmsnorm(u, w_c, mla_gains, l)
        gqk = _mm_rope(u, w_g, l, cs_g, cos128, sin128, GQA_DIM // 2, (True,) * 10)

        o_diff = _diff_attention(qk, vv, diff_lambda, diff_subln3, l, lambda_init)

        q_mla = _mm_rope(cn, wuq, l, cs_q, cos64, sin64, MLA_ROPE // 2, (False, True) * MLA_HEADS, lead=0)
        kv_up = _mm_plain(cn, wukv, l, lead=1)
        o_mla = _mla_attention(q_mla, kv_up, qk)

        o_gqa = _window_gqa(gqk, vv, sink_flat, l)

        merged = _merge(u, o_diff, o_mla, o_gqa, w_gate, b_gate3, w_branch, l)
        memkv = _mm_plain(_rmsnorm(mem2, mem_norm3, l), xa_wkv_b, l)
        h, hb = _cross_attention(merged, h, w_o, xa_norm3, xa_wq, memkv, xa_wo, l)

        a = _ffn_up(hb, ffn2_norm_col, ffn2_w_gu, l)
        if l + 1 < depth:
            h, hb = _mm_residual(a, ffn2_w_down, l, h, 0.5, emit_bf16=True)
        else:
            h = _mm_residual(a, ffn2_w_down, l, h, 0.5)

    out = _rmsnorm(h, final_norm.reshape(1, d), None, out_dtype=F32)
    return out.reshape(b, s, d)
```

```python
import functools
import math

import jax
import jax.numpy as jnp
from jax import lax
from jax.experimental import pallas as pl
from jax.experimental.pallas import tpu as pltpu

F32 = jnp.float32
BF16 = jnp.bfloat16

DIFF_HEADS = 8
DIFF_QK_DIM = 64
DIFF_V_DIM = 128
MLA_HEADS = 8
MLA_Q_RANK = 512
MLA_KV_RANK = 512
MLA_NOPE = 128
MLA_ROPE = 64
MLA_V = 128
GQA_HEADS = 8
GQA_KV_HEADS = 2
GQA_GROUP = GQA_HEADS // GQA_KV_HEADS
GQA_DIM = 128
WINDOW = 128
N_BRANCH = 3
BRANCH_WIDTH = 1024
XA_HEADS = 4
XA_DIM = 128
D_FF = 5632
ROPE_THETA = 10000.0
EPS = 1e-6
DIFF_SUBLN_EPS = 1e-5
NEG_INF = -1e30
LOG2E = math.log2(math.e)

LANES = 128
V7X_VMEM_LIMIT_BYTES = 56 * 1024 * 1024


def _cparams(semantics):
    return pltpu.CompilerParams(dimension_semantics=semantics, vmem_limit_bytes=V7X_VMEM_LIMIT_BYTES)


def _pick(dim, pref):
    if dim <= pref:
        return dim
    b = pref
    while dim % b:
        b //= 2
    return b


def _rope_table_kernel(pos_ref, inv_ref, sign_ref, cos_ref, sin_ref):
    ang = pos_ref[...] * inv_ref[...]
    cos_ref[...] = jnp.cos(ang)
    sin_ref[...] = jnp.sin(ang) * sign_ref[...]


def _rope_tables(pos_b, dim):
    s = pos_b.shape[0]
    half = dim // 2
    inv = ROPE_THETA ** (-jnp.arange(0, dim, 2, dtype=F32) / dim)
    inv_row = jnp.tile(inv, LANES // half)[None, :]
    sign_row = jnp.tile(jnp.concatenate([-jnp.ones((half,), F32), jnp.ones((half,), F32)]), LANES // dim)[None, :]
    bm = _pick(s, 1024)
    return pl.pallas_call(
        _rope_table_kernel,
        grid=(s // bm,),
        in_specs=[pl.BlockSpec((bm, LANES), lambda i: (i, 0)),
                  pl.BlockSpec((1, LANES), lambda i: (0, 0)),
                  pl.BlockSpec((1, LANES), lambda i: (0, 0))],
        out_specs=[pl.BlockSpec((bm, LANES), lambda i: (i, 0))] * 2,
        out_shape=[jax.ShapeDtypeStruct((s, LANES), F32)] * 2,
        compiler_params=_cparams(("parallel",)),
        name="rope_tables",
    )(pos_b, inv_row, sign_row)


def _rope_block(x, cos, sin_signed, half):
    if 2 * half == LANES:
        partner = pltpu.roll(x, half, axis=1)
    else:
        lane = lax.broadcasted_iota(jnp.int32, x.shape, 1)
        first = (lane % (2 * half)) < half
        partner = jnp.where(first, pltpu.roll(x, LANES - half, axis=1), pltpu.roll(x, half, axis=1))
    return x * cos + partner * sin_signed


def _rmsnorm_kernel(x_ref, g_ref, o_ref, *, eps):
    x = x_ref[...]
    y = x * lax.rsqrt(jnp.mean(x * x, axis=-1, keepdims=True) + eps)
    o_ref[...] = (y * g_ref[...]).astype(o_ref.dtype)


def _rmsnorm(x, g_stack, l, out_dtype=BF16):
    m, d = x.shape
    bm = _pick(m, 512)
    if l is None:
        g_spec = pl.BlockSpec((1, d), lambda i: (0, 0))
    else:
        g_spec = pl.BlockSpec((None, 1, d), lambda i: (l, 0, 0))
    return pl.pallas_call(
        functools.partial(_rmsnorm_kernel, eps=EPS),
        grid=(m // bm,),
        in_specs=[pl.BlockSpec((bm, d), lambda i: (i, 0)), g_spec],
        out_specs=pl.BlockSpec((bm, d), lambda i: (i, 0)),
        out_shape=jax.ShapeDtypeStruct((m, d), out_dtype),
        compiler_params=_cparams(("parallel",)),
        name="rmsnorm",
    )(x, g_stack)


def _cast_weights_once(pairs):
    @pl.when(pl.program_id(1) == 0)
    def _():
        for w_ref, w_s in pairs:
            w_s[...] = w_ref[...].astype(BF16)


def _norm_rows(x, g, eps=EPS):
    return (x * lax.rsqrt(jnp.mean(x * x, axis=-1, keepdims=True) + eps) * g).astype(BF16)


def _mm_swiglu_kernel(x_ref, gc_ref, wg_ref, wu_ref, o_ref, wg_s, wu_s, *, sub):
    @pl.when(pl.program_id(1) == 0)
    def _():
        gc = gc_ref[...]
        wg_s[...] = (wg_ref[...] * gc).astype(BF16)
        wu_s[...] = (wu_ref[...] * gc).astype(BF16)

    def rows_tile(t, carry):
        rows = pl.ds(pl.multiple_of(t * sub, sub), sub)
        x = x_ref[rows, :]
        xf = x.astype(F32)
        rs = lax.rsqrt(jnp.mean(xf * xf, axis=-1, keepdims=True) + EPS)
        g = jnp.dot(x, wg_s[...], preferred_element_type=F32) * rs
        u = jnp.dot(x, wu_s[...], preferred_element_type=F32) * rs
        o_ref[rows, :] = (g * jax.nn.sigmoid(g) * u).astype(o_ref.dtype)
        return carry

    lax.fori_loop(0, x_ref.shape[0] // sub, rows_tile, 0)


def _ffn_up(hb, gain_col, w_gu, l):
    m, d = hb.shape
    n = w_gu.shape[-1] // 2
    bm = _pick(m, 2048)
    bn = _pick(n, 512)
    nj = n // bn
    return pl.pallas_call(
        functools.partial(_mm_swiglu_kernel, sub=_pick(bm, 1024)),
        grid=(nj, m // bm),
        in_specs=[pl.BlockSpec((bm, d), lambda j, i: (i, 0)),
                  pl.BlockSpec((None, d, 1), lambda j, i: (l, 0, 0)),
                  pl.BlockSpec((None, d, bn), lambda j, i: (l, 0, j)),
                  pl.BlockSpec((None, d, bn), lambda j, i: (l, 0, j + nj))],
        out_specs=pl.BlockSpec((bm, bn), lambda j, i: (i, j)),
        out_shape=jax.ShapeDtypeStruct((m, n), BF16),
        scratch_shapes=[pltpu.VMEM((d, bn), BF16), pltpu.VMEM((d, bn), BF16)],
        compiler_params=_cparams(("arbitrary", "arbitrary")),
        name="ffn_up",
    )(hb, gain_col, w_gu, w_gu)


def _mm_residual_kernel(a_ref, w_ref, r_ref, o_ref, *rest, scale):
    w_s = rest[-1]
    _cast_weights_once(((w_ref, w_s),))
    out = r_ref[...] + scale * jnp.dot(a_ref[...], w_s[...], preferred_element_type=F32)
    o_ref[...] = out
    if len(rest) == 2:
        rest[0][...] = out.astype(BF16)


def _mm_residual(a, w, l, res, scale, emit_bf16=False):
    m, k = a.shape
    n = w.shape[-1]
    bm, bn = _pick(m, 256), _pick(n, 1024)
    o_spec = pl.BlockSpec((bm, bn), lambda j, i: (i, j))
    out_shape = jax.ShapeDtypeStruct((m, n), F32)
    return pl.pallas_call(
        functools.partial(_mm_residual_kernel, scale=scale),
        grid=(n // bn, m // bm),
        in_specs=[pl.BlockSpec((bm, k), lambda j, i: (i, 0)),
                  pl.BlockSpec((None, k, bn), lambda j, i: (l, 0, j), pipeline_mode=pl.Buffered(1)),
                  o_spec],
        out_specs=[o_spec, o_spec] if emit_bf16 else o_spec,
        out_shape=[out_shape, jax.ShapeDtypeStruct((m, n), BF16)] if emit_bf16 else out_shape,
        scratch_shapes=[pltpu.VMEM((k, bn), BF16)],
        compiler_params=_cparams(("arbitrary", "arbitrary")),
        name="mm_residual",
    )(a, w, res)


def _mm_plain_kernel(x_ref, w_ref, o_ref):
    o_ref[...] = jnp.dot(x_ref[...], w_ref[...], preferred_element_type=F32).astype(o_ref.dtype)


def _rows_spec(x, bm, lead):
    k = x.shape[-1]
    if lead is None:
        return pl.BlockSpec((bm, k), lambda i: (i, 0))
    return pl.BlockSpec((None, bm, k), lambda i: (lead, i, 0))


def _mm_plain(x, w, l, lead=None):
    m, k = x.shape[-2:]
    n = w.shape[-1]
    bm = _pick(m, 1024)
    w_spec = pl.BlockSpec((None, k, n), lambda i: (l, 0, 0))
    return pl.pallas_call(
        _mm_plain_kernel,
        grid=(m // bm,),
        in_specs=[_rows_spec(x, bm, lead), w_spec],
        out_specs=pl.BlockSpec((bm, n), lambda i: (i, 0)),
        out_shape=jax.ShapeDtypeStruct((m, n), BF16),
        compiler_params=_cparams(("parallel",)),
        name="mm_plain",
    )(x, w)


def _mm_rope_kernel(x_ref, w_ref, cs_ref, cos_ref, sin_ref, *rest, half, pattern):
    if len(rest) == 3:
        g_ref, o_ref, u_ref = rest
        x = _norm_rows(x_ref[...], g_ref[...])
        u_ref[...] = x
    else:
        (o_ref,) = rest
        x = x_ref[...]
    acc = jnp.dot(x, w_ref[...], preferred_element_type=F32) * cs_ref[...]
    cos = cos_ref[...]
    sin = sin_ref[...]
    for b, roped in enumerate(pattern):
        blk = acc[:, b * LANES:(b + 1) * LANES]
        if roped:
            blk = _rope_block(blk, cos, sin, half)
        o_ref[:, b * LANES:(b + 1) * LANES] = blk.astype(o_ref.dtype)


def _mm_rope(x, w, l, colscale, cos, sin, half, pattern, gain=None, lead=None):
    m, k = x.shape[-2:]
    n = w.shape[-1]
    assert n == LANES * len(pattern)
    bm = _pick(m, 512 if gain is not None else 1024)
    x_spec = pl.BlockSpec((bm, k), lambda i: (i, 0))
    o_spec = pl.BlockSpec((bm, n), lambda i: (i, 0))
    out_shape = jax.ShapeDtypeStruct((m, n), BF16)
    in_specs = [_rows_spec(x, bm, lead),
                pl.BlockSpec((None, k, n), lambda i: (l, 0, 0)),
                pl.BlockSpec((1, n), lambda i: (0, 0)),
                pl.BlockSpec((bm, LANES), lambda i: (i, 0)),
                pl.BlockSpec((bm, LANES), lambda i: (i, 0))]
    args = (x, w, colscale, cos, sin)
    if gain is not None:
        in_specs.append(pl.BlockSpec((None, 1, k), lambda i: (l, 0, 0)))
        args += (gain,)
    return pl.pallas_call(
        functools.partial(_mm_rope_kernel, half=half, pattern=pattern),
        grid=(m // bm,),
        in_specs=in_specs,
        out_specs=o_spec if gain is None else [o_spec, x_spec],
        out_shape=out_shape if gain is None else [out_shape, jax.ShapeDtypeStruct((m, k), BF16)],
        compiler_params=_cparams(("parallel",)),
        name="mm_rope",
    )(*args)


def _mm_rmsnorm_kernel(x_ref, w_ref, g_ref, o_ref):
    acc = jnp.dot(x_ref[...], w_ref[...], preferred_element_type=F32)
    y = acc * lax.rsqrt(jnp.mean(acc * acc, axis=-1, keepdims=True) + EPS)
    o_ref[...] = (y * g_ref[...]).astype(o_ref.dtype)


def _mm_rmsnorm(x, w, g, l):
    m, k = x.shape
    _, t, _, r = g.shape
    bm = _pick(m, 1024)
    return pl.pallas_call(
        _mm_rmsnorm_kernel,
        grid=(t, m // bm),
        in_specs=[pl.BlockSpec((bm, k), lambda j, i: (i, 0)),
                  pl.BlockSpec((None, k, r), lambda j, i: (l, 0, j)),
                  pl.BlockSpec((None, None, 1, r), lambda j, i: (l, j, 0, 0))],
        out_specs=pl.BlockSpec((None, bm, r), lambda j, i: (j, i, 0)),
        out_shape=jax.ShapeDtypeStruct((t, m, r), BF16),
        compiler_params=_cparams(("parallel", "parallel")),
        name="mm_rmsnorm",
    )(x, w, g)


def _logits(q, k, mlane):
    s = lax.dot_general(q, k, (((1,), (1,)), ((), ())), preferred_element_type=F32)
    for b in range(s.shape[1] // LANES):
        blk = s[:, b * LANES:(b + 1) * LANES]
        mlane = blk if mlane is None else jnp.maximum(mlane, blk)
    return s, mlane


def _row_max(mlane):
    return jnp.broadcast_to(jnp.max(mlane, axis=1, keepdims=True), mlane.shape)


def _weigh(s, m, v1, acc):
    ps = [jnp.exp2(s[:, b * LANES:(b + 1) * LANES] - m).astype(BF16) for b in range(s.shape[1] // LANES)]
    pv = jnp.dot(jnp.concatenate(ps, axis=1), v1, preferred_element_type=F32)
    return pv if acc is None else acc + pv


def _softmax_pv(qs, k_chunk, v1_chunk, nchunks):
    n = len(qs)
    logits = [[None] * nchunks for _ in range(n)]
    acc = [None] * n
    mlane = None
    for j in range(nchunks):
        logits[0][j], mlane = _logits(qs[0], k_chunk(j), mlane)
    m = _row_max(mlane)
    for p in range(1, n):
        mlane = None
        for j in range(nchunks):
            acc[p - 1] = _weigh(logits[p - 1][j], m, v1_chunk(j), acc[p - 1])
            logits[p][j], mlane = _logits(qs[p], k_chunk(j), mlane)
        m = _row_max(mlane)
    for j in range(nchunks):
        acc[n - 1] = _weigh(logits[n - 1][j], m, v1_chunk(j), acc[n - 1])
    return [a[:, :a.shape[1] - LANES] / a[:, a.shape[1] - LANES:] for a in acc]


def _attn_tiles(s):
    rows = _pick(s, 512)
    return rows, (2 if s % (2 * rows) == 0 else 1), _pick(s, 1024)


def _diff_attn_kernel(lp_ref, q_ref, k_ref, v_ref, g_ref, o_ref, *, lambda_init, tk, nsub):
    bq = q_ref.shape[0] // nsub
    ones = jnp.ones((tk, LANES), BF16)
    k_chunk = lambda j: k_ref[j * tk:(j + 1) * tk, :]
    v1_chunk = lambda j: jnp.concatenate([v_ref[j * tk:(j + 1) * tk, :], ones], axis=1)
    qs = []
    for r in range(nsub):
        q = q_ref[r * bq:(r + 1) * bq, :]
        lane = lax.broadcasted_iota(jnp.int32, q.shape, 1)
        zero = jnp.zeros_like(q)
        qs.append(jnp.concatenate([jnp.where(lane < DIFF_QK_DIM, q, zero),
                                   jnp.where(lane >= DIFF_QK_DIM, q, zero)], axis=0))
    outs = _softmax_pv(qs, k_chunk, v1_chunk, k_ref.shape[0] // tk)
    lp = lp_ref[...]
    lam = (jnp.exp(jnp.sum(lp[0:1] * lp[1:2], axis=1, keepdims=True))
           - jnp.exp(jnp.sum(lp[2:3] * lp[3:4], axis=1, keepdims=True)) + lambda_init)
    for r in range(nsub):
        o = outs[r][:bq] - lam * outs[r][bq:]
        y = o * lax.rsqrt(jnp.mean(o * o, axis=-1, keepdims=True) + DIFF_SUBLN_EPS)
        o_ref[r * bq:(r + 1) * bq, :] = (y * g_ref[...] * (1.0 - lambda_init)).astype(o_ref.dtype)


def _diff_attention(qk, v, diff_lambda, diff_subln, l, lambda_init):
    s = qk.shape[0]
    prob_rows, nsub, tk = _attn_tiles(s)
    rows = prob_rows // 2 * nsub
    h = DIFF_HEADS
    return pl.pallas_call(
        functools.partial(_diff_attn_kernel, lambda_init=lambda_init, tk=tk, nsub=nsub),
        grid=(h, s // rows),
        in_specs=[pl.BlockSpec((None, 4, DIFF_QK_DIM), lambda hh, i: (l, 0, 0)),
                  pl.BlockSpec((rows, LANES), lambda hh, i: (i, hh)),
                  pl.BlockSpec((s, LANES), lambda hh, i: (0, h + hh)),
                  pl.BlockSpec((s, LANES), lambda hh, i: (0, hh)),
                  pl.BlockSpec((None, 1, DIFF_V_DIM), lambda hh, i: (l, 0, 0))],
        out_specs=pl.BlockSpec((rows, LANES), lambda hh, i: (i, hh)),
        out_shape=jax.ShapeDtypeStruct((s, h * DIFF_V_DIM), BF16),
        compiler_params=_cparams(("parallel", "parallel")),
        name="diff_attention",
    )(diff_lambda, qk, qk, v, diff_subln)


def _mla_attn_kernel(q_ref, kn_ref, kr_ref, v_ref, o_ref, *, tk, nsub):
    bq = q_ref.shape[0] // nsub
    ones = jnp.ones((tk, LANES), BF16)
    k_chunk = lambda j: jnp.concatenate([kn_ref[j * tk:(j + 1) * tk, :], kr_ref[j * tk:(j + 1) * tk, :]], axis=1)
    v1_chunk = lambda j: jnp.concatenate([v_ref[j * tk:(j + 1) * tk, :], ones], axis=1)
    qs = [q_ref[r * bq:(r + 1) * bq, :] for r in range(nsub)]
    outs = _softmax_pv(qs, k_chunk, v1_chunk, kn_ref.shape[0] // tk)
    for r in range(nsub):
        o_ref[r * bq:(r + 1) * bq, :] = outs[r].astype(o_ref.dtype)


def _mla_attention(q, kv_up, qk):
    s = q.shape[0]
    bq, nsub, tk = _attn_tiles(s)
    rows = bq * nsub
    h = MLA_HEADS
    kr_blk = 2 * DIFF_HEADS
    return pl.pallas_call(
        functools.partial(_mla_attn_kernel, tk=tk, nsub=nsub),
        grid=(h, s // rows),
        in_specs=[pl.BlockSpec((rows, 2 * LANES), lambda hh, i: (i, hh)),
                  pl.BlockSpec((s, LANES), lambda hh, i: (0, 2 * hh)),
                  pl.BlockSpec((s, LANES), lambda hh, i: (0, kr_blk)),
                  pl.BlockSpec((s, LANES), lambda hh, i: (0, 2 * hh + 1))],
        out_specs=pl.BlockSpec((rows, LANES), lambda hh, i: (i, hh)),
        out_shape=jax.ShapeDtypeStruct((s, h * MLA_V), BF16),
        compiler_params=_cparams(("parallel", "parallel")),
        name="mla_attention",
    )(q, kv_up, qk, kv_up)


def _gqa_kernel(sink_ref, q_ref, kp_ref, kc_ref, kn_ref, vp_ref, vc_ref, vn_ref, o_ref, *, layer, seq, bq):
    g = pl.program_id(0)
    i = pl.program_id(1)
    nsub = q_ref.shape[0] // bq
    nk = bq + 2 * WINDOW
    band_k = jnp.concatenate([kp_ref[...], kc_ref[...], kn_ref[...]], axis=0)
    band_v = jnp.concatenate([vp_ref[...], vc_ref[...], vn_ref[...]], axis=0)
    band_v1 = jnp.concatenate([band_v, jnp.ones_like(band_v)], axis=1)
    for r in range(nsub):
        q0 = (i * nsub + r) * bq
        kb = band_k[r * bq:r * bq + nk]
        v1 = band_v1[r * bq:r * bq + nk]
        qpos = q0 + lax.broadcasted_iota(jnp.int32, (bq, nk), 0)
        kpos = q0 - WINDOW + lax.broadcasted_iota(jnp.int32, (bq, nk), 1)
        valid = (jnp.abs(kpos - qpos) <= WINDOW) & (kpos >= 0) & (kpos < seq)
        bias = jnp.where(valid, 0.0, NEG_INF).astype(F32)
        for hh in range(GQA_GROUP):
            q = q_ref[r * bq:(r + 1) * bq, hh * GQA_DIM:(hh + 1) * GQA_DIM]
            s = lax.dot_general(q, kb, (((1,), (1,)), ((), ())), preferred_element_type=F32) + bias
            sink = sink_ref[layer * GQA_HEADS + g * GQA_GROUP + hh] * LOG2E
            m = jnp.maximum(jnp.max(s, axis=1, keepdims=True), sink)
            pv = jnp.dot(jnp.exp2(s - m).astype(BF16), v1, preferred_element_type=F32)
            denom = pv[:, GQA_DIM:] + jnp.exp2(sink - m)
            o_ref[r * bq:(r + 1) * bq, hh * GQA_DIM:(hh + 1) * GQA_DIM] = (pv[:, :GQA_DIM] / denom).astype(o_ref.dtype)


def _window_gqa(gqk, vv, sink_flat, l):
    s = gqk.shape[0]
    sub = _pick(s, 256)
    bq = _pick(s, 4 * sub)
    assert sub % WINDOW == 0 and WINDOW == LANES
    nb = s // bq
    r = bq // WINDOW
    nh = s // WINDOW
    qw = GQA_GROUP * GQA_DIM
    kcol = GQA_HEADS
    vcol = DIFF_HEADS

    def prev(i):
        return jnp.maximum(i * r - 1, 0)

    def nxt(i):
        return jnp.minimum((i + 1) * r, nh - 1)

    return pl.pallas_call(
        functools.partial(_gqa_kernel, layer=l, seq=s, bq=sub),
        grid=(GQA_KV_HEADS, nb),
        in_specs=[pl.BlockSpec(memory_space=pltpu.SMEM),
                  pl.BlockSpec((bq, qw), lambda g, i: (i, g)),
                  pl.BlockSpec((WINDOW, LANES), lambda g, i: (prev(i), kcol + g)),
                  pl.BlockSpec((bq, LANES), lambda g, i: (i, kcol + g)),
                  pl.BlockSpec((WINDOW, LANES), lambda g, i: (nxt(i), kcol + g)),
                  pl.BlockSpec((WINDOW, LANES), lambda g, i: (prev(i), vcol + g)),
                  pl.BlockSpec((bq, LANES), lambda g, i: (i, vcol + g)),
                  pl.BlockSpec((WINDOW, LANES), lambda g, i: (nxt(i), vcol + g))],
        out_specs=pl.BlockSpec((bq, qw), lambda g, i: (i, g)),
        out_shape=jax.ShapeDtypeStruct((s, GQA_HEADS * GQA_DIM), BF16),
        compiler_params=_cparams(("parallel", "parallel")),
        name="window_gqa",
    )(sink_flat, gqk, gqk, gqk, gqk, vv, vv, vv)


def _merge_kernel(u_ref, o0_ref, o1_ref, o2_ref, wg0_ref, wg1_ref, wg2_ref, b0_ref, b1_ref, b2_ref,
                  wb0_ref, wb1_ref, wb2_ref, out_ref, wg0_s, wg1_s, wg2_s, wb0_s, wb1_s, wb2_s):
    _cast_weights_once(((wg0_ref, wg0_s), (wg1_ref, wg1_s), (wg2_ref, wg2_s),
                        (wb0_ref, wb0_s), (wb1_ref, wb1_s), (wb2_ref, wb2_s)))
    u = u_ref[...]
    acc = None
    for o_ref, wg_s, b_ref, wb_s in ((o0_ref, wg0_s, b0_ref, wb0_s),
                                     (o1_ref, wg1_s, b1_ref, wb1_s),
                                     (o2_ref, wg2_s, b2_ref, wb2_s)):
        gate = jax.nn.sigmoid(jnp.dot(u, wg_s[...], preferred_element_type=F32) + b_ref[...])
        term = jnp.dot(o_ref[...], wb_s[...], preferred_element_type=F32) * gate
        acc = term if acc is None else acc + term
    out_ref[...] = acc.astype(out_ref.dtype)


def _merge(u, o_diff, o_mla, o_gqa, w_gate, b_gate, w_branch, l):
    m, d = u.shape
    bw = o_diff.shape[1]
    bm = _pick(m, 1024)
    bn = _pick(d, 256)
    nj = d // bn
    o_spec = pl.BlockSpec((bm, bw), lambda j, i: (i, 0))

    def wg_spec(n):
        return pl.BlockSpec((None, d, bn), lambda j, i: (l, 0, n * nj + j))

    def b_spec(n):
        return pl.BlockSpec((None, 1, bn), lambda j, i: (l, 0, n * nj + j))

    def wb_spec(n):
        return pl.BlockSpec((None, None, bw, bn), lambda j, i: (l, n, 0, j))

    return pl.pallas_call(
        _merge_kernel,
        grid=(nj, m // bm),
        in_specs=[pl.BlockSpec((bm, d), lambda j, i: (i, 0)), o_spec, o_spec, o_spec,
                  wg_spec(0), wg_spec(1), wg_spec(2), b_spec(0), b_spec(1), b_spec(2),
                  wb_spec(0), wb_spec(1), wb_spec(2)],
        out_specs=pl.BlockSpec((bm, bn), lambda j, i: (i, j)),
        out_shape=jax.ShapeDtypeStruct((m, d), BF16),
        scratch_shapes=[pltpu.VMEM((d, bn), BF16)] * 3 + [pltpu.VMEM((bw, bn), BF16)] * 3,
        compiler_params=_cparams(("arbitrary", "arbitrary")),
        name="branch_merge",
    )(u, o_diff, o_mla, o_gqa, w_gate, w_gate, w_gate, b_gate, b_gate, b_gate,
      w_branch, w_branch, w_branch)


def _xattn_kernel(mg_ref, h_ref, wm_ref, g_ref, wq_ref, k_ref, v_ref, wo_ref, o_ref, ob_ref, wm_s, wq_s, wo_s):
    @pl.when(pl.program_id(0) == 0)
    def _():
        wm_s[...] = wm_ref[...].astype(BF16)
        wq_s[...] = wq_ref[...].astype(BF16)
        wo_s[...] = wo_ref[...].astype(BF16)

    h1 = h_ref[...] + jnp.dot(mg_ref[...], wm_s[...], preferred_element_type=F32)
    hn = _norm_rows(h1, g_ref[...])
    q = jnp.dot(hn, wq_s[...], preferred_element_type=F32) * (XA_DIM ** -0.5 * LOG2E)
    q = q.astype(BF16)
    heads = []
    for hh in range(XA_HEADS):
        sl = slice(hh * XA_DIM, (hh + 1) * XA_DIM)
        s = lax.dot_general(q[:, sl], k_ref[:, sl], (((1,), (1,)), ((), ())), preferred_element_type=F32)
        p = jnp.exp2(s - jnp.max(s, axis=1, keepdims=True))
        denom = jnp.sum(p, axis=1, keepdims=True)
        heads.append((jnp.dot(p.astype(BF16), v_ref[:, sl], preferred_element_type=F32) / denom).astype(BF16))
    o = jnp.concatenate(heads, axis=1)
    out = h1 + jnp.dot(o, wo_s[...], preferred_element_type=F32)
    o_ref[...] = out
    ob_ref[...] = out.astype(BF16)


def _cross_attention(merged, h, w_o, gain, xa_wq, memkv, xa_wo, l):
    m, d = h.shape
    w = XA_HEADS * XA_DIM
    ml = memkv.shape[0]
    bm = _pick(m, 256)
    once = dict(pipeline_mode=pl.Buffered(1))
    return pl.pallas_call(
        _xattn_kernel,
        grid=(m // bm,),
        in_specs=[pl.BlockSpec((bm, d), lambda i: (i, 0)),
                  pl.BlockSpec((bm, d), lambda i: (i, 0)),
                  pl.BlockSpec((None, d, d), lambda i: (l, 0, 0), **once),
                  pl.BlockSpec((None, 1, d), lambda i: (l, 0, 0)),
                  pl.BlockSpec((None, d, w), lambda i: (l, 0, 0), **once),
                  pl.BlockSpec((ml, w), lambda i: (0, 0)),
                  pl.BlockSpec((ml, w), lambda i: (0, 1)),
                  pl.BlockSpec((None, w, d), lambda i: (l, 0, 0), **once)],
        out_specs=[pl.BlockSpec((bm, d), lambda i: (i, 0))] * 2,
        out_shape=[jax.ShapeDtypeStruct((m, d), F32), jax.ShapeDtypeStruct((m, d), BF16)],
        scratch_shapes=[pltpu.VMEM((d, d), BF16), pltpu.VMEM((d, w), BF16), pltpu.VMEM((w, d), BF16)],
        compiler_params=_cparams(("arbitrary",)),
        name="cross_attention",
    )(merged, h, w_o, gain, xa_wq, memkv, memkv, xa_wo)


def _pad_cols(w, width):
    return jnp.pad(w, ((0, 0),) * (w.ndim - 1) + ((0, width - w.shape[-1]),))


_DQ, _DV, _CQ, _KR, _GQ, _GV, _END = 0, 2048, 3072, 4096, 4160, 5440, 5696


def _regroup_kernel(wt_ref, a_ref, v_ref, c_ref, g_ref):
    def cols(lo, hi):
        return wt_ref[lo:hi, :].T

    kr = jnp.concatenate([wt_ref[_KR:_GQ, :], jnp.zeros((LANES - (_GQ - _KR), wt_ref.shape[1]), wt_ref.dtype)], axis=0)
    a_ref[...] = jnp.concatenate([cols(_DQ, _DV), kr.T], axis=1).astype(BF16)
    v_ref[...] = jnp.concatenate([cols(_DV, _CQ), cols(_GV, _END)], axis=1).astype(BF16)
    c_ref[...] = cols(_CQ, _KR).astype(BF16)
    g_ref[...] = cols(_GQ, _GV).astype(BF16)


def _regroup_w_in(w_in):
    depth, k, n = w_in.shape
    assert n == _END
    bk = _pick(k, 256)
    widths = (_DV - _DQ + LANES, (_CQ - _DV) + (_END - _GV), _KR - _CQ, _GV - _GQ)
    return pl.pallas_call(
        _regroup_kernel,
        grid=(depth, k // bk),
        in_specs=[pl.BlockSpec((None, n, bk), lambda l, i: (l, 0, i))],
        out_specs=[pl.BlockSpec((None, bk, w), lambda l, i: (l, i, 0)) for w in widths],
        out_shape=[jax.ShapeDtypeStruct((depth, k, w), BF16) for w in widths],
        compiler_params=_cparams(("parallel", "parallel")),
        name="regroup_w_in",
    )(jnp.swapaxes(w_in, 1, 2))


def kernel(x, mem, positions, ffn1_norm, ffn1_w_gu, ffn1_w_down, mix_norm, w_in, diff_lambda, diff_subln,
           mla_q_norm, mla_kv_norm, mla_w_uq, mla_w_ukv, gqa_sink, w_branch, w_gate, b_gate, w_o, xa_norm,
           mem_norm, xa_wq, xa_wkv, xa_wo, ffn2_norm, ffn2_w_gu, ffn2_w_down, final_norm):
    b, s, d = x.shape
    assert b == 1
    depth = ffn1_norm.shape[0]
    h = x.reshape(s, d)
    mem2 = mem.reshape(mem.shape[1], d)

    pos_b = jnp.broadcast_to(positions.reshape(s, 1).astype(F32), (s, LANES))
    cos64, sin64 = _rope_tables(pos_b, DIFF_QK_DIM)
    cos128, sin128 = _rope_tables(pos_b, GQA_DIM)

    bf = lambda w: w.astype(BF16)
    xa_wkv_b = bf(xa_wkv)
    b_gate3 = b_gate.reshape(depth, 1, -1)
    norm3 = lambda g: g.reshape(depth, 1, -1)
    mix_norm3, xa_norm3, mem_norm3 = map(norm3, (mix_norm, xa_norm, mem_norm))
    ffn1_norm_col, ffn2_norm_col = ffn1_norm.reshape(depth, d, 1), ffn2_norm.reshape(depth, d, 1)
    hb = bf(h)
    diff_subln3 = diff_subln.reshape(depth, 1, -1)
    sink_flat = gqa_sink.reshape(-1)

    diff_scale = DIFF_QK_DIM ** -0.5 * LOG2E
    mla_scale = (MLA_NOPE + MLA_ROPE) ** -0.5 * LOG2E
    gqa_scale = GQA_DIM ** -0.5 * LOG2E
    cs_a = jnp.concatenate([jnp.full((1, 1024), diff_scale, F32), jnp.ones((1, 1024 + LANES), F32)], axis=1)
    cs_g = jnp.concatenate([jnp.full((1, 1024), gqa_scale, F32), jnp.ones((1, 256), F32)], axis=1)
    cs_q = jnp.full((1, MLA_HEADS * 2 * LANES), mla_scale, F32)

    w_a, w_v, w_c, w_g = _regroup_w_in(w_in)
    wuq = mla_w_uq.reshape(depth, MLA_Q_RANK, MLA_HEADS, MLA_NOPE + MLA_ROPE)
    wuq = _pad_cols(wuq, 2 * LANES).reshape(depth, MLA_Q_RANK, MLA_HEADS * 2 * LANES)
    wuq, wukv = bf(wuq), bf(mla_w_ukv)
    mla_gains = jnp.stack([mla_q_norm, mla_kv_norm], axis=1)[:, :, None, :]

    for l in range(depth):
        lambda_init = 0.8 - 0.6 * math.exp(-0.3 * l)

        a = _ffn_up(hb, ffn1_norm_col, ffn1_w_gu, l)
        h = _mm_residual(a, ffn1_w_down, l, h, 0.5)

        qk, u = _mm_rope(h, w_a, l, cs_a, cos64, sin64, DIFF_QK_DIM // 2, (True,) * 17, gain=mix_norm3)
        vv = _mm_plain(u, w_v, l)
        cn = _mm_rmsnorm(u, w_c, mla_gains, l)
        gqk = _mm_rope(u, w_g, l, cs_g, cos128, sin128, GQA_DIM // 2, (True,) * 10)

        o_diff = _diff_attention(qk, vv, diff_lambda, diff_subln3, l, lambda_init)

        q_mla = _mm_rope(cn, wuq, l, cs_q, cos64, sin64, MLA_ROPE // 2, (False, True) * MLA_HEADS, lead=0)
        kv_up = _mm_plain(cn, wukv, l, lead=1)
        o_mla = _mla_attention(q_mla, kv_up, qk)

        o_gqa = _window_gqa(gqk, vv, sink_flat, l)

        merged = _merge(u, o_diff, o_mla, o_gqa, w_gate, b_gate3, w_branch, l)
        memkv = _mm_plain(_rmsnorm(mem2, mem_norm3, l), xa_wkv_b, l)
        h, hb = _cross_attention(merged, h, w_o, xa_norm3, xa_wq, memkv, xa_wo, l)

        a = _ffn_up(hb, ffn2_norm_col, ffn2_w_gu, l)
        if l + 1 < depth:
            h, hb = _mm_residual(a, ffn2_w_down, l, h, 0.5, emit_bf16=True)
        else:
            h = _mm_residual(a, ffn2_w_down, l, h, 0.5)

    out = _rmsnorm(h, final_norm.reshape(1, d), None, out_dtype=F32)
    return out.reshape(b, s, d)
```

```python
import functools
import math

import jax
import jax.numpy as jnp
from jax import lax
from jax.experimental import pallas as pl
from jax.experimental.pallas import tpu as pltpu

F32 = jnp.float32
BF16 = jnp.bfloat16

DIFF_HEADS = 8
DIFF_QK_DIM = 64
DIFF_V_DIM = 128
MLA_HEADS = 8
MLA_Q_RANK = 512
MLA_NOPE = 128
MLA_ROPE = 64
MLA_V = 128
GQA_HEADS = 8
GQA_KV_HEADS = 2
GQA_GROUP = GQA_HEADS // GQA_KV_HEADS
GQA_DIM = 128
WINDOW = 128
XA_HEADS = 4
XA_DIM = 128
D_FF = 5632
ROPE_THETA = 10000.0
EPS = 1e-6
DIFF_SUBLN_EPS = 1e-5
NEG_INF = -1e30
LOG2E = math.log2(math.e)

LANES = 128
V7X_VMEM_LIMIT_BYTES = 56 * 1024 * 1024


def _cparams(semantics):
    return pltpu.CompilerParams(dimension_semantics=semantics, vmem_limit_bytes=V7X_VMEM_LIMIT_BYTES)


def _pick(dim, pref):
    if dim <= pref:
        return dim
    b = pref
    while dim % b:
        b //= 2
    return b


def _rope_table_kernel(pos_ref, inv_ref, sign_ref, cos_ref, sin_ref):
    ang = pos_ref[...] * inv_ref[...]
    cos_ref[...] = jnp.cos(ang)
    sin_ref[...] = jnp.sin(ang) * sign_ref[...]


def _rope_tables(pos_b, dim):
    s = pos_b.shape[0]
    half = dim // 2
    inv = ROPE_THETA ** (-jnp.arange(0, dim, 2, dtype=F32) / dim)
    inv_row = jnp.tile(inv, LANES // half)[None, :]
    sign_row = jnp.tile(jnp.concatenate([-jnp.ones((half,), F32), jnp.ones((half,), F32)]), LANES // dim)[None, :]
    bm = _pick(s, 1024)
    return pl.pallas_call(
        _rope_table_kernel,
        grid=(s // bm,),
        in_specs=[pl.BlockSpec((bm, LANES), lambda i: (i, 0)),
                  pl.BlockSpec((1, LANES), lambda i: (0, 0)),
                  pl.BlockSpec((1, LANES), lambda i: (0, 0))],
        out_specs=[pl.BlockSpec((bm, LANES), lambda i: (i, 0))] * 2,
        out_shape=[jax.ShapeDtypeStruct((s, LANES), F32)] * 2,
        compiler_params=_cparams(("parallel",)),
        name="rope_tables",
    )(pos_b, inv_row, sign_row)


def _rope_block(x, cos, sin_signed, half):
    if 2 * half == LANES:
        partner = pltpu.roll(x, half, axis=1)
    else:
        lane = lax.broadcasted_iota(jnp.int32, x.shape, 1)
        first = (lane % (2 * half)) < half
        partner = jnp.where(first, pltpu.roll(x, LANES - half, axis=1), pltpu.roll(x, half, axis=1))
    return x * cos + partner * sin_signed


def _rmsnorm_kernel(x_ref, g_ref, o_ref, *, eps):
    x = x_ref[...]
    y = x * lax.rsqrt(jnp.mean(x * x, axis=-1, keepdims=True) + eps)
    o_ref[...] = (y * g_ref[...]).astype(o_ref.dtype)


def _rmsnorm(x, g_stack, l, out_dtype=BF16):
    m, d = x.shape
    bm = _pick(m, 512)
    if l is None:
        g_spec = pl.BlockSpec((1, d), lambda i: (0, 0))
    else:
        g_spec = pl.BlockSpec((None, 1, d), lambda i: (l, 0, 0))
    return pl.pallas_call(
        functools.partial(_rmsnorm_kernel, eps=EPS),
        grid=(m // bm,),
        in_specs=[pl.BlockSpec((bm, d), lambda i: (i, 0)), g_spec],
        out_specs=pl.BlockSpec((bm, d), lambda i: (i, 0)),
        out_shape=jax.ShapeDtypeStruct((m, d), out_dtype),
        compiler_params=_cparams(("parallel",)),
        name="rmsnorm",
    )(x, g_stack)


def _cast_weights_once(pairs):
    @pl.when(pl.program_id(1) == 0)
    def _():
        for w_ref, w_s in pairs:
            w_s[...] = w_ref[...].astype(BF16)


def _norm_rows(x, g, eps=EPS):
    return (x * lax.rsqrt(jnp.mean(x * x, axis=-1, keepdims=True) + eps) * g).astype(BF16)


def _mm_swiglu_kernel(x_ref, gc_ref, wg_ref, wu_ref, o_ref, wg_s, wu_s, *, sub):
    @pl.when(pl.program_id(1) == 0)
    def _():
        gc = gc_ref[...]
        wg_s[...] = (wg_ref[...] * gc).astype(BF16)
        wu_s[...] = (wu_ref[...] * gc).astype(BF16)

    def rows_tile(t, carry):
        rows = pl.ds(pl.multiple_of(t * sub, sub), sub)
        x = x_ref[rows, :]
        xf = x.astype(F32)
        x = x.astype(BF16)
        rs = lax.rsqrt(jnp.mean(xf * xf, axis=-1, keepdims=True) + EPS)
        g = jnp.dot(x, wg_s[...], preferred_element_type=F32) * rs
        u = jnp.dot(x, wu_s[...], preferred_element_type=F32) * rs
        o_ref[rows, :] = (g * jax.nn.sigmoid(g) * u).astype(o_ref.dtype)
        return carry

    lax.fori_loop(0, x_ref.shape[0] // sub, rows_tile, 0)


def _ffn_up(hb, gain_col, w_gu, l):
    m, d = hb.shape
    n = w_gu.shape[-1] // 2
    bm = _pick(m, 2048 if hb.dtype == BF16 else 1024)
    bn = _pick(n, 512)
    nj = n // bn
    return pl.pallas_call(
        functools.partial(_mm_swiglu_kernel, sub=_pick(bm, 1024)),
        grid=(nj, m // bm),
        in_specs=[pl.BlockSpec((bm, d), lambda j, i: (i, 0)),
                  pl.BlockSpec((None, d, 1), lambda j, i: (l, 0, 0)),
                  pl.BlockSpec((None, d, bn), lambda j, i: (l, 0, j)),
                  pl.BlockSpec((None, d, bn), lambda j, i: (l, 0, j + nj))],
        out_specs=pl.BlockSpec((bm, bn), lambda j, i: (i, j)),
        out_shape=jax.ShapeDtypeStruct((m, n), BF16),
        scratch_shapes=[pltpu.VMEM((d, bn), BF16), pltpu.VMEM((d, bn), BF16)],
        compiler_params=_cparams(("arbitrary", "arbitrary")),
        name="ffn_up",
    )(hb, gain_col, w_gu, w_gu)


def _mm_residual_kernel(a_ref, w_ref, r_ref, o_ref, *rest, scale):
    w_s = rest[-1]
    _cast_weights_once(((w_ref, w_s),))
    out = r_ref[...] + scale * jnp.dot(a_ref[...], w_s[...], preferred_element_type=F32)
    o_ref[...] = out
    if len(rest) == 2:
        rest[0][...] = out.astype(BF16)


def _mm_residual(a, w, l, res, scale, emit_bf16=False):
    m, k = a.shape
    n = w.shape[-1]
    bm, bn = _pick(m, 256), _pick(n, 1024)
    o_spec = pl.BlockSpec((bm, bn), lambda j, i: (i, j))
    out_shape = jax.ShapeDtypeStruct((m, n), F32)
    return pl.pallas_call(
        functools.partial(_mm_residual_kernel, scale=scale),
        grid=(n // bn, m // bm),
        in_specs=[pl.BlockSpec((bm, k), lambda j, i: (i, 0)),
                  pl.BlockSpec((None, k, bn), lambda j, i: (l, 0, j), pipeline_mode=pl.Buffered(1)),
                  o_spec],
        out_specs=[o_spec, o_spec] if emit_bf16 else o_spec,
        out_shape=[out_shape, jax.ShapeDtypeStruct((m, n), BF16)] if emit_bf16 else out_shape,
        scratch_shapes=[pltpu.VMEM((k, bn), BF16)],
        compiler_params=_cparams(("arbitrary", "arbitrary")),
        name="mm_residual",
    )(a, w, res)


def _mm_plain_kernel(x_ref, w_ref, o_ref):
    o_ref[...] = jnp.dot(x_ref[...], w_ref[...], preferred_element_type=F32).astype(o_ref.dtype)


def _rows_spec(x, bm, lead):
    k = x.shape[-1]
    if lead is None:
        return pl.BlockSpec((bm, k), lambda i: (i, 0))
    return pl.BlockSpec((None, bm, k), lambda i: (lead, i, 0))


def _mm_plain(x, w, l, lead=None):
    m, k = x.shape[-2:]
    n = w.shape[-1]
    bm = _pick(m, 1024)
    w_spec = pl.BlockSpec((None, k, n), lambda i: (l, 0, 0))
    return pl.pallas_call(
        _mm_plain_kernel,
        grid=(m // bm,),
        in_specs=[_rows_spec(x, bm, lead), w_spec],
        out_specs=pl.BlockSpec((bm, n), lambda i: (i, 0)),
        out_shape=jax.ShapeDtypeStruct((m, n), BF16),
        compiler_params=_cparams(("parallel",)),
        name="mm_plain",
    )(x, w)


def _mm_rope_kernel(x_ref, w_ref, cs_ref, cos_ref, sin_ref, *rest, half, pattern):
    if len(rest) == 3:
        g_ref, o_ref, u_ref = rest
        x = _norm_rows(x_ref[...], g_ref[...])
        u_ref[...] = x
    else:
        (o_ref,) = rest
        x = x_ref[...]
    acc = jnp.dot(x, w_ref[...], preferred_element_type=F32) * cs_ref[...]
    cos = cos_ref[...]
    sin = sin_ref[...]
    for b, roped in enumerate(pattern):
        blk = acc[:, b * LANES:(b + 1) * LANES]
        if roped:
            blk = _rope_block(blk, cos, sin, half)
        o_ref[:, b * LANES:(b + 1) * LANES] = blk.astype(o_ref.dtype)


def _mm_rope(x, w, l, colscale, cos, sin, half, pattern, gain=None, lead=None):
    m, k = x.shape[-2:]
    n = w.shape[-1]
    assert n == LANES * len(pattern)
    bm = _pick(m, 512 if gain is not None else 1024)
    x_spec = pl.BlockSpec((bm, k), lambda i: (i, 0))
    o_spec = pl.BlockSpec((bm, n), lambda i: (i, 0))
    out_shape = jax.ShapeDtypeStruct((m, n), BF16)
    in_specs = [_rows_spec(x, bm, lead),
                pl.BlockSpec((None, k, n), lambda i: (l, 0, 0)),
                pl.BlockSpec((1, n), lambda i: (0, 0)),
                pl.BlockSpec((bm, LANES), lambda i: (i, 0)),
                pl.BlockSpec((bm, LANES), lambda i: (i, 0))]
    args = (x, w, colscale, cos, sin)
    if gain is not None:
        in_specs.append(pl.BlockSpec((None, 1, k), lambda i: (l, 0, 0)))
        args += (gain,)
    return pl.pallas_call(
        functools.partial(_mm_rope_kernel, half=half, pattern=pattern),
        grid=(m // bm,),
        in_specs=in_specs,
        out_specs=o_spec if gain is None else [o_spec, x_spec],
        out_shape=out_shape if gain is None else [out_shape, jax.ShapeDtypeStruct((m, k), BF16)],
        compiler_params=_cparams(("parallel",)),
        name="mm_rope",
    )(*args)


def _mm_rmsnorm_kernel(x_ref, w_ref, g_ref, o_ref):
    acc = jnp.dot(x_ref[...], w_ref[...], preferred_element_type=F32)
    y = acc * lax.rsqrt(jnp.mean(acc * acc, axis=-1, keepdims=True) + EPS)
    o_ref[...] = (y * g_ref[...]).astype(o_ref.dtype)


def _mm_rmsnorm(x, w, g, l):
    m, k = x.shape
    _, t, _, r = g.shape
    bm = _pick(m, 1024)
    return pl.pallas_call(
        _mm_rmsnorm_kernel,
        grid=(t, m // bm),
        in_specs=[pl.BlockSpec((bm, k), lambda j, i: (i, 0)),
                  pl.BlockSpec((None, k, r), lambda j, i: (l, 0, j)),
                  pl.BlockSpec((None, None, 1, r), lambda j, i: (l, j, 0, 0))],
        out_specs=pl.BlockSpec((None, bm, r), lambda j, i: (j, i, 0)),
        out_shape=jax.ShapeDtypeStruct((t, m, r), BF16),
        compiler_params=_cparams(("parallel", "parallel")),
        name="mm_rmsnorm",
    )(x, w, g)


def _logits(q, k, mlane):
    s = lax.dot_general(q, k, (((1,), (1,)), ((), ())), preferred_element_type=F32)
    for b in range(s.shape[1] // LANES):
        blk = s[:, b * LANES:(b + 1) * LANES]
        mlane = blk if mlane is None else jnp.maximum(mlane, blk)
    return s, mlane


def _row_max(mlane):
    return jnp.broadcast_to(jnp.max(mlane, axis=1, keepdims=True), mlane.shape)


def _weigh(s, m, v1, acc):
    ps = [jnp.exp2(s[:, b * LANES:(b + 1) * LANES] - m).astype(BF16) for b in range(s.shape[1] // LANES)]
    pv = jnp.dot(jnp.concatenate(ps, axis=1), v1, preferred_element_type=F32)
    return pv if acc is None else acc + pv


def _softmax_pv(qs, k_chunk, v1_chunk, nchunks):
    n = len(qs)
    logits = [[None] * nchunks for _ in range(n)]
    acc = [None] * n
    mlane = None
    for j in range(nchunks):
        logits[0][j], mlane = _logits(qs[0], k_chunk(j), mlane)
    m = _row_max(mlane)
    for p in range(1, n):
        mlane = None
        for j in range(nchunks):
            acc[p - 1] = _weigh(logits[p - 1][j], m, v1_chunk(j), acc[p - 1])
            logits[p][j], mlane = _logits(qs[p], k_chunk(j), mlane)
        m = _row_max(mlane)
    for j in range(nchunks):
        acc[n - 1] = _weigh(logits[n - 1][j], m, v1_chunk(j), acc[n - 1])
    return [a[:, :a.shape[1] - LANES] / a[:, a.shape[1] - LANES:] for a in acc]


def _attn_tiles(s):
    rows = _pick(s, 512)
    return rows, (2 if s % (2 * rows) == 0 else 1), _pick(s, 1024)


def _diff_attn_kernel(lp_ref, q_ref, k_ref, v_ref, g_ref, o_ref, *, lambda_init, tk, nsub):
    bq = q_ref.shape[0] // nsub
    ones = jnp.ones((tk, LANES), BF16)
    k_chunk = lambda j: k_ref[j * tk:(j + 1) * tk, :]
    v1_chunk = lambda j: jnp.concatenate([v_ref[j * tk:(j + 1) * tk, :], ones], axis=1)
    qs = []
    for r in range(nsub):
        q = q_ref[r * bq:(r + 1) * bq, :]
        lane = lax.broadcasted_iota(jnp.int32, q.shape, 1)
        zero = jnp.zeros_like(q)
        qs.append(jnp.concatenate([jnp.where(lane < DIFF_QK_DIM, q, zero),
                                   jnp.where(lane >= DIFF_QK_DIM, q, zero)], axis=0))
    outs = _softmax_pv(qs, k_chunk, v1_chunk, k_ref.shape[0] // tk)
    lp = lp_ref[...]
    lam = (jnp.exp(jnp.sum(lp[0:1] * lp[1:2], axis=1, keepdims=True))
           - jnp.exp(jnp.sum(lp[2:3] * lp[3:4], axis=1, keepdims=True)) + lambda_init)
    for r in range(nsub):
        o = outs[r][:bq] - lam * outs[r][bq:]
        y = o * lax.rsqrt(jnp.mean(o * o, axis=-1, keepdims=True) + DIFF_SUBLN_EPS)
        o_ref[r * bq:(r + 1) * bq, :] = (y * g_ref[...] * (1.0 - lambda_init)).astype(o_ref.dtype)


def _diff_attention(qk, v, diff_lambda, diff_subln, l, lambda_init):
    s = qk.shape[0]
    prob_rows, nsub, tk = _attn_tiles(s)
    rows = prob_rows // 2 * nsub
    h = DIFF_HEADS
    return pl.pallas_call(
        functools.partial(_diff_attn_kernel, lambda_init=lambda_init, tk=tk, nsub=nsub),
        grid=(h, s // rows),
        in_specs=[pl.BlockSpec((None, 4, DIFF_QK_DIM), lambda hh, i: (l, 0, 0)),
                  pl.BlockSpec((rows, LANES), lambda hh, i: (i, hh)),
                  pl.BlockSpec((s, LANES), lambda hh, i: (0, h + hh)),
                  pl.BlockSpec((s, LANES), lambda hh, i: (0, hh)),
                  pl.BlockSpec((None, 1, DIFF_V_DIM), lambda hh, i: (l, 0, 0))],
        out_specs=pl.BlockSpec((rows, LANES), lambda hh, i: (i, hh)),
        out_shape=jax.ShapeDtypeStruct((s, h * DIFF_V_DIM), BF16),
        compiler_params=_cparams(("parallel", "parallel")),
        name="diff_attention",
    )(diff_lambda, qk, qk, v, diff_subln)


def _mla_attn_kernel(q_ref, kn_ref, kr_ref, v_ref, o_ref, *, tk, nsub):
    bq = q_ref.shape[0] // nsub
    ones = jnp.ones((tk, LANES), BF16)
    k_chunk = lambda j: jnp.concatenate([kn_ref[j * tk:(j + 1) * tk, :], kr_ref[j * tk:(j + 1) * tk, :]], axis=1)
    v1_chunk = lambda j: jnp.concatenate([v_ref[j * tk:(j + 1) * tk, :], ones], axis=1)
    qs = [q_ref[r * bq:(r + 1) * bq, :] for r in range(nsub)]
    outs = _softmax_pv(qs, k_chunk, v1_chunk, kn_ref.shape[0] // tk)
    for r in range(nsub):
        o_ref[r * bq:(r + 1) * bq, :] = outs[r].astype(o_ref.dtype)


def _mla_attention(q, kv_up, qk):
    s = q.shape[0]
    bq, nsub, tk = _attn_tiles(s)
    rows = bq * nsub
    h = MLA_HEADS
    kr_blk = 2 * DIFF_HEADS
    return pl.pallas_call(
        functools.partial(_mla_attn_kernel, tk=tk, nsub=nsub),
        grid=(h, s // rows),
        in_specs=[pl.BlockSpec((rows, 2 * LANES), lambda hh, i: (i, hh)),
                  pl.BlockSpec((s, LANES), lambda hh, i: (0, 2 * hh)),
                  pl.BlockSpec((s, LANES), lambda hh, i: (0, kr_blk)),
                  pl.BlockSpec((s, LANES), lambda hh, i: (0, 2 * hh + 1))],
        out_specs=pl.BlockSpec((rows, LANES), lambda hh, i: (i, hh)),
        out_shape=jax.ShapeDtypeStruct((s, h * MLA_V), BF16),
        compiler_params=_cparams(("parallel", "parallel")),
        name="mla_attention",
    )(q, kv_up, qk, kv_up)


def _gqa_kernel(sink_ref, q_ref, kp_ref, kc_ref, kn_ref, vp_ref, vc_ref, vn_ref, o_ref, *, layer, seq, bq):
    g = pl.program_id(0)
    i = pl.program_id(1)
    nsub = q_ref.shape[0] // bq
    nk = bq + 2 * WINDOW
    band_k = jnp.concatenate([kp_ref[...], kc_ref[...], kn_ref[...]], axis=0)
    band_v = jnp.concatenate([vp_ref[...], vc_ref[...], vn_ref[...]], axis=0)
    band_v1 = jnp.concatenate([band_v, jnp.ones_like(band_v)], axis=1)
    for r in range(nsub):
        q0 = (i * nsub + r) * bq
        kb = band_k[r * bq:r * bq + nk]
        v1 = band_v1[r * bq:r * bq + nk]
        qpos = q0 + lax.broadcasted_iota(jnp.int32, (bq, nk), 0)
        kpos = q0 - WINDOW + lax.broadcasted_iota(jnp.int32, (bq, nk), 1)
        valid = (jnp.abs(kpos - qpos) <= WINDOW) & (kpos >= 0) & (kpos < seq)
        bias = jnp.where(valid, 0.0, NEG_INF).astype(F32)
        for hh in range(GQA_GROUP):
            q = q_ref[r * bq:(r + 1) * bq, hh * GQA_DIM:(hh + 1) * GQA_DIM]
            s = lax.dot_general(q, kb, (((1,), (1,)), ((), ())), preferred_element_type=F32) + bias
            sink = sink_ref[layer * GQA_HEADS + g * GQA_GROUP + hh] * LOG2E
            m = jnp.maximum(jnp.max(s, axis=1, keepdims=True), sink)
            pv = jnp.dot(jnp.exp2(s - m).astype(BF16), v1, preferred_element_type=F32)
            denom = pv[:, GQA_DIM:] + jnp.exp2(sink - m)
            o_ref[r * bq:(r + 1) * bq, hh * GQA_DIM:(hh + 1) * GQA_DIM] = (pv[:, :GQA_DIM] / denom).astype(o_ref.dtype)


def _window_gqa(gqk, vv, sink_flat, l):
    s = gqk.shape[0]
    sub = _pick(s, 256)
    bq = _pick(s, 4 * sub)
    assert sub % WINDOW == 0 and WINDOW == LANES
    nb = s // bq
    r = bq // WINDOW
    nh = s // WINDOW
    qw = GQA_GROUP * GQA_DIM
    kcol = GQA_HEADS
    vcol = DIFF_HEADS

    def prev(i):
        return jnp.maximum(i * r - 1, 0)

    def nxt(i):
        return jnp.minimum((i + 1) * r, nh - 1)

    return pl.pallas_call(
        functools.partial(_gqa_kernel, layer=l, seq=s, bq=sub),
        grid=(GQA_KV_HEADS, nb),
        in_specs=[pl.BlockSpec(memory_space=pltpu.SMEM),
                  pl.BlockSpec((bq, qw), lambda g, i: (i, g)),
                  pl.BlockSpec((WINDOW, LANES), lambda g, i: (prev(i), kcol + g)),
                  pl.BlockSpec((bq, LANES), lambda g, i: (i, kcol + g)),
                  pl.BlockSpec((WINDOW, LANES), lambda g, i: (nxt(i), kcol + g)),
                  pl.BlockSpec((WINDOW, LANES), lambda g, i: (prev(i), vcol + g)),
                  pl.BlockSpec((bq, LANES), lambda g, i: (i, vcol + g)),
                  pl.BlockSpec((WINDOW, LANES), lambda g, i: (nxt(i), vcol + g))],
        out_specs=pl.BlockSpec((bq, qw), lambda g, i: (i, g)),
        out_shape=jax.ShapeDtypeStruct((s, GQA_HEADS * GQA_DIM), BF16),
        compiler_params=_cparams(("parallel", "parallel")),
        name="window_gqa",
    )(sink_flat, gqk, gqk, gqk, gqk, vv, vv, vv)


def _merge_kernel(u_ref, o0_ref, o1_ref, o2_ref, wg0_ref, wg1_ref, wg2_ref, b0_ref, b1_ref, b2_ref,
                  wb0_ref, wb1_ref, wb2_ref, out_ref, wg0_s, wg1_s, wg2_s, wb0_s, wb1_s, wb2_s):
    _cast_weights_once(((wg0_ref, wg0_s), (wg1_ref, wg1_s), (wg2_ref, wg2_s),
                        (wb0_ref, wb0_s), (wb1_ref, wb1_s), (wb2_ref, wb2_s)))
    u = u_ref[...]
    acc = None
    for o_ref, wg_s, b_ref, wb_s in ((o0_ref, wg0_s, b0_ref, wb0_s),
                                     (o1_ref, wg1_s, b1_ref, wb1_s),
                                     (o2_ref, wg2_s, b2_ref, wb2_s)):
        gate = jax.nn.sigmoid(jnp.dot(u, wg_s[...], preferred_element_type=F32) + b_ref[...])
        term = jnp.dot(o_ref[...], wb_s[...], preferred_element_type=F32) * gate
        acc = term if acc is None else acc + term
    out_ref[...] = acc.astype(out_ref.dtype)


def _merge(u, o_diff, o_mla, o_gqa, w_gate, b_gate, w_branch, l):
    m, d = u.shape
    bw = o_diff.shape[1]
    bm = _pick(m, 1024)
    bn = _pick(d, 256)
    nj = d // bn
    o_spec = pl.BlockSpec((bm, bw), lambda j, i: (i, 0))

    def wg_spec(n):
        return pl.BlockSpec((None, d, bn), lambda j, i: (l, 0, n * nj + j))

    def b_spec(n):
        return pl.BlockSpec((None, 1, bn), lambda j, i: (l, 0, n * nj + j))

    def wb_spec(n):
        return pl.BlockSpec((None, None, bw, bn), lambda j, i: (l, n, 0, j))

    return pl.pallas_call(
        _merge_kernel,
        grid=(nj, m // bm),
        in_specs=[pl.BlockSpec((bm, d), lambda j, i: (i, 0)), o_spec, o_spec, o_spec,
                  wg_spec(0), wg_spec(1), wg_spec(2), b_spec(0), b_spec(1), b_spec(2),
                  wb_spec(0), wb_spec(1), wb_spec(2)],
        out_specs=pl.BlockSpec((bm, bn), lambda j, i: (i, j)),
        out_shape=jax.ShapeDtypeStruct((m, d), BF16),
        scratch_shapes=[pltpu.VMEM((d, bn), BF16)] * 3 + [pltpu.VMEM((bw, bn), BF16)] * 3,
        compiler_params=_cparams(("arbitrary", "arbitrary")),
        name="branch_merge",
    )(u, o_diff, o_mla, o_gqa, w_gate, w_gate, w_gate, b_gate, b_gate, b_gate,
      w_branch, w_branch, w_branch)


def _xattn_kernel(mg_ref, h_ref, wm_ref, g_ref, wq_ref, k_ref, v_ref, wo_ref, o_ref, ob_ref, wm_s, wq_s, wo_s):
    @pl.when(pl.program_id(0) == 0)
    def _():
        wm_s[...] = wm_ref[...].astype(BF16)
        wq_s[...] = wq_ref[...].astype(BF16)
        wo_s[...] = wo_ref[...].astype(BF16)

    h1 = h_ref[...] + jnp.dot(mg_ref[...], wm_s[...], preferred_element_type=F32)
    hn = _norm_rows(h1, g_ref[...])
    q = jnp.dot(hn, wq_s[...], preferred_element_type=F32) * (XA_DIM ** -0.5 * LOG2E)
    q = q.astype(BF16)
    heads = []
    for hh in range(XA_HEADS):
        sl = slice(hh * XA_DIM, (hh + 1) * XA_DIM)
        s = lax.dot_general(q[:, sl], k_ref[:, sl], (((1,), (1,)), ((), ())), preferred_element_type=F32)
        p = jnp.exp2(s - jnp.max(s, axis=1, keepdims=True))
        denom = jnp.sum(p, axis=1, keepdims=True)
        heads.append((jnp.dot(p.astype(BF16), v_ref[:, sl], preferred_element_type=F32) / denom).astype(BF16))
    o = jnp.concatenate(heads, axis=1)
    out = h1 + jnp.dot(o, wo_s[...], preferred_element_type=F32)
    o_ref[...] = out
    ob_ref[...] = out.astype(BF16)


def _cross_attention(merged, h, w_o, gain, xa_wq, memkv, xa_wo, l):
    m, d = h.shape
    w = XA_HEADS * XA_DIM
    ml = memkv.shape[0]
    bm = _pick(m, 256)
    once = dict(pipeline_mode=pl.Buffered(1))
    return pl.pallas_call(
        _xattn_kernel,
        grid=(m // bm,),
        in_specs=[pl.BlockSpec((bm, d), lambda i: (i, 0)),
                  pl.BlockSpec((bm, d), lambda i: (i, 0)),
                  pl.BlockSpec((None, d, d), lambda i: (l, 0, 0), **once),
                  pl.BlockSpec((None, 1, d), lambda i: (l, 0, 0)),
                  pl.BlockSpec((None, d, w), lambda i: (l, 0, 0), **once),
                  pl.BlockSpec((ml, w), lambda i: (0, 0)),
                  pl.BlockSpec((ml, w), lambda i: (0, 1)),
                  pl.BlockSpec((None, w, d), lambda i: (l, 0, 0), **once)],
        out_specs=[pl.BlockSpec((bm, d), lambda i: (i, 0))] * 2,
        out_shape=[jax.ShapeDtypeStruct((m, d), F32), jax.ShapeDtypeStruct((m, d), BF16)],
        scratch_shapes=[pltpu.VMEM((d, d), BF16), pltpu.VMEM((d, w), BF16), pltpu.VMEM((w, d), BF16)],
        compiler_params=_cparams(("arbitrary",)),
        name="cross_attention",
    )(merged, h, w_o, gain, xa_wq, memkv, memkv, xa_wo)


def _pad_cols(w, width):
    return jnp.pad(w, ((0, 0),) * (w.ndim - 1) + ((0, width - w.shape[-1]),))


_DQ, _DV, _CQ, _KR, _GQ, _GV, _END = 0, 2048, 3072, 4096, 4160, 5440, 5696


def _regroup_kernel(wt_ref, a_ref, v_ref, c_ref, g_ref):
    def cols(lo, hi):
        return wt_ref[lo:hi, :].T

    kr = jnp.concatenate([wt_ref[_KR:_GQ, :], jnp.zeros((LANES - (_GQ - _KR), wt_ref.shape[1]), wt_ref.dtype)], axis=0)
    a_ref[...] = jnp.concatenate([cols(_DQ, _DV), kr.T], axis=1).astype(BF16)
    v_ref[...] = jnp.concatenate([cols(_DV, _CQ), cols(_GV, _END)], axis=1).astype(BF16)
    c_ref[...] = cols(_CQ, _KR).astype(BF16)
    g_ref[...] = cols(_GQ, _GV).astype(BF16)


def _regroup_w_in(w_in):
    depth, k, n = w_in.shape
    assert n == _END
    bk = _pick(k, 256)
    widths = (_DV - _DQ + LANES, (_CQ - _DV) + (_END - _GV), _KR - _CQ, _GV - _GQ)
    return pl.pallas_call(
        _regroup_kernel,
        grid=(depth, k // bk),
        in_specs=[pl.BlockSpec((None, n, bk), lambda l, i: (l, 0, i))],
        out_specs=[pl.BlockSpec((None, bk, w), lambda l, i: (l, i, 0)) for w in widths],
        out_shape=[jax.ShapeDtypeStruct((depth, k, w), BF16) for w in widths],
        compiler_params=_cparams(("parallel", "parallel")),
        name="regroup_w_in",
    )(jnp.swapaxes(w_in, 1, 2))


def kernel(x, mem, positions, ffn1_norm, ffn1_w_gu, ffn1_w_down, mix_norm, w_in, diff_lambda, diff_subln,
           mla_q_norm, mla_kv_norm, mla_w_uq, mla_w_ukv, gqa_sink, w_branch, w_gate, b_gate, w_o, xa_norm,
           mem_norm, xa_wq, xa_wkv, xa_wo, ffn2_norm, ffn2_w_gu, ffn2_w_down, final_norm):
    b, s, d = x.shape
    assert b == 1
    depth = ffn1_norm.shape[0]
    h = x.reshape(s, d)
    mem2 = mem.reshape(mem.shape[1], d)

    pos_b = jnp.broadcast_to(positions.reshape(s, 1).astype(F32), (s, LANES))
    cos64, sin64 = _rope_tables(pos_b, DIFF_QK_DIM)
    cos128, sin128 = _rope_tables(pos_b, GQA_DIM)

    bf = lambda w: w.astype(BF16)
    xa_wkv_b = bf(xa_wkv)
    b_gate3 = b_gate.reshape(depth, 1, -1)
    norm3 = lambda g: g.reshape(depth, 1, -1)
    mix_norm3, xa_norm3, mem_norm3 = map(norm3, (mix_norm, xa_norm, mem_norm))
    ffn1_norm_col, ffn2_norm_col = ffn1_norm.reshape(depth, d, 1), ffn2_norm.reshape(depth, d, 1)
    hb = h
    diff_subln3 = diff_subln.reshape(depth, 1, -1)
    sink_flat = gqa_sink.reshape(-1)

    diff_scale = DIFF_QK_DIM ** -0.5 * LOG2E
    mla_scale = (MLA_NOPE + MLA_ROPE) ** -0.5 * LOG2E
    gqa_scale = GQA_DIM ** -0.5 * LOG2E
    cs_a = jnp.concatenate([jnp.full((1, 1024), diff_scale, F32), jnp.ones((1, 1024 + LANES), F32)], axis=1)
    cs_g = jnp.concatenate([jnp.full((1, 1024), gqa_scale, F32), jnp.ones((1, 256), F32)], axis=1)
    cs_q = jnp.full((1, MLA_HEADS * 2 * LANES), mla_scale, F32)

    w_a, w_v, w_c, w_g = _regroup_w_in(w_in)
    wuq = mla_w_uq.reshape(depth, MLA_Q_RANK, MLA_HEADS, MLA_NOPE + MLA_ROPE)
    wuq = _pad_cols(wuq, 2 * LANES).reshape(depth, MLA_Q_RANK, MLA_HEADS * 2 * LANES)
    wuq, wukv = bf(wuq), bf(mla_w_ukv)
    mla_gains = jnp.stack([mla_q_norm, mla_kv_norm], axis=1)[:, :, None, :]

    for l in range(depth):
        lambda_init = 0.8 - 0.6 * math.exp(-0.3 * l)

        a = _ffn_up(hb, ffn1_norm_col, ffn1_w_gu, l)
        h = _mm_residual(a, ffn1_w_down, l, h, 0.5)

        qk, u = _mm_rope(h, w_a, l, cs_a, cos64, sin64, DIFF_QK_DIM // 2, (True,) * 17, gain=mix_norm3)
        vv = _mm_plain(u, w_v, l)
        cn = _mm_rmsnorm(u, w_c, mla_gains, l)
        gqk = _mm_rope(u, w_g, l, cs_g, cos128, sin128, GQA_DIM // 2, (True,) * 10)

        o_diff = _diff_attention(qk, vv, diff_lambda, diff_subln3, l, lambda_init)

        q_mla = _mm_rope(cn, wuq, l, cs_q, cos64, sin64, MLA_ROPE // 2, (False, True) * MLA_HEADS, lead=0)
        kv_up = _mm_plain(cn, wukv, l, lead=1)
        o_mla = _mla_attention(q_mla, kv_up, qk)

        o_gqa = _window_gqa(gqk, vv, sink_flat, l)

        merged = _merge(u, o_diff, o_mla, o_gqa, w_gate, b_gate3, w_branch, l)
        memkv = _mm_plain(_rmsnorm(mem2, mem_norm3, l), xa_wkv_b, l)
        h, hb = _cross_attention(merged, h, w_o, xa_norm3, xa_wq, memkv, xa_wo, l)

        a = _ffn_up(hb, ffn2_norm_col, ffn2_w_gu, l)
        if l + 1 < depth:
            h, hb = _mm_residual(a, ffn2_w_down, l, h, 0.5, emit_bf16=True)
        else:
            h = _mm_residual(a, ffn2_w_down, l, h, 0.5)

    out = _rmsnorm(h, final_norm.reshape(1, d), None, out_dtype=F32)
    return out.reshape(b, s, d)
```

```python
import functools
import math

import jax
import jax.numpy as jnp
from jax import lax
from jax.experimental import pallas as pl
from jax.experimental.pallas import tpu as pltpu

F32 = jnp.float32
BF16 = jnp.bfloat16

DIFF_HEADS = 8
DIFF_QK_DIM = 64
DIFF_V_DIM = 128
MLA_HEADS = 8
MLA_Q_RANK = 512
MLA_NOPE = 128
MLA_ROPE = 64
MLA_V = 128
GQA_HEADS = 8
GQA_KV_HEADS = 2
GQA_GROUP = GQA_HEADS // GQA_KV_HEADS
GQA_DIM = 128
WINDOW = 128
XA_HEADS = 4
XA_DIM = 128
D_FF = 5632
ROPE_THETA = 10000.0
EPS = 1e-6
DIFF_SUBLN_EPS = 1e-5
NEG_INF = -1e30
LOG2E = math.log2(math.e)

LANES = 128
V7X_VMEM_LIMIT_BYTES = 56 * 1024 * 1024


def _cparams(semantics):
    return pltpu.CompilerParams(dimension_semantics=semantics, vmem_limit_bytes=V7X_VMEM_LIMIT_BYTES)


def _pick(dim, pref):
    if dim <= pref:
        return dim
    b = pref
    while dim % b:
        b //= 2
    return b


def _rope_table_kernel(pos_ref, inv_ref, sign_ref, cos_ref, sin_ref):
    ang = pos_ref[...] * inv_ref[...]
    cos_ref[...] = jnp.cos(ang)
    sin_ref[...] = jnp.sin(ang) * sign_ref[...]


def _rope_tables(pos_b, dim):
    s = pos_b.shape[0]
    half = dim // 2
    inv = ROPE_THETA ** (-jnp.arange(0, dim, 2, dtype=F32) / dim)
    inv_row = jnp.tile(inv, LANES // half)[None, :]
    sign_row = jnp.tile(jnp.concatenate([-jnp.ones((half,), F32), jnp.ones((half,), F32)]), LANES // dim)[None, :]
    bm = _pick(s, 1024)
    return pl.pallas_call(
        _rope_table_kernel,
        grid=(s // bm,),
        in_specs=[pl.BlockSpec((bm, LANES), lambda i: (i, 0)),
                  pl.BlockSpec((1, LANES), lambda i: (0, 0)),
                  pl.BlockSpec((1, LANES), lambda i: (0, 0))],
        out_specs=[pl.BlockSpec((bm, LANES), lambda i: (i, 0))] * 2,
        out_shape=[jax.ShapeDtypeStruct((s, LANES), F32)] * 2,
        compiler_params=_cparams(("parallel",)),
        name="rope_tables",
    )(pos_b, inv_row, sign_row)


def _rope_block(x, cos, sin_signed, half):
    if 2 * half == LANES:
        partner = pltpu.roll(x, half, axis=1)
    else:
        lane = lax.broadcasted_iota(jnp.int32, x.shape, 1)
        first = (lane % (2 * half)) < half
        partner = jnp.where(first, pltpu.roll(x, LANES - half, axis=1), pltpu.roll(x, half, axis=1))
    return x * cos + partner * sin_signed


def _rmsnorm_kernel(x_ref, g_ref, o_ref, *, eps):
    x = x_ref[...]
    y = x * lax.rsqrt(jnp.mean(x * x, axis=-1, keepdims=True) + eps)
    o_ref[...] = (y * g_ref[...]).astype(o_ref.dtype)


def _rmsnorm(x, g_stack, l, out_dtype=BF16):
    m, d = x.shape
    bm = _pick(m, 512)
    if l is None:
        g_spec = pl.BlockSpec((1, d), lambda i: (0, 0))
    else:
        g_spec = pl.BlockSpec((None, 1, d), lambda i: (l, 0, 0))
    return pl.pallas_call(
        functools.partial(_rmsnorm_kernel, eps=EPS),
        grid=(m // bm,),
        in_specs=[pl.BlockSpec((bm, d), lambda i: (i, 0)), g_spec],
        out_specs=pl.BlockSpec((bm, d), lambda i: (i, 0)),
        out_shape=jax.ShapeDtypeStruct((m, d), out_dtype),
        compiler_params=_cparams(("parallel",)),
        name="rmsnorm",
    )(x, g_stack)


def _cast_weights_once(pairs):
    @pl.when(pl.program_id(1) == 0)
    def _():
        for w_ref, w_s in pairs:
            w_s[...] = w_ref[...].astype(BF16)


def _norm_rows(x, g, eps=EPS):
    return (x * lax.rsqrt(jnp.mean(x * x, axis=-1, keepdims=True) + eps) * g).astype(BF16)


def _mm_swiglu_kernel(x_ref, gc_ref, wg_ref, wu_ref, o_ref, wg_s, wu_s, *, sub):
    @pl.when(pl.program_id(1) == 0)
    def _():
        gc = gc_ref[...]
        wg_s[...] = (wg_ref[...] * gc).astype(BF16)
        wu_s[...] = (wu_ref[...] * gc).astype(BF16)

    def rows_tile(t, carry):
        rows = pl.ds(pl.multiple_of(t * sub, sub), sub)
        x = x_ref[rows, :]
        xf = x.astype(F32)
        x = x.astype(BF16)
        rs = lax.rsqrt(jnp.mean(xf * xf, axis=-1, keepdims=True) + EPS)
        g = jnp.dot(x, wg_s[...], preferred_element_type=F32) * rs
        u = jnp.dot(x, wu_s[...], preferred_element_type=F32) * rs
        o_ref[rows, :] = (g * jax.nn.sigmoid(g) * u).astype(o_ref.dtype)
        return carry

    lax.fori_loop(0, x_ref.shape[0] // sub, rows_tile, 0)


def _ffn_up(hb, gain_col, w_gu, l):
    m, d = hb.shape
    n = w_gu.shape[-1] // 2
    bm = _pick(m, 2048 if hb.dtype == BF16 else 1024)
    bn = _pick(n, 512)
    nj = n // bn
    return pl.pallas_call(
        functools.partial(_mm_swiglu_kernel, sub=_pick(bm, 1024)),
        grid=(nj, m // bm),
        in_specs=[pl.BlockSpec((bm, d), lambda j, i: (i, 0)),
                  pl.BlockSpec((None, d, 1), lambda j, i: (l, 0, 0)),
                  pl.BlockSpec((None, d, bn), lambda j, i: (l, 0, j)),
                  pl.BlockSpec((None, d, bn), lambda j, i: (l, 0, j + nj))],
        out_specs=pl.BlockSpec((bm, bn), lambda j, i: (i, j)),
        out_shape=jax.ShapeDtypeStruct((m, n), BF16),
        scratch_shapes=[pltpu.VMEM((d, bn), BF16), pltpu.VMEM((d, bn), BF16)],
        compiler_params=_cparams(("arbitrary", "arbitrary")),
        name="ffn_up",
    )(hb, gain_col, w_gu, w_gu)


def _mm_residual_kernel(a_ref, w_ref, r_ref, o_ref, *rest, scale):
    w_s = rest[-1]
    _cast_weights_once(((w_ref, w_s),))
    out = r_ref[...] + scale * jnp.dot(a_ref[...], w_s[...], preferred_element_type=F32)
    o_ref[...] = out
    if len(rest) == 2:
        rest[0][...] = out.astype(BF16)


def _mm_residual(a, w, l, res, scale, emit_bf16=False):
    m, k = a.shape
    n = w.shape[-1]
    bm, bn = _pick(m, 256), _pick(n, 1024)
    o_spec = pl.BlockSpec((bm, bn), lambda j, i: (i, j))
    out_shape = jax.ShapeDtypeStruct((m, n), F32)
    return pl.pallas_call(
        functools.partial(_mm_residual_kernel, scale=scale),
        grid=(n // bn, m // bm),
        in_specs=[pl.BlockSpec((bm, k), lambda j, i: (i, 0)),
                  pl.BlockSpec((None, k, bn), lambda j, i: (l, 0, j), pipeline_mode=pl.Buffered(1)),
                  o_spec],
        out_specs=[o_spec, o_spec] if emit_bf16 else o_spec,
        out_shape=[out_shape, jax.ShapeDtypeStruct((m, n), BF16)] if emit_bf16 else out_shape,
        scratch_shapes=[pltpu.VMEM((k, bn), BF16)],
        compiler_params=_cparams(("arbitrary", "arbitrary")),
        name="mm_residual",
    )(a, w, res)


def _mm_plain_kernel(x_ref, w_ref, o_ref):
    o_ref[...] = jnp.dot(x_ref[...], w_ref[...], preferred_element_type=F32).astype(o_ref.dtype)


def _mm_plain(x, w, l):
    m, k = x.shape
    n = w.shape[-1]
    bm = _pick(m, 1024)
    w_spec = pl.BlockSpec((None, k, n), lambda i: (l, 0, 0))
    return pl.pallas_call(
        _mm_plain_kernel,
        grid=(m // bm,),
        in_specs=[pl.BlockSpec((bm, k), lambda i: (i, 0)), w_spec],
        out_specs=pl.BlockSpec((bm, n), lambda i: (i, 0)),
        out_shape=jax.ShapeDtypeStruct((m, n), BF16),
        compiler_params=_cparams(("parallel",)),
        name="mm_plain",
    )(x, w)


def _mm_rope_kernel(x_ref, w_ref, cs_ref, cos_ref, sin_ref, *rest, half, pattern):
    if len(rest) == 3:
        g_ref, o_ref, u_ref = rest
        x = _norm_rows(x_ref[...], g_ref[...])
        u_ref[...] = x
    else:
        (o_ref,) = rest
        x = x_ref[...]
    acc = jnp.dot(x, w_ref[...], preferred_element_type=F32) * cs_ref[...]
    cos = cos_ref[...]
    sin = sin_ref[...]
    for b, roped in enumerate(pattern):
        blk = acc[:, b * LANES:(b + 1) * LANES]
        if roped:
            blk = _rope_block(blk, cos, sin, half)
        o_ref[:, b * LANES:(b + 1) * LANES] = blk.astype(o_ref.dtype)


def _mm_rope(x, w, l, colscale, cos, sin, half, pattern, gain=None):
    m, k = x.shape
    n = w.shape[-1]
    assert n == LANES * len(pattern)
    bm = _pick(m, 512 if gain is not None else 1024)
    x_spec = pl.BlockSpec((bm, k), lambda i: (i, 0))
    o_spec = pl.BlockSpec((bm, n), lambda i: (i, 0))
    out_shape = jax.ShapeDtypeStruct((m, n), BF16)
    in_specs = [x_spec,
                pl.BlockSpec((None, k, n), lambda i: (l, 0, 0)),
                pl.BlockSpec((1, n), lambda i: (0, 0)),
                pl.BlockSpec((bm, LANES), lambda i: (i, 0)),
                pl.BlockSpec((bm, LANES), lambda i: (i, 0))]
    args = (x, w, colscale, cos, sin)
    if gain is not None:
        in_specs.append(pl.BlockSpec((None, 1, k), lambda i: (l, 0, 0)))
        args += (gain,)
    return pl.pallas_call(
        functools.partial(_mm_rope_kernel, half=half, pattern=pattern),
        grid=(m // bm,),
        in_specs=in_specs,
        out_specs=o_spec if gain is None else [o_spec, x_spec],
        out_shape=out_shape if gain is None else [out_shape, jax.ShapeDtypeStruct((m, k), BF16)],
        compiler_params=_cparams(("parallel",)),
        name="mm_rope",
    )(*args)


def _mla_proj_kernel(u_ref, wc_ref, g_ref, wuq_ref, wukv_ref, cs_ref, cos_ref, sin_ref, q_ref, kv_ref, *, pattern):
    lat = jnp.dot(u_ref[...], wc_ref[...], preferred_element_type=F32)
    r = lat.shape[1] // 2
    cq = _norm_rows(lat[:, :r], g_ref[0])
    ckv = _norm_rows(lat[:, r:], g_ref[1])
    kv_ref[...] = jnp.dot(ckv, wukv_ref[...], preferred_element_type=F32).astype(kv_ref.dtype)
    q = jnp.dot(cq, wuq_ref[...], preferred_element_type=F32) * cs_ref[...]
    cos = cos_ref[...]
    sin = sin_ref[...]
    for b, roped in enumerate(pattern):
        blk = q[:, b * LANES:(b + 1) * LANES]
        if roped:
            blk = _rope_block(blk, cos, sin, MLA_ROPE // 2)
        q_ref[:, b * LANES:(b + 1) * LANES] = blk.astype(q_ref.dtype)


def _mla_proj(u, w_c, gains, wuq, wukv, l, colscale, cos, sin):
    m, k = u.shape
    nc = w_c.shape[-1]
    r = nc // 2
    nq, nkv = wuq.shape[-1], wukv.shape[-1]
    bm = _pick(m, 512)
    return pl.pallas_call(
        functools.partial(_mla_proj_kernel, pattern=(False, True) * MLA_HEADS),
        grid=(m // bm,),
        in_specs=[pl.BlockSpec((bm, k), lambda i: (i, 0)),
                  pl.BlockSpec((None, k, nc), lambda i: (l, 0, 0)),
                  pl.BlockSpec((None, 2, 1, r), lambda i: (l, 0, 0, 0)),
                  pl.BlockSpec((None, r, nq), lambda i: (l, 0, 0)),
                  pl.BlockSpec((None, r, nkv), lambda i: (l, 0, 0)),
                  pl.BlockSpec((1, nq), lambda i: (0, 0)),
                  pl.BlockSpec((bm, LANES), lambda i: (i, 0)),
                  pl.BlockSpec((bm, LANES), lambda i: (i, 0))],
        out_specs=[pl.BlockSpec((bm, nq), lambda i: (i, 0)), pl.BlockSpec((bm, nkv), lambda i: (i, 0))],
        out_shape=[jax.ShapeDtypeStruct((m, nq), BF16), jax.ShapeDtypeStruct((m, nkv), BF16)],
        compiler_params=_cparams(("parallel",)),
        name="mla_proj",
    )(u, w_c, gains, wuq, wukv, colscale, cos, sin)


def _logits(q, k, mlane):
    s = lax.dot_general(q, k, (((1,), (1,)), ((), ())), preferred_element_type=F32)
    for b in range(s.shape[1] // LANES):
        blk = s[:, b * LANES:(b + 1) * LANES]
        mlane = blk if mlane is None else jnp.maximum(mlane, blk)
    return s, mlane


def _row_max(mlane):
    return jnp.broadcast_to(jnp.max(mlane, axis=1, keepdims=True), mlane.shape)


def _weigh(s, m, v1, acc):
    ps = [jnp.exp2(s[:, b * LANES:(b + 1) * LANES] - m).astype(BF16) for b in range(s.shape[1] // LANES)]
    pv = jnp.dot(jnp.concatenate(ps, axis=1), v1, preferred_element_type=F32)
    return pv if acc is None else acc + pv


def _softmax_pv(qs, k_chunk, v1_chunk, nchunks):
    n = len(qs)
    logits = [[None] * nchunks for _ in range(n)]
    acc = [None] * n
    mlane = None
    for j in range(nchunks):
        logits[0][j], mlane = _logits(qs[0], k_chunk(j), mlane)
    m = _row_max(mlane)
    for p in range(1, n):
        mlane = None
        for j in range(nchunks):
            acc[p - 1] = _weigh(logits[p - 1][j], m, v1_chunk(j), acc[p - 1])
            logits[p][j], mlane = _logits(qs[p], k_chunk(j), mlane)
        m = _row_max(mlane)
    for j in range(nchunks):
        acc[n - 1] = _weigh(logits[n - 1][j], m, v1_chunk(j), acc[n - 1])
    return [a[:, :a.shape[1] - LANES] / a[:, a.shape[1] - LANES:] for a in acc]


def _attn_tiles(s):
    rows = _pick(s, 512)
    return rows, (2 if s % (2 * rows) == 0 else 1), _pick(s, 1024)


def _diff_attn_kernel(lp_ref, q_ref, k_ref, v_ref, g_ref, o_ref, *, lambda_init, tk, nsub):
    bq = q_ref.shape[0] // nsub
    ones = jnp.ones((tk, LANES), BF16)
    k_chunk = lambda j: k_ref[j * tk:(j + 1) * tk, :]
    v1_chunk = lambda j: jnp.concatenate([v_ref[j * tk:(j + 1) * tk, :], ones], axis=1)
    qs = []
    for r in range(nsub):
        q = q_ref[r * bq:(r + 1) * bq, :]
        lane = lax.broadcasted_iota(jnp.int32, q.shape, 1)
        zero = jnp.zeros_like(q)
        qs.append(jnp.concatenate([jnp.where(lane < DIFF_QK_DIM, q, zero),
                                   jnp.where(lane >= DIFF_QK_DIM, q, zero)], axis=0))
    outs = _softmax_pv(qs, k_chunk, v1_chunk, k_ref.shape[0] // tk)
    lp = lp_ref[...]
    lam = (jnp.exp(jnp.sum(lp[0:1] * lp[1:2], axis=1, keepdims=True))
           - jnp.exp(jnp.sum(lp[2:3] * lp[3:4], axis=1, keepdims=True)) + lambda_init)
    for r in range(nsub):
        o = outs[r][:bq] - lam * outs[r][bq:]
        y = o * lax.rsqrt(jnp.mean(o * o, axis=-1, keepdims=True) + DIFF_SUBLN_EPS)
        o_ref[r * bq:(r + 1) * bq, :] = (y * g_ref[...] * (1.0 - lambda_init)).astype(o_ref.dtype)


def _diff_attention(qk, v, diff_lambda, diff_subln, l, lambda_init):
    s = qk.shape[0]
    prob_rows, nsub, tk = _attn_tiles(s)
    rows = prob_rows // 2 * nsub
    h = DIFF_HEADS
    return pl.pallas_call(
        functools.partial(_diff_attn_kernel, lambda_init=lambda_init, tk=tk, nsub=nsub),
        grid=(h, s // rows),
        in_specs=[pl.BlockSpec((None, 4, DIFF_QK_DIM), lambda hh, i: (l, 0, 0)),
                  pl.BlockSpec((rows, LANES), lambda hh, i: (i, hh)),
                  pl.BlockSpec((s, LANES), lambda hh, i: (0, h + hh)),
                  pl.BlockSpec((s, LANES), lambda hh, i: (0, hh)),
                  pl.BlockSpec((None, 1, DIFF_V_DIM), lambda hh, i: (l, 0, 0))],
        out_specs=pl.BlockSpec((rows, LANES), lambda hh, i: (i, hh)),
        out_shape=jax.ShapeDtypeStruct((s, h * DIFF_V_DIM), BF16),
        compiler_params=_cparams(("parallel", "parallel")),
        name="diff_attention",
    )(diff_lambda, qk, qk, v, diff_subln)


def _mla_attn_kernel(q_ref, kn_ref, kr_ref, v_ref, o_ref, *, tk, nsub):
    bq = q_ref.shape[0] // nsub
    ones = jnp.ones((tk, LANES), BF16)
    k_chunk = lambda j: jnp.concatenate([kn_ref[j * tk:(j + 1) * tk, :], kr_ref[j * tk:(j + 1) * tk, :]], axis=1)
    v1_chunk = lambda j: jnp.concatenate([v_ref[j * tk:(j + 1) * tk, :], ones], axis=1)
    qs = [q_ref[r * bq:(r + 1) * bq, :] for r in range(nsub)]
    outs = _softmax_pv(qs, k_chunk, v1_chunk, kn_ref.shape[0] // tk)
    for r in range(nsub):
        o_ref[r * bq:(r + 1) * bq, :] = outs[r].astype(o_ref.dtype)


def _mla_attention(q, kv_up, qk):
    s = q.shape[0]
    bq, nsub, tk = _attn_tiles(s)
    rows = bq * nsub
    h = MLA_HEADS
    kr_blk = 2 * DIFF_HEADS
    return pl.pallas_call(
        functools.partial(_mla_attn_kernel, tk=tk, nsub=nsub),
        grid=(h, s // rows),
        in_specs=[pl.BlockSpec((rows, 2 * LANES), lambda hh, i: (i, hh)),
                  pl.BlockSpec((s, LANES), lambda hh, i: (0, 2 * hh)),
                  pl.BlockSpec((s, LANES), lambda hh, i: (0, kr_blk)),
                  pl.BlockSpec((s, LANES), lambda hh, i: (0, 2 * hh + 1))],
        out_specs=pl.BlockSpec((rows, LANES), lambda hh, i: (i, hh)),
        out_shape=jax.ShapeDtypeStruct((s, h * MLA_V), BF16),
        compiler_params=_cparams(("parallel", "parallel")),
        name="mla_attention",
    )(q, kv_up, qk, kv_up)


def _gqa_kernel(sink_ref, q_ref, kp_ref, kc_ref, kn_ref, vp_ref, vc_ref, vn_ref, o_ref, *, layer, seq, bq):
    g = pl.program_id(0)
    i = pl.program_id(1)
    nsub = q_ref.shape[0] // bq
    nk = bq + 2 * WINDOW
    band_k = jnp.concatenate([kp_ref[...], kc_ref[...], kn_ref[...]], axis=0)
    band_v = jnp.concatenate([vp_ref[...], vc_ref[...], vn_ref[...]], axis=0)
    band_v1 = jnp.concatenate([band_v, jnp.ones_like(band_v)], axis=1)
    for r in range(nsub):
        q0 = (i * nsub + r) * bq
        kb = band_k[r * bq:r * bq + nk]
        v1 = band_v1[r * bq:r * bq + nk]
        qpos = q0 + lax.broadcasted_iota(jnp.int32, (bq, nk), 0)
        kpos = q0 - WINDOW + lax.broadcasted_iota(jnp.int32, (bq, nk), 1)
        valid = (jnp.abs(kpos - qpos) <= WINDOW) & (kpos >= 0) & (kpos < seq)
        bias = jnp.where(valid, 0.0, NEG_INF).astype(F32)
        for hh in range(GQA_GROUP):
            q = q_ref[r * bq:(r + 1) * bq, hh * GQA_DIM:(hh + 1) * GQA_DIM]
            s = lax.dot_general(q, kb, (((1,), (1,)), ((), ())), preferred_element_type=F32) + bias
            sink = sink_ref[layer * GQA_HEADS + g * GQA_GROUP + hh] * LOG2E
            m = jnp.maximum(jnp.max(s, axis=1, keepdims=True), sink)
            pv = jnp.dot(jnp.exp2(s - m).astype(BF16), v1, preferred_element_type=F32)
            denom = pv[:, GQA_DIM:] + jnp.exp2(sink - m)
            o_ref[r * bq:(r + 1) * bq, hh * GQA_DIM:(hh + 1) * GQA_DIM] = (pv[:, :GQA_DIM] / denom).astype(o_ref.dtype)


def _window_gqa(gqk, vv, sink_flat, l):
    s = gqk.shape[0]
    sub = _pick(s, 256)
    bq = _pick(s, 4 * sub)
    assert sub % WINDOW == 0 and WINDOW == LANES
    nb = s // bq
    r = bq // WINDOW
    nh = s // WINDOW
    qw = GQA_GROUP * GQA_DIM
    kcol = GQA_HEADS
    vcol = DIFF_HEADS

    def prev(i):
        return jnp.maximum(i * r - 1, 0)

    def nxt(i):
        return jnp.minimum((i + 1) * r, nh - 1)

    return pl.pallas_call(
        functools.partial(_gqa_kernel, layer=l, seq=s, bq=sub),
        grid=(GQA_KV_HEADS, nb),
        in_specs=[pl.BlockSpec(memory_space=pltpu.SMEM),
                  pl.BlockSpec((bq, qw), lambda g, i: (i, g)),
                  pl.BlockSpec((WINDOW, LANES), lambda g, i: (prev(i), kcol + g)),
                  pl.BlockSpec((bq, LANES), lambda g, i: (i, kcol + g)),
                  pl.BlockSpec((WINDOW, LANES), lambda g, i: (nxt(i), kcol + g)),
                  pl.BlockSpec((WINDOW, LANES), lambda g, i: (prev(i), vcol + g)),
                  pl.BlockSpec((bq, LANES), lambda g, i: (i, vcol + g)),
                  pl.BlockSpec((WINDOW, LANES), lambda g, i: (nxt(i), vcol + g))],
        out_specs=pl.BlockSpec((bq, qw), lambda g, i: (i, g)),
        out_shape=jax.ShapeDtypeStruct((s, GQA_HEADS * GQA_DIM), BF16),
        compiler_params=_cparams(("parallel", "parallel")),
        name="window_gqa",
    )(sink_flat, gqk, gqk, gqk, gqk, vv, vv, vv)


def _merge_kernel(u_ref, o0_ref, o1_ref, o2_ref, wg0_ref, wg1_ref, wg2_ref, b0_ref, b1_ref, b2_ref,
                  wb0_ref, wb1_ref, wb2_ref, out_ref, wg0_s, wg1_s, wg2_s, wb0_s, wb1_s, wb2_s):
    _cast_weights_once(((wg0_ref, wg0_s), (wg1_ref, wg1_s), (wg2_ref, wg2_s),
                        (wb0_ref, wb0_s), (wb1_ref, wb1_s), (wb2_ref, wb2_s)))
    u = u_ref[...]
    acc = None
    for o_ref, wg_s, b_ref, wb_s in ((o0_ref, wg0_s, b0_ref, wb0_s),
                                     (o1_ref, wg1_s, b1_ref, wb1_s),
                                     (o2_ref, wg2_s, b2_ref, wb2_s)):
        gate = jax.nn.sigmoid(jnp.dot(u, wg_s[...], preferred_element_type=F32) + b_ref[...])
        term = jnp.dot(o_ref[...], wb_s[...], preferred_element_type=F32) * gate
        acc = term if acc is None else acc + term
    out_ref[...] = acc.astype(out_ref.dtype)


def _merge(u, o_diff, o_mla, o_gqa, w_gate, b_gate, w_branch, l):
    m, d = u.shape
    bw = o_diff.shape[1]
    bm = _pick(m, 1024)
    bn = _pick(d, 256)
    nj = d // bn
    o_spec = pl.BlockSpec((bm, bw), lambda j, i: (i, 0))

    def wg_spec(n):
        return pl.BlockSpec((None, d, bn), lambda j, i: (l, 0, n * nj + j))

    def b_spec(n):
        return pl.BlockSpec((None, 1, bn), lambda j, i: (l, 0, n * nj + j))

    def wb_spec(n):
        return pl.BlockSpec((None, None, bw, bn), lambda j, i: (l, n, 0, j))

    return pl.pallas_call(
        _merge_kernel,
        grid=(nj, m // bm),
        in_specs=[pl.BlockSpec((bm, d), lambda j, i: (i, 0)), o_spec, o_spec, o_spec,
                  wg_spec(0), wg_spec(1), wg_spec(2), b_spec(0), b_spec(1), b_spec(2),
                  wb_spec(0), wb_spec(1), wb_spec(2)],
        out_specs=pl.BlockSpec((bm, bn), lambda j, i: (i, j)),
        out_shape=jax.ShapeDtypeStruct((m, d), BF16),
        scratch_shapes=[pltpu.VMEM((d, bn), BF16)] * 3 + [pltpu.VMEM((bw, bn), BF16)] * 3,
        compiler_params=_cparams(("arbitrary", "arbitrary")),
        name="branch_merge",
    )(u, o_diff, o_mla, o_gqa, w_gate, w_gate, w_gate, b_gate, b_gate, b_gate,
      w_branch, w_branch, w_branch)


def _xattn_kernel(mg_ref, h_ref, wm_ref, g_ref, wq_ref, k_ref, v_ref, wo_ref, o_ref, ob_ref, wm_s, wq_s, wo_s):
    @pl.when(pl.program_id(0) == 0)
    def _():
        wm_s[...] = wm_ref[...].astype(BF16)
        wq_s[...] = wq_ref[...].astype(BF16)
        wo_s[...] = wo_ref[...].astype(BF16)

    h1 = h_ref[...] + jnp.dot(mg_ref[...], wm_s[...], preferred_element_type=F32)
    hn = _norm_rows(h1, g_ref[...])
    q = jnp.dot(hn, wq_s[...], preferred_element_type=F32) * (XA_DIM ** -0.5 * LOG2E)
    q = q.astype(BF16)
    heads = []
    for hh in range(XA_HEADS):
        sl = slice(hh * XA_DIM, (hh + 1) * XA_DIM)
        s = lax.dot_general(q[:, sl], k_ref[:, sl], (((1,), (1,)), ((), ())), preferred_element_type=F32)
        p = jnp.exp2(s - jnp.max(s, axis=1, keepdims=True))
        denom = jnp.sum(p, axis=1, keepdims=True)
        heads.append((jnp.dot(p.astype(BF16), v_ref[:, sl], preferred_element_type=F32) / denom).astype(BF16))
    o = jnp.concatenate(heads, axis=1)
    out = h1 + jnp.dot(o, wo_s[...], preferred_element_type=F32)
    o_ref[...] = out
    ob_ref[...] = out.astype(BF16)


def _cross_attention(merged, h, w_o, gain, xa_wq, memkv, xa_wo, l):
    m, d = h.shape
    w = XA_HEADS * XA_DIM
    ml = memkv.shape[0]
    bm = _pick(m, 256)
    once = dict(pipeline_mode=pl.Buffered(1))
    return pl.pallas_call(
        _xattn_kernel,
        grid=(m // bm,),
        in_specs=[pl.BlockSpec((bm, d), lambda i: (i, 0)),
                  pl.BlockSpec((bm, d), lambda i: (i, 0)),
                  pl.BlockSpec((None, d, d), lambda i: (l, 0, 0), **once),
                  pl.BlockSpec((None, 1, d), lambda i: (l, 0, 0)),
                  pl.BlockSpec((None, d, w), lambda i: (l, 0, 0), **once),
                  pl.BlockSpec((ml, w), lambda i: (0, 0)),
                  pl.BlockSpec((ml, w), lambda i: (0, 1)),
                  pl.BlockSpec((None, w, d), lambda i: (l, 0, 0), **once)],
        out_specs=[pl.BlockSpec((bm, d), lambda i: (i, 0))] * 2,
        out_shape=[jax.ShapeDtypeStruct((m, d), F32), jax.ShapeDtypeStruct((m, d), BF16)],
        scratch_shapes=[pltpu.VMEM((d, d), BF16), pltpu.VMEM((d, w), BF16), pltpu.VMEM((w, d), BF16)],
        compiler_params=_cparams(("arbitrary",)),
        name="cross_attention",
    )(merged, h, w_o, gain, xa_wq, memkv, memkv, xa_wo)


def _pad_cols(w, width):
    return jnp.pad(w, ((0, 0),) * (w.ndim - 1) + ((0, width - w.shape[-1]),))


_DQ, _DV, _CQ, _KR, _GQ, _GV, _END = 0, 2048, 3072, 4096, 4160, 5440, 5696


def _regroup_kernel(wt_ref, a_ref, v_ref, c_ref, g_ref):
    def cols(lo, hi):
        return wt_ref[lo:hi, :].T

    kr = jnp.concatenate([wt_ref[_KR:_GQ, :], jnp.zeros((LANES - (_GQ - _KR), wt_ref.shape[1]), wt_ref.dtype)], axis=0)
    a_ref[...] = jnp.concatenate([cols(_DQ, _DV), kr.T], axis=1).astype(BF16)
    v_ref[...] = jnp.concatenate([cols(_DV, _CQ), cols(_GV, _END)], axis=1).astype(BF16)
    c_ref[...] = cols(_CQ, _KR).astype(BF16)
    g_ref[...] = cols(_GQ, _GV).astype(BF16)


def _regroup_w_in(w_in):
    depth, k, n = w_in.shape
    assert n == _END
    bk = _pick(k, 256)
    widths = (_DV - _DQ + LANES, (_CQ - _DV) + (_END - _GV), _KR - _CQ, _GV - _GQ)
    return pl.pallas_call(
        _regroup_kernel,
        grid=(depth, k // bk),
        in_specs=[pl.BlockSpec((None, n, bk), lambda l, i: (l, 0, i))],
        out_specs=[pl.BlockSpec((None, bk, w), lambda l, i: (l, i, 0)) for w in widths],
        out_shape=[jax.ShapeDtypeStruct((depth, k, w), BF16) for w in widths],
        compiler_params=_cparams(("parallel", "parallel")),
        name="regroup_w_in",
    )(jnp.swapaxes(w_in, 1, 2))


def kernel(x, mem, positions, ffn1_norm, ffn1_w_gu, ffn1_w_down, mix_norm, w_in, diff_lambda, diff_subln,
           mla_q_norm, mla_kv_norm, mla_w_uq, mla_w_ukv, gqa_sink, w_branch, w_gate, b_gate, w_o, xa_norm,
           mem_norm, xa_wq, xa_wkv, xa_wo, ffn2_norm, ffn2_w_gu, ffn2_w_down, final_norm):
    b, s, d = x.shape
    assert b == 1
    depth = ffn1_norm.shape[0]
    h = x.reshape(s, d)
    mem2 = mem.reshape(mem.shape[1], d)

    pos_b = jnp.broadcast_to(positions.reshape(s, 1).astype(F32), (s, LANES))
    cos64, sin64 = _rope_tables(pos_b, DIFF_QK_DIM)
    cos128, sin128 = _rope_tables(pos_b, GQA_DIM)

    bf = lambda w: w.astype(BF16)
    xa_wkv_b = bf(xa_wkv)
    b_gate3 = b_gate.reshape(depth, 1, -1)
    norm3 = lambda g: g.reshape(depth, 1, -1)
    mix_norm3, xa_norm3, mem_norm3 = map(norm3, (mix_norm, xa_norm, mem_norm))
    ffn1_norm_col, ffn2_norm_col = ffn1_norm.reshape(depth, d, 1), ffn2_norm.reshape(depth, d, 1)
    hb = h
    diff_subln3 = diff_subln.reshape(depth, 1, -1)
    sink_flat = gqa_sink.reshape(-1)

    diff_scale = DIFF_QK_DIM ** -0.5 * LOG2E
    mla_scale = (MLA_NOPE + MLA_ROPE) ** -0.5 * LOG2E
    gqa_scale = GQA_DIM ** -0.5 * LOG2E
    cs_a = jnp.concatenate([jnp.full((1, 1024), diff_scale, F32), jnp.ones((1, 1024 + LANES), F32)], axis=1)
    cs_g = jnp.concatenate([jnp.full((1, 1024), gqa_scale, F32), jnp.ones((1, 256), F32)], axis=1)
    cs_q = jnp.full((1, MLA_HEADS * 2 * LANES), mla_scale, F32)

    w_a, w_v, w_c, w_g = _regroup_w_in(w_in)
    wuq = mla_w_uq.reshape(depth, MLA_Q_RANK, MLA_HEADS, MLA_NOPE + MLA_ROPE)
    wuq = _pad_cols(wuq, 2 * LANES).reshape(depth, MLA_Q_RANK, MLA_HEADS * 2 * LANES)
    wuq, wukv = bf(wuq), bf(mla_w_ukv)
    mla_gains = jnp.stack([mla_q_norm, mla_kv_norm], axis=1)[:, :, None, :]

    for l in range(depth):
        lambda_init = 0.8 - 0.6 * math.exp(-0.3 * l)

        a = _ffn_up(hb, ffn1_norm_col, ffn1_w_gu, l)
        h = _mm_residual(a, ffn1_w_down, l, h, 0.5)

        qk, u = _mm_rope(h, w_a, l, cs_a, cos64, sin64, DIFF_QK_DIM // 2, (True,) * 17, gain=mix_norm3)
        vv = _mm_plain(u, w_v, l)
        gqk = _mm_rope(u, w_g, l, cs_g, cos128, sin128, GQA_DIM // 2, (True,) * 10)

        o_diff = _diff_attention(qk, vv, diff_lambda, diff_subln3, l, lambda_init)

        q_mla, kv_up = _mla_proj(u, w_c, mla_gains, wuq, wukv, l, cs_q, cos64, sin64)
        o_mla = _mla_attention(q_mla, kv_up, qk)

        o_gqa = _window_gqa(gqk, vv, sink_flat, l)

        merged = _merge(u, o_diff, o_mla, o_gqa, w_gate, b_gate3, w_branch, l)
        memkv = _mm_plain(_rmsnorm(mem2, mem_norm3, l), xa_wkv_b, l)
        h, hb = _cross_attention(merged, h, w_o, xa_norm3, xa_wq, memkv, xa_wo, l)

        a = _ffn_up(hb, ffn2_norm_col, ffn2_w_gu, l)
        if l + 1 < depth:
            h, hb = _mm_residual(a, ffn2_w_down, l, h, 0.5, emit_bf16=True)
        else:
            h = _mm_residual(a, ffn2_w_down, l, h, 0.5)

    out = _rmsnorm(h, final_norm.reshape(1, d), None, out_dtype=F32)
    return out.reshape(b, s, d)
```

```python
import functools
import math

import jax
import jax.numpy as jnp
from jax import lax
from jax.experimental import pallas as pl
from jax.experimental.pallas import tpu as pltpu

F32 = jnp.float32
BF16 = jnp.bfloat16

DIFF_HEADS = 8
DIFF_QK_DIM = 64
DIFF_V_DIM = 128
MLA_HEADS = 8
MLA_Q_RANK = 512
MLA_NOPE = 128
MLA_ROPE = 64
MLA_V = 128
GQA_HEADS = 8
GQA_KV_HEADS = 2
GQA_GROUP = GQA_HEADS // GQA_KV_HEADS
GQA_DIM = 128
WINDOW = 128
XA_HEADS = 4
XA_DIM = 128
D_FF = 5632
ROPE_THETA = 10000.0
EPS = 1e-6
DIFF_SUBLN_EPS = 1e-5
NEG_INF = -1e30
LOG2E = math.log2(math.e)

LANES = 128
V7X_VMEM_LIMIT_BYTES = 56 * 1024 * 1024


def _cparams(semantics):
    return pltpu.CompilerParams(dimension_semantics=semantics, vmem_limit_bytes=V7X_VMEM_LIMIT_BYTES)


def _pick(dim, pref):
    if dim <= pref:
        return dim
    b = pref
    while dim % b:
        b //= 2
    return b


def _rope_table_kernel(pos_ref, inv_ref, sign_ref, cos_ref, sin_ref):
    ang = pos_ref[...] * inv_ref[...]
    cos_ref[...] = jnp.cos(ang)
    sin_ref[...] = jnp.sin(ang) * sign_ref[...]


def _rope_tables(pos_b, dim):
    s = pos_b.shape[0]
    half = dim // 2
    inv = ROPE_THETA ** (-jnp.arange(0, dim, 2, dtype=F32) / dim)
    inv_row = jnp.tile(inv, LANES // half)[None, :]
    sign_row = jnp.tile(jnp.concatenate([-jnp.ones((half,), F32), jnp.ones((half,), F32)]), LANES // dim)[None, :]
    bm = _pick(s, 1024)
    return pl.pallas_call(
        _rope_table_kernel,
        grid=(s // bm,),
        in_specs=[pl.BlockSpec((bm, LANES), lambda i: (i, 0)),
                  pl.BlockSpec((1, LANES), lambda i: (0, 0)),
                  pl.BlockSpec((1, LANES), lambda i: (0, 0))],
        out_specs=[pl.BlockSpec((bm, LANES), lambda i: (i, 0))] * 2,
        out_shape=[jax.ShapeDtypeStruct((s, LANES), F32)] * 2,
        compiler_params=_cparams(("parallel",)),
        name="rope_tables",
    )(pos_b, inv_row, sign_row)


def _rope_block(x, cos, sin_signed, half):
    if 2 * half == LANES:
        partner = pltpu.roll(x, half, axis=1)
    else:
        lane = lax.broadcasted_iota(jnp.int32, x.shape, 1)
        first = (lane % (2 * half)) < half
        partner = jnp.where(first, pltpu.roll(x, LANES - half, axis=1), pltpu.roll(x, half, axis=1))
    return x * cos + partner * sin_signed


def _rmsnorm_kernel(x_ref, g_ref, o_ref, *, eps):
    x = x_ref[...]
    y = x * lax.rsqrt(jnp.mean(x * x, axis=-1, keepdims=True) + eps)
    o_ref[...] = (y * g_ref[...]).astype(o_ref.dtype)


def _rmsnorm(x, g_stack, l, out_dtype=BF16):
    m, d = x.shape
    bm = _pick(m, 512)
    if l is None:
        g_spec = pl.BlockSpec((1, d), lambda i: (0, 0))
    else:
        g_spec = pl.BlockSpec((None, 1, d), lambda i: (l, 0, 0))
    return pl.pallas_call(
        functools.partial(_rmsnorm_kernel, eps=EPS),
        grid=(m // bm,),
        in_specs=[pl.BlockSpec((bm, d), lambda i: (i, 0)), g_spec],
        out_specs=pl.BlockSpec((bm, d), lambda i: (i, 0)),
        out_shape=jax.ShapeDtypeStruct((m, d), out_dtype),
        compiler_params=_cparams(("parallel",)),
        name="rmsnorm",
    )(x, g_stack)


def _cast_weights_once(pairs):
    @pl.when(pl.program_id(1) == 0)
    def _():
        for w_ref, w_s in pairs:
            w_s[...] = w_ref[...].astype(BF16)


def _norm_rows(x, g, eps=EPS):
    return (x * lax.rsqrt(jnp.mean(x * x, axis=-1, keepdims=True) + eps) * g).astype(BF16)


def _mm_swiglu_kernel(x_ref, gc_ref, wg_ref, wu_ref, o_ref, wg_s, wu_s, *, sub):
    @pl.when(pl.program_id(1) == 0)
    def _():
        gc = gc_ref[...]
        wg_s[...] = (wg_ref[...] * gc).astype(BF16)
        wu_s[...] = (wu_ref[...] * gc).astype(BF16)

    def rows_tile(t, carry):
        rows = pl.ds(pl.multiple_of(t * sub, sub), sub)
        x = x_ref[rows, :]
        xf = x.astype(F32)
        x = x.astype(BF16)
        rs = lax.rsqrt(jnp.mean(xf * xf, axis=-1, keepdims=True) + EPS)
        g = jnp.dot(x, wg_s[...], preferred_element_type=F32) * rs
        u = jnp.dot(x, wu_s[...], preferred_element_type=F32) * rs
        o_ref[rows, :] = (g * jax.nn.sigmoid(g) * u).astype(o_ref.dtype)
        return carry

    lax.fori_loop(0, x_ref.shape[0] // sub, rows_tile, 0)


def _ffn_up(hb, gain_col, w_gu, l):
    m, d = hb.shape
    n = w_gu.shape[-1] // 2
    bm = _pick(m, 2048 if hb.dtype == BF16 else 1024)
    bn = _pick(n, 512)
    nj = n // bn
    return pl.pallas_call(
        functools.partial(_mm_swiglu_kernel, sub=_pick(bm, 1024)),
        grid=(nj, m // bm),
        in_specs=[pl.BlockSpec((bm, d), lambda j, i: (i, 0)),
                  pl.BlockSpec((None, d, 1), lambda j, i: (l, 0, 0)),
                  pl.BlockSpec((None, d, bn), lambda j, i: (l, 0, j)),
                  pl.BlockSpec((None, d, bn), lambda j, i: (l, 0, j + nj))],
        out_specs=pl.BlockSpec((bm, bn), lambda j, i: (i, j)),
        out_shape=jax.ShapeDtypeStruct((m, n), BF16),
        scratch_shapes=[pltpu.VMEM((d, bn), BF16), pltpu.VMEM((d, bn), BF16)],
        compiler_params=_cparams(("arbitrary", "arbitrary")),
        name="ffn_up",
    )(hb, gain_col, w_gu, w_gu)


def _mm_residual_kernel(a_ref, w_ref, r_ref, o_ref, *rest, scale):
    w_s = rest[-1]
    _cast_weights_once(((w_ref, w_s),))
    out = r_ref[...] + scale * jnp.dot(a_ref[...], w_s[...], preferred_element_type=F32)
    o_ref[...] = out
    if len(rest) == 2:
        rest[0][...] = out.astype(BF16)


def _mm_residual(a, w, l, res, scale, emit_bf16=False):
    m, k = a.shape
    n = w.shape[-1]
    bm, bn = _pick(m, 256), _pick(n, 1024)
    o_spec = pl.BlockSpec((bm, bn), lambda j, i: (i, j))
    out_shape = jax.ShapeDtypeStruct((m, n), F32)
    return pl.pallas_call(
        functools.partial(_mm_residual_kernel, scale=scale),
        grid=(n // bn, m // bm),
        in_specs=[pl.BlockSpec((bm, k), lambda j, i: (i, 0)),
                  pl.BlockSpec((None, k, bn), lambda j, i: (l, 0, j), pipeline_mode=pl.Buffered(1)),
                  o_spec],
        out_specs=[o_spec, o_spec] if emit_bf16 else o_spec,
        out_shape=[out_shape, jax.ShapeDtypeStruct((m, n), BF16)] if emit_bf16 else out_shape,
        scratch_shapes=[pltpu.VMEM((k, bn), BF16)],
        compiler_params=_cparams(("arbitrary", "arbitrary")),
        name="mm_residual",
    )(a, w, res)


def _mm_plain_kernel(x_ref, w_ref, o_ref):
    o_ref[...] = jnp.dot(x_ref[...], w_ref[...], preferred_element_type=F32).astype(o_ref.dtype)


def _mm_plain(x, w, l):
    m, k = x.shape
    n = w.shape[-1]
    bm = _pick(m, 1024)
    w_spec = pl.BlockSpec((None, k, n), lambda i: (l, 0, 0))
    return pl.pallas_call(
        _mm_plain_kernel,
        grid=(m // bm,),
        in_specs=[pl.BlockSpec((bm, k), lambda i: (i, 0)), w_spec],
        out_specs=pl.BlockSpec((bm, n), lambda i: (i, 0)),
        out_shape=jax.ShapeDtypeStruct((m, n), BF16),
        compiler_params=_cparams(("parallel",)),
        name="mm_plain",
    )(x, w)


def _rope_store(acc, cos, sin, half, o_ref):
    for b in range(acc.shape[1] // LANES):
        o_ref[:, b * LANES:(b + 1) * LANES] = _rope_block(acc[:, b * LANES:(b + 1) * LANES], cos, sin, half).astype(o_ref.dtype)


def _mixer_proj_kernel(h_ref, g_ref, wa_ref, wv_ref, wg_ref, csa_ref, csg_ref, cos64_ref, sin64_ref,
                       cos128_ref, sin128_ref, qk_ref, vv_ref, gqk_ref, u_ref):
    u = _norm_rows(h_ref[...], g_ref[...])
    u_ref[...] = u
    qk = jnp.dot(u, wa_ref[...], preferred_element_type=F32) * csa_ref[...]
    _rope_store(qk, cos64_ref[...], sin64_ref[...], DIFF_QK_DIM // 2, qk_ref)
    vv_ref[...] = jnp.dot(u, wv_ref[...], preferred_element_type=F32).astype(vv_ref.dtype)
    gqk = jnp.dot(u, wg_ref[...], preferred_element_type=F32) * csg_ref[...]
    _rope_store(gqk, cos128_ref[...], sin128_ref[...], GQA_DIM // 2, gqk_ref)


def _mixer_proj(h, gain, w_a, w_v, w_g, l, cs_a, cs_g, cos64, sin64, cos128, sin128):
    m, k = h.shape
    na, nv, ng = w_a.shape[-1], w_v.shape[-1], w_g.shape[-1]
    bm = _pick(m, 512)
    once = dict(pipeline_mode=pl.Buffered(1))
    rows = lambda n: pl.BlockSpec((bm, n), lambda i: (i, 0))
    whole = lambda n: pl.BlockSpec((None, k, n), lambda i: (l, 0, 0), **once)
    return pl.pallas_call(
        _mixer_proj_kernel,
        grid=(m // bm,),
        in_specs=[rows(k), pl.BlockSpec((None, 1, k), lambda i: (l, 0, 0)), whole(na), whole(nv), whole(ng),
                  pl.BlockSpec((1, na), lambda i: (0, 0)), pl.BlockSpec((1, ng), lambda i: (0, 0)),
                  rows(LANES), rows(LANES), rows(LANES), rows(LANES)],
        out_specs=[rows(na), rows(nv), rows(ng), rows(k)],
        out_shape=[jax.ShapeDtypeStruct((m, n), BF16) for n in (na, nv, ng, k)],
        compiler_params=_cparams(("parallel",)),
        name="mixer_proj",
    )(h, gain, w_a, w_v, w_g, cs_a, cs_g, cos64, sin64, cos128, sin128)


def _mla_proj_kernel(u_ref, wc_ref, g_ref, wuq_ref, wukv_ref, cs_ref, cos_ref, sin_ref, q_ref, kv_ref, *, pattern):
    lat = jnp.dot(u_ref[...], wc_ref[...], preferred_element_type=F32)
    r = lat.shape[1] // 2
    cq = _norm_rows(lat[:, :r], g_ref[0])
    ckv = _norm_rows(lat[:, r:], g_ref[1])
    kv_ref[...] = jnp.dot(ckv, wukv_ref[...], preferred_element_type=F32).astype(kv_ref.dtype)
    q = jnp.dot(cq, wuq_ref[...], preferred_element_type=F32) * cs_ref[...]
    cos = cos_ref[...]
    sin = sin_ref[...]
    for b, roped in enumerate(pattern):
        blk = q[:, b * LANES:(b + 1) * LANES]
        if roped:
            blk = _rope_block(blk, cos, sin, MLA_ROPE // 2)
        q_ref[:, b * LANES:(b + 1) * LANES] = blk.astype(q_ref.dtype)


def _mla_proj(u, w_c, gains, wuq, wukv, l, colscale, cos, sin):
    m, k = u.shape
    nc = w_c.shape[-1]
    r = nc // 2
    nq, nkv = wuq.shape[-1], wukv.shape[-1]
    bm = _pick(m, 512)
    return pl.pallas_call(
        functools.partial(_mla_proj_kernel, pattern=(False, True) * MLA_HEADS),
        grid=(m // bm,),
        in_specs=[pl.BlockSpec((bm, k), lambda i: (i, 0)),
                  pl.BlockSpec((None, k, nc), lambda i: (l, 0, 0)),
                  pl.BlockSpec((None, 2, 1, r), lambda i: (l, 0, 0, 0)),
                  pl.BlockSpec((None, r, nq), lambda i: (l, 0, 0)),
                  pl.BlockSpec((None, r, nkv), lambda i: (l, 0, 0)),
                  pl.BlockSpec((1, nq), lambda i: (0, 0)),
                  pl.BlockSpec((bm, LANES), lambda i: (i, 0)),
                  pl.BlockSpec((bm, LANES), lambda i: (i, 0))],
        out_specs=[pl.BlockSpec((bm, nq), lambda i: (i, 0)), pl.BlockSpec((bm, nkv), lambda i: (i, 0))],
        out_shape=[jax.ShapeDtypeStruct((m, nq), BF16), jax.ShapeDtypeStruct((m, nkv), BF16)],
        compiler_params=_cparams(("parallel",)),
        name="mla_proj",
    )(u, w_c, gains, wuq, wukv, colscale, cos, sin)


def _logits(q, k, mlane):
    s = lax.dot_general(q, k, (((1,), (1,)), ((), ())), preferred_element_type=F32)
    for b in range(s.shape[1] // LANES):
        blk = s[:, b * LANES:(b + 1) * LANES]
        mlane = blk if mlane is None else jnp.maximum(mlane, blk)
    return s, mlane


def _row_max(mlane):
    return jnp.broadcast_to(jnp.max(mlane, axis=1, keepdims=True), mlane.shape)


def _weigh(s, m, v1, acc):
    ps = [jnp.exp2(s[:, b * LANES:(b + 1) * LANES] - m).astype(BF16) for b in range(s.shape[1] // LANES)]
    pv = jnp.dot(jnp.concatenate(ps, axis=1), v1, preferred_element_type=F32)
    return pv if acc is None else acc + pv


def _softmax_pv(qs, k_chunk, v1_chunk, nchunks):
    n = len(qs)
    logits = [[None] * nchunks for _ in range(n)]
    acc = [None] * n
    mlane = None
    for j in range(nchunks):
        logits[0][j], mlane = _logits(qs[0], k_chunk(j), mlane)
    m = _row_max(mlane)
    for p in range(1, n):
        mlane = None
        for j in range(nchunks):
            acc[p - 1] = _weigh(logits[p - 1][j], m, v1_chunk(j), acc[p - 1])
            logits[p][j], mlane = _logits(qs[p], k_chunk(j), mlane)
        m = _row_max(mlane)
    for j in range(nchunks):
        acc[n - 1] = _weigh(logits[n - 1][j], m, v1_chunk(j), acc[n - 1])
    return [a[:, :a.shape[1] - LANES] / a[:, a.shape[1] - LANES:] for a in acc]


def _attn_tiles(s):
    rows = _pick(s, 512)
    return rows, (2 if s % (2 * rows) == 0 else 1), _pick(s, 1024)


def _diff_attn_kernel(lp_ref, q_ref, k_ref, v_ref, g_ref, o_ref, *, lambda_init, tk, nsub):
    bq = q_ref.shape[0] // nsub
    ones = jnp.ones((tk, LANES), BF16)
    k_chunk = lambda j: k_ref[j * tk:(j + 1) * tk, :]
    v1_chunk = lambda j: jnp.concatenate([v_ref[j * tk:(j + 1) * tk, :], ones], axis=1)
    qs = []
    for r in range(nsub):
        q = q_ref[r * bq:(r + 1) * bq, :]
        lane = lax.broadcasted_iota(jnp.int32, q.shape, 1)
        zero = jnp.zeros_like(q)
        qs.append(jnp.concatenate([jnp.where(lane < DIFF_QK_DIM, q, zero),
                                   jnp.where(lane >= DIFF_QK_DIM, q, zero)], axis=0))
    outs = _softmax_pv(qs, k_chunk, v1_chunk, k_ref.shape[0] // tk)
    lp = lp_ref[...]
    lam = (jnp.exp(jnp.sum(lp[0:1] * lp[1:2], axis=1, keepdims=True))
           - jnp.exp(jnp.sum(lp[2:3] * lp[3:4], axis=1, keepdims=True)) + lambda_init)
    for r in range(nsub):
        o = outs[r][:bq] - lam * outs[r][bq:]
        y = o * lax.rsqrt(jnp.mean(o * o, axis=-1, keepdims=True) + DIFF_SUBLN_EPS)
        o_ref[r * bq:(r + 1) * bq, :] = (y * g_ref[...] * (1.0 - lambda_init)).astype(o_ref.dtype)


def _diff_attention(qk, v, diff_lambda, diff_subln, l, lambda_init):
    s = qk.shape[0]
    prob_rows, nsub, tk = _attn_tiles(s)
    rows = prob_rows // 2 * nsub
    h = DIFF_HEADS
    return pl.pallas_call(
        functools.partial(_diff_attn_kernel, lambda_init=lambda_init, tk=tk, nsub=nsub),
        grid=(h, s // rows),
        in_specs=[pl.BlockSpec((None, 4, DIFF_QK_DIM), lambda hh, i: (l, 0, 0)),
                  pl.BlockSpec((rows, LANES), lambda hh, i: (i, hh)),
                  pl.BlockSpec((s, LANES), lambda hh, i: (0, h + hh)),
                  pl.BlockSpec((s, LANES), lambda hh, i: (0, hh)),
                  pl.BlockSpec((None, 1, DIFF_V_DIM), lambda hh, i: (l, 0, 0))],
        out_specs=pl.BlockSpec((rows, LANES), lambda hh, i: (i, hh)),
        out_shape=jax.ShapeDtypeStruct((s, h * DIFF_V_DIM), BF16),
        compiler_params=_cparams(("parallel", "parallel")),
        name="diff_attention",
    )(diff_lambda, qk, qk, v, diff_subln)


def _mla_attn_kernel(q_ref, kn_ref, kr_ref, v_ref, o_ref, *, tk, nsub):
    bq = q_ref.shape[0] // nsub
    ones = jnp.ones((tk, LANES), BF16)
    k_chunk = lambda j: jnp.concatenate([kn_ref[j * tk:(j + 1) * tk, :], kr_ref[j * tk:(j + 1) * tk, :]], axis=1)
    v1_chunk = lambda j: jnp.concatenate([v_ref[j * tk:(j + 1) * tk, :], ones], axis=1)
    qs = [q_ref[r * bq:(r + 1) * bq, :] for r in range(nsub)]
    outs = _softmax_pv(qs, k_chunk, v1_chunk, kn_ref.shape[0] // tk)
    for r in range(nsub):
        o_ref[r * bq:(r + 1) * bq, :] = outs[r].astype(o_ref.dtype)


def _mla_attention(q, kv_up, qk):
    s = q.shape[0]
    bq, nsub, tk = _attn_tiles(s)
    rows = bq * nsub
    h = MLA_HEADS
    kr_blk = 2 * DIFF_HEADS
    return pl.pallas_call(
        functools.partial(_mla_attn_kernel, tk=tk, nsub=nsub),
        grid=(h, s // rows),
        in_specs=[pl.BlockSpec((rows, 2 * LANES), lambda hh, i: (i, hh)),
                  pl.BlockSpec((s, LANES), lambda hh, i: (0, 2 * hh)),
                  pl.BlockSpec((s, LANES), lambda hh, i: (0, kr_blk)),
                  pl.BlockSpec((s, LANES), lambda hh, i: (0, 2 * hh + 1))],
        out_specs=pl.BlockSpec((rows, LANES), lambda hh, i: (i, hh)),
        out_shape=jax.ShapeDtypeStruct((s, h * MLA_V), BF16),
        compiler_params=_cparams(("parallel", "parallel")),
        name="mla_attention",
    )(q, kv_up, qk, kv_up)


def _gqa_kernel(sink_ref, q_ref, kp_ref, kc_ref, kn_ref, vp_ref, vc_ref, vn_ref, o_ref, *, layer, seq, bq):
    g = pl.program_id(0)
    i = pl.program_id(1)
    nsub = q_ref.shape[0] // bq
    nk = bq + 2 * WINDOW
    band_k = jnp.concatenate([kp_ref[...], kc_ref[...], kn_ref[...]], axis=0)
    band_v = jnp.concatenate([vp_ref[...], vc_ref[...], vn_ref[...]], axis=0)
    band_v1 = jnp.concatenate([band_v, jnp.ones_like(band_v)], axis=1)
    for r in range(nsub):
        q0 = (i * nsub + r) * bq
        kb = band_k[r * bq:r * bq + nk]
        v1 = band_v1[r * bq:r * bq + nk]
        qpos = q0 + lax.broadcasted_iota(jnp.int32, (bq, nk), 0)
        kpos = q0 - WINDOW + lax.broadcasted_iota(jnp.int32, (bq, nk), 1)
        valid = (jnp.abs(kpos - qpos) <= WINDOW) & (kpos >= 0) & (kpos < seq)
        bias = jnp.where(valid, 0.0, NEG_INF).astype(F32)
        for hh in range(GQA_GROUP):
            q = q_ref[r * bq:(r + 1) * bq, hh * GQA_DIM:(hh + 1) * GQA_DIM]
            s = lax.dot_general(q, kb, (((1,), (1,)), ((), ())), preferred_element_type=F32) + bias
            sink = sink_ref[layer * GQA_HEADS + g * GQA_GROUP + hh] * LOG2E
            m = jnp.maximum(jnp.max(s, axis=1, keepdims=True), sink)
            pv = jnp.dot(jnp.exp2(s - m).astype(BF16), v1, preferred_element_type=F32)
            denom = pv[:, GQA_DIM:] + jnp.exp2(sink - m)
            o_ref[r * bq:(r + 1) * bq, hh * GQA_DIM:(hh + 1) * GQA_DIM] = (pv[:, :GQA_DIM] / denom).astype(o_ref.dtype)


def _window_gqa(gqk, vv, sink_flat, l):
    s = gqk.shape[0]
    sub = _pick(s, 256)
    bq = _pick(s, 4 * sub)
    assert sub % WINDOW == 0 and WINDOW == LANES
    nb = s // bq
    r = bq // WINDOW
    nh = s // WINDOW
    qw = GQA_GROUP * GQA_DIM
    kcol = GQA_HEADS
    vcol = DIFF_HEADS

    def prev(i):
        return jnp.maximum(i * r - 1, 0)

    def nxt(i):
        return jnp.minimum((i + 1) * r, nh - 1)

    return pl.pallas_call(
        functools.partial(_gqa_kernel, layer=l, seq=s, bq=sub),
        grid=(GQA_KV_HEADS, nb),
        in_specs=[pl.BlockSpec(memory_space=pltpu.SMEM),
                  pl.BlockSpec((bq, qw), lambda g, i: (i, g)),
                  pl.BlockSpec((WINDOW, LANES), lambda g, i: (prev(i), kcol + g)),
                  pl.BlockSpec((bq, LANES), lambda g, i: (i, kcol + g)),
                  pl.BlockSpec((WINDOW, LANES), lambda g, i: (nxt(i), kcol + g)),
                  pl.BlockSpec((WINDOW, LANES), lambda g, i: (prev(i), vcol + g)),
                  pl.BlockSpec((bq, LANES), lambda g, i: (i, vcol + g)),
                  pl.BlockSpec((WINDOW, LANES), lambda g, i: (nxt(i), vcol + g))],
        out_specs=pl.BlockSpec((bq, qw), lambda g, i: (i, g)),
        out_shape=jax.ShapeDtypeStruct((s, GQA_HEADS * GQA_DIM), BF16),
        compiler_params=_cparams(("parallel", "parallel")),
        name="window_gqa",
    )(sink_flat, gqk, gqk, gqk, gqk, vv, vv, vv)


def _merge_kernel(u_ref, o0_ref, o1_ref, o2_ref, wg0_ref, wg1_ref, wg2_ref, b0_ref, b1_ref, b2_ref,
                  wb0_ref, wb1_ref, wb2_ref, out_ref, wg0_s, wg1_s, wg2_s, wb0_s, wb1_s, wb2_s):
    _cast_weights_once(((wg0_ref, wg0_s), (wg1_ref, wg1_s), (wg2_ref, wg2_s),
                        (wb0_ref, wb0_s), (wb1_ref, wb1_s), (wb2_ref, wb2_s)))
    u = u_ref[...]
    acc = None
    for o_ref, wg_s, b_ref, wb_s in ((o0_ref, wg0_s, b0_ref, wb0_s),
                                     (o1_ref, wg1_s, b1_ref, wb1_s),
                                     (o2_ref, wg2_s, b2_ref, wb2_s)):
        gate = jax.nn.sigmoid(jnp.dot(u, wg_s[...], preferred_element_type=F32) + b_ref[...])
        term = jnp.dot(o_ref[...], wb_s[...], preferred_element_type=F32) * gate
        acc = term if acc is None else acc + term
    out_ref[...] = acc.astype(out_ref.dtype)


def _merge(u, o_diff, o_mla, o_gqa, w_gate, b_gate, w_branch, l):
    m, d = u.shape
    bw = o_diff.shape[1]
    bm = _pick(m, 1024)
    bn = _pick(d, 256)
    nj = d // bn
    o_spec = pl.BlockSpec((bm, bw), lambda j, i: (i, 0))

    def wg_spec(n):
        return pl.BlockSpec((None, d, bn), lambda j, i: (l, 0, n * nj + j))

    def b_spec(n):
        return pl.BlockSpec((None, 1, bn), lambda j, i: (l, 0, n * nj + j))

    def wb_spec(n):
        return pl.BlockSpec((None, None, bw, bn), lambda j, i: (l, n, 0, j))

    return pl.pallas_call(
        _merge_kernel,
        grid=(nj, m // bm),
        in_specs=[pl.BlockSpec((bm, d), lambda j, i: (i, 0)), o_spec, o_spec, o_spec,
                  wg_spec(0), wg_spec(1), wg_spec(2), b_spec(0), b_spec(1), b_spec(2),
                  wb_spec(0), wb_spec(1), wb_spec(2)],
        out_specs=pl.BlockSpec((bm, bn), lambda j, i: (i, j)),
        out_shape=jax.ShapeDtypeStruct((m, d), BF16),
        scratch_shapes=[pltpu.VMEM((d, bn), BF16)] * 3 + [pltpu.VMEM((bw, bn), BF16)] * 3,
        compiler_params=_cparams(("arbitrary", "arbitrary")),
        name="branch_merge",
    )(u, o_diff, o_mla, o_gqa, w_gate, w_gate, w_gate, b_gate, b_gate, b_gate,
      w_branch, w_branch, w_branch)


def _xattn_kernel(mg_ref, h_ref, wm_ref, g_ref, wq_ref, k_ref, v_ref, wo_ref, o_ref, ob_ref, wm_s, wq_s, wo_s):
    @pl.when(pl.program_id(0) == 0)
    def _():
        wm_s[...] = wm_ref[...].astype(BF16)
        wq_s[...] = wq_ref[...].astype(BF16)
        wo_s[...] = wo_ref[...].astype(BF16)

    h1 = h_ref[...] + jnp.dot(mg_ref[...], wm_s[...], preferred_element_type=F32)
    hn = _norm_rows(h1, g_ref[...])
    q = jnp.dot(hn, wq_s[...], preferred_element_type=F32) * (XA_DIM ** -0.5 * LOG2E)
    q = q.astype(BF16)
    heads = []
    for hh in range(XA_HEADS):
        sl = slice(hh * XA_DIM, (hh + 1) * XA_DIM)
        s = lax.dot_general(q[:, sl], k_ref[:, sl], (((1,), (1,)), ((), ())), preferred_element_type=F32)
        p = jnp.exp2(s - jnp.max(s, axis=1, keepdims=True))
        denom = jnp.sum(p, axis=1, keepdims=True)
        heads.append((jnp.dot(p.astype(BF16), v_ref[:, sl], preferred_element_type=F32) / denom).astype(BF16))
    o = jnp.concatenate(heads, axis=1)
    out = h1 + jnp.dot(o, wo_s[...], preferred_element_type=F32)
    o_ref[...] = out
    ob_ref[...] = out.astype(BF16)


def _cross_attention(merged, h, w_o, gain, xa_wq, memkv, xa_wo, l):
    m, d = h.shape
    w = XA_HEADS * XA_DIM
    ml = memkv.shape[0]
    bm = _pick(m, 256)
    once = dict(pipeline_mode=pl.Buffered(1))
    return pl.pallas_call(
        _xattn_kernel,
        grid=(m // bm,),
        in_specs=[pl.BlockSpec((bm, d), lambda i: (i, 0)),
                  pl.BlockSpec((bm, d), lambda i: (i, 0)),
                  pl.BlockSpec((None, d, d), lambda i: (l, 0, 0), **once),
                  pl.BlockSpec((None, 1, d), lambda i: (l, 0, 0)),
                  pl.BlockSpec((None, d, w), lambda i: (l, 0, 0), **once),
                  pl.BlockSpec((ml, w), lambda i: (0, 0)),
                  pl.BlockSpec((ml, w), lambda i: (0, 1)),
                  pl.BlockSpec((None, w, d), lambda i: (l, 0, 0), **once)],
        out_specs=[pl.BlockSpec((bm, d), lambda i: (i, 0))] * 2,
        out_shape=[jax.ShapeDtypeStruct((m, d), F32), jax.ShapeDtypeStruct((m, d), BF16)],
        scratch_shapes=[pltpu.VMEM((d, d), BF16), pltpu.VMEM((d, w), BF16), pltpu.VMEM((w, d), BF16)],
        compiler_params=_cparams(("arbitrary",)),
        name="cross_attention",
    )(merged, h, w_o, gain, xa_wq, memkv, memkv, xa_wo)


def _pad_cols(w, width):
    return jnp.pad(w, ((0, 0),) * (w.ndim - 1) + ((0, width - w.shape[-1]),))


_DQ, _DV, _CQ, _KR, _GQ, _GV, _END = 0, 2048, 3072, 4096, 4160, 5440, 5696


def _regroup_kernel(wt_ref, a_ref, v_ref, c_ref, g_ref):
    def cols(lo, hi):
        return wt_ref[lo:hi, :].T

    kr = jnp.concatenate([wt_ref[_KR:_GQ, :], jnp.zeros((LANES - (_GQ - _KR), wt_ref.shape[1]), wt_ref.dtype)], axis=0)
    a_ref[...] = jnp.concatenate([cols(_DQ, _DV), kr.T], axis=1).astype(BF16)
    v_ref[...] = jnp.concatenate([cols(_DV, _CQ), cols(_GV, _END)], axis=1).astype(BF16)
    c_ref[...] = cols(_CQ, _KR).astype(BF16)
    g_ref[...] = cols(_GQ, _GV).astype(BF16)


def _regroup_w_in(w_in):
    depth, k, n = w_in.shape
    assert n == _END
    bk = _pick(k, 256)
    widths = (_DV - _DQ + LANES, (_CQ - _DV) + (_END - _GV), _KR - _CQ, _GV - _GQ)
    return pl.pallas_call(
        _regroup_kernel,
        grid=(depth, k // bk),
        in_specs=[pl.BlockSpec((None, n, bk), lambda l, i: (l, 0, i))],
        out_specs=[pl.BlockSpec((None, bk, w), lambda l, i: (l, i, 0)) for w in widths],
        out_shape=[jax.ShapeDtypeStruct((depth, k, w), BF16) for w in widths],
        compiler_params=_cparams(("parallel", "parallel")),
        name="regroup_w_in",
    )(jnp.swapaxes(w_in, 1, 2))


def kernel(x, mem, positions, ffn1_norm, ffn1_w_gu, ffn1_w_down, mix_norm, w_in, diff_lambda, diff_subln,
           mla_q_norm, mla_kv_norm, mla_w_uq, mla_w_ukv, gqa_sink, w_branch, w_gate, b_gate, w_o, xa_norm,
           mem_norm, xa_wq, xa_wkv, xa_wo, ffn2_norm, ffn2_w_gu, ffn2_w_down, final_norm):
    b, s, d = x.shape
    assert b == 1
    depth = ffn1_norm.shape[0]
    h = x.reshape(s, d)
    mem2 = mem.reshape(mem.shape[1], d)

    pos_b = jnp.broadcast_to(positions.reshape(s, 1).astype(F32), (s, LANES))
    cos64, sin64 = _rope_tables(pos_b, DIFF_QK_DIM)
    cos128, sin128 = _rope_tables(pos_b, GQA_DIM)

    bf = lambda w: w.astype(BF16)
    xa_wkv_b = bf(xa_wkv)
    b_gate3 = b_gate.reshape(depth, 1, -1)
    norm3 = lambda g: g.reshape(depth, 1, -1)
    mix_norm3, xa_norm3, mem_norm3 = map(norm3, (mix_norm, xa_norm, mem_norm))
    ffn1_norm_col, ffn2_norm_col = ffn1_norm.reshape(depth, d, 1), ffn2_norm.reshape(depth, d, 1)
    hb = h
    diff_subln3 = diff_subln.reshape(depth, 1, -1)
    sink_flat = gqa_sink.reshape(-1)

    diff_scale = DIFF_QK_DIM ** -0.5 * LOG2E
    mla_scale = (MLA_NOPE + MLA_ROPE) ** -0.5 * LOG2E
    gqa_scale = GQA_DIM ** -0.5 * LOG2E
    cs_a = jnp.concatenate([jnp.full((1, 1024), diff_scale, F32), jnp.ones((1, 1024 + LANES), F32)], axis=1)
    cs_g = jnp.concatenate([jnp.full((1, 1024), gqa_scale, F32), jnp.ones((1, 256), F32)], axis=1)
    cs_q = jnp.full((1, MLA_HEADS * 2 * LANES), mla_scale, F32)

    w_a, w_v, w_c, w_g = _regroup_w_in(w_in)
    wuq = mla_w_uq.reshape(depth, MLA_Q_RANK, MLA_HEADS, MLA_NOPE + MLA_ROPE)
    wuq = _pad_cols(wuq, 2 * LANES).reshape(depth, MLA_Q_RANK, MLA_HEADS * 2 * LANES)
    wuq, wukv = bf(wuq), bf(mla_w_ukv)
    mla_gains = jnp.stack([mla_q_norm, mla_kv_norm], axis=1)[:, :, None, :]

    for l in range(depth):
        lambda_init = 0.8 - 0.6 * math.exp(-0.3 * l)

        a = _ffn_up(hb, ffn1_norm_col, ffn1_w_gu, l)
        h = _mm_residual(a, ffn1_w_down, l, h, 0.5)

        qk, vv, gqk, u = _mixer_proj(h, mix_norm3, w_a, w_v, w_g, l, cs_a, cs_g, cos64, sin64, cos128, sin128)

        o_diff = _diff_attention(qk, vv, diff_lambda, diff_subln3, l, lambda_init)

        q_mla, kv_up = _mla_proj(u, w_c, mla_gains, wuq, wukv, l, cs_q, cos64, sin64)
        o_mla = _mla_attention(q_mla, kv_up, qk)

        o_gqa = _window_gqa(gqk, vv, sink_flat, l)

        merged = _merge(u, o_diff, o_mla, o_gqa, w_gate, b_gate3, w_branch, l)
        memkv = _mm_plain(_rmsnorm(mem2, mem_norm3, l), xa_wkv_b, l)
        h, hb = _cross_attention(merged, h, w_o, xa_norm3, xa_wq, memkv, xa_wo, l)

        a = _ffn_up(hb, ffn2_norm_col, ffn2_w_gu, l)
        if l + 1 < depth:
            h, hb = _mm_residual(a, ffn2_w_down, l, h, 0.5, emit_bf16=True)
        else:
            h = _mm_residual(a, ffn2_w_down, l, h, 0.5)

    out = _rmsnorm(h, final_norm.reshape(1, d), None, out_dtype=F32)
    return out.reshape(b, s, d)
```

```python
import functools
import math

import jax
import jax.numpy as jnp
from jax import lax
from jax.experimental import pallas as pl
from jax.experimental.pallas import tpu as pltpu

F32 = jnp.float32
BF16 = jnp.bfloat16

DIFF_HEADS = 8
DIFF_QK_DIM = 64
DIFF_V_DIM = 128
MLA_HEADS = 8
MLA_Q_RANK = 512
MLA_NOPE = 128
MLA_ROPE = 64
MLA_V = 128
GQA_HEADS = 8
GQA_KV_HEADS = 2
GQA_GROUP = GQA_HEADS // GQA_KV_HEADS
GQA_DIM = 128
WINDOW = 128
XA_HEADS = 4
XA_DIM = 128
D_FF = 5632
ROPE_THETA = 10000.0
EPS = 1e-6
DIFF_SUBLN_EPS = 1e-5
NEG_INF = -1e30
LOG2E = math.log2(math.e)

LANES = 128
V7X_VMEM_LIMIT_BYTES = 56 * 1024 * 1024


def _cparams(semantics):
    return pltpu.CompilerParams(dimension_semantics=semantics, vmem_limit_bytes=V7X_VMEM_LIMIT_BYTES)


def _pick(dim, pref):
    if dim <= pref:
        return dim
    b = pref
    while dim % b:
        b //= 2
    return b


def _rope_table_kernel(pos_ref, inv_ref, sign_ref, cos_ref, sin_ref):
    ang = pos_ref[...] * inv_ref[...]
    cos_ref[...] = jnp.cos(ang)
    sin_ref[...] = jnp.sin(ang) * sign_ref[...]


def _rope_tables(pos_b, dim):
    s = pos_b.shape[0]
    half = dim // 2
    inv = ROPE_THETA ** (-jnp.arange(0, dim, 2, dtype=F32) / dim)
    inv_row = jnp.tile(inv, LANES // half)[None, :]
    sign_row = jnp.tile(jnp.concatenate([-jnp.ones((half,), F32), jnp.ones((half,), F32)]), LANES // dim)[None, :]
    bm = _pick(s, 1024)
    return pl.pallas_call(
        _rope_table_kernel,
        grid=(s // bm,),
        in_specs=[pl.BlockSpec((bm, LANES), lambda i: (i, 0)),
                  pl.BlockSpec((1, LANES), lambda i: (0, 0)),
                  pl.BlockSpec((1, LANES), lambda i: (0, 0))],
        out_specs=[pl.BlockSpec((bm, LANES), lambda i: (i, 0))] * 2,
        out_shape=[jax.ShapeDtypeStruct((s, LANES), F32)] * 2,
        compiler_params=_cparams(("parallel",)),
        name="rope_tables",
    )(pos_b, inv_row, sign_row)


def _rope_block(x, cos, sin_signed, half):
    if 2 * half == LANES:
        partner = pltpu.roll(x, half, axis=1)
    else:
        lane = lax.broadcasted_iota(jnp.int32, x.shape, 1)
        first = (lane % (2 * half)) < half
        partner = jnp.where(first, pltpu.roll(x, LANES - half, axis=1), pltpu.roll(x, half, axis=1))
    return x * cos + partner * sin_signed


def _rmsnorm_kernel(x_ref, g_ref, o_ref, *, eps):
    x = x_ref[...]
    y = x * lax.rsqrt(jnp.mean(x * x, axis=-1, keepdims=True) + eps)
    o_ref[...] = (y * g_ref[...]).astype(o_ref.dtype)


def _rmsnorm(x, g_stack, l, out_dtype=BF16):
    m, d = x.shape
    bm = _pick(m, 512)
    if l is None:
        g_spec = pl.BlockSpec((1, d), lambda i: (0, 0))
    else:
        g_spec = pl.BlockSpec((None, 1, d), lambda i: (l, 0, 0))
    return pl.pallas_call(
        functools.partial(_rmsnorm_kernel, eps=EPS),
        grid=(m // bm,),
        in_specs=[pl.BlockSpec((bm, d), lambda i: (i, 0)), g_spec],
        out_specs=pl.BlockSpec((bm, d), lambda i: (i, 0)),
        out_shape=jax.ShapeDtypeStruct((m, d), out_dtype),
        compiler_params=_cparams(("parallel",)),
        name="rmsnorm",
    )(x, g_stack)


def _cast_weights_once(pairs):
    @pl.when(pl.program_id(1) == 0)
    def _():
        for w_ref, w_s in pairs:
            w_s[...] = w_ref[...].astype(BF16)


def _norm_rows(x, g, eps=EPS):
    return (x * lax.rsqrt(jnp.mean(x * x, axis=-1, keepdims=True) + eps) * g).astype(BF16)


def _mm_swiglu_kernel(x_ref, gc_ref, wg_ref, wu_ref, o_ref, wg_s, wu_s, *, sub):
    @pl.when(pl.program_id(1) == 0)
    def _():
        gc = gc_ref[...]
        wg_s[...] = (wg_ref[...] * gc).astype(BF16)
        wu_s[...] = (wu_ref[...] * gc).astype(BF16)

    def rows_tile(t, carry):
        rows = pl.ds(pl.multiple_of(t * sub, sub), sub)
        x = x_ref[rows, :]
        xf = x.astype(F32)
        x = x.astype(BF16)
        rs = lax.rsqrt(jnp.mean(xf * xf, axis=-1, keepdims=True) + EPS)
        g = jnp.dot(x, wg_s[...], preferred_element_type=F32) * rs
        u = jnp.dot(x, wu_s[...], preferred_element_type=F32) * rs
        o_ref[rows, :] = (g * jax.nn.sigmoid(g) * u).astype(o_ref.dtype)
        return carry

    lax.fori_loop(0, x_ref.shape[0] // sub, rows_tile, 0)


def _ffn_up(hb, gain_col, w_gu, l):
    m, d = hb.shape
    n = w_gu.shape[-1] // 2
    bm = _pick(m, 2048 if hb.dtype == BF16 else 1024)
    bn = _pick(n, 512)
    nj = n // bn
    return pl.pallas_call(
        functools.partial(_mm_swiglu_kernel, sub=_pick(bm, 1024)),
        grid=(nj, m // bm),
        in_specs=[pl.BlockSpec((bm, d), lambda j, i: (i, 0)),
                  pl.BlockSpec((None, d, 1), lambda j, i: (l, 0, 0)),
                  pl.BlockSpec((None, d, bn), lambda j, i: (l, 0, j)),
                  pl.BlockSpec((None, d, bn), lambda j, i: (l, 0, j + nj))],
        out_specs=pl.BlockSpec((bm, bn), lambda j, i: (i, j)),
        out_shape=jax.ShapeDtypeStruct((m, n), BF16),
        scratch_shapes=[pltpu.VMEM((d, bn), BF16), pltpu.VMEM((d, bn), BF16)],
        compiler_params=_cparams(("arbitrary", "arbitrary")),
        name="ffn_up",
    )(hb, gain_col, w_gu, w_gu)


def _mm_residual_kernel(a_ref, w_ref, r_ref, o_ref, *rest, scale):
    w_s = rest[-1]
    _cast_weights_once(((w_ref, w_s),))
    out = r_ref[...] + scale * jnp.dot(a_ref[...], w_s[...], preferred_element_type=F32)
    o_ref[...] = out
    if len(rest) == 2:
        rest[0][...] = out.astype(BF16)


def _mm_residual(a, w, l, res, scale, emit_bf16=False):
    m, k = a.shape
    n = w.shape[-1]
    bm, bn = _pick(m, 256), _pick(n, 1024)
    o_spec = pl.BlockSpec((bm, bn), lambda j, i: (i, j))
    out_shape = jax.ShapeDtypeStruct((m, n), F32)
    return pl.pallas_call(
        functools.partial(_mm_residual_kernel, scale=scale),
        grid=(n // bn, m // bm),
        in_specs=[pl.BlockSpec((bm, k), lambda j, i: (i, 0)),
                  pl.BlockSpec((None, k, bn), lambda j, i: (l, 0, j), pipeline_mode=pl.Buffered(1)),
                  o_spec],
        out_specs=[o_spec, o_spec] if emit_bf16 else o_spec,
        out_shape=[out_shape, jax.ShapeDtypeStruct((m, n), BF16)] if emit_bf16 else out_shape,
        scratch_shapes=[pltpu.VMEM((k, bn), BF16)],
        compiler_params=_cparams(("arbitrary", "arbitrary")),
        name="mm_residual",
    )(a, w, res)


def _mem_kv_kernel(x_ref, g_ref, w_ref, o_ref):
    o_ref[...] = jnp.dot(_norm_rows(x_ref[...], g_ref[...]), w_ref[...], preferred_element_type=F32).astype(o_ref.dtype)


def _mem_kv(mem, gain, w):
    m, k = mem.shape
    depth, _, n = w.shape
    return pl.pallas_call(
        _mem_kv_kernel,
        grid=(depth,),
        in_specs=[pl.BlockSpec((m, k), lambda l: (0, 0)),
                  pl.BlockSpec((None, 1, k), lambda l: (l, 0, 0)),
                  pl.BlockSpec((None, k, n), lambda l: (l, 0, 0))],
        out_specs=pl.BlockSpec((None, m, n), lambda l: (l, 0, 0)),
        out_shape=jax.ShapeDtypeStruct((depth, m, n), BF16),
        compiler_params=_cparams(("parallel",)),
        name="mem_kv",
    )(mem, gain, w)


def _rope_store(acc, cos, sin, half, o_ref):
    for b in range(acc.shape[1] // LANES):
        o_ref[:, b * LANES:(b + 1) * LANES] = _rope_block(acc[:, b * LANES:(b + 1) * LANES], cos, sin, half).astype(o_ref.dtype)


def _mixer_proj_kernel(h_ref, g_ref, wa_ref, wv_ref, wg_ref, csa_ref, csg_ref, cos64_ref, sin64_ref,
                       cos128_ref, sin128_ref, qk_ref, vv_ref, gqk_ref, u_ref):
    u = _norm_rows(h_ref[...], g_ref[...])
    u_ref[...] = u
    qk = jnp.dot(u, wa_ref[...], preferred_element_type=F32) * csa_ref[...]
    _rope_store(qk, cos64_ref[...], sin64_ref[...], DIFF_QK_DIM // 2, qk_ref)
    vv_ref[...] = jnp.dot(u, wv_ref[...], preferred_element_type=F32).astype(vv_ref.dtype)
    gqk = jnp.dot(u, wg_ref[...], preferred_element_type=F32) * csg_ref[...]
    _rope_store(gqk, cos128_ref[...], sin128_ref[...], GQA_DIM // 2, gqk_ref)


def _mixer_proj(h, gain, w_a, w_v, w_g, l, cs_a, cs_g, cos64, sin64, cos128, sin128):
    m, k = h.shape
    na, nv, ng = w_a.shape[-1], w_v.shape[-1], w_g.shape[-1]
    bm = _pick(m, 512)
    once = dict(pipeline_mode=pl.Buffered(1))
    rows = lambda n: pl.BlockSpec((bm, n), lambda i: (i, 0))
    whole = lambda n: pl.BlockSpec((None, k, n), lambda i: (l, 0, 0), **once)
    return pl.pallas_call(
        _mixer_proj_kernel,
        grid=(m // bm,),
        in_specs=[rows(k), pl.BlockSpec((None, 1, k), lambda i: (l, 0, 0)), whole(na), whole(nv), whole(ng),
                  pl.BlockSpec((1, na), lambda i: (0, 0)), pl.BlockSpec((1, ng), lambda i: (0, 0)),
                  rows(LANES), rows(LANES), rows(LANES), rows(LANES)],
        out_specs=[rows(na), rows(nv), rows(ng), rows(k)],
        out_shape=[jax.ShapeDtypeStruct((m, n), BF16) for n in (na, nv, ng, k)],
        compiler_params=_cparams(("parallel",)),
        name="mixer_proj",
    )(h, gain, w_a, w_v, w_g, cs_a, cs_g, cos64, sin64, cos128, sin128)


def _mla_proj_kernel(u_ref, wc_ref, g_ref, wuq_ref, wukv_ref, cs_ref, cos_ref, sin_ref, q_ref, kv_ref, *, pattern):
    lat = jnp.dot(u_ref[...], wc_ref[...], preferred_element_type=F32)
    r = lat.shape[1] // 2
    cq = _norm_rows(lat[:, :r], g_ref[0])
    ckv = _norm_rows(lat[:, r:], g_ref[1])
    kv_ref[...] = jnp.dot(ckv, wukv_ref[...], preferred_element_type=F32).astype(kv_ref.dtype)
    q = jnp.dot(cq, wuq_ref[...], preferred_element_type=F32) * cs_ref[...]
    cos = cos_ref[...]
    sin = sin_ref[...]
    for b, roped in enumerate(pattern):
        blk = q[:, b * LANES:(b + 1) * LANES]
        if roped:
            blk = _rope_block(blk, cos, sin, MLA_ROPE // 2)
        q_ref[:, b * LANES:(b + 1) * LANES] = blk.astype(q_ref.dtype)


def _mla_proj(u, w_c, gains, wuq, wukv, l, colscale, cos, sin):
    m, k = u.shape
    nc = w_c.shape[-1]
    r = nc // 2
    nq, nkv = wuq.shape[-1], wukv.shape[-1]
    bm = _pick(m, 512)
    return pl.pallas_call(
        functools.partial(_mla_proj_kernel, pattern=(False, True) * MLA_HEADS),
        grid=(m // bm,),
        in_specs=[pl.BlockSpec((bm, k), lambda i: (i, 0)),
                  pl.BlockSpec((None, k, nc), lambda i: (l, 0, 0)),
                  pl.BlockSpec((None, 2, 1, r), lambda i: (l, 0, 0, 0)),
                  pl.BlockSpec((None, r, nq), lambda i: (l, 0, 0)),
                  pl.BlockSpec((None, r, nkv), lambda i: (l, 0, 0)),
                  pl.BlockSpec((1, nq), lambda i: (0, 0)),
                  pl.BlockSpec((bm, LANES), lambda i: (i, 0)),
                  pl.BlockSpec((bm, LANES), lambda i: (i, 0))],
        out_specs=[pl.BlockSpec((bm, nq), lambda i: (i, 0)), pl.BlockSpec((bm, nkv), lambda i: (i, 0))],
        out_shape=[jax.ShapeDtypeStruct((m, nq), BF16), jax.ShapeDtypeStruct((m, nkv), BF16)],
        compiler_params=_cparams(("parallel",)),
        name="mla_proj",
    )(u, w_c, gains, wuq, wukv, colscale, cos, sin)


def _logits(q, k, mlane):
    s = lax.dot_general(q, k, (((1,), (1,)), ((), ())), preferred_element_type=F32)
    for b in range(s.shape[1] // LANES):
        blk = s[:, b * LANES:(b + 1) * LANES]
        mlane = blk if mlane is None else jnp.maximum(mlane, blk)
    return s, mlane


def _row_max(mlane):
    return jnp.broadcast_to(jnp.max(mlane, axis=1, keepdims=True), mlane.shape)


def _weigh(s, m, v1, acc):
    ps = [jnp.exp2(s[:, b * LANES:(b + 1) * LANES] - m).astype(BF16) for b in range(s.shape[1] // LANES)]
    pv = jnp.dot(jnp.concatenate(ps, axis=1), v1, preferred_element_type=F32)
    return pv if acc is None else acc + pv


def _softmax_pv(qs, k_chunk, v1_chunk, nchunks):
    n = len(qs)
    logits = [[None] * nchunks for _ in range(n)]
    acc = [None] * n
    mlane = None
    for j in range(nchunks):
        logits[0][j], mlane = _logits(qs[0], k_chunk(j), mlane)
    m = _row_max(mlane)
    for p in range(1, n):
        mlane = None
        for j in range(nchunks):
            acc[p - 1] = _weigh(logits[p - 1][j], m, v1_chunk(j), acc[p - 1])
            logits[p][j], mlane = _logits(qs[p], k_chunk(j), mlane)
        m = _row_max(mlane)
    for j in range(nchunks):
        acc[n - 1] = _weigh(logits[n - 1][j], m, v1_chunk(j), acc[n - 1])
    return [a[:, :a.shape[1] - LANES] / a[:, a.shape[1] - LANES:] for a in acc]


def _attn_tiles(s):
    rows = _pick(s, 512)
    return rows, (2 if s % (2 * rows) == 0 else 1), _pick(s, 1024)


def _diff_attn_kernel(lp_ref, q_ref, k_ref, v_ref, g_ref, o_ref, *, lambda_init, tk, nsub):
    bq = q_ref.shape[0] // nsub
    ones = jnp.ones((tk, LANES), BF16)
    k_chunk = lambda j: k_ref[j * tk:(j + 1) * tk, :]
    v1_chunk = lambda j: jnp.concatenate([v_ref[j * tk:(j + 1) * tk, :], ones], axis=1)
    qs = []
    for r in range(nsub):
        q = q_ref[r * bq:(r + 1) * bq, :]
        lane = lax.broadcasted_iota(jnp.int32, q.shape, 1)
        zero = jnp.zeros_like(q)
        qs.append(jnp.concatenate([jnp.where(lane < DIFF_QK_DIM, q, zero),
                                   jnp.where(lane >= DIFF_QK_DIM, q, zero)], axis=0))
    outs = _softmax_pv(qs, k_chunk, v1_chunk, k_ref.shape[0] // tk)
    lp = lp_ref[...]
    lam = (jnp.exp(jnp.sum(lp[0:1] * lp[1:2], axis=1, keepdims=True))
           - jnp.exp(jnp.sum(lp[2:3] * lp[3:4], axis=1, keepdims=True)) + lambda_init)
    for r in range(nsub):
        o = outs[r][:bq] - lam * outs[r][bq:]
        y = o * lax.rsqrt(jnp.mean(o * o, axis=-1, keepdims=True) + DIFF_SUBLN_EPS)
        o_ref[r * bq:(r + 1) * bq, :] = (y * g_ref[...] * (1.0 - lambda_init)).astype(o_ref.dtype)


def _diff_attention(qk, v, diff_lambda, diff_subln, l, lambda_init):
    s = qk.shape[0]
    prob_rows, nsub, tk = _attn_tiles(s)
    rows = prob_rows // 2 * nsub
    h = DIFF_HEADS
    return pl.pallas_call(
        functools.partial(_diff_attn_kernel, lambda_init=lambda_init, tk=tk, nsub=nsub),
        grid=(h, s // rows),
        in_specs=[pl.BlockSpec((None, 4, DIFF_QK_DIM), lambda hh, i: (l, 0, 0)),
                  pl.BlockSpec((rows, LANES), lambda hh, i: (i, hh)),
                  pl.BlockSpec((s, LANES), lambda hh, i: (0, h + hh)),
                  pl.BlockSpec((s, LANES), lambda hh, i: (0, hh)),
                  pl.BlockSpec((None, 1, DIFF_V_DIM), lambda hh, i: (l, 0, 0))],
        out_specs=pl.BlockSpec((rows, LANES), lambda hh, i: (i, hh)),
        out_shape=jax.ShapeDtypeStruct((s, h * DIFF_V_DIM), BF16),
        compiler_params=_cparams(("parallel", "parallel")),
        name="diff_attention",
    )(diff_lambda, qk, qk, v, diff_subln)


def _mla_attn_kernel(q_ref, kn_ref, kr_ref, v_ref, o_ref, *, tk, nsub):
    bq = q_ref.shape[0] // nsub
    ones = jnp.ones((tk, LANES), BF16)
    k_chunk = lambda j: jnp.concatenate([kn_ref[j * tk:(j + 1) * tk, :], kr_ref[j * tk:(j + 1) * tk, :]], axis=1)
    v1_chunk = lambda j: jnp.concatenate([v_ref[j * tk:(j + 1) * tk, :], ones], axis=1)
    qs = [q_ref[r * bq:(r + 1) * bq, :] for r in range(nsub)]
    outs = _softmax_pv(qs, k_chunk, v1_chunk, kn_ref.shape[0] // tk)
    for r in range(nsub):
        o_ref[r * bq:(r + 1) * bq, :] = outs[r].astype(o_ref.dtype)


def _mla_attention(q, kv_up, qk):
    s = q.shape[0]
    bq, nsub, tk = _attn_tiles(s)
    rows = bq * nsub
    h = MLA_HEADS
    kr_blk = 2 * DIFF_HEADS
    return pl.pallas_call(
        functools.partial(_mla_attn_kernel, tk=tk, nsub=nsub),
        grid=(h, s // rows),
        in_specs=[pl.BlockSpec((rows, 2 * LANES), lambda hh, i: (i, hh)),
                  pl.BlockSpec((s, LANES), lambda hh, i: (0, 2 * hh)),
                  pl.BlockSpec((s, LANES), lambda hh, i: (0, kr_blk)),
                  pl.BlockSpec((s, LANES), lambda hh, i: (0, 2 * hh + 1))],
        out_specs=pl.BlockSpec((rows, LANES), lambda hh, i: (i, hh)),
        out_shape=jax.ShapeDtypeStruct((s, h * MLA_V), BF16),
        compiler_params=_cparams(("parallel", "parallel")),
        name="mla_attention",
    )(q, kv_up, qk, kv_up)


def _gqa_kernel(sink_ref, q_ref, kp_ref, kc_ref, kn_ref, vp_ref, vc_ref, vn_ref, o_ref, *, layer, seq, bq):
    g = pl.program_id(0)
    i = pl.program_id(1)
    nsub = q_ref.shape[0] // bq
    nk = bq + 2 * WINDOW
    band_k = jnp.concatenate([kp_ref[...], kc_ref[...], kn_ref[...]], axis=0)
    band_v = jnp.concatenate([vp_ref[...], vc_ref[...], vn_ref[...]], axis=0)
    band_v1 = jnp.concatenate([band_v, jnp.ones_like(band_v)], axis=1)
    for r in range(nsub):
        q0 = (i * nsub + r) * bq
        kb = band_k[r * bq:r * bq + nk]
        v1 = band_v1[r * bq:r * bq + nk]
        qpos = q0 + lax.broadcasted_iota(jnp.int32, (bq, nk), 0)
        kpos = q0 - WINDOW + lax.broadcasted_iota(jnp.int32, (bq, nk), 1)
        valid = (jnp.abs(kpos - qpos) <= WINDOW) & (kpos >= 0) & (kpos < seq)
        bias = jnp.where(valid, 0.0, NEG_INF).astype(F32)
        for hh in range(GQA_GROUP):
            q = q_ref[r * bq:(r + 1) * bq, hh * GQA_DIM:(hh + 1) * GQA_DIM]
            s = lax.dot_general(q, kb, (((1,), (1,)), ((), ())), preferred_element_type=F32) + bias
            sink = sink_ref[layer * GQA_HEADS + g * GQA_GROUP + hh] * LOG2E
            m = jnp.maximum(jnp.max(s, axis=1, keepdims=True), sink)
            pv = jnp.dot(jnp.exp2(s - m).astype(BF16), v1, preferred_element_type=F32)
            denom = pv[:, GQA_DIM:] + jnp.exp2(sink - m)
            o_ref[r * bq:(r + 1) * bq, hh * GQA_DIM:(hh + 1) * GQA_DIM] = (pv[:, :GQA_DIM] / denom).astype(o_ref.dtype)


def _window_gqa(gqk, vv, sink_flat, l):
    s = gqk.shape[0]
    sub = _pick(s, 256)
    bq = _pick(s, 4 * sub)
    assert sub % WINDOW == 0 and WINDOW == LANES
    nb = s // bq
    r = bq // WINDOW
    nh = s // WINDOW
    qw = GQA_GROUP * GQA_DIM
    kcol = GQA_HEADS
    vcol = DIFF_HEADS

    def prev(i):
        return jnp.maximum(i * r - 1, 0)

    def nxt(i):
        return jnp.minimum((i + 1) * r, nh - 1)

    return pl.pallas_call(
        functools.partial(_gqa_kernel, layer=l, seq=s, bq=sub),
        grid=(GQA_KV_HEADS, nb),
        in_specs=[pl.BlockSpec(memory_space=pltpu.SMEM),
                  pl.BlockSpec((bq, qw), lambda g, i: (i, g)),
                  pl.BlockSpec((WINDOW, LANES), lambda g, i: (prev(i), kcol + g)),
                  pl.BlockSpec((bq, LANES), lambda g, i: (i, kcol + g)),
                  pl.BlockSpec((WINDOW, LANES), lambda g, i: (nxt(i), kcol + g)),
                  pl.BlockSpec((WINDOW, LANES), lambda g, i: (prev(i), vcol + g)),
                  pl.BlockSpec((bq, LANES), lambda g, i: (i, vcol + g)),
                  pl.BlockSpec((WINDOW, LANES), lambda g, i: (nxt(i), vcol + g))],
        out_specs=pl.BlockSpec((bq, qw), lambda g, i: (i, g)),
        out_shape=jax.ShapeDtypeStruct((s, GQA_HEADS * GQA_DIM), BF16),
        compiler_params=_cparams(("parallel", "parallel")),
        name="window_gqa",
    )(sink_flat, gqk, gqk, gqk, gqk, vv, vv, vv)


def _merge_kernel(u_ref, o0_ref, o1_ref, o2_ref, wg0_ref, wg1_ref, wg2_ref, b0_ref, b1_ref, b2_ref,
                  wb0_ref, wb1_ref, wb2_ref, out_ref, wg0_s, wg1_s, wg2_s, wb0_s, wb1_s, wb2_s):
    _cast_weights_once(((wg0_ref, wg0_s), (wg1_ref, wg1_s), (wg2_ref, wg2_s),
                        (wb0_ref, wb0_s), (wb1_ref, wb1_s), (wb2_ref, wb2_s)))
    u = u_ref[...]
    acc = None
    for o_ref, wg_s, b_ref, wb_s in ((o0_ref, wg0_s, b0_ref, wb0_s),
                                     (o1_ref, wg1_s, b1_ref, wb1_s),
                                     (o2_ref, wg2_s, b2_ref, wb2_s)):
        gate = jax.nn.sigmoid(jnp.dot(u, wg_s[...], preferred_element_type=F32) + b_ref[...])
        term = jnp.dot(o_ref[...], wb_s[...], preferred_element_type=F32) * gate
        acc = term if acc is None else acc + term
    out_ref[...] = acc.astype(out_ref.dtype)


def _merge(u, o_diff, o_mla, o_gqa, w_gate, b_gate, w_branch, l):
    m, d = u.shape
    bw = o_diff.shape[1]
    bm = _pick(m, 1024)
    bn = _pick(d, 256)
    nj = d // bn
    o_spec = pl.BlockSpec((bm, bw), lambda j, i: (i, 0))

    def wg_spec(n):
        return pl.BlockSpec((None, d, bn), lambda j, i: (l, 0, n * nj + j))

    def b_spec(n):
        return pl.BlockSpec((None, 1, bn), lambda j, i: (l, 0, n * nj + j))

    def wb_spec(n):
        return pl.BlockSpec((None, None, bw, bn), lambda j, i: (l, n, 0, j))

    return pl.pallas_call(
        _merge_kernel,
        grid=(nj, m // bm),
        in_specs=[pl.BlockSpec((bm, d), lambda j, i: (i, 0)), o_spec, o_spec, o_spec,
                  wg_spec(0), wg_spec(1), wg_spec(2), b_spec(0), b_spec(1), b_spec(2),
                  wb_spec(0), wb_spec(1), wb_spec(2)],
        out_specs=pl.BlockSpec((bm, bn), lambda j, i: (i, j)),
        out_shape=jax.ShapeDtypeStruct((m, d), BF16),
        scratch_shapes=[pltpu.VMEM((d, bn), BF16)] * 3 + [pltpu.VMEM((bw, bn), BF16)] * 3,
        compiler_params=_cparams(("arbitrary", "arbitrary")),
        name="branch_merge",
    )(u, o_diff, o_mla, o_gqa, w_gate, w_gate, w_gate, b_gate, b_gate, b_gate,
      w_branch, w_branch, w_branch)


def _xattn_kernel(mg_ref, h_ref, wm_ref, g_ref, wq_ref, k_ref, v_ref, wo_ref, o_ref, ob_ref, wq_s, wo_s):
    @pl.when(pl.program_id(0) == 0)
    def _():
        wq_s[...] = wq_ref[...].astype(BF16)
        wo_s[...] = wo_ref[...].astype(BF16)

    h1 = h_ref[...] + jnp.dot(mg_ref[...], wm_ref[...], preferred_element_type=F32)
    hn = _norm_rows(h1, g_ref[...])
    q = jnp.dot(hn, wq_s[...], preferred_element_type=F32) * (XA_DIM ** -0.5 * LOG2E)
    q = q.astype(BF16)
    heads = []
    for hh in range(XA_HEADS):
        sl = slice(hh * XA_DIM, (hh + 1) * XA_DIM)
        s = lax.dot_general(q[:, sl], k_ref[:, sl], (((1,), (1,)), ((), ())), preferred_element_type=F32)
        p = jnp.exp2(s - jnp.max(s, axis=1, keepdims=True))
        denom = jnp.sum(p, axis=1, keepdims=True)
        heads.append((jnp.dot(p.astype(BF16), v_ref[:, sl], preferred_element_type=F32) / denom).astype(BF16))
    o = jnp.concatenate(heads, axis=1)
    out = h1 + jnp.dot(o, wo_s[...], preferred_element_type=F32)
    o_ref[...] = out
    ob_ref[...] = out.astype(BF16)


def _cross_attention(merged, h, w_o, gain, xa_wq, memkv, xa_wo, l):
    m, d = h.shape
    w = XA_HEADS * XA_DIM
    ml = memkv.shape[1]
    bm = _pick(m, 512)
    once = dict(pipeline_mode=pl.Buffered(1))
    return pl.pallas_call(
        _xattn_kernel,
        grid=(m // bm,),
        in_specs=[pl.BlockSpec((bm, d), lambda i: (i, 0)),
                  pl.BlockSpec((bm, d), lambda i: (i, 0)),
                  pl.BlockSpec((None, d, d), lambda i: (l, 0, 0), **once),
                  pl.BlockSpec((None, 1, d), lambda i: (l, 0, 0)),
                  pl.BlockSpec((None, d, w), lambda i: (l, 0, 0), **once),
                  pl.BlockSpec((None, ml, w), lambda i: (l, 0, 0)),
                  pl.BlockSpec((None, ml, w), lambda i: (l, 0, 1)),
                  pl.BlockSpec((None, w, d), lambda i: (l, 0, 0), **once)],
        out_specs=[pl.BlockSpec((bm, d), lambda i: (i, 0))] * 2,
        out_shape=[jax.ShapeDtypeStruct((m, d), F32), jax.ShapeDtypeStruct((m, d), BF16)],
        scratch_shapes=[pltpu.VMEM((d, w), BF16), pltpu.VMEM((w, d), BF16)],
        compiler_params=_cparams(("arbitrary",)),
        name="cross_attention",
    )(merged, h, w_o, gain, xa_wq, memkv, memkv, xa_wo)


def _pad_cols(w, width):
    return jnp.pad(w, ((0, 0),) * (w.ndim - 1) + ((0, width - w.shape[-1]),))


_DQ, _DV, _CQ, _KR, _GQ, _GV, _END = 0, 2048, 3072, 4096, 4160, 5440, 5696


def _regroup_kernel(wt_ref, a_ref, v_ref, c_ref, g_ref):
    def cols(lo, hi):
        return wt_ref[lo:hi, :].T

    kr = jnp.concatenate([wt_ref[_KR:_GQ, :], jnp.zeros((LANES - (_GQ - _KR), wt_ref.shape[1]), wt_ref.dtype)], axis=0)
    a_ref[...] = jnp.concatenate([cols(_DQ, _DV), kr.T], axis=1).astype(BF16)
    v_ref[...] = jnp.concatenate([cols(_DV, _CQ), cols(_GV, _END)], axis=1).astype(BF16)
    c_ref[...] = cols(_CQ, _KR).astype(BF16)
    g_ref[...] = cols(_GQ, _GV).astype(BF16)


def _regroup_w_in(w_in):
    depth, k, n = w_in.shape
    assert n == _END
    bk = _pick(k, 256)
    widths = (_DV - _DQ + LANES, (_CQ - _DV) + (_END - _GV), _KR - _CQ, _GV - _GQ)
    return pl.pallas_call(
        _regroup_kernel,
        grid=(depth, k // bk),
        in_specs=[pl.BlockSpec((None, n, bk), lambda l, i: (l, 0, i))],
        out_specs=[pl.BlockSpec((None, bk, w), lambda l, i: (l, i, 0)) for w in widths],
        out_shape=[jax.ShapeDtypeStruct((depth, k, w), BF16) for w in widths],
        compiler_params=_cparams(("parallel", "parallel")),
        name="regroup_w_in",
    )(jnp.swapaxes(w_in, 1, 2))


def kernel(x, mem, positions, ffn1_norm, ffn1_w_gu, ffn1_w_down, mix_norm, w_in, diff_lambda, diff_subln,
           mla_q_norm, mla_kv_norm, mla_w_uq, mla_w_ukv, gqa_sink, w_branch, w_gate, b_gate, w_o, xa_norm,
           mem_norm, xa_wq, xa_wkv, xa_wo, ffn2_norm, ffn2_w_gu, ffn2_w_down, final_norm):
    b, s, d = x.shape
    assert b == 1
    depth = ffn1_norm.shape[0]
    h = x.reshape(s, d)
    mem2 = mem.reshape(mem.shape[1], d)

    pos_b = jnp.broadcast_to(positions.reshape(s, 1).astype(F32), (s, LANES))
    cos64, sin64 = _rope_tables(pos_b, DIFF_QK_DIM)
    cos128, sin128 = _rope_tables(pos_b, GQA_DIM)

    bf = lambda w: w.astype(BF16)
    w_o_b = bf(w_o)
    b_gate3 = b_gate.reshape(depth, 1, -1)
    norm3 = lambda g: g.reshape(depth, 1, -1)
    mix_norm3, xa_norm3, mem_norm3 = map(norm3, (mix_norm, xa_norm, mem_norm))
    memkv = _mem_kv(mem2, mem_norm3, bf(xa_wkv))
    ffn1_norm_col, ffn2_norm_col = ffn1_norm.reshape(depth, d, 1), ffn2_norm.reshape(depth, d, 1)
    hb = h
    diff_subln3 = diff_subln.reshape(depth, 1, -1)
    sink_flat = gqa_sink.reshape(-1)

    diff_scale = DIFF_QK_DIM ** -0.5 * LOG2E
    mla_scale = (MLA_NOPE + MLA_ROPE) ** -0.5 * LOG2E
    gqa_scale = GQA_DIM ** -0.5 * LOG2E
    cs_a = jnp.concatenate([jnp.full((1, 1024), diff_scale, F32), jnp.ones((1, 1024 + LANES), F32)], axis=1)
    cs_g = jnp.concatenate([jnp.full((1, 1024), gqa_scale, F32), jnp.ones((1, 256), F32)], axis=1)
    cs_q = jnp.full((1, MLA_HEADS * 2 * LANES), mla_scale, F32)

    w_a, w_v, w_c, w_g = _regroup_w_in(w_in)
    wuq = mla_w_uq.reshape(depth, MLA_Q_RANK, MLA_HEADS, MLA_NOPE + MLA_ROPE)
    wuq = _pad_cols(wuq, 2 * LANES).reshape(depth, MLA_Q_RANK, MLA_HEADS * 2 * LANES)
    wuq, wukv = bf(wuq), bf(mla_w_ukv)
    mla_gains = jnp.stack([mla_q_norm, mla_kv_norm], axis=1)[:, :, None, :]

    for l in range(depth):
        lambda_init = 0.8 - 0.6 * math.exp(-0.3 * l)

        a = _ffn_up(hb, ffn1_norm_col, ffn1_w_gu, l)
        h = _mm_residual(a, ffn1_w_down, l, h, 0.5)

        qk, vv, gqk, u = _mixer_proj(h, mix_norm3, w_a, w_v, w_g, l, cs_a, cs_g, cos64, sin64, cos128, sin128)

        o_diff = _diff_attention(qk, vv, diff_lambda, diff_subln3, l, lambda_init)

        q_mla, kv_up = _mla_proj(u, w_c, mla_gains, wuq, wukv, l, cs_q, cos64, sin64)
        o_mla = _mla_attention(q_mla, kv_up, qk)

        o_gqa = _window_gqa(gqk, vv, sink_flat, l)

        merged = _merge(u, o_diff, o_mla, o_gqa, w_gate, b_gate3, w_branch, l)
        h, hb = _cross_attention(merged, h, w_o_b, xa_norm3, xa_wq, memkv, xa_wo, l)

        a = _ffn_up(hb, ffn2_norm_col, ffn2_w_gu, l)
        if l + 1 < depth:
            h, hb = _mm_residual(a, ffn2_w_down, l, h, 0.5, emit_bf16=True)
        else:
            h = _mm_residual(a, ffn2_w_down, l, h, 0.5)

    out = _rmsnorm(h, final_norm.reshape(1, d), None, out_dtype=F32)
    return out.reshape(b, s, d)
```

```python
import functools
import math

import jax
import jax.numpy as jnp
from jax import lax
from jax.experimental import pallas as pl
from jax.experimental.pallas import tpu as pltpu

F32 = jnp.float32
BF16 = jnp.bfloat16

DIFF_HEADS = 8
DIFF_QK_DIM = 64
DIFF_V_DIM = 128
MLA_HEADS = 8
MLA_Q_RANK = 512
MLA_NOPE = 128
MLA_ROPE = 64
MLA_V = 128
GQA_HEADS = 8
GQA_KV_HEADS = 2
GQA_GROUP = GQA_HEADS // GQA_KV_HEADS
GQA_DIM = 128
WINDOW = 128
XA_HEADS = 4
XA_DIM = 128
D_FF = 5632
ROPE_THETA = 10000.0
EPS = 1e-6
DIFF_SUBLN_EPS = 1e-5
NEG_INF = -1e30
LOG2E = math.log2(math.e)

LANES = 128
V7X_VMEM_LIMIT_BYTES = 56 * 1024 * 1024


def _cparams(semantics):
    return pltpu.CompilerParams(dimension_semantics=semantics, vmem_limit_bytes=V7X_VMEM_LIMIT_BYTES)


def _pick(dim, pref):
    if dim <= pref:
        return dim
    b = pref
    while dim % b:
        b //= 2
    return b


def _rope_table_kernel(pos_ref, inv_ref, sign_ref, cos_ref, sin_ref):
    ang = pos_ref[...] * inv_ref[...]
    cos_ref[...] = jnp.cos(ang)
    sin_ref[...] = jnp.sin(ang) * sign_ref[...]


def _rope_tables(pos_b, dim):
    s = pos_b.shape[0]
    half = dim // 2
    inv = ROPE_THETA ** (-jnp.arange(0, dim, 2, dtype=F32) / dim)
    inv_row = jnp.tile(inv, LANES // half)[None, :]
    sign_row = jnp.tile(jnp.concatenate([-jnp.ones((half,), F32), jnp.ones((half,), F32)]), LANES // dim)[None, :]
    bm = _pick(s, 1024)
    return pl.pallas_call(
        _rope_table_kernel,
        grid=(s // bm,),
        in_specs=[pl.BlockSpec((bm, LANES), lambda i: (i, 0)),
                  pl.BlockSpec((1, LANES), lambda i: (0, 0)),
                  pl.BlockSpec((1, LANES), lambda i: (0, 0))],
        out_specs=[pl.BlockSpec((bm, LANES), lambda i: (i, 0))] * 2,
        out_shape=[jax.ShapeDtypeStruct((s, LANES), F32)] * 2,
        compiler_params=_cparams(("parallel",)),
        name="rope_tables",
    )(pos_b, inv_row, sign_row)


def _rope_block(x, cos, sin_signed, half):
    if 2 * half == LANES:
        partner = pltpu.roll(x, half, axis=1)
    else:
        lane = lax.broadcasted_iota(jnp.int32, x.shape, 1)
        first = (lane % (2 * half)) < half
        partner = jnp.where(first, pltpu.roll(x, LANES - half, axis=1), pltpu.roll(x, half, axis=1))
    return x * cos + partner * sin_signed


def _rmsnorm_kernel(x_ref, g_ref, o_ref, *, eps):
    x = x_ref[...]
    y = x * lax.rsqrt(jnp.mean(x * x, axis=-1, keepdims=True) + eps)
    o_ref[...] = (y * g_ref[...]).astype(o_ref.dtype)


def _rmsnorm(x, gain, out_dtype):
    m, d = x.shape
    bm = _pick(m, 512)
    return pl.pallas_call(
        functools.partial(_rmsnorm_kernel, eps=EPS),
        grid=(m // bm,),
        in_specs=[pl.BlockSpec((bm, d), lambda i: (i, 0)), pl.BlockSpec((1, d), lambda i: (0, 0))],
        out_specs=pl.BlockSpec((bm, d), lambda i: (i, 0)),
        out_shape=jax.ShapeDtypeStruct((m, d), out_dtype),
        compiler_params=_cparams(("parallel",)),
        name="rmsnorm",
    )(x, gain)


def _cast_weights_once(pairs):
    @pl.when(pl.program_id(1) == 0)
    def _():
        for w_ref, w_s in pairs:
            w_s[...] = w_ref[...].astype(BF16)


def _norm_rows(x, g, eps=EPS):
    return (x * lax.rsqrt(jnp.mean(x * x, axis=-1, keepdims=True) + eps) * g).astype(BF16)


def _mm_swiglu_kernel(x_ref, gc_ref, wg_ref, wu_ref, o_ref, wg_s, wu_s, *, sub):
    @pl.when(pl.program_id(1) == 0)
    def _():
        gc = gc_ref[...]
        wg_s[...] = (wg_ref[...] * gc).astype(BF16)
        wu_s[...] = (wu_ref[...] * gc).astype(BF16)

    def rows_tile(t, carry):
        rows = pl.ds(pl.multiple_of(t * sub, sub), sub)
        x = x_ref[rows, :]
        xf = x.astype(F32)
        x = x.astype(BF16)
        rs = lax.rsqrt(jnp.mean(xf * xf, axis=-1, keepdims=True) + EPS)
        g = jnp.dot(x, wg_s[...], preferred_element_type=F32) * rs
        u = jnp.dot(x, wu_s[...], preferred_element_type=F32) * rs
        o_ref[rows, :] = (g * jax.nn.sigmoid(g) * u).astype(o_ref.dtype)
        return carry

    lax.fori_loop(0, x_ref.shape[0] // sub, rows_tile, 0)


def _ffn_up(hb, gain_col, w_gu, l):
    m, d = hb.shape
    n = w_gu.shape[-1] // 2
    bm = _pick(m, 2048 if hb.dtype == BF16 else 1024)
    bn = _pick(n, 512)
    nj = n // bn
    return pl.pallas_call(
        functools.partial(_mm_swiglu_kernel, sub=_pick(bm, 1024)),
        grid=(nj, m // bm),
        in_specs=[pl.BlockSpec((bm, d), lambda j, i: (i, 0)),
                  pl.BlockSpec((None, d, 1), lambda j, i: (l, 0, 0)),
                  pl.BlockSpec((None, d, bn), lambda j, i: (l, 0, j)),
                  pl.BlockSpec((None, d, bn), lambda j, i: (l, 0, j + nj))],
        out_specs=pl.BlockSpec((bm, bn), lambda j, i: (i, j)),
        out_shape=jax.ShapeDtypeStruct((m, n), BF16),
        scratch_shapes=[pltpu.VMEM((d, bn), BF16), pltpu.VMEM((d, bn), BF16)],
        compiler_params=_cparams(("arbitrary", "arbitrary")),
        name="ffn_up",
    )(hb, gain_col, w_gu, w_gu)


def _mm_residual_kernel(a_ref, w_ref, r_ref, o_ref, *rest, scale):
    w_s = rest[-1]
    _cast_weights_once(((w_ref, w_s),))
    out = r_ref[...] + scale * jnp.dot(a_ref[...], w_s[...], preferred_element_type=F32)
    o_ref[...] = out
    if len(rest) == 2:
        rest[0][...] = out.astype(BF16)


def _mm_residual(a, w, l, res, scale, emit_bf16=False):
    m, k = a.shape
    n = w.shape[-1]
    bm, bn = _pick(m, 256), _pick(n, 1024)
    o_spec = pl.BlockSpec((bm, bn), lambda j, i: (i, j))
    out_shape = jax.ShapeDtypeStruct((m, n), F32)
    return pl.pallas_call(
        functools.partial(_mm_residual_kernel, scale=scale),
        grid=(n // bn, m // bm),
        in_specs=[pl.BlockSpec((bm, k), lambda j, i: (i, 0)),
                  pl.BlockSpec((None, k, bn), lambda j, i: (l, 0, j), pipeline_mode=pl.Buffered(1)),
                  o_spec],
        out_specs=[o_spec, o_spec] if emit_bf16 else o_spec,
        out_shape=[out_shape, jax.ShapeDtypeStruct((m, n), BF16)] if emit_bf16 else out_shape,
        scratch_shapes=[pltpu.VMEM((k, bn), BF16)],
        compiler_params=_cparams(("arbitrary", "arbitrary")),
        name="mm_residual",
    )(a, w, res)


def _mem_kv_kernel(x_ref, g_ref, w_ref, o_ref):
    o_ref[...] = jnp.dot(_norm_rows(x_ref[...], g_ref[...]), w_ref[...], preferred_element_type=F32).astype(o_ref.dtype)


def _mem_kv(mem, gain, w):
    m, k = mem.shape
    depth, _, n = w.shape
    return pl.pallas_call(
        _mem_kv_kernel,
        grid=(depth,),
        in_specs=[pl.BlockSpec((m, k), lambda l: (0, 0)),
                  pl.BlockSpec((None, 1, k), lambda l: (l, 0, 0)),
                  pl.BlockSpec((None, k, n), lambda l: (l, 0, 0))],
        out_specs=pl.BlockSpec((None, m, n), lambda l: (l, 0, 0)),
        out_shape=jax.ShapeDtypeStruct((depth, m, n), BF16),
        compiler_params=_cparams(("parallel",)),
        name="mem_kv",
    )(mem, gain, w)


def _rope_store(acc, cos, sin, half, o_ref):
    for b in range(acc.shape[1] // LANES):
        o_ref[:, b * LANES:(b + 1) * LANES] = _rope_block(acc[:, b * LANES:(b + 1) * LANES], cos, sin, half).astype(o_ref.dtype)


def _mixer_proj_kernel(h_ref, g_ref, wa_ref, wv_ref, wg_ref, csa_ref, csg_ref, cos64_ref, sin64_ref,
                       cos128_ref, sin128_ref, qk_ref, vv_ref, gqk_ref, u_ref):
    u = _norm_rows(h_ref[...], g_ref[...])
    u_ref[...] = u
    qk = jnp.dot(u, wa_ref[...], preferred_element_type=F32) * csa_ref[...]
    _rope_store(qk, cos64_ref[...], sin64_ref[...], DIFF_QK_DIM // 2, qk_ref)
    vv_ref[...] = jnp.dot(u, wv_ref[...], preferred_element_type=F32).astype(vv_ref.dtype)
    gqk = jnp.dot(u, wg_ref[...], preferred_element_type=F32) * csg_ref[...]
    _rope_store(gqk, cos128_ref[...], sin128_ref[...], GQA_DIM // 2, gqk_ref)


def _mixer_proj(h, gain, w_a, w_v, w_g, l, cs_a, cs_g, cos64, sin64, cos128, sin128):
    m, k = h.shape
    na, nv, ng = w_a.shape[-1], w_v.shape[-1], w_g.shape[-1]
    bm = _pick(m, 512)
    once = dict(pipeline_mode=pl.Buffered(1))
    rows = lambda n: pl.BlockSpec((bm, n), lambda i: (i, 0))
    whole = lambda n: pl.BlockSpec((None, k, n), lambda i: (l, 0, 0), **once)
    return pl.pallas_call(
        _mixer_proj_kernel,
        grid=(m // bm,),
        in_specs=[rows(k), pl.BlockSpec((None, 1, k), lambda i: (l, 0, 0)), whole(na), whole(nv), whole(ng),
                  pl.BlockSpec((1, na), lambda i: (0, 0)), pl.BlockSpec((1, ng), lambda i: (0, 0)),
                  rows(LANES), rows(LANES), rows(LANES), rows(LANES)],
        out_specs=[rows(na), rows(nv), rows(ng), rows(k)],
        out_shape=[jax.ShapeDtypeStruct((m, n), BF16) for n in (na, nv, ng, k)],
        compiler_params=_cparams(("parallel",)),
        name="mixer_proj",
    )(h, gain, w_a, w_v, w_g, cs_a, cs_g, cos64, sin64, cos128, sin128)


def _mla_proj_kernel(u_ref, wc_ref, g_ref, wuq_ref, wukv_ref, cs_ref, cos_ref, sin_ref, q_ref, kv_ref, *, pattern):
    lat = jnp.dot(u_ref[...], wc_ref[...], preferred_element_type=F32)
    r = lat.shape[1] // 2
    cq = _norm_rows(lat[:, :r], g_ref[0])
    ckv = _norm_rows(lat[:, r:], g_ref[1])
    kv_ref[...] = jnp.dot(ckv, wukv_ref[...], preferred_element_type=F32).astype(kv_ref.dtype)
    q = jnp.dot(cq, wuq_ref[...], preferred_element_type=F32) * cs_ref[...]
    cos = cos_ref[...]
    sin = sin_ref[...]
    for b, roped in enumerate(pattern):
        blk = q[:, b * LANES:(b + 1) * LANES]
        if roped:
            blk = _rope_block(blk, cos, sin, MLA_ROPE // 2)
        q_ref[:, b * LANES:(b + 1) * LANES] = blk.astype(q_ref.dtype)


def _mla_proj(u, w_c, gains, wuq, wukv, l, colscale, cos, sin):
    m, k = u.shape
    nc = w_c.shape[-1]
    r = nc // 2
    nq, nkv = wuq.shape[-1], wukv.shape[-1]
    bm = _pick(m, 512)
    return pl.pallas_call(
        functools.partial(_mla_proj_kernel, pattern=(False, True) * MLA_HEADS),
        grid=(m // bm,),
        in_specs=[pl.BlockSpec((bm, k), lambda i: (i, 0)),
                  pl.BlockSpec((None, k, nc), lambda i: (l, 0, 0)),
                  pl.BlockSpec((None, 2, 1, r), lambda i: (l, 0, 0, 0)),
                  pl.BlockSpec((None, r, nq), lambda i: (l, 0, 0)),
                  pl.BlockSpec((None, r, nkv), lambda i: (l, 0, 0)),
                  pl.BlockSpec((1, nq), lambda i: (0, 0)),
                  pl.BlockSpec((bm, LANES), lambda i: (i, 0)),
                  pl.BlockSpec((bm, LANES), lambda i: (i, 0))],
        out_specs=[pl.BlockSpec((bm, nq), lambda i: (i, 0)), pl.BlockSpec((bm, nkv), lambda i: (i, 0))],
        out_shape=[jax.ShapeDtypeStruct((m, nq), BF16), jax.ShapeDtypeStruct((m, nkv), BF16)],
        compiler_params=_cparams(("parallel",)),
        name="mla_proj",
    )(u, w_c, gains, wuq, wukv, colscale, cos, sin)


def _logits(q, k, mlane):
    s = lax.dot_general(q, k, (((1,), (1,)), ((), ())), preferred_element_type=F32)
    for b in range(s.shape[1] // LANES):
        blk = s[:, b * LANES:(b + 1) * LANES]
        mlane = blk if mlane is None else jnp.maximum(mlane, blk)
    return s, mlane


def _row_max(mlane):
    return jnp.broadcast_to(jnp.max(mlane, axis=1, keepdims=True), mlane.shape)


def _weigh(s, m, v1, acc):
    ps = [jnp.exp2(s[:, b * LANES:(b + 1) * LANES] - m).astype(BF16) for b in range(s.shape[1] // LANES)]
    pv = jnp.dot(jnp.concatenate(ps, axis=1), v1, preferred_element_type=F32)
    return pv if acc is None else acc + pv


def _softmax_pv(qs, k_chunk, v1_chunk, nchunks):
    n = len(qs)
    logits = [[None] * nchunks for _ in range(n)]
    acc = [None] * n
    mlane = None
    for j in range(nchunks):
        logits[0][j], mlane = _logits(qs[0], k_chunk(j), mlane)
    m = _row_max(mlane)
    for p in range(1, n):
        mlane = None
        for j in range(nchunks):
            acc[p - 1] = _weigh(logits[p - 1][j], m, v1_chunk(j), acc[p - 1])
            logits[p][j], mlane = _logits(qs[p], k_chunk(j), mlane)
        m = _row_max(mlane)
    for j in range(nchunks):
        acc[n - 1] = _weigh(logits[n - 1][j], m, v1_chunk(j), acc[n - 1])
    return [a[:, :a.shape[1] - LANES] / a[:, a.shape[1] - LANES:] for a in acc]


def _attn_tiles(s):
    rows = _pick(s, 512)
    return rows, (2 if s % (2 * rows) == 0 else 1), _pick(s, 1024)


def _diff_attn_kernel(lp_ref, q_ref, k_ref, v_ref, g_ref, o_ref, *, lambda_init, tk, nsub):
    bq = q_ref.shape[0] // nsub
    ones = jnp.ones((tk, LANES), BF16)
    k_chunk = lambda j: k_ref[j * tk:(j + 1) * tk, :]
    v1_chunk = lambda j: jnp.concatenate([v_ref[j * tk:(j + 1) * tk, :], ones], axis=1)
    qs = []
    for r in range(nsub):
        q = q_ref[r * bq:(r + 1) * bq, :]
        lane = lax.broadcasted_iota(jnp.int32, q.shape, 1)
        zero = jnp.zeros_like(q)
        qs.append(jnp.concatenate([jnp.where(lane < DIFF_QK_DIM, q, zero),
                                   jnp.where(lane >= DIFF_QK_DIM, q, zero)], axis=0))
    outs = _softmax_pv(qs, k_chunk, v1_chunk, k_ref.shape[0] // tk)
    lp = lp_ref[...]
    lam = (jnp.exp(jnp.sum(lp[0:1] * lp[1:2], axis=1, keepdims=True))
           - jnp.exp(jnp.sum(lp[2:3] * lp[3:4], axis=1, keepdims=True)) + lambda_init)
    for r in range(nsub):
        o = outs[r][:bq] - lam * outs[r][bq:]
        y = o * lax.rsqrt(jnp.mean(o * o, axis=-1, keepdims=True) + DIFF_SUBLN_EPS)
        o_ref[r * bq:(r + 1) * bq, :] = (y * g_ref[...] * (1.0 - lambda_init)).astype(o_ref.dtype)


def _diff_attention(qk, v, diff_lambda, diff_subln, l, lambda_init):
    s = qk.shape[0]
    prob_rows, nsub, tk = _attn_tiles(s)
    rows = prob_rows // 2 * nsub
    h = DIFF_HEADS
    return pl.pallas_call(
        functools.partial(_diff_attn_kernel, lambda_init=lambda_init, tk=tk, nsub=nsub),
        grid=(h, s // rows),
        in_specs=[pl.BlockSpec((None, 4, DIFF_QK_DIM), lambda hh, i: (l, 0, 0)),
                  pl.BlockSpec((rows, LANES), lambda hh, i: (i, hh)),
                  pl.BlockSpec((s, LANES), lambda hh, i: (0, h + hh)),
                  pl.BlockSpec((s, LANES), lambda hh, i: (0, hh)),
                  pl.BlockSpec((None, 1, DIFF_V_DIM), lambda hh, i: (l, 0, 0))],
        out_specs=pl.BlockSpec((rows, LANES), lambda hh, i: (i, hh)),
        out_shape=jax.ShapeDtypeStruct((s, h * DIFF_V_DIM), BF16),
        compiler_params=_cparams(("parallel", "parallel")),
        name="diff_attention",
    )(diff_lambda, qk, qk, v, diff_subln)


def _mla_attn_kernel(q_ref, kn_ref, kr_ref, v_ref, o_ref, *, tk, nsub):
    bq = q_ref.shape[0] // nsub
    ones = jnp.ones((tk, LANES), BF16)
    k_chunk = lambda j: jnp.concatenate([kn_ref[j * tk:(j + 1) * tk, :], kr_ref[j * tk:(j + 1) * tk, :]], axis=1)
    v1_chunk = lambda j: jnp.concatenate([v_ref[j * tk:(j + 1) * tk, :], ones], axis=1)
    qs = [q_ref[r * bq:(r + 1) * bq, :] for r in range(nsub)]
    outs = _softmax_pv(qs, k_chunk, v1_chunk, kn_ref.shape[0] // tk)
    for r in range(nsub):
        o_ref[r * bq:(r + 1) * bq, :] = outs[r].astype(o_ref.dtype)


def _mla_attention(q, kv_up, qk):
    s = q.shape[0]
    bq, nsub, tk = _attn_tiles(s)
    rows = bq * nsub
    h = MLA_HEADS
    kr_blk = 2 * DIFF_HEADS
    return pl.pallas_call(
        functools.partial(_mla_attn_kernel, tk=tk, nsub=nsub),
        grid=(h, s // rows),
        in_specs=[pl.BlockSpec((rows, 2 * LANES), lambda hh, i: (i, hh)),
                  pl.BlockSpec((s, LANES), lambda hh, i: (0, 2 * hh)),
                  pl.BlockSpec((s, LANES), lambda hh, i: (0, kr_blk)),
                  pl.BlockSpec((s, LANES), lambda hh, i: (0, 2 * hh + 1))],
        out_specs=pl.BlockSpec((rows, LANES), lambda hh, i: (i, hh)),
        out_shape=jax.ShapeDtypeStruct((s, h * MLA_V), BF16),
        compiler_params=_cparams(("parallel", "parallel")),
        name="mla_attention",
    )(q, kv_up, qk, kv_up)


def _gqa_kernel(sink_ref, q_ref, kp_ref, kc_ref, kn_ref, vp_ref, vc_ref, vn_ref, o_ref, *, layer, seq, bq):
    g = pl.program_id(0)
    i = pl.program_id(1)
    nsub = q_ref.shape[0] // bq
    nk = bq + 2 * WINDOW
    band_k = jnp.concatenate([kp_ref[...], kc_ref[...], kn_ref[...]], axis=0)
    band_v = jnp.concatenate([vp_ref[...], vc_ref[...], vn_ref[...]], axis=0)
    band_v1 = jnp.concatenate([band_v, jnp.ones_like(band_v)], axis=1)
    for r in range(nsub):
        q0 = (i * nsub + r) * bq
        kb = band_k[r * bq:r * bq + nk]
        v1 = band_v1[r * bq:r * bq + nk]
        qpos = q0 + lax.broadcasted_iota(jnp.int32, (bq, nk), 0)
        kpos = q0 - WINDOW + lax.broadcasted_iota(jnp.int32, (bq, nk), 1)
        valid = (jnp.abs(kpos - qpos) <= WINDOW) & (kpos >= 0) & (kpos < seq)
        bias = jnp.where(valid, 0.0, NEG_INF).astype(F32)
        for hh in range(GQA_GROUP):
            q = q_ref[r * bq:(r + 1) * bq, hh * GQA_DIM:(hh + 1) * GQA_DIM]
            s = lax.dot_general(q, kb, (((1,), (1,)), ((), ())), preferred_element_type=F32) + bias
            sink = sink_ref[layer * GQA_HEADS + g * GQA_GROUP + hh] * LOG2E
            m = jnp.maximum(jnp.max(s, axis=1, keepdims=True), sink)
            pv = jnp.dot(jnp.exp2(s - m).astype(BF16), v1, preferred_element_type=F32)
            denom = pv[:, GQA_DIM:] + jnp.exp2(sink - m)
            o_ref[r * bq:(r + 1) * bq, hh * GQA_DIM:(hh + 1) * GQA_DIM] = (pv[:, :GQA_DIM] / denom).astype(o_ref.dtype)


def _window_gqa(gqk, vv, sink_flat, l):
    s = gqk.shape[0]
    sub = _pick(s, 256)
    bq = _pick(s, 8 * sub)
    assert sub % WINDOW == 0 and WINDOW == LANES
    nb = s // bq
    r = bq // WINDOW
    nh = s // WINDOW
    qw = GQA_GROUP * GQA_DIM
    kcol = GQA_HEADS
    vcol = DIFF_HEADS

    def prev(i):
        return jnp.maximum(i * r - 1, 0)

    def nxt(i):
        return jnp.minimum((i + 1) * r, nh - 1)

    return pl.pallas_call(
        functools.partial(_gqa_kernel, layer=l, seq=s, bq=sub),
        grid=(GQA_KV_HEADS, nb),
        in_specs=[pl.BlockSpec(memory_space=pltpu.SMEM),
                  pl.BlockSpec((bq, qw), lambda g, i: (i, g)),
                  pl.BlockSpec((WINDOW, LANES), lambda g, i: (prev(i), kcol + g)),
                  pl.BlockSpec((bq, LANES), lambda g, i: (i, kcol + g)),
                  pl.BlockSpec((WINDOW, LANES), lambda g, i: (nxt(i), kcol + g)),
                  pl.BlockSpec((WINDOW, LANES), lambda g, i: (prev(i), vcol + g)),
                  pl.BlockSpec((bq, LANES), lambda g, i: (i, vcol + g)),
                  pl.BlockSpec((WINDOW, LANES), lambda g, i: (nxt(i), vcol + g))],
        out_specs=pl.BlockSpec((bq, qw), lambda g, i: (i, g)),
        out_shape=jax.ShapeDtypeStruct((s, GQA_HEADS * GQA_DIM), BF16),
        compiler_params=_cparams(("parallel", "parallel")),
        name="window_gqa",
    )(sink_flat, gqk, gqk, gqk, gqk, vv, vv, vv)


def _merge_kernel(u_ref, o0_ref, o1_ref, o2_ref, wg0_ref, wg1_ref, wg2_ref, b0_ref, b1_ref, b2_ref,
                  wb0_ref, wb1_ref, wb2_ref, out_ref, wg0_s, wg1_s, wg2_s, wb0_s, wb1_s, wb2_s):
    _cast_weights_once(((wg0_ref, wg0_s), (wg1_ref, wg1_s), (wg2_ref, wg2_s),
                        (wb0_ref, wb0_s), (wb1_ref, wb1_s), (wb2_ref, wb2_s)))
    u = u_ref[...]
    acc = None
    for o_ref, wg_s, b_ref, wb_s in ((o0_ref, wg0_s, b0_ref, wb0_s),
                                     (o1_ref, wg1_s, b1_ref, wb1_s),
                                     (o2_ref, wg2_s, b2_ref, wb2_s)):
        gate = jax.nn.sigmoid(jnp.dot(u, wg_s[...], preferred_element_type=F32) + b_ref[...])
        term = jnp.dot(o_ref[...], wb_s[...], preferred_element_type=F32) * gate
        acc = term if acc is None else acc + term
    out_ref[...] = acc.astype(out_ref.dtype)


def _merge(u, o_diff, o_mla, o_gqa, w_gate, b_gate, w_branch, l):
    m, d = u.shape
    bw = o_diff.shape[1]
    bm = _pick(m, 1024)
    bn = _pick(d, 256)
    nj = d // bn
    o_spec = pl.BlockSpec((bm, bw), lambda j, i: (i, 0))

    def wg_spec(n):
        return pl.BlockSpec((None, d, bn), lambda j, i: (l, 0, n * nj + j))

    def b_spec(n):
        return pl.BlockSpec((None, 1, bn), lambda j, i: (l, 0, n * nj + j))

    def wb_spec(n):
        return pl.BlockSpec((None, None, bw, bn), lambda j, i: (l, n, 0, j))

    return pl.pallas_call(
        _merge_kernel,
        grid=(nj, m // bm),
        in_specs=[pl.BlockSpec((bm, d), lambda j, i: (i, 0)), o_spec, o_spec, o_spec,
                  wg_spec(0), wg_spec(1), wg_spec(2), b_spec(0), b_spec(1), b_spec(2),
                  wb_spec(0), wb_spec(1), wb_spec(2)],
        out_specs=pl.BlockSpec((bm, bn), lambda j, i: (i, j)),
        out_shape=jax.ShapeDtypeStruct((m, d), BF16),
        scratch_shapes=[pltpu.VMEM((d, bn), BF16)] * 3 + [pltpu.VMEM((bw, bn), BF16)] * 3,
        compiler_params=_cparams(("arbitrary", "arbitrary")),
        name="branch_merge",
    )(u, o_diff, o_mla, o_gqa, w_gate, w_gate, w_gate, b_gate, b_gate, b_gate,
      w_branch, w_branch, w_branch)


def _xattn_kernel(mg_ref, h_ref, wm_ref, g_ref, wq_ref, k_ref, v_ref, wo_ref, o_ref, ob_ref, wq_s, wo_s):
    @pl.when(pl.program_id(0) == 0)
    def _():
        wq_s[...] = wq_ref[...].astype(BF16)
        wo_s[...] = wo_ref[...].astype(BF16)

    h1 = h_ref[...] + jnp.dot(mg_ref[...], wm_ref[...], preferred_element_type=F32)
    hn = _norm_rows(h1, g_ref[...])
    q = jnp.dot(hn, wq_s[...], preferred_element_type=F32) * (XA_DIM ** -0.5 * LOG2E)
    q = q.astype(BF16)
    heads = []
    for hh in range(XA_HEADS):
        sl = slice(hh * XA_DIM, (hh + 1) * XA_DIM)
        s = lax.dot_general(q[:, sl], k_ref[:, sl], (((1,), (1,)), ((), ())), preferred_element_type=F32)
        p = jnp.exp2(s - jnp.max(s, axis=1, keepdims=True))
        denom = jnp.sum(p, axis=1, keepdims=True)
        heads.append((jnp.dot(p.astype(BF16), v_ref[:, sl], preferred_element_type=F32) / denom).astype(BF16))
    o = jnp.concatenate(heads, axis=1)
    out = h1 + jnp.dot(o, wo_s[...], preferred_element_type=F32)
    o_ref[...] = out
    ob_ref[...] = out.astype(BF16)


def _cross_attention(merged, h, w_o, gain, xa_wq, memkv, xa_wo, l):
    m, d = h.shape
    w = XA_HEADS * XA_DIM
    ml = memkv.shape[1]
    bm = _pick(m, 512)
    once = dict(pipeline_mode=pl.Buffered(1))
    return pl.pallas_call(
        _xattn_kernel,
        grid=(m // bm,),
        in_specs=[pl.BlockSpec((bm, d), lambda i: (i, 0)),
                  pl.BlockSpec((bm, d), lambda i: (i, 0)),
                  pl.BlockSpec((None, d, d), lambda i: (l, 0, 0), **once),
                  pl.BlockSpec((None, 1, d), lambda i: (l, 0, 0)),
                  pl.BlockSpec((None, d, w), lambda i: (l, 0, 0), **once),
                  pl.BlockSpec((None, ml, w), lambda i: (l, 0, 0)),
                  pl.BlockSpec((None, ml, w), lambda i: (l, 0, 1)),
                  pl.BlockSpec((None, w, d), lambda i: (l, 0, 0), **once)],
        out_specs=[pl.BlockSpec((bm, d), lambda i: (i, 0))] * 2,
        out_shape=[jax.ShapeDtypeStruct((m, d), F32), jax.ShapeDtypeStruct((m, d), BF16)],
        scratch_shapes=[pltpu.VMEM((d, w), BF16), pltpu.VMEM((w, d), BF16)],
        compiler_params=_cparams(("arbitrary",)),
        name="cross_attention",
    )(merged, h, w_o, gain, xa_wq, memkv, memkv, xa_wo)


def _pad_cols(w, width):
    return jnp.pad(w, ((0, 0),) * (w.ndim - 1) + ((0, width - w.shape[-1]),))


_DQ, _DV, _CQ, _KR, _GQ, _GV, _END = 0, 2048, 3072, 4096, 4160, 5440, 5696


def _regroup_kernel(wt_ref, a_ref, v_ref, c_ref, g_ref):
    def cols(lo, hi):
        return wt_ref[lo:hi, :].T

    kr = jnp.concatenate([wt_ref[_KR:_GQ, :], jnp.zeros((LANES - (_GQ - _KR), wt_ref.shape[1]), wt_ref.dtype)], axis=0)
    a_ref[...] = jnp.concatenate([cols(_DQ, _DV), kr.T], axis=1).astype(BF16)
    v_ref[...] = jnp.concatenate([cols(_DV, _CQ), cols(_GV, _END)], axis=1).astype(BF16)
    c_ref[...] = cols(_CQ, _KR).astype(BF16)
    g_ref[...] = cols(_GQ, _GV).astype(BF16)


def _regroup_w_in(w_in):
    depth, k, n = w_in.shape
    assert n == _END
    bk = _pick(k, 256)
    widths = (_DV - _DQ + LANES, (_CQ - _DV) + (_END - _GV), _KR - _CQ, _GV - _GQ)
    return pl.pallas_call(
        _regroup_kernel,
        grid=(depth, k // bk),
        in_specs=[pl.BlockSpec((None, n, bk), lambda l, i: (l, 0, i))],
        out_specs=[pl.BlockSpec((None, bk, w), lambda l, i: (l, i, 0)) for w in widths],
        out_shape=[jax.ShapeDtypeStruct((depth, k, w), BF16) for w in widths],
        compiler_params=_cparams(("parallel", "parallel")),
        name="regroup_w_in",
    )(jnp.swapaxes(w_in, 1, 2))


def kernel(x, mem, positions, ffn1_norm, ffn1_w_gu, ffn1_w_down, mix_norm, w_in, diff_lambda, diff_subln,
           mla_q_norm, mla_kv_norm, mla_w_uq, mla_w_ukv, gqa_sink, w_branch, w_gate, b_gate, w_o, xa_norm,
           mem_norm, xa_wq, xa_wkv, xa_wo, ffn2_norm, ffn2_w_gu, ffn2_w_down, final_norm):
    b, s, d = x.shape
    assert b == 1
    depth = ffn1_norm.shape[0]
    h = x.reshape(s, d)
    mem2 = mem.reshape(mem.shape[1], d)

    pos_b = jnp.broadcast_to(positions.reshape(s, 1).astype(F32), (s, LANES))
    cos64, sin64 = _rope_tables(pos_b, DIFF_QK_DIM)
    cos128, sin128 = _rope_tables(pos_b, GQA_DIM)

    bf = lambda w: w.astype(BF16)
    w_o_b = bf(w_o)
    b_gate3 = b_gate.reshape(depth, 1, -1)
    norm3 = lambda g: g.reshape(depth, 1, -1)
    mix_norm3, xa_norm3, mem_norm3 = map(norm3, (mix_norm, xa_norm, mem_norm))
    memkv = _mem_kv(mem2, mem_norm3, bf(xa_wkv))
    ffn1_norm_col, ffn2_norm_col = ffn1_norm.reshape(depth, d, 1), ffn2_norm.reshape(depth, d, 1)
    hb = h
    diff_subln3 = diff_subln.reshape(depth, 1, -1)
    sink_flat = gqa_sink.reshape(-1)

    diff_scale = DIFF_QK_DIM ** -0.5 * LOG2E
    mla_scale = (MLA_NOPE + MLA_ROPE) ** -0.5 * LOG2E
    gqa_scale = GQA_DIM ** -0.5 * LOG2E
    cs_a = jnp.concatenate([jnp.full((1, 1024), diff_scale, F32), jnp.ones((1, 1024 + LANES), F32)], axis=1)
    cs_g = jnp.concatenate([jnp.full((1, 1024), gqa_scale, F32), jnp.ones((1, 256), F32)], axis=1)
    cs_q = jnp.full((1, MLA_HEADS * 2 * LANES), mla_scale, F32)

    w_a, w_v, w_c, w_g = _regroup_w_in(w_in)
    wuq = mla_w_uq.reshape(depth, MLA_Q_RANK, MLA_HEADS, MLA_NOPE + MLA_ROPE)
    wuq = _pad_cols(wuq, 2 * LANES).reshape(depth, MLA_Q_RANK, MLA_HEADS * 2 * LANES)
    wuq, wukv = bf(wuq), bf(mla_w_ukv)
    mla_gains = jnp.stack([mla_q_norm, mla_kv_norm], axis=1)[:, :, None, :]

    for l in range(depth):
        lambda_init = 0.8 - 0.6 * math.exp(-0.3 * l)

        a = _ffn_up(hb, ffn1_norm_col, ffn1_w_gu, l)
        h = _mm_residual(a, ffn1_w_down, l, h, 0.5)

        qk, vv, gqk, u = _mixer_proj(h, mix_norm3, w_a, w_v, w_g, l, cs_a, cs_g, cos64, sin64, cos128, sin128)

        o_diff = _diff_attention(qk, vv, diff_lambda, diff_subln3, l, lambda_init)

        q_mla, kv_up = _mla_proj(u, w_c, mla_gains, wuq, wukv, l, cs_q, cos64, sin64)
        o_mla = _mla_attention(q_mla, kv_up, qk)

        o_gqa = _window_gqa(gqk, vv, sink_flat, l)

        merged = _merge(u, o_diff, o_mla, o_gqa, w_gate, b_gate3, w_branch, l)
        h, hb = _cross_attention(merged, h, w_o_b, xa_norm3, xa_wq, memkv, xa_wo, l)

        a = _ffn_up(hb, ffn2_norm_col, ffn2_w_gu, l)
        if l + 1 < depth:
            h, hb = _mm_residual(a, ffn2_w_down, l, h, 0.5, emit_bf16=True)
        else:
            h = _mm_residual(a, ffn2_w_down, l, h, 0.5)

    out = _rmsnorm(h, final_norm.reshape(1, d), F32)
    return out.reshape(b, s, d)
```

```python
import functools
import math

import jax
import jax.numpy as jnp
from jax import lax
from jax.experimental import pallas as pl
from jax.experimental.pallas import tpu as pltpu

F32 = jnp.float32
BF16 = jnp.bfloat16

DIFF_HEADS = 8
DIFF_QK_DIM = 64
DIFF_V_DIM = 128
MLA_HEADS = 8
MLA_Q_RANK = 512
MLA_NOPE = 128
MLA_ROPE = 64
MLA_V = 128
GQA_HEADS = 8
GQA_KV_HEADS = 2
GQA_GROUP = GQA_HEADS // GQA_KV_HEADS
GQA_DIM = 128
WINDOW = 128
XA_HEADS = 4
XA_DIM = 128
D_FF = 5632
ROPE_THETA = 10000.0
EPS = 1e-6
DIFF_SUBLN_EPS = 1e-5
NEG_INF = -1e30
LOG2E = math.log2(math.e)

LANES = 128
V7X_VMEM_LIMIT_BYTES = 56 * 1024 * 1024


def _cparams(semantics):
    return pltpu.CompilerParams(dimension_semantics=semantics, vmem_limit_bytes=V7X_VMEM_LIMIT_BYTES)


def _pick(dim, pref):
    if dim <= pref:
        return dim
    b = pref
    while dim % b:
        b //= 2
    return b


def _rope_table_kernel(pos_ref, inv_ref, sign_ref, cos_ref, sin_ref):
    ang = pos_ref[...] * inv_ref[...]
    cos_ref[...] = jnp.cos(ang)
    sin_ref[...] = jnp.sin(ang) * sign_ref[...]


def _rope_tables(pos_b, dim):
    s = pos_b.shape[0]
    half = dim // 2
    inv = ROPE_THETA ** (-jnp.arange(0, dim, 2, dtype=F32) / dim)
    inv_row = jnp.tile(inv, LANES // half)[None, :]
    sign_row = jnp.tile(jnp.concatenate([-jnp.ones((half,), F32), jnp.ones((half,), F32)]), LANES // dim)[None, :]
    bm = _pick(s, 1024)
    return pl.pallas_call(
        _rope_table_kernel,
        grid=(s // bm,),
        in_specs=[pl.BlockSpec((bm, LANES), lambda i: (i, 0)),
                  pl.BlockSpec((1, LANES), lambda i: (0, 0)),
                  pl.BlockSpec((1, LANES), lambda i: (0, 0))],
        out_specs=[pl.BlockSpec((bm, LANES), lambda i: (i, 0))] * 2,
        out_shape=[jax.ShapeDtypeStruct((s, LANES), F32)] * 2,
        compiler_params=_cparams(("parallel",)),
        name="rope_tables",
    )(pos_b, inv_row, sign_row)


def _rope_block(x, cos, sin_signed, half):
    if 2 * half == LANES:
        partner = pltpu.roll(x, half, axis=1)
    else:
        lane = lax.broadcasted_iota(jnp.int32, x.shape, 1)
        first = (lane % (2 * half)) < half
        partner = jnp.where(first, pltpu.roll(x, LANES - half, axis=1), pltpu.roll(x, half, axis=1))
    return x * cos + partner * sin_signed


def _rmsnorm_kernel(x_ref, g_ref, o_ref, *, eps):
    x = x_ref[...]
    y = x * lax.rsqrt(jnp.mean(x * x, axis=-1, keepdims=True) + eps)
    o_ref[...] = (y * g_ref[...]).astype(o_ref.dtype)


def _rmsnorm(x, gain, out_dtype):
    m, d = x.shape
    bm = _pick(m, 512)
    return pl.pallas_call(
        functools.partial(_rmsnorm_kernel, eps=EPS),
        grid=(m // bm,),
        in_specs=[pl.BlockSpec((bm, d), lambda i: (i, 0)), pl.BlockSpec((1, d), lambda i: (0, 0))],
        out_specs=pl.BlockSpec((bm, d), lambda i: (i, 0)),
        out_shape=jax.ShapeDtypeStruct((m, d), out_dtype),
        compiler_params=_cparams(("parallel",)),
        name="rmsnorm",
    )(x, gain)


def _cast_weights_once(pairs):
    @pl.when(pl.program_id(1) == 0)
    def _():
        for w_ref, w_s in pairs:
            w_s[...] = w_ref[...].astype(BF16)


def _norm_rows(x, g, eps=EPS):
    return (x * lax.rsqrt(jnp.mean(x * x, axis=-1, keepdims=True) + eps) * g).astype(BF16)


def _mm_swiglu_kernel(x_ref, gc_ref, wg_ref, wu_ref, o_ref, wg_s, wu_s, *, sub):
    @pl.when(pl.program_id(1) == 0)
    def _():
        gc = gc_ref[...]
        wg_s[...] = (wg_ref[...] * gc).astype(BF16)
        wu_s[...] = (wu_ref[...] * gc).astype(BF16)

    def rows_tile(t, carry):
        rows = pl.ds(pl.multiple_of(t * sub, sub), sub)
        x = x_ref[rows, :]
        xf = x.astype(F32)
        x = x.astype(BF16)
        rs = lax.rsqrt(jnp.mean(xf * xf, axis=-1, keepdims=True) + EPS)
        g = jnp.dot(x, wg_s[...], preferred_element_type=F32) * rs
        u = jnp.dot(x, wu_s[...], preferred_element_type=F32) * rs
        o_ref[rows, :] = (g * jax.nn.sigmoid(g) * u).astype(o_ref.dtype)
        return carry

    lax.fori_loop(0, x_ref.shape[0] // sub, rows_tile, 0)


def _ffn_up(hb, gain_col, w_gu, l):
    m, d = hb.shape
    n = w_gu.shape[-1] // 2
    bm = _pick(m, 2048 if hb.dtype == BF16 else 1024)
    bn = _pick(n, 512)
    nj = n // bn
    return pl.pallas_call(
        functools.partial(_mm_swiglu_kernel, sub=_pick(bm, 1024)),
        grid=(nj, m // bm),
        in_specs=[pl.BlockSpec((bm, d), lambda j, i: (i, 0)),
                  pl.BlockSpec((None, d, 1), lambda j, i: (l, 0, 0)),
                  pl.BlockSpec((None, d, bn), lambda j, i: (l, 0, j)),
                  pl.BlockSpec((None, d, bn), lambda j, i: (l, 0, j + nj))],
        out_specs=pl.BlockSpec((bm, bn), lambda j, i: (i, j)),
        out_shape=jax.ShapeDtypeStruct((m, n), BF16),
        scratch_shapes=[pltpu.VMEM((d, bn), BF16), pltpu.VMEM((d, bn), BF16)],
        compiler_params=_cparams(("arbitrary", "arbitrary")),
        name="ffn_up",
    )(hb, gain_col, w_gu, w_gu)


def _mm_residual_kernel(a_ref, w_ref, r_ref, o_ref, *rest, scale):
    w_s = rest[-1]
    _cast_weights_once(((w_ref, w_s),))
    out = r_ref[...] + scale * jnp.dot(a_ref[...], w_s[...], preferred_element_type=F32)
    o_ref[...] = out
    if len(rest) == 2:
        rest[0][...] = out.astype(BF16)


def _mm_residual(a, w, l, res, scale, emit_bf16=False):
    m, k = a.shape
    n = w.shape[-1]
    bm, bn = _pick(m, 256), _pick(n, 1024)
    o_spec = pl.BlockSpec((bm, bn), lambda j, i: (i, j))
    out_shape = jax.ShapeDtypeStruct((m, n), F32)
    return pl.pallas_call(
        functools.partial(_mm_residual_kernel, scale=scale),
        grid=(n // bn, m // bm),
        in_specs=[pl.BlockSpec((bm, k), lambda j, i: (i, 0)),
                  pl.BlockSpec((None, k, bn), lambda j, i: (l, 0, j), pipeline_mode=pl.Buffered(1)),
                  o_spec],
        out_specs=[o_spec, o_spec] if emit_bf16 else o_spec,
        out_shape=[out_shape, jax.ShapeDtypeStruct((m, n), BF16)] if emit_bf16 else out_shape,
        scratch_shapes=[pltpu.VMEM((k, bn), BF16)],
        compiler_params=_cparams(("arbitrary", "arbitrary")),
        name="mm_residual",
    )(a, w, res)


def _mem_kv_kernel(x_ref, g_ref, w_ref, o_ref):
    o_ref[...] = jnp.dot(_norm_rows(x_ref[...], g_ref[...]), w_ref[...], preferred_element_type=F32).astype(o_ref.dtype)


def _mem_kv(mem, gain, w):
    m, k = mem.shape
    depth, _, n = w.shape
    return pl.pallas_call(
        _mem_kv_kernel,
        grid=(depth,),
        in_specs=[pl.BlockSpec((m, k), lambda l: (0, 0)),
                  pl.BlockSpec((None, 1, k), lambda l: (l, 0, 0)),
                  pl.BlockSpec((None, k, n), lambda l: (l, 0, 0))],
        out_specs=pl.BlockSpec((None, m, n), lambda l: (l, 0, 0)),
        out_shape=jax.ShapeDtypeStruct((depth, m, n), BF16),
        compiler_params=_cparams(("parallel",)),
        name="mem_kv",
    )(mem, gain, w)


def _rope_store(acc, cos, sin, half, o_ref):
    for b in range(acc.shape[1] // LANES):
        o_ref[:, b * LANES:(b + 1) * LANES] = _rope_block(acc[:, b * LANES:(b + 1) * LANES], cos, sin, half).astype(o_ref.dtype)


def _mixer_proj_kernel(h_ref, g_ref, wa_ref, wv_ref, wg_ref, csa_ref, csg_ref, cos64_ref, sin64_ref,
                       cos128_ref, sin128_ref, qk_ref, vv_ref, gqk_ref, u_ref):
    u = _norm_rows(h_ref[...], g_ref[...])
    u_ref[...] = u
    qk = jnp.dot(u, wa_ref[...], preferred_element_type=F32) * csa_ref[...]
    _rope_store(qk, cos64_ref[...], sin64_ref[...], DIFF_QK_DIM // 2, qk_ref)
    vv_ref[...] = jnp.dot(u, wv_ref[...], preferred_element_type=F32).astype(vv_ref.dtype)
    gqk = jnp.dot(u, wg_ref[...], preferred_element_type=F32) * csg_ref[...]
    _rope_store(gqk, cos128_ref[...], sin128_ref[...], GQA_DIM // 2, gqk_ref)


def _mixer_proj(h, gain, w_a, w_v, w_g, l, cs_a, cs_g, cos64, sin64, cos128, sin128):
    m, k = h.shape
    na, nv, ng = w_a.shape[-1], w_v.shape[-1], w_g.shape[-1]
    bm = _pick(m, 512)
    once = dict(pipeline_mode=pl.Buffered(1))
    rows = lambda n: pl.BlockSpec((bm, n), lambda i: (i, 0))
    whole = lambda n: pl.BlockSpec((None, k, n), lambda i: (l, 0, 0), **once)
    return pl.pallas_call(
        _mixer_proj_kernel,
        grid=(m // bm,),
        in_specs=[rows(k), pl.BlockSpec((None, 1, k), lambda i: (l, 0, 0)), whole(na), whole(nv), whole(ng),
                  pl.BlockSpec((1, na), lambda i: (0, 0)), pl.BlockSpec((1, ng), lambda i: (0, 0)),
                  rows(LANES), rows(LANES), rows(LANES), rows(LANES)],
        out_specs=[rows(na), rows(nv), rows(ng), rows(k)],
        out_shape=[jax.ShapeDtypeStruct((m, n), BF16) for n in (na, nv, ng, k)],
        compiler_params=_cparams(("parallel",)),
        name="mixer_proj",
    )(h, gain, w_a, w_v, w_g, cs_a, cs_g, cos64, sin64, cos128, sin128)


def _mla_proj_kernel(u_ref, wc_ref, g_ref, wuq_ref, wukv_ref, cs_ref, cos_ref, sin_ref, q_ref, kv_ref, *, pattern):
    lat = jnp.dot(u_ref[...], wc_ref[...], preferred_element_type=F32)
    r = lat.shape[1] // 2
    cq = _norm_rows(lat[:, :r], g_ref[0])
    ckv = _norm_rows(lat[:, r:], g_ref[1])
    kv_ref[...] = jnp.dot(ckv, wukv_ref[...], preferred_element_type=F32).astype(kv_ref.dtype)
    q = jnp.dot(cq, wuq_ref[...], preferred_element_type=F32) * cs_ref[...]
    cos = cos_ref[...]
    sin = sin_ref[...]
    for b, roped in enumerate(pattern):
        blk = q[:, b * LANES:(b + 1) * LANES]
        if roped:
            blk = _rope_block(blk, cos, sin, MLA_ROPE // 2)
        q_ref[:, b * LANES:(b + 1) * LANES] = blk.astype(q_ref.dtype)


def _mla_proj(u, w_c, gains, wuq, wukv, l, colscale, cos, sin):
    m, k = u.shape
    nc = w_c.shape[-1]
    r = nc // 2
    nq, nkv = wuq.shape[-1], wukv.shape[-1]
    bm = _pick(m, 1024)
    return pl.pallas_call(
        functools.partial(_mla_proj_kernel, pattern=(False, True) * MLA_HEADS),
        grid=(m // bm,),
        in_specs=[pl.BlockSpec((bm, k), lambda i: (i, 0)),
                  pl.BlockSpec((None, k, nc), lambda i: (l, 0, 0)),
                  pl.BlockSpec((None, 2, 1, r), lambda i: (l, 0, 0, 0)),
                  pl.BlockSpec((None, r, nq), lambda i: (l, 0, 0)),
                  pl.BlockSpec((None, r, nkv), lambda i: (l, 0, 0)),
                  pl.BlockSpec((1, nq), lambda i: (0, 0)),
                  pl.BlockSpec((bm, LANES), lambda i: (i, 0)),
                  pl.BlockSpec((bm, LANES), lambda i: (i, 0))],
        out_specs=[pl.BlockSpec((bm, nq), lambda i: (i, 0)), pl.BlockSpec((bm, nkv), lambda i: (i, 0))],
        out_shape=[jax.ShapeDtypeStruct((m, nq), BF16), jax.ShapeDtypeStruct((m, nkv), BF16)],
        compiler_params=_cparams(("parallel",)),
        name="mla_proj",
    )(u, w_c, gains, wuq, wukv, colscale, cos, sin)


def _logits(q, k, mlane):
    s = lax.dot_general(q, k, (((1,), (1,)), ((), ())), preferred_element_type=F32)
    for b in range(s.shape[1] // LANES):
        blk = s[:, b * LANES:(b + 1) * LANES]
        mlane = blk if mlane is None else jnp.maximum(mlane, blk)
    return s, mlane


def _row_max(mlane):
    return jnp.broadcast_to(jnp.max(mlane, axis=1, keepdims=True), mlane.shape)


def _weigh(s, m, v1, acc):
    ps = [jnp.exp2(s[:, b * LANES:(b + 1) * LANES] - m).astype(BF16) for b in range(s.shape[1] // LANES)]
    pv = jnp.dot(jnp.concatenate(ps, axis=1), v1, preferred_element_type=F32)
    return pv if acc is None else acc + pv


def _softmax_pv(qs, k_chunk, v1_chunk, nchunks):
    n = len(qs)
    logits = [[None] * nchunks for _ in range(n)]
    acc = [None] * n
    mlane = None
    for j in range(nchunks):
        logits[0][j], mlane = _logits(qs[0], k_chunk(j), mlane)
    m = _row_max(mlane)
    for p in range(1, n):
        mlane = None
        for j in range(nchunks):
            acc[p - 1] = _weigh(logits[p - 1][j], m, v1_chunk(j), acc[p - 1])
            logits[p][j], mlane = _logits(qs[p], k_chunk(j), mlane)
        m = _row_max(mlane)
    for j in range(nchunks):
        acc[n - 1] = _weigh(logits[n - 1][j], m, v1_chunk(j), acc[n - 1])
    return [a[:, :a.shape[1] - LANES] / a[:, a.shape[1] - LANES:] for a in acc]


def _attn_tiles(s):
    rows = _pick(s, 512)
    return rows, (2 if s % (2 * rows) == 0 else 1), _pick(s, 1024)


def _diff_attn_kernel(lp_ref, q_ref, k_ref, v_ref, g_ref, o_ref, *, lambda_init, tk, nsub):
    bq = q_ref.shape[0] // nsub
    ones = jnp.ones((tk, LANES), BF16)
    k_chunk = lambda j: k_ref[j * tk:(j + 1) * tk, :]
    v1_chunk = lambda j: jnp.concatenate([v_ref[j * tk:(j + 1) * tk, :], ones], axis=1)
    qs = []
    for r in range(nsub):
        q = q_ref[r * bq:(r + 1) * bq, :]
        lane = lax.broadcasted_iota(jnp.int32, q.shape, 1)
        zero = jnp.zeros_like(q)
        qs.append(jnp.concatenate([jnp.where(lane < DIFF_QK_DIM, q, zero),
                                   jnp.where(lane >= DIFF_QK_DIM, q, zero)], axis=0))
    outs = _softmax_pv(qs, k_chunk, v1_chunk, k_ref.shape[0] // tk)
    lp = lp_ref[...]
    lam = (jnp.exp(jnp.sum(lp[0:1] * lp[1:2], axis=1, keepdims=True))
           - jnp.exp(jnp.sum(lp[2:3] * lp[3:4], axis=1, keepdims=True)) + lambda_init)
    for r in range(nsub):
        o = outs[r][:bq] - lam * outs[r][bq:]
        y = o * lax.rsqrt(jnp.mean(o * o, axis=-1, keepdims=True) + DIFF_SUBLN_EPS)
        o_ref[r * bq:(r + 1) * bq, :] = (y * g_ref[...] * (1.0 - lambda_init)).astype(o_ref.dtype)


def _diff_attention(qk, v, diff_lambda, diff_subln, l, lambda_init):
    s = qk.shape[0]
    prob_rows, nsub, tk = _attn_tiles(s)
    rows = prob_rows // 2 * nsub
    h = DIFF_HEADS
    return pl.pallas_call(
        functools.partial(_diff_attn_kernel, lambda_init=lambda_init, tk=tk, nsub=nsub),
        grid=(h, s // rows),
        in_specs=[pl.BlockSpec((None, 4, DIFF_QK_DIM), lambda hh, i: (l, 0, 0)),
                  pl.BlockSpec((rows, LANES), lambda hh, i: (i, hh)),
                  pl.BlockSpec((s, LANES), lambda hh, i: (0, h + hh)),
                  pl.BlockSpec((s, LANES), lambda hh, i: (0, hh)),
                  pl.BlockSpec((None, 1, DIFF_V_DIM), lambda hh, i: (l, 0, 0))],
        out_specs=pl.BlockSpec((rows, LANES), lambda hh, i: (i, hh)),
        out_shape=jax.ShapeDtypeStruct((s, h * DIFF_V_DIM), BF16),
        compiler_params=_cparams(("parallel", "parallel")),
        name="diff_attention",
    )(diff_lambda, qk, qk, v, diff_subln)


def _mla_attn_kernel(q_ref, kn_ref, kr_ref, v_ref, o_ref, *, tk, nsub):
    bq = q_ref.shape[0] // nsub
    ones = jnp.ones((tk, LANES), BF16)
    k_chunk = lambda j: jnp.concatenate([kn_ref[j * tk:(j + 1) * tk, :], kr_ref[j * tk:(j + 1) * tk, :]], axis=1)
    v1_chunk = lambda j: jnp.concatenate([v_ref[j * tk:(j + 1) * tk, :], ones], axis=1)
    qs = [q_ref[r * bq:(r + 1) * bq, :] for r in range(nsub)]
    outs = _softmax_pv(qs, k_chunk, v1_chunk, kn_ref.shape[0] // tk)
    for r in range(nsub):
        o_ref[r * bq:(r + 1) * bq, :] = outs[r].astype(o_ref.dtype)


def _mla_attention(q, kv_up, qk):
    s = q.shape[0]
    bq, nsub, tk = _attn_tiles(s)
    rows = bq * nsub
    h = MLA_HEADS
    kr_blk = 2 * DIFF_HEADS
    return pl.pallas_call(
        functools.partial(_mla_attn_kernel, tk=tk, nsub=nsub),
        grid=(h, s // rows),
        in_specs=[pl.BlockSpec((rows, 2 * LANES), lambda hh, i: (i, hh)),
                  pl.BlockSpec((s, LANES), lambda hh, i: (0, 2 * hh)),
                  pl.BlockSpec((s, LANES), lambda hh, i: (0, kr_blk)),
                  pl.BlockSpec((s, LANES), lambda hh, i: (0, 2 * hh + 1))],
        out_specs=pl.BlockSpec((rows, LANES), lambda hh, i: (i, hh)),
        out_shape=jax.ShapeDtypeStruct((s, h * MLA_V), BF16),
        compiler_params=_cparams(("parallel", "parallel")),
        name="mla_attention",
    )(q, kv_up, qk, kv_up)


def _gqa_kernel(sink_ref, q_ref, kp_ref, kc_ref, kn_ref, vp_ref, vc_ref, vn_ref, o_ref, *, layer, seq, bq):
    g = pl.program_id(0)
    i = pl.program_id(1)
    nsub = q_ref.shape[0] // bq
    nk = bq + 2 * WINDOW
    band_k = jnp.concatenate([kp_ref[...], kc_ref[...], kn_ref[...]], axis=0)
    band_v = jnp.concatenate([vp_ref[...], vc_ref[...], vn_ref[...]], axis=0)
    band_v1 = jnp.concatenate([band_v, jnp.ones_like(band_v)], axis=1)
    for r in range(nsub):
        q0 = (i * nsub + r) * bq
        kb = band_k[r * bq:r * bq + nk]
        v1 = band_v1[r * bq:r * bq + nk]
        qpos = q0 + lax.broadcasted_iota(jnp.int32, (bq, nk), 0)
        kpos = q0 - WINDOW + lax.broadcasted_iota(jnp.int32, (bq, nk), 1)
        valid = (jnp.abs(kpos - qpos) <= WINDOW) & (kpos >= 0) & (kpos < seq)
        bias = jnp.where(valid, 0.0, NEG_INF).astype(F32)
        for hh in range(GQA_GROUP):
            q = q_ref[r * bq:(r + 1) * bq, hh * GQA_DIM:(hh + 1) * GQA_DIM]
            s = lax.dot_general(q, kb, (((1,), (1,)), ((), ())), preferred_element_type=F32) + bias
            sink = sink_ref[layer * GQA_HEADS + g * GQA_GROUP + hh] * LOG2E
            m = jnp.maximum(jnp.max(s, axis=1, keepdims=True), sink)
            pv = jnp.dot(jnp.exp2(s - m).astype(BF16), v1, preferred_element_type=F32)
            denom = pv[:, GQA_DIM:] + jnp.exp2(sink - m)
            o_ref[r * bq:(r + 1) * bq, hh * GQA_DIM:(hh + 1) * GQA_DIM] = (pv[:, :GQA_DIM] / denom).astype(o_ref.dtype)


def _window_gqa(gqk, vv, sink_flat, l):
    s = gqk.shape[0]
    sub = _pick(s, 256)
    bq = _pick(s, 8 * sub)
    assert sub % WINDOW == 0 and WINDOW == LANES
    nb = s // bq
    r = bq // WINDOW
    nh = s // WINDOW
    qw = GQA_GROUP * GQA_DIM
    kcol = GQA_HEADS
    vcol = DIFF_HEADS

    def prev(i):
        return jnp.maximum(i * r - 1, 0)

    def nxt(i):
        return jnp.minimum((i + 1) * r, nh - 1)

    return pl.pallas_call(
        functools.partial(_gqa_kernel, layer=l, seq=s, bq=sub),
        grid=(GQA_KV_HEADS, nb),
        in_specs=[pl.BlockSpec(memory_space=pltpu.SMEM),
                  pl.BlockSpec((bq, qw), lambda g, i: (i, g)),
                  pl.BlockSpec((WINDOW, LANES), lambda g, i: (prev(i), kcol + g)),
                  pl.BlockSpec((bq, LANES), lambda g, i: (i, kcol + g)),
                  pl.BlockSpec((WINDOW, LANES), lambda g, i: (nxt(i), kcol + g)),
                  pl.BlockSpec((WINDOW, LANES), lambda g, i: (prev(i), vcol + g)),
                  pl.BlockSpec((bq, LANES), lambda g, i: (i, vcol + g)),
                  pl.BlockSpec((WINDOW, LANES), lambda g, i: (nxt(i), vcol + g))],
        out_specs=pl.BlockSpec((bq, qw), lambda g, i: (i, g)),
        out_shape=jax.ShapeDtypeStruct((s, GQA_HEADS * GQA_DIM), BF16),
        compiler_params=_cparams(("parallel", "parallel")),
        name="window_gqa",
    )(sink_flat, gqk, gqk, gqk, gqk, vv, vv, vv)


def _merge_kernel(u_ref, o0_ref, o1_ref, o2_ref, wg0_ref, wg1_ref, wg2_ref, b0_ref, b1_ref, b2_ref,
                  wb0_ref, wb1_ref, wb2_ref, out_ref, wg0_s, wg1_s, wg2_s, wb0_s, wb1_s, wb2_s):
    _cast_weights_once(((wg0_ref, wg0_s), (wg1_ref, wg1_s), (wg2_ref, wg2_s),
                        (wb0_ref, wb0_s), (wb1_ref, wb1_s), (wb2_ref, wb2_s)))
    u = u_ref[...]
    acc = None
    for o_ref, wg_s, b_ref, wb_s in ((o0_ref, wg0_s, b0_ref, wb0_s),
                                     (o1_ref, wg1_s, b1_ref, wb1_s),
                                     (o2_ref, wg2_s, b2_ref, wb2_s)):
        gate = jax.nn.sigmoid(jnp.dot(u, wg_s[...], preferred_element_type=F32) + b_ref[...])
        term = jnp.dot(o_ref[...], wb_s[...], preferred_element_type=F32) * gate
        acc = term if acc is None else acc + term
    out_ref[...] = acc.astype(out_ref.dtype)


def _merge(u, o_diff, o_mla, o_gqa, w_gate, b_gate, w_branch, l):
    m, d = u.shape
    bw = o_diff.shape[1]
    bm = _pick(m, 1024)
    bn = _pick(d, 256)
    nj = d // bn
    o_spec = pl.BlockSpec((bm, bw), lambda j, i: (i, 0))

    def wg_spec(n):
        return pl.BlockSpec((None, d, bn), lambda j, i: (l, 0, n * nj + j))

    def b_spec(n):
        return pl.BlockSpec((None, 1, bn), lambda j, i: (l, 0, n * nj + j))

    def wb_spec(n):
        return pl.BlockSpec((None, None, bw, bn), lambda j, i: (l, n, 0, j))

    return pl.pallas_call(
        _merge_kernel,
        grid=(nj, m // bm),
        in_specs=[pl.BlockSpec((bm, d), lambda j, i: (i, 0)), o_spec, o_spec, o_spec,
                  wg_spec(0), wg_spec(1), wg_spec(2), b_spec(0), b_spec(1), b_spec(2),
                  wb_spec(0), wb_spec(1), wb_spec(2)],
        out_specs=pl.BlockSpec((bm, bn), lambda j, i: (i, j)),
        out_shape=jax.ShapeDtypeStruct((m, d), BF16),
        scratch_shapes=[pltpu.VMEM((d, bn), BF16)] * 3 + [pltpu.VMEM((bw, bn), BF16)] * 3,
        compiler_params=_cparams(("arbitrary", "arbitrary")),
        name="branch_merge",
    )(u, o_diff, o_mla, o_gqa, w_gate, w_gate, w_gate, b_gate, b_gate, b_gate,
      w_branch, w_branch, w_branch)


def _xattn_kernel(mg_ref, h_ref, wm_ref, g_ref, wq_ref, k_ref, v_ref, wo_ref, o_ref, ob_ref, wq_s, wo_s):
    @pl.when(pl.program_id(0) == 0)
    def _():
        wq_s[...] = wq_ref[...].astype(BF16)
        wo_s[...] = wo_ref[...].astype(BF16)

    h1 = h_ref[...] + jnp.dot(mg_ref[...], wm_ref[...], preferred_element_type=F32)
    hn = _norm_rows(h1, g_ref[...])
    q = jnp.dot(hn, wq_s[...], preferred_element_type=F32) * (XA_DIM ** -0.5 * LOG2E)
    q = q.astype(BF16)
    heads = []
    for hh in range(XA_HEADS):
        sl = slice(hh * XA_DIM, (hh + 1) * XA_DIM)
        s = lax.dot_general(q[:, sl], k_ref[:, sl], (((1,), (1,)), ((), ())), preferred_element_type=F32)
        p = jnp.exp2(s - jnp.max(s, axis=1, keepdims=True))
        denom = jnp.sum(p, axis=1, keepdims=True)
        heads.append((jnp.dot(p.astype(BF16), v_ref[:, sl], preferred_element_type=F32) / denom).astype(BF16))
    o = jnp.concatenate(heads, axis=1)
    out = h1 + jnp.dot(o, wo_s[...], preferred_element_type=F32)
    o_ref[...] = out
    ob_ref[...] = out.astype(BF16)


def _cross_attention(merged, h, w_o, gain, xa_wq, memkv, xa_wo, l):
    m, d = h.shape
    w = XA_HEADS * XA_DIM
    ml = memkv.shape[1]
    bm = _pick(m, 512)
    once = dict(pipeline_mode=pl.Buffered(1))
    return pl.pallas_call(
        _xattn_kernel,
        grid=(m // bm,),
        in_specs=[pl.BlockSpec((bm, d), lambda i: (i, 0)),
                  pl.BlockSpec((bm, d), lambda i: (i, 0)),
                  pl.BlockSpec((None, d, d), lambda i: (l, 0, 0), **once),
                  pl.BlockSpec((None, 1, d), lambda i: (l, 0, 0)),
                  pl.BlockSpec((None, d, w), lambda i: (l, 0, 0), **once),
                  pl.BlockSpec((None, ml, w), lambda i: (l, 0, 0)),
                  pl.BlockSpec((None, ml, w), lambda i: (l, 0, 1)),
                  pl.BlockSpec((None, w, d), lambda i: (l, 0, 0), **once)],
        out_specs=[pl.BlockSpec((bm, d), lambda i: (i, 0))] * 2,
        out_shape=[jax.ShapeDtypeStruct((m, d), F32), jax.ShapeDtypeStruct((m, d), BF16)],
        scratch_shapes=[pltpu.VMEM((d, w), BF16), pltpu.VMEM((w, d), BF16)],
        compiler_params=_cparams(("arbitrary",)),
        name="cross_attention",
    )(merged, h, w_o, gain, xa_wq, memkv, memkv, xa_wo)


def _pad_cols(w, width):
    return jnp.pad(w, ((0, 0),) * (w.ndim - 1) + ((0, width - w.shape[-1]),))


_DQ, _DV, _CQ, _KR, _GQ, _GV, _END = 0, 2048, 3072, 4096, 4160, 5440, 5696


def _regroup_kernel(wt_ref, a_ref, v_ref, c_ref, g_ref):
    def cols(lo, hi):
        return wt_ref[lo:hi, :].T

    kr = jnp.concatenate([wt_ref[_KR:_GQ, :], jnp.zeros((LANES - (_GQ - _KR), wt_ref.shape[1]), wt_ref.dtype)], axis=0)
    a_ref[...] = jnp.concatenate([cols(_DQ, _DV), kr.T], axis=1).astype(BF16)
    v_ref[...] = jnp.concatenate([cols(_DV, _CQ), cols(_GV, _END)], axis=1).astype(BF16)
    c_ref[...] = cols(_CQ, _KR).astype(BF16)
    g_ref[...] = cols(_GQ, _GV).astype(BF16)


def _regroup_w_in(w_in):
    depth, k, n = w_in.shape
    assert n == _END
    bk = _pick(k, 256)
    widths = (_DV - _DQ + LANES, (_CQ - _DV) + (_END - _GV), _KR - _CQ, _GV - _GQ)
    return pl.pallas_call(
        _regroup_kernel,
        grid=(depth, k // bk),
        in_specs=[pl.BlockSpec((None, n, bk), lambda l, i: (l, 0, i))],
        out_specs=[pl.BlockSpec((None, bk, w), lambda l, i: (l, i, 0)) for w in widths],
        out_shape=[jax.ShapeDtypeStruct((depth, k, w), BF16) for w in widths],
        compiler_params=_cparams(("parallel", "parallel")),
        name="regroup_w_in",
    )(jnp.swapaxes(w_in, 1, 2))


def kernel(x, mem, positions, ffn1_norm, ffn1_w_gu, ffn1_w_down, mix_norm, w_in, diff_lambda, diff_subln,
           mla_q_norm, mla_kv_norm, mla_w_uq, mla_w_ukv, gqa_sink, w_branch, w_gate, b_gate, w_o, xa_norm,
           mem_norm, xa_wq, xa_wkv, xa_wo, ffn2_norm, ffn2_w_gu, ffn2_w_down, final_norm):
    b, s, d = x.shape
    assert b == 1
    depth = ffn1_norm.shape[0]
    h = x.reshape(s, d)
    mem2 = mem.reshape(mem.shape[1], d)

    pos_b = jnp.broadcast_to(positions.reshape(s, 1).astype(F32), (s, LANES))
    cos64, sin64 = _rope_tables(pos_b, DIFF_QK_DIM)
    cos128, sin128 = _rope_tables(pos_b, GQA_DIM)

    bf = lambda w: w.astype(BF16)
    w_o_b = bf(w_o)
    b_gate3 = b_gate.reshape(depth, 1, -1)
    norm3 = lambda g: g.reshape(depth, 1, -1)
    mix_norm3, xa_norm3, mem_norm3 = map(norm3, (mix_norm, xa_norm, mem_norm))
    memkv = _mem_kv(mem2, mem_norm3, bf(xa_wkv))
    ffn1_norm_col, ffn2_norm_col = ffn1_norm.reshape(depth, d, 1), ffn2_norm.reshape(depth, d, 1)
    hb = h
    diff_subln3 = diff_subln.reshape(depth, 1, -1)
    sink_flat = gqa_sink.reshape(-1)

    diff_scale = DIFF_QK_DIM ** -0.5 * LOG2E
    mla_scale = (MLA_NOPE + MLA_ROPE) ** -0.5 * LOG2E
    gqa_scale = GQA_DIM ** -0.5 * LOG2E
    cs_a = jnp.concatenate([jnp.full((1, 1024), diff_scale, F32), jnp.ones((1, 1024 + LANES), F32)], axis=1)
    cs_g = jnp.concatenate([jnp.full((1, 1024), gqa_scale, F32), jnp.ones((1, 256), F32)], axis=1)
    cs_q = jnp.full((1, MLA_HEADS * 2 * LANES), mla_scale, F32)

    w_a, w_v, w_c, w_g = _regroup_w_in(w_in)
    wuq = mla_w_uq.reshape(depth, MLA_Q_RANK, MLA_HEADS, MLA_NOPE + MLA_ROPE)
    wuq = _pad_cols(wuq, 2 * LANES).reshape(depth, MLA_Q_RANK, MLA_HEADS * 2 * LANES)
    wuq, wukv = bf(wuq), bf(mla_w_ukv)
    mla_gains = jnp.stack([mla_q_norm, mla_kv_norm], axis=1)[:, :, None, :]

    for l in range(depth):
        lambda_init = 0.8 - 0.6 * math.exp(-0.3 * l)

        a = _ffn_up(hb, ffn1_norm_col, ffn1_w_gu, l)
        h = _mm_residual(a, ffn1_w_down, l, h, 0.5)

        qk, vv, gqk, u = _mixer_proj(h, mix_norm3, w_a, w_v, w_g, l, cs_a, cs_g, cos64, sin64, cos128, sin128)

        o_diff = _diff_attention(qk, vv, diff_lambda, diff_subln3, l, lambda_init)

        q_mla, kv_up = _mla_proj(u, w_c, mla_gains, wuq, wukv, l, cs_q, cos64, sin64)
        o_mla = _mla_attention(q_mla, kv_up, qk)

        o_gqa = _window_gqa(gqk, vv, sink_flat, l)

        merged = _merge(u, o_diff, o_mla, o_gqa, w_gate, b_gate3, w_branch, l)
        h, hb = _cross_attention(merged, h, w_o_b, xa_norm3, xa_wq, memkv, xa_wo, l)

        a = _ffn_up(hb, ffn2_norm_col, ffn2_w_gu, l)
        if l + 1 < depth:
            h, hb = _mm_residual(a, ffn2_w_down, l, h, 0.5, emit_bf16=True)
        else:
            h = _mm_residual(a, ffn2_w_down, l, h, 0.5)

    out = _rmsnorm(h, final_norm.reshape(1, d), F32)
    return out.reshape(b, s, d)
```
